```python
import math
import jax, jax.numpy as jnp
from jax import lax
import numpy as np

D_MODEL = 1024
BATCH = 8
SEQ = 2048
DEPTH = 4

CONV_DIM = 512
CONV_WIDTH = 31
N_GROUPS = 3
HEADS_PER_GROUP = 8
HEAD_DIM = 64
N_HEADS = N_GROUPS * HEADS_PER_GROUP
ATTN_DIM = N_HEADS * HEAD_DIM
ATTN_OUT_DIM = HEADS_PER_GROUP * HEAD_DIM
WINDOWS = (128, 512, 2048)
DILATIONS = (1, 4, 16)
SUB_WINDOW = 128
BLOCK = 128
NUM_BUCKETS = 32
MAX_REL_DISTANCE = 2048
D_FF = 4 * D_MODEL
EPS = 1e-6
NEG_INF = -1e30
IN_COLS = 2 * CONV_DIM + 3 * ATTN_DIM + 2 * D_MODEL

kernel_name = "hybrid_conformer_conv_dilated_attn_gated"


def rms_norm(x, g):
    xf = x.astype(jnp.float32)
    y = xf * lax.rsqrt(jnp.mean(xf * xf, axis=-1, keepdims=True) + EPS)
    return (y * g.astype(jnp.float32)).astype(x.dtype)


def layer_norm(x, g, b):
    xf = x.astype(jnp.float32)
    mu = jnp.mean(xf, axis=-1, keepdims=True)
    xc = xf - mu
    y = xc * lax.rsqrt(jnp.mean(xc * xc, axis=-1, keepdims=True) + EPS)
    return (y * g.astype(jnp.float32) + b.astype(jnp.float32)).astype(x.dtype)


def t5_bucket(dist):
    max_exact = NUM_BUCKETS // 2
    nf = jnp.maximum(dist, 1).astype(jnp.float32)
    large = max_exact + (jnp.log(nf / max_exact) / math.log(MAX_REL_DISTANCE / max_exact)
                         * (NUM_BUCKETS - max_exact)).astype(jnp.int32)
    large = jnp.minimum(large, NUM_BUCKETS - 1)
    return jnp.where(dist < max_exact, dist, large)


def conformer_conv(u, dw_w, dw_b, ln_g, ln_b, w_pw):
    a, gt = jnp.split(u, 2, axis=-1)
    z = a * jax.nn.sigmoid(gt)
    z = lax.conv_general_dilated(
        z, dw_w[:, None, :].astype(z.dtype), window_strides=(1,),
        padding=((CONV_WIDTH - 1, 0),),
        dimension_numbers=('NWC', 'WIO', 'NWC'),
        feature_group_count=CONV_DIM) + dw_b
    z = jax.nn.silu(layer_norm(z, ln_g, ln_b))
    return z @ w_pw


def dilated_group(q, k, v, bias_g, d):
    B, S, H, Dh = q.shape
    L = S // d
    nb = -(-L // BLOCK)
    Lp = nb * BLOCK

    def to_blocks(t):
        t = t.reshape(B, L, d, H, Dh)
        t = jnp.pad(t, ((0, 0), (0, Lp - L), (0, 0), (0, 0), (0, 0)))
        return t.reshape(B, nb, BLOCK, d, H, Dh)

    def band_keys(t):
        prev = jnp.pad(t, ((0, 0), (1, 0), (0, 0), (0, 0), (0, 0), (0, 0)))[:, :-1]
        return jnp.concatenate([prev, t], axis=2)

    qb = to_blocks(q)
    kw = band_keys(to_blocks(k))
    vw = band_keys(to_blocks(v))
    s = jnp.einsum('bnqrhe,bnkrhe->bnrhqk', qb, kw)

    qi = jnp.arange(BLOCK)[:, None]
    kj = jnp.arange(2 * BLOCK)[None, :]
    off = qi + BLOCK - kj
    band = (off >= 0) & (off <= SUB_WINDOW)
    blk = jnp.arange(nb)[:, None, None]
    valid = band[None] & (blk * BLOCK + kj[None] - BLOCK >= 0)
    bucket = t5_bucket(jnp.clip(off, 0, SUB_WINDOW) * d)
    bias = jnp.transpose(bias_g.astype(jnp.float32)[bucket], (2, 0, 1))

    s = jnp.where(valid[None, :, None, None], s + bias, NEG_INF)
    lse = jax.nn.logsumexp(s, axis=-1)
    p = jnp.exp(s - lse[..., None])
    o = jnp.einsum('bnrhqk,bnkrhe->bnqrhe', p, vw)
    o = o.reshape(B, Lp, d, H, Dh)[:, :L].reshape(B, S, H, Dh)
    lse = jnp.transpose(lse, (0, 1, 4, 2, 3)).reshape(B, Lp, d, H)[:, :L].reshape(B, S, H)
    return o, lse


def dilated_attention(qkv, q_g, k_g, rel_bias):
    B, S, _ = qkv.shape
    q, k, v = jnp.split(qkv.astype(jnp.float32), 3, axis=-1)
    shp = (B, S, N_GROUPS, HEADS_PER_GROUP, HEAD_DIM)
    q = rms_norm(q.reshape(shp), q_g) * (HEAD_DIM ** -0.5)
    k = rms_norm(k.reshape(shp), k_g)
    v = v.reshape(shp)
    outs, lses = [], []
    for g in range(N_GROUPS):
        o, l = dilated_group(q[:, :, g], k[:, :, g], v[:, :, g],
                             rel_bias[:, g * HEADS_PER_GROUP:(g + 1) * HEADS_PER_GROUP], DILATIONS[g])
        outs.append(o)
        lses.append(l)
    w = jax.nn.softmax(jnp.stack(lses, axis=0), axis=0)
    o = jnp.sum(w[..., None] * jnp.stack(outs, axis=0), axis=0)
    return o.reshape(B, S, ATTN_OUT_DIM)


def _fwd_setup_inputs(seed: int = 0) -> dict:
    key = jax.random.key(seed)
    ks = jax.random.split(key, 20)
    f32 = jnp.float32

    def nrm(k, shape, scale):
        return jax.random.normal(k, shape, f32) * scale

    res_scale = (2 * DEPTH) ** -0.5
    return {
        "x": nrm(ks[0], (BATCH, SEQ, D_MODEL), 1.0),
        "rel_bias": nrm(ks[1], (NUM_BUCKETS, N_HEADS), 0.5),
        "norm1_g": 1.0 + nrm(ks[2], (DEPTH, D_MODEL), 0.02),
        "w_in": nrm(ks[3], (DEPTH, D_MODEL, IN_COLS), D_MODEL ** -0.5),
        "q_norm_g": 1.0 + nrm(ks[4], (DEPTH, HEAD_DIM), 0.02),
        "k_norm_g": 1.0 + nrm(ks[5], (DEPTH, HEAD_DIM), 0.02),
        "conv_dw_w": nrm(ks[6], (DEPTH, CONV_WIDTH, CONV_DIM), CONV_WIDTH ** -0.5),
        "conv_dw_b": nrm(ks[7], (DEPTH, CONV_DIM), 0.02),
        "conv_ln_g": 1.0 + nrm(ks[8], (DEPTH, CONV_DIM), 0.02),
        "conv_ln_b": nrm(ks[9], (DEPTH, CONV_DIM), 0.02),
        "w_conv_out": nrm(ks[10], (DEPTH, CONV_DIM, D_MODEL), CONV_DIM ** -0.5),
        "w_attn_out": nrm(ks[11], (DEPTH, ATTN_OUT_DIM, D_MODEL), ATTN_OUT_DIM ** -0.5),
        "w_out": nrm(ks[12], (DEPTH, D_MODEL, D_MODEL), D_MODEL ** -0.5 * res_scale),
        "norm2_g": 1.0 + nrm(ks[13], (DEPTH, D_MODEL), 0.02),
        "w_ff1": nrm(ks[14], (DEPTH, D_MODEL, D_FF), D_MODEL ** -0.5),
        "w_ff2": nrm(ks[15], (DEPTH, D_FF, D_MODEL), D_FF ** -0.5 * res_scale),
    }


def _fwd_reference(x, rel_bias, norm1_g, w_in, q_norm_g, k_norm_g, conv_dw_w, conv_dw_b,
              conv_ln_g, conv_ln_b, w_conv_out, w_attn_out, w_out, norm2_g, w_ff1, w_ff2):
    c_conv = 2 * CONV_DIM
    c_attn = c_conv + 3 * ATTN_DIM
    for l in range(DEPTH):
        h = rms_norm(x, norm1_g[l])
        u = h @ w_in[l]
        y_conv = conformer_conv(u[..., :c_conv], conv_dw_w[l], conv_dw_b[l],
                                conv_ln_g[l], conv_ln_b[l], w_conv_out[l])
        y_attn = dilated_attention(u[..., c_conv:c_attn], q_norm_g[l], k_norm_g[l],
                                   rel_bias).astype(x.dtype) @ w_attn_out[l]
        g_conv, g_attn = jnp.split(jax.nn.sigmoid(u[..., c_attn:]), 2, axis=-1)
        x = x + (g_conv * y_conv + g_attn * y_attn) @ w_out[l]
        h = rms_norm(x, norm2_g[l])
        x = x + jnp.square(jax.nn.relu(h @ w_ff1[l])) @ w_ff2[l]
    return x


import jax as _jax
import jax.numpy as _jnp

TWIN_FORMAT = 'train_step'
FWD_PARAMS = ['x', 'rel_bias', 'norm1_g', 'w_in', 'q_norm_g', 'k_norm_g', 'conv_dw_w', 'conv_dw_b', 'conv_ln_g', 'conv_ln_b', 'w_conv_out', 'w_attn_out', 'w_out', 'norm2_g', 'w_ff1', 'w_ff2']
TWIN_WEIGHTS = ['rel_bias', 'norm1_g', 'w_in', 'q_norm_g', 'k_norm_g', 'conv_dw_w', 'conv_dw_b', 'conv_ln_g', 'conv_ln_b', 'w_conv_out', 'w_attn_out', 'w_out', 'norm2_g', 'w_ff1', 'w_ff2']
TWIN_DIFF_INPUT = 'x'
TWIN_INPUTS = ['x', 'rel_bias', 'norm1_g', 'w_in', 'q_norm_g', 'k_norm_g', 'conv_dw_w', 'conv_dw_b', 'conv_ln_g', 'conv_ln_b', 'w_conv_out', 'w_attn_out', 'w_out', 'norm2_g', 'w_ff1', 'w_ff2', 'loss_target', 'm_rel_bias', 'm_norm1_g', 'm_w_in', 'm_q_norm_g', 'm_k_norm_g', 'm_conv_dw_w', 'm_conv_dw_b', 'm_conv_ln_g', 'm_conv_ln_b', 'm_w_conv_out', 'm_w_attn_out', 'm_w_out', 'm_norm2_g', 'm_w_ff1', 'm_w_ff2', 'v_rel_bias', 'v_norm1_g', 'v_w_in', 'v_q_norm_g', 'v_k_norm_g', 'v_conv_dw_w', 'v_conv_dw_b', 'v_conv_ln_g', 'v_conv_ln_b', 'v_w_conv_out', 'v_w_attn_out', 'v_w_out', 'v_norm2_g', 'v_w_ff1', 'v_w_ff2']
TWIN_OUTPUTS = ['loss', 'grad_x', 'grad_rel_bias', 'grad_norm1_g', 'grad_w_in', 'grad_q_norm_g', 'grad_k_norm_g', 'grad_conv_dw_w', 'grad_conv_dw_b', 'grad_conv_ln_g', 'grad_conv_ln_b', 'grad_w_conv_out', 'grad_w_attn_out', 'grad_w_out', 'grad_norm2_g', 'grad_w_ff1', 'grad_w_ff2', 'delta_rel_bias', 'delta_norm1_g', 'delta_w_in', 'delta_q_norm_g', 'delta_k_norm_g', 'delta_conv_dw_w', 'delta_conv_dw_b', 'delta_conv_ln_g', 'delta_conv_ln_b', 'delta_w_conv_out', 'delta_w_attn_out', 'delta_w_out', 'delta_norm2_g', 'delta_w_ff1', 'delta_w_ff2', 'new_m_rel_bias', 'new_m_norm1_g', 'new_m_w_in', 'new_m_q_norm_g', 'new_m_k_norm_g', 'new_m_conv_dw_w', 'new_m_conv_dw_b', 'new_m_conv_ln_g', 'new_m_conv_ln_b', 'new_m_w_conv_out', 'new_m_w_attn_out', 'new_m_w_out', 'new_m_norm2_g', 'new_m_w_ff1', 'new_m_w_ff2', 'new_v_rel_bias', 'new_v_norm1_g', 'new_v_w_in', 'new_v_q_norm_g', 'new_v_k_norm_g', 'new_v_conv_dw_w', 'new_v_conv_dw_b', 'new_v_conv_ln_g', 'new_v_conv_ln_b', 'new_v_w_conv_out', 'new_v_w_attn_out', 'new_v_w_out', 'new_v_norm2_g', 'new_v_w_ff1', 'new_v_w_ff2']
TWIN_LEAF_KINDS = {'loss': 'loss', 'grad_x': 'grad_x', 'grad_rel_bias': 'grad_w', 'grad_norm1_g': 'grad_w', 'grad_w_in': 'grad_w', 'grad_q_norm_g': 'grad_w', 'grad_k_norm_g': 'grad_w', 'grad_conv_dw_w': 'grad_w', 'grad_conv_dw_b': 'grad_w', 'grad_conv_ln_g': 'grad_w', 'grad_conv_ln_b': 'grad_w', 'grad_w_conv_out': 'grad_w', 'grad_w_attn_out': 'grad_w', 'grad_w_out': 'grad_w', 'grad_norm2_g': 'grad_w', 'grad_w_ff1': 'grad_w', 'grad_w_ff2': 'grad_w', 'delta_rel_bias': 'delta_w', 'delta_norm1_g': 'delta_w', 'delta_w_in': 'delta_w', 'delta_q_norm_g': 'delta_w', 'delta_k_norm_g': 'delta_w', 'delta_conv_dw_w': 'delta_w', 'delta_conv_dw_b': 'delta_w', 'delta_conv_ln_g': 'delta_w', 'delta_conv_ln_b': 'delta_w', 'delta_w_conv_out': 'delta_w', 'delta_w_attn_out': 'delta_w', 'delta_w_out': 'delta_w', 'delta_norm2_g': 'delta_w', 'delta_w_ff1': 'delta_w', 'delta_w_ff2': 'delta_w', 'new_m_rel_bias': 'new_m', 'new_m_norm1_g': 'new_m', 'new_m_w_in': 'new_m', 'new_m_q_norm_g': 'new_m', 'new_m_k_norm_g': 'new_m', 'new_m_conv_dw_w': 'new_m', 'new_m_conv_dw_b': 'new_m', 'new_m_conv_ln_g': 'new_m', 'new_m_conv_ln_b': 'new_m', 'new_m_w_conv_out': 'new_m', 'new_m_w_attn_out': 'new_m', 'new_m_w_out': 'new_m', 'new_m_norm2_g': 'new_m', 'new_m_w_ff1': 'new_m', 'new_m_w_ff2': 'new_m', 'new_v_rel_bias': 'new_v', 'new_v_norm1_g': 'new_v', 'new_v_w_in': 'new_v', 'new_v_q_norm_g': 'new_v', 'new_v_k_norm_g': 'new_v', 'new_v_conv_dw_w': 'new_v', 'new_v_conv_dw_b': 'new_v', 'new_v_conv_ln_g': 'new_v', 'new_v_conv_ln_b': 'new_v', 'new_v_w_conv_out': 'new_v', 'new_v_w_attn_out': 'new_v', 'new_v_w_out': 'new_v', 'new_v_norm2_g': 'new_v', 'new_v_w_ff1': 'new_v', 'new_v_w_ff2': 'new_v'}


def _forward(args):
    return _fwd_reference(*[args[k] for k in FWD_PARAMS])


def _output_shape():
    out = _jax.eval_shape(lambda: _forward(_fwd_setup_inputs(0)))
    return out.shape, out.dtype

N_MICROBATCH = 1
ADAM_LR = 0.001
ADAM_B1 = 0.9
ADAM_B2 = 0.999
ADAM_EPS = 1e-08
ADAM_WD = 0.01
ADAM_STEP = 10
PER_EXAMPLE_BATCH_AXIS = {'x': 0, 'loss_target': 0}
SHARED_INPUTS = []
_WEIGHT_DTYPES = {'rel_bias': _jnp.float32, 'norm1_g': _jnp.float32, 'w_in': _jnp.float32, 'q_norm_g': _jnp.float32, 'k_norm_g': _jnp.float32, 'conv_dw_w': _jnp.float32, 'conv_dw_b': _jnp.float32, 'conv_ln_g': _jnp.float32, 'conv_ln_b': _jnp.float32, 'w_conv_out': _jnp.float32, 'w_attn_out': _jnp.float32, 'w_out': _jnp.float32, 'norm2_g': _jnp.float32, 'w_ff1': _jnp.float32, 'w_ff2': _jnp.float32}
MOMENT_SCALE = {'rel_bias': 1.006275e-01, 'norm1_g': 1.748144e-01, 'w_in': 5.936396e-02, 'q_norm_g': 7.831186e-02, 'k_norm_g': 7.806411e-02, 'conv_dw_w': 1.673301e-01, 'conv_dw_b': 1.542262e+00, 'conv_ln_g': 8.048565e-01, 'conv_ln_b': 9.219990e-01, 'w_conv_out': 2.752724e-01, 'w_attn_out': 1.225843e-01, 'w_out': 8.296293e-01, 'norm2_g': 5.988943e+00, 'w_ff1': 2.441526e-01, 'w_ff2': 3.405430e+00}


def _to_microbatches(a, axis):
    t = _jnp.moveaxis(a, axis, 0)
    t = t.reshape((N_MICROBATCH, t.shape[0] // N_MICROBATCH) + t.shape[1:])
    return _jnp.moveaxis(t, 1, axis + 1)


def setup_inputs(seed: int = 0) -> dict:
    inp = _fwd_setup_inputs(seed)
    key = _jax.random.fold_in(_jax.random.key(seed), 7919)
    shape, _ = _output_shape()
    out = dict(inp)
    out["loss_target"] = _jax.random.normal(_jax.random.fold_in(key, 0), shape, _jnp.float32)
    for i, name in enumerate(TWIN_WEIGHTS):
        w = inp[name].astype(_jnp.float32)
        if MOMENT_SCALE is None:
            s = _jnp.sqrt(_jnp.mean(_jnp.square(w)) + 1e-30)
        else:
            s = MOMENT_SCALE[name]
        km, kv = _jax.random.split(_jax.random.fold_in(key, i + 1))
        out[name] = w
        out["m_" + name] = s * _jax.random.normal(km, w.shape, _jnp.float32)
        out["v_" + name] = (s * s) * _jax.random.uniform(kv, w.shape, _jnp.float32, 0.5, 1.5)
    if N_MICROBATCH > 1:
        for name, axis in PER_EXAMPLE_BATCH_AXIS.items():
            out[name] = _to_microbatches(out[name], axis)
    return {'x': out['x'], 'rel_bias': out['rel_bias'], 'norm1_g': out['norm1_g'], 'w_in': out['w_in'], 'q_norm_g': out['q_norm_g'], 'k_norm_g': out['k_norm_g'], 'conv_dw_w': out['conv_dw_w'], 'conv_dw_b': out['conv_dw_b'], 'conv_ln_g': out['conv_ln_g'], 'conv_ln_b': out['conv_ln_b'], 'w_conv_out': out['w_conv_out'], 'w_attn_out': out['w_attn_out'], 'w_out': out['w_out'], 'norm2_g': out['norm2_g'], 'w_ff1': out['w_ff1'], 'w_ff2': out['w_ff2'], 'loss_target': out['loss_target'], 'm_rel_bias': out['m_rel_bias'], 'm_norm1_g': out['m_norm1_g'], 'm_w_in': out['m_w_in'], 'm_q_norm_g': out['m_q_norm_g'], 'm_k_norm_g': out['m_k_norm_g'], 'm_conv_dw_w': out['m_conv_dw_w'], 'm_conv_dw_b': out['m_conv_dw_b'], 'm_conv_ln_g': out['m_conv_ln_g'], 'm_conv_ln_b': out['m_conv_ln_b'], 'm_w_conv_out': out['m_w_conv_out'], 'm_w_attn_out': out['m_w_attn_out'], 'm_w_out': out['m_w_out'], 'm_norm2_g': out['m_norm2_g'], 'm_w_ff1': out['m_w_ff1'], 'm_w_ff2': out['m_w_ff2'], 'v_rel_bias': out['v_rel_bias'], 'v_norm1_g': out['v_norm1_g'], 'v_w_in': out['v_w_in'], 'v_q_norm_g': out['v_q_norm_g'], 'v_k_norm_g': out['v_k_norm_g'], 'v_conv_dw_w': out['v_conv_dw_w'], 'v_conv_dw_b': out['v_conv_dw_b'], 'v_conv_ln_g': out['v_conv_ln_g'], 'v_conv_ln_b': out['v_conv_ln_b'], 'v_w_conv_out': out['v_w_conv_out'], 'v_w_attn_out': out['v_w_attn_out'], 'v_w_out': out['v_w_out'], 'v_norm2_g': out['v_norm2_g'], 'v_w_ff1': out['v_w_ff1'], 'v_w_ff2': out['v_w_ff2']}


def _loss(weights, diff, rest, loss_target):
    with _jax.named_scope("forward"):
        args = {**rest, TWIN_DIFF_INPUT: diff, **{k: w.astype(_WEIGHT_DTYPES[k]) for k, w in weights.items()}}
        y = _forward(args)
    with _jax.named_scope("loss_head"):
        err = _jnp.square(y.astype(_jnp.float32) - loss_target)
        return 0.5 * _jnp.sum(_jnp.mean(err, axis=-1)) if err.ndim else 0.5 * err


def _adamw(w, g, m, v):
    m = ADAM_B1 * m + (1.0 - ADAM_B1) * g
    v = ADAM_B2 * v + (1.0 - ADAM_B2) * _jnp.square(g)
    m_hat = m / (1.0 - ADAM_B1 ** ADAM_STEP)
    v_hat = v / (1.0 - ADAM_B2 ** ADAM_STEP)
    delta = -ADAM_LR * (m_hat / (_jnp.sqrt(v_hat) + ADAM_EPS) + ADAM_WD * w)
    return delta, m, v


def reference(x, rel_bias, norm1_g, w_in, q_norm_g, k_norm_g, conv_dw_w, conv_dw_b, conv_ln_g, conv_ln_b, w_conv_out, w_attn_out, w_out, norm2_g, w_ff1, w_ff2, loss_target, m_rel_bias, m_norm1_g, m_w_in, m_q_norm_g, m_k_norm_g, m_conv_dw_w, m_conv_dw_b, m_conv_ln_g, m_conv_ln_b, m_w_conv_out, m_w_attn_out, m_w_out, m_norm2_g, m_w_ff1, m_w_ff2, v_rel_bias, v_norm1_g, v_w_in, v_q_norm_g, v_k_norm_g, v_conv_dw_w, v_conv_dw_b, v_conv_ln_g, v_conv_ln_b, v_w_conv_out, v_w_attn_out, v_w_out, v_norm2_g, v_w_ff1, v_w_ff2):
    given = dict(x=x, rel_bias=rel_bias, norm1_g=norm1_g, w_in=w_in, q_norm_g=q_norm_g, k_norm_g=k_norm_g, conv_dw_w=conv_dw_w, conv_dw_b=conv_dw_b, conv_ln_g=conv_ln_g, conv_ln_b=conv_ln_b, w_conv_out=w_conv_out, w_attn_out=w_attn_out, w_out=w_out, norm2_g=norm2_g, w_ff1=w_ff1, w_ff2=w_ff2, loss_target=loss_target, m_rel_bias=m_rel_bias, m_norm1_g=m_norm1_g, m_w_in=m_w_in, m_q_norm_g=m_q_norm_g, m_k_norm_g=m_k_norm_g, m_conv_dw_w=m_conv_dw_w, m_conv_dw_b=m_conv_dw_b, m_conv_ln_g=m_conv_ln_g, m_conv_ln_b=m_conv_ln_b, m_w_conv_out=m_w_conv_out, m_w_attn_out=m_w_attn_out, m_w_out=m_w_out, m_norm2_g=m_norm2_g, m_w_ff1=m_w_ff1, m_w_ff2=m_w_ff2, v_rel_bias=v_rel_bias, v_norm1_g=v_norm1_g, v_w_in=v_w_in, v_q_norm_g=v_q_norm_g, v_k_norm_g=v_k_norm_g, v_conv_dw_w=v_conv_dw_w, v_conv_dw_b=v_conv_dw_b, v_conv_ln_g=v_conv_ln_g, v_conv_ln_b=v_conv_ln_b, v_w_conv_out=v_w_conv_out, v_w_attn_out=v_w_attn_out, v_w_out=v_w_out, v_norm2_g=v_norm2_g, v_w_ff1=v_w_ff1, v_w_ff2=v_w_ff2)
    weights = {n: given[n] for n in TWIN_WEIGHTS}
    shared = {n: given[n] for n in SHARED_INPUTS}
    per_example = {n: given[n] for n in ['x']}
    grad_fn = _jax.value_and_grad(_loss, argnums=(0, 1))

    def one_microbatch(ex, loss_target):
        ex = dict(ex)
        diff = ex.pop(TWIN_DIFF_INPUT)
        return grad_fn(weights, diff, {**shared, **ex}, loss_target)

    if N_MICROBATCH == 1:
        loss, (grad_w, grad_x) = one_microbatch(per_example, given["loss_target"])
    else:
        def body(carry, xs):
            loss_sum, grad_sum = carry
            l_k, (gw_k, gx_k) = one_microbatch(xs[0], xs[1])
            with _jax.named_scope("update"):
                return (loss_sum + l_k, _jax.tree.map(_jnp.add, grad_sum, gw_k)), gx_k

        init = (_jnp.zeros((), _jnp.float32), _jax.tree.map(_jnp.zeros_like, weights))
        (loss, grad_w), grad_x = _jax.lax.scan(body, init, (per_example, given["loss_target"]))
    with _jax.named_scope("update"):
        delta_w, new_m, new_v = {}, {}, {}
        for n in TWIN_WEIGHTS:
            delta_w[n], new_m[n], new_v[n] = _adamw(weights[n], grad_w[n], given["m_" + n], given["v_" + n])
    return (loss, grad_x, *[grad_w[n] for n in TWIN_WEIGHTS], *[delta_w[n] for n in TWIN_WEIGHTS],
            *[new_m[n] for n in TWIN_WEIGHTS], *[new_v[n] for n in TWIN_WEIGHTS])
```

```python
import functools
import math

import numpy as np
import jax
import jax.numpy as jnp
from jax import lax
from jax.experimental import pallas as pl
from jax.experimental.pallas import tpu as pltpu

F32 = jnp.float32
BF16 = jnp.bfloat16

T = 2048
D = 1024
DEPTH = 4
CONV = 512
KW = 31
NG = 3
HD = 64
AOUT = 512
DFF = 4096
INC = 7680
DIL = (1, 4, 16)
BLK = 128
NBUCKET = 32
EPS = 1e-6
NEG = -1e30
NCHIP = 4
UB_A, UB_GT, UB_Q, UB_K, UB_V, UB_GC, UB_GA = 0, 1, 2, 5, 8, 11, 13

ADAM_LR, ADAM_B1, ADAM_B2, ADAM_EPS, ADAM_WD, ADAM_STEP = 0.001, 0.9, 0.999, 1e-08, 0.01, 10

VMEM_LIMIT = 48 * 1024 * 1024
TB = 256
HBM_SPEC = pl.BlockSpec(memory_space=pltpu.HBM)


def _pcall(body, *, name, out_shape, grid=(), in_specs=None, out_specs=None, scratch=(), aliases=None,
           semantics=None):
    kw = {}
    if in_specs is not None:
        kw["in_specs"] = in_specs
    if out_specs is not None:
        kw["out_specs"] = out_specs
    return pl.pallas_call(
        body, name=name, out_shape=out_shape, grid=grid, scratch_shapes=scratch,
        input_output_aliases=aliases or {},
        compiler_params=pltpu.CompilerParams(vmem_limit_bytes=VMEM_LIMIT, dimension_semantics=semantics),
        **kw)


def _sds(shape, dtype=F32):
    return jax.ShapeDtypeStruct(shape, dtype)


NN = (((1,), (0,)), ((), ()))
NT = (((1,), (1,)), ((), ()))
TN = (((0,), (0,)), ((), ()))


def _mm(name, a, b, *, out_shape, out_dtype, grid, a_spec, b_spec, o_spec, acc_shape, dims, add=None,
        add_spec=None):
    nk = grid[2]

    def body(*refs):
        if add is None:
            a_ref, b_ref, o_ref, acc_ref = refs
            r_ref = None
        else:
            a_ref, b_ref, r_ref, o_ref, acc_ref = refs
        k = pl.program_id(2)

        @pl.when(k == 0)
        def _():
            acc_ref[...] = jnp.zeros_like(acc_ref)

        acc_ref[...] += lax.dot_general(a_ref[...].astype(BF16), b_ref[...].astype(BF16), dims,
                                        preferred_element_type=F32)

        @pl.when(k == nk - 1)
        def _():
            res = acc_ref[...]
            if r_ref is not None:
                res = res + r_ref[...]
            o_ref[...] = res.astype(out_dtype)

    ins = [a, b] if add is None else [a, b, add]
    specs = [a_spec, b_spec] if add is None else [a_spec, b_spec, add_spec]
    return _pcall(body, name=name, out_shape=_sds(out_shape, out_dtype), grid=grid, in_specs=specs,
                  out_specs=o_spec, scratch=[pltpu.VMEM(acc_shape, F32)],
                  semantics=("parallel", "parallel", "arbitrary"))(*ins)


TM = 512


def _mm_x_wcols(name, a, w4, *, tn, out_dtype=F32):
    _, k, ns = w4.shape
    nj = ns // tn
    return _mm(name, a, w4, out_shape=(T, NCHIP * ns), out_dtype=out_dtype, grid=(T // TM, NCHIP * nj, 1),
               a_spec=pl.BlockSpec((TM, k), lambda i, j, kk: (i, 0)),
               b_spec=pl.BlockSpec((None, k, tn), lambda i, j, kk: (j // nj, 0, j % nj)),
               o_spec=pl.BlockSpec((TM, tn), lambda i, j, kk: (i, j)), acc_shape=(TM, tn), dims=NN)


def _mm_x_wrows(name, a, w4, add, *, tk, tn):
    _, ks, n = w4.shape
    nkk = ks // tk
    return _mm(name, a, w4, out_shape=(T, n), out_dtype=F32, grid=(T // TM, n // tn, NCHIP * nkk),
               a_spec=pl.BlockSpec((TM, tk), lambda i, j, kk: (i, kk)),
               b_spec=pl.BlockSpec((None, tk, tn), lambda i, j, kk: (kk // nkk, kk % nkk, j)),
               o_spec=pl.BlockSpec((TM, tn), lambda i, j, kk: (i, j)), acc_shape=(TM, tn), dims=NN,
               add=add, add_spec=pl.BlockSpec((TM, tn), lambda i, j, kk: (i, j)))


def _mm_g_wcols_t(name, g, w4, *, tk, tn, out_dtype=F32):
    _, k, ns = w4.shape
    nkk = ns // tk
    return _mm(name, g, w4, out_shape=(T, k), out_dtype=out_dtype, grid=(T // TM, k // tn, NCHIP * nkk),
               a_spec=pl.BlockSpec((TM, tk), lambda i, j, kk: (i, kk)),
               b_spec=pl.BlockSpec((None, tn, tk), lambda i, j, kk: (kk // nkk, j, kk % nkk)),
               o_spec=pl.BlockSpec((TM, tn), lambda i, j, kk: (i, j)), acc_shape=(TM, tn), dims=NT)


def _mm_g_wrows_t(name, g, w4, *, tn, out_dtype=F32):
    _, ks, n = w4.shape
    nj = ks // tn
    return _mm(name, g, w4, out_shape=(T, NCHIP * ks), out_dtype=out_dtype, grid=(T // TM, NCHIP * nj, 1),
               a_spec=pl.BlockSpec((TM, n), lambda i, j, kk: (i, 0)),
               b_spec=pl.BlockSpec((None, tn, n), lambda i, j, kk: (j // nj, j % nj, 0)),
               o_spec=pl.BlockSpec((TM, tn), lambda i, j, kk: (i, j)), acc_shape=(TM, tn), dims=NT)


TT = 512


def _mm_dw_cols(name, a, g, *, ns, tm, tn):
    k = a.shape[1]
    nj = ns // tn
    return _mm(name, a, g, out_shape=(NCHIP, k, ns), out_dtype=F32, grid=(k // tm, NCHIP * nj, T // TT),
               a_spec=pl.BlockSpec((TT, tm), lambda i, j, kk: (kk, i)),
               b_spec=pl.BlockSpec((TT, tn), lambda i, j, kk: (kk, j)),
               o_spec=pl.BlockSpec((None, tm, tn), lambda i, j, kk: (j // nj, i, j % nj)),
               acc_shape=(tm, tn), dims=TN)


def _mm_dw_rows(name, a, g, *, ks, tm, tn):
    n = g.shape[1]
    ni = ks // tm
    return _mm(name, a, g, out_shape=(NCHIP, ks, n), out_dtype=F32, grid=(NCHIP * ni, n // tn, T // TT),
               a_spec=pl.BlockSpec((TT, tm), lambda i, j, kk: (kk, i)),
               b_spec=pl.BlockSpec((TT, tn), lambda i, j, kk: (kk, j)),
               o_spec=pl.BlockSpec((None, tm, tn), lambda i, j, kk: (i // ni, i % ni, j)),
               acc_shape=(tm, tn), dims=TN)


def _row_spec(width, col=0):
    return pl.BlockSpec((TB, width), lambda i: (i, col))


def _vec_spec(width):
    return pl.BlockSpec((1, width), lambda i: (0, 0))


def _rms_fwd(x, g):
    def body(x_ref, g_ref, h_ref):
        x = x_ref[...]
        r = lax.rsqrt(jnp.mean(x * x, axis=-1, keepdims=True) + EPS)
        h_ref[...] = (x * r * g_ref[...]).astype(BF16)

    return _pcall(body, name="rms_fwd", out_shape=_sds((T, D), BF16), grid=(T // TB,),
                  in_specs=[_row_spec(D), _vec_spec(D)], out_specs=_row_spec(D), semantics=("parallel",))(x, g)


def _rms_bwd(x, g, dh, dres):
    def body(x_ref, g_ref, dh_ref, dres_ref, dx_ref, dg_ref):
        x = x_ref[...]
        r = lax.rsqrt(jnp.mean(x * x, axis=-1, keepdims=True) + EPS)
        y = x * r
        dh = dh_ref[...]
        dy = dh * g_ref[...]
        dx_ref[...] = dres_ref[...] + r * (dy - y * jnp.mean(dy * y, axis=-1, keepdims=True))

        @pl.when(pl.program_id(0) == 0)
        def _():
            dg_ref[...] = jnp.zeros_like(dg_ref)

        dg_ref[...] += jnp.sum(dh * y, axis=0, keepdims=True)

    return _pcall(body, name="rms_bwd", out_shape=(_sds((T, D)), _sds((1, D))), grid=(T // TB,),
                  in_specs=[_row_spec(D), _vec_spec(D), _row_spec(D), _row_spec(D)],
                  out_specs=(_row_spec(D), _vec_spec(D)), semantics=("arbitrary",))(x, g, dh, dres)


def _sigmoid(x):
    return 1.0 / (1.0 + jnp.exp(-x))


def _gate_fwd(u, ycv, yat):
    def body(gc_ref, ga_ref, yc_ref, ya_ref, m_ref):
        m_ref[...] = (_sigmoid(gc_ref[...]) * yc_ref[...] + _sigmoid(ga_ref[...]) * ya_ref[...]).astype(BF16)

    blk = lambda off: pl.BlockSpec((TB, 512), lambda i, j: (i, off + j))
    return _pcall(body, name="gate_fwd", out_shape=_sds((T, D), BF16), grid=(T // TB, 2),
                  in_specs=[blk(UB_GC), blk(UB_GA), blk(0), blk(0)], out_specs=blk(0),
                  semantics=("parallel", "parallel"))(u, u, ycv, yat)


def _gate_bwd(u, ycv, yat, dm):
    def body(gc_ref, ga_ref, yc_ref, ya_ref, dm_ref, dyc_ref, dya_ref, dgc_ref, dga_ref):
        dm = dm_ref[...]
        sc = _sigmoid(gc_ref[...])
        sa = _sigmoid(ga_ref[...])
        dyc_ref[...] = (dm * sc).astype(BF16)
        dya_ref[...] = (dm * sa).astype(BF16)
        dgc_ref[...] = dm * yc_ref[...] * sc * (1.0 - sc)
        dga_ref[...] = dm * ya_ref[...] * sa * (1.0 - sa)

    blk = lambda off: pl.BlockSpec((TB, 512), lambda i, j: (i, off + j))
    return _pcall(body, name="gate_bwd",
                  out_shape=(_sds((T, D), BF16), _sds((T, D), BF16), _sds((T, D)), _sds((T, D))),
                  grid=(T // TB, 2), in_specs=[blk(UB_GC), blk(UB_GA), blk(0), blk(0), blk(0)],
                  out_specs=(blk(0), blk(0), blk(0), blk(0)),
                  semantics=("parallel", "parallel"))(u, u, ycv, yat, dm)


def _relu2_fwd(f):
    def body(f_ref, r_ref):
        a = jnp.maximum(f_ref[...], 0.0)
        r_ref[...] = (a * a).astype(BF16)

    blk = pl.BlockSpec((TB, 1024), lambda i, j: (i, j))
    return _pcall(body, name="relu2_fwd", out_shape=_sds((T, DFF), BF16), grid=(T // TB, DFF // 1024),
                  in_specs=[blk], out_specs=blk, semantics=("parallel", "parallel"))(f)


def _relu2_bwd(f, dr):
    def body(f_ref, dr_ref, r_ref, df_ref):
        a = jnp.maximum(f_ref[...], 0.0)
        r_ref[...] = (a * a).astype(BF16)
        df_ref[...] = (dr_ref[...] * (2.0 * a)).astype(BF16)

    blk = pl.BlockSpec((TB, 1024), lambda i, j: (i, j))
    return _pcall(body, name="relu2_bwd", out_shape=(_sds((T, DFF), BF16), _sds((T, DFF), BF16)),
                  grid=(T // TB, DFF // 1024), in_specs=[blk, blk], out_specs=(blk, blk),
                  semantics=("parallel", "parallel"))(f, dr)


def _loss_fwd_bwd(y, target):
    def body(y_ref, t_ref, loss_ref, dy_ref):
        e = y_ref[...] - t_ref[...]
        dy_ref[...] = e * (1.0 / D)

        @pl.when(pl.program_id(0) == 0)
        def _():
            loss_ref[...] = jnp.zeros_like(loss_ref)

        loss_ref[...] += 0.5 * jnp.sum(jnp.mean(e * e, axis=-1, keepdims=True))

    return _pcall(body, name="loss", out_shape=(_sds((8, 128)), _sds((T, D))), grid=(T // TB,),
                  in_specs=[_row_spec(D), _row_spec(D)],
                  out_specs=(pl.BlockSpec((8, 128), lambda i: (0, 0)), _row_spec(D)),
                  semantics=("arbitrary",))(y, target)


PAD = 32
CCH = 256


def _conv_fwd(u, dw_w, dw_b):
    def body(a_ref, gt_ref, w_ref, b_ref, z1_ref, zp_ref):
        zp_ref[0:PAD, :] = jnp.zeros((PAD, 128), F32)
        zp_ref[PAD:PAD + T, :] = a_ref[...] * _sigmoid(gt_ref[...])
        for c in range(T // CCH):
            acc = jnp.broadcast_to(b_ref[...], (CCH, 128))
            for j in range(KW):
                acc = acc + w_ref[j:j + 1, :] * zp_ref[pl.ds(c * CCH + j + PAD - (KW - 1), CCH), :]
            z1_ref[c * CCH:(c + 1) * CCH, :] = acc

    col = lambda off: pl.BlockSpec((T, 128), lambda j: (0, off * 4 + j))
    return _pcall(body, name="conv_fwd", out_shape=_sds((T, CONV)), grid=(CONV // 128,),
                  in_specs=[col(UB_A), col(UB_GT), pl.BlockSpec((KW, 128), lambda j: (0, j)),
                            pl.BlockSpec((1, 128), lambda j: (0, j))],
                  out_specs=col(0), scratch=[pltpu.VMEM((T + PAD, 128), F32)],
                  semantics=("parallel",))(u, u, dw_w, dw_b)


def _ln_silu_fwd(z1, g, b):
    def body(z_ref, g_ref, b_ref, o_ref):
        z = z_ref[...]
        mu = jnp.mean(z, axis=-1, keepdims=True)
        zc = z - mu
        zh = zc * lax.rsqrt(jnp.mean(zc * zc, axis=-1, keepdims=True) + EPS)
        z2 = zh * g_ref[...] + b_ref[...]
        o_ref[...] = (z2 * _sigmoid(z2)).astype(BF16)

    return _pcall(body, name="ln_silu_fwd", out_shape=_sds((T, CONV), BF16), grid=(T // TB,),
                  in_specs=[_row_spec(CONV), _vec_spec(CONV), _vec_spec(CONV)], out_specs=_row_spec(CONV),
                  semantics=("parallel",))(z1, g, b)


def _ln_silu_bwd(z1, g, b, dz3):
    def body(z_ref, g_ref, b_ref, d_ref, z3_ref, dz1_ref, dg_ref, db_ref):
        z = z_ref[...]
        mu = jnp.mean(z, axis=-1, keepdims=True)
        zc = z - mu
        rs = lax.rsqrt(jnp.mean(zc * zc, axis=-1, keepdims=True) + EPS)
        zh = zc * rs
        z2 = zh * g_ref[...] + b_ref[...]
        s = _sigmoid(z2)
        z3_ref[...] = (z2 * s).astype(BF16)
        dz2 = d_ref[...] * (s * (1.0 + z2 * (1.0 - s)))
        dzh = dz2 * g_ref[...]
        dz1_ref[...] = rs * (dzh - jnp.mean(dzh, axis=-1, keepdims=True)
                             - zh * jnp.mean(dzh * zh, axis=-1, keepdims=True))

        @pl.when(pl.program_id(0) == 0)
        def _():
            dg_ref[...] = jnp.zeros_like(dg_ref)
            db_ref[...] = jnp.zeros_like(db_ref)

        dg_ref[...] += jnp.sum(dz2 * zh, axis=0, keepdims=True)
        db_ref[...] += jnp.sum(dz2, axis=0, keepdims=True)

    return _pcall(body, name="ln_silu_bwd",
                  out_shape=(_sds((T, CONV), BF16), _sds((T, CONV)), _sds((1, CONV)), _sds((1, CONV))),
                  grid=(T // TB,),
                  in_specs=[_row_spec(CONV), _vec_spec(CONV), _vec_spec(CONV), _row_spec(CONV)],
                  out_specs=(_row_spec(CONV), _row_spec(CONV), _vec_spec(CONV), _vec_spec(CONV)),
                  semantics=("arbitrary",))(z1, g, b, dz3)


def _conv_bwd(u, dw_w, dz1):
    def body(a_ref, gt_ref, w_ref, dz1_ref, da_ref, dgt_ref, dw_ref, db_ref, zp_ref, dp_ref):
        a = a_ref[...]
        s = _sigmoid(gt_ref[...])
        zp_ref[0:PAD, :] = jnp.zeros((PAD, 128), F32)
        zp_ref[PAD:PAD + T, :] = a * s
        dp_ref[0:T, :] = dz1_ref[...]
        dp_ref[T:T + PAD, :] = jnp.zeros((PAD, 128), F32)
        db_ref[...] = jnp.sum(dz1_ref[...], axis=0, keepdims=True)
        for j in range(KW):
            tot = jnp.zeros((1, 128), F32)
            for c in range(T // CCH):
                tot = tot + jnp.sum(dz1_ref[c * CCH:(c + 1) * CCH, :]
                                    * zp_ref[pl.ds(c * CCH + j + PAD - (KW - 1), CCH), :], axis=0, keepdims=True)
            dw_ref[j:j + 1, :] = tot
        for c in range(T // CCH):
            acc = jnp.zeros((CCH, 128), F32)
            for j in range(KW):
                acc = acc + w_ref[j:j + 1, :] * dp_ref[pl.ds(c * CCH + (KW - 1) - j, CCH), :]
            rows = slice(c * CCH, (c + 1) * CCH)
            sc = _sigmoid(gt_ref[rows, :])
            da_ref[rows, :] = acc * sc
            dgt_ref[rows, :] = acc * a_ref[rows, :] * sc * (1.0 - sc)

    col = lambda off: pl.BlockSpec((T, 128), lambda j: (0, off * 4 + j))
    wspec = pl.BlockSpec((KW, 128), lambda j: (0, j))
    return _pcall(body, name="conv_bwd",
                  out_shape=(_sds((T, CONV)), _sds((T, CONV)), _sds((KW, CONV)), _sds((1, CONV))),
                  grid=(CONV // 128,), in_specs=[col(UB_A), col(UB_GT), wspec, col(0)],
                  out_specs=(col(0), col(0), wspec, pl.BlockSpec((1, 128), lambda j: (0, j))),
                  scratch=[pltpu.VMEM((T + PAD, 128), F32), pltpu.VMEM((T + PAD, 128), F32)],
                  semantics=("parallel",))(u, u, dw_w, dz1)


def _bucket_tables():
    qi = np.arange(BLK)[:, None]
    kj = np.arange(2 * BLK)[None, :]
    off = np.clip(qi + BLK - kj, 0, BLK)
    out = []
    for d in DIL:
        dist = (off * d).astype(np.int32)
        nf = np.maximum(dist, 1).astype(np.float32)
        large = 16 + (np.log(nf / np.float32(16)) / np.float32(math.log(2048 / 16)) * np.float32(16)).astype(np.int32)
        large = np.minimum(large, NBUCKET - 1)
        out.append(np.where(dist < 16, dist, large))
    return np.stack(out).astype(np.int32)


def _band():
    off = lax.broadcasted_iota(jnp.int32, (BLK, 2 * BLK), 0) + BLK - lax.broadcasted_iota(jnp.int32, (BLK, 2 * BLK), 1)
    return (off >= 0) & (off <= BLK)


def _bias_table(rel_bias_t, buckets):
    def body(rb_ref, bk_ref, o_ref):
        h = pl.program_id(0)
        bk = bk_ref[...]
        acc = jnp.zeros((BLK, 2 * BLK), F32)
        for b in range(NBUCKET):
            acc = jnp.where(bk == b, rb_ref[h, b], acc)
        o_ref[...] = jnp.where(_band(), acc, NEG)

    return _pcall(body, name="bias_table", out_shape=_sds((3 * 8, BLK, 2 * BLK)), grid=(24,),
                  in_specs=[pl.BlockSpec(memory_space=pltpu.SMEM),
                            pl.BlockSpec((None, BLK, 2 * BLK), lambda h: (h // 8, 0, 0))],
                  out_specs=pl.BlockSpec((None, BLK, 2 * BLK), lambda h: (h, 0, 0)),
                  semantics=("parallel",))(rel_bias_t, buckets)


def _bias_grad(ds_acc, buckets):
    def body(a_ref, bk_ref, o_ref):
        acc = a_ref[0]
        for l in range(1, DEPTH):
            acc = acc + a_ref[l]
        bk = bk_ref[...]
        lane = lax.broadcasted_iota(jnp.int32, (1, 128), 1)
        row = jnp.zeros((1, 128), F32)
        for b in range(NBUCKET):
            row = jnp.where(lane == b, jnp.sum(jnp.where(bk == b, acc, 0.0)), row)
        o_ref[...] = row

    return _pcall(body, name="bias_grad", out_shape=_sds((24, 1, 128)), grid=(24,),
                  in_specs=[pl.BlockSpec((DEPTH, None, BLK, 2 * BLK), lambda h: (0, h, 0, 0)),
                            pl.BlockSpec((None, BLK, 2 * BLK), lambda h: (h // 8, 0, 0))],
                  out_specs=pl.BlockSpec((None, 1, 128), lambda h: (h, 0, 0)),
                  semantics=("parallel",))(ds_acc, buckets)


def _head_mask():
    return lax.broadcasted_iota(jnp.int32, (1, 128), 1) < HD


def _seg_ones(width):
    r = lax.broadcasted_iota(jnp.int32, (width, width), 0) >> 6
    c = lax.broadcasted_iota(jnp.int32, (width, width), 1) >> 6
    return (r == c).astype(BF16)


def _seg_sum(x, ones):
    hi = x.astype(BF16)
    lo = (x - hi.astype(F32)).astype(BF16)
    return (jnp.dot(hi, ones, preferred_element_type=F32) + jnp.dot(lo, ones, preferred_element_type=F32))


def _dot(a, b, dims):
    return lax.dot_general(a, b, dims, preferred_element_type=F32)


def _tile_rows(d, r, n):
    stride = None if d == 1 else d
    q_rows = pl.ds(r + d * n * BLK, BLK, stride=stride)
    if n == 0:
        return q_rows, q_rows, BLK
    return q_rows, pl.ds(r + d * (n - 1) * BLK, 2 * BLK, stride=stride), 2 * BLK


NCH = 256


def _qk_norm_prep(q_ref, k_ref, gq_ref, gk_ref, qn_ref, kn_ref, ones):
    def prep(i, carry):
        rows = pl.ds(pl.multiple_of(i * NCH, NCH), NCH)
        q = q_ref[rows, :]
        qn_ref[rows, :] = q * lax.rsqrt(_seg_sum(q * q, ones) * (1.0 / HD) + EPS) * gq_ref[...] * (HD ** -0.5)
        k = k_ref[rows, :]
        kn_ref[rows, :] = k * lax.rsqrt(_seg_sum(k * k, ones) * (1.0 / HD) + EPS) * gk_ref[...]
        return carry

    lax.fori_loop(0, T // NCH, prep, 0)


def _attn_specs(g):
    ucol = lambda base: pl.BlockSpec((T, 128), lambda hp: (0, (base + g) * 4 + hp))
    col = pl.BlockSpec((T, 128), lambda hp: (0, hp))
    vec = pl.BlockSpec((1, 128), lambda hp: (0, 0))
    bm = pl.BlockSpec((2, BLK, 2 * BLK), lambda hp: (g * 4 + hp, 0, 0))
    return ucol, col, vec, bm


def _attn_fwd(g, u, gq, gk, bm):
    d = DIL[g]

    def body(q_ref, k_ref, v_ref, gq_ref, gk_ref, bm_ref, o_ref, lse_ref, qn_ref, kn_ref):
        ones = _seg_ones(128)
        _qk_norm_prep(q_ref, k_ref, gq_ref, gk_ref, qn_ref, kn_ref, ones)
        m_a = _head_mask()
        for r in range(d):
            for n in range(T // d // BLK):
                q_rows, k_rows, nk = _tile_rows(d, r, n)
                qt = qn_ref[q_rows, :]
                kt = kn_ref[k_rows, :].astype(BF16)
                vt = v_ref[k_rows, :].astype(BF16)
                o_t = None
                for h in range(2):
                    mh = m_a if h == 0 else jnp.logical_not(m_a)
                    qh = jnp.where(mh, qt, 0.0).astype(BF16)
                    s = _dot(qh, kt, NT) + bm_ref[h, :, 2 * BLK - nk:]
                    mx = jnp.max(s, axis=1, keepdims=True)
                    p = jnp.exp(s - mx)
                    l = jnp.sum(p, axis=1, keepdims=True)
                    o_h = _dot(p.astype(BF16), vt, NN) / l
                    lse_h = jnp.broadcast_to(mx + jnp.log(l), (BLK, 128))
                    if h == 0:
                        o_t, lse_t = o_h, lse_h
                    else:
                        o_t = jnp.where(m_a, o_t, o_h)
                        lse_t = jnp.where(m_a, lse_t, lse_h)
                o_ref[q_rows, :] = o_t
                lse_ref[q_rows, :] = lse_t

    ucol, col, vec, bmspec = _attn_specs(g)
    return _pcall(body, name=f"attn_fwd_g{g}", out_shape=(_sds((T, AOUT)), _sds((T, AOUT))), grid=(4,),
                  in_specs=[ucol(UB_Q), ucol(UB_K), ucol(UB_V), vec, vec, bmspec], out_specs=(col, col),
                  scratch=[pltpu.VMEM((T, 128), F32), pltpu.VMEM((T, 128), F32)],
                  semantics=("parallel",))(u, u, u, gq, gk, bm)


def _attn_bwd(g, u, gq, gk, bm, dog, cb, lse):
    d = DIL[g]

    def body(q_ref, k_ref, v_ref, gq_ref, gk_ref, bm_ref, do_ref, cb_ref, lse_ref,
             dq_ref, dk_ref, dv_ref, dgq_ref, dgk_ref, dsa_ref, qn_ref, kn_ref):
        ones = _seg_ones(128)
        _qk_norm_prep(q_ref, k_ref, gq_ref, gk_ref, qn_ref, kn_ref, ones)
        m_a = _head_mask()
        dk_ref[...] = jnp.zeros_like(dk_ref)
        dv_ref[...] = jnp.zeros_like(dv_ref)
        dsa_ref[...] = jnp.zeros_like(dsa_ref)
        for r in range(d):
            for n in range(T // d // BLK):
                q_rows, k_rows, nk = _tile_rows(d, r, n)
                qt = qn_ref[q_rows, :]
                kt = kn_ref[k_rows, :]
                ktb = kt.astype(BF16)
                vtb = v_ref[k_rows, :].astype(BF16)
                do_t = do_ref[q_rows, :]
                c_t = cb_ref[q_rows, :]
                lse_t = lse_ref[q_rows, :]
                dq_t = jnp.zeros((BLK, 128), F32)
                dk_t = jnp.zeros((nk, 128), F32)
                dv_t = jnp.zeros((nk, 128), F32)
                for h in range(2):
                    mh = m_a if h == 0 else jnp.logical_not(m_a)
                    qh = jnp.where(mh, qt, 0.0).astype(BF16)
                    kh = jnp.where(mh, kt, 0.0).astype(BF16)
                    doh = jnp.where(mh, do_t, 0.0).astype(BF16)
                    lse_c = jnp.max(jnp.where(mh, lse_t, -3e38), axis=1, keepdims=True)
                    c_c = jnp.max(jnp.where(mh, c_t, -3e38), axis=1, keepdims=True)
                    s = _dot(qh, ktb, NT) + bm_ref[h, :, 2 * BLK - nk:]
                    p = jnp.exp(s - lse_c)
                    dp = _dot(doh, vtb, NT)
                    ds = p * (dp + c_c)
                    dsb = ds.astype(BF16)
                    dv_t = dv_t + _dot(p.astype(BF16), doh, TN)
                    dq_t = dq_t + _dot(dsb, kh, NN)
                    dk_t = dk_t + _dot(dsb, qh, TN)
                    dsa_ref[h, :, 2 * BLK - nk:] += ds
                dq_ref[q_rows, :] = dq_t
                dk_ref[k_rows, :] += dk_t
                dv_ref[k_rows, :] += dv_t

        @pl.when(pl.program_id(0) == 0)
        def _():
            dgq_ref[...] = jnp.zeros_like(dgq_ref)
            dgk_ref[...] = jnp.zeros_like(dgk_ref)

        def norm_bwd(i, carry):
            rows = pl.ds(pl.multiple_of(i * NCH, NCH), NCH)
            for x_ref, g_ref, dx_ref, dg_ref, scale in ((q_ref, gq_ref, dq_ref, dgq_ref, HD ** -0.5),
                                                       (k_ref, gk_ref, dk_ref, dgk_ref, 1.0)):
                x = x_ref[rows, :]
                rs = lax.rsqrt(_seg_sum(x * x, ones) * (1.0 / HD) + EPS)
                xh = x * rs
                dn = dx_ref[rows, :] * scale
                dxh = dn * g_ref[...]
                dx_ref[rows, :] = rs * (dxh - xh * (_seg_sum(dxh * xh, ones) * (1.0 / HD)))
                dg_ref[...] += jnp.sum(dn * xh, axis=0, keepdims=True)
            return carry

        lax.fori_loop(0, T // NCH, norm_bwd, 0)

    ucol, col, vec, bmspec = _attn_specs(g)
    return _pcall(body, name=f"attn_bwd_g{g}",
                  out_shape=(_sds((T, AOUT)), _sds((T, AOUT)), _sds((T, AOUT)), _sds((1, 128)), _sds((1, 128)),
                             _sds((8, BLK, 2 * BLK))),
                  grid=(4,),
                  in_specs=[ucol(UB_Q), ucol(UB_K), ucol(UB_V), vec, vec, bmspec, col, col, col],
                  out_specs=(col, col, col, vec, vec, pl.BlockSpec((2, BLK, 2 * BLK), lambda hp: (hp, 0, 0))),
                  scratch=[pltpu.VMEM((T, 128), F32), pltpu.VMEM((T, 128), F32)],
                  semantics=("arbitrary",))(u, u, u, gq, gk, bm, dog, cb, lse)


def _combine_fwd(ogs, lses):
    def body(o0, o1, o2, l0, l1, l2, o_ref):
        ls = [l0[...], l1[...], l2[...]]
        mx = jnp.maximum(jnp.maximum(ls[0], ls[1]), ls[2])
        es = [jnp.exp(l - mx) for l in ls]
        inv = 1.0 / (es[0] + es[1] + es[2])
        o_ref[...] = ((es[0] * o0[...] + es[1] * o1[...] + es[2] * o2[...]) * inv).astype(BF16)

    return _pcall(body, name="combine_fwd", out_shape=_sds((T, AOUT), BF16), grid=(T // TB,),
                  in_specs=[_row_spec(AOUT)] * 6, out_specs=_row_spec(AOUT), semantics=("parallel",))(*ogs, *lses)


def _combine_bwd(ogs, lses, do):
    def body(o0, o1, o2, l0, l1, l2, do_ref, d0, d1, d2, c0, c1, c2):
        ls = [l0[...], l1[...], l2[...]]
        mx = jnp.maximum(jnp.maximum(ls[0], ls[1]), ls[2])
        es = [jnp.exp(l - mx) for l in ls]
        inv = 1.0 / (es[0] + es[1] + es[2])
        ws = [e * inv for e in es]
        do = do_ref[...]
        o = ws[0] * o0[...] + ws[1] * o1[...] + ws[2] * o2[...]
        s = _seg_sum(do * o, _seg_ones(AOUT))
        for w, d_ref, c_ref in zip(ws, (d0, d1, d2), (c0, c1, c2)):
            d_ref[...] = w * do
            c_ref[...] = -(w * s)

    return _pcall(body, name="combine_bwd", out_shape=tuple(_sds((T, AOUT)) for _ in range(6)), grid=(T // TB,),
                  in_specs=[_row_spec(AOUT)] * 7, out_specs=tuple(_row_spec(AOUT) for _ in range(6)),
                  semantics=("parallel",))(*ogs, *lses, do)


def _layer_fwd(x, p, bm):
    h1 = _rms_fwd(x, p["n1g"])
    u = _mm_x_wcols("mm_u", h1, p["win4"], tn=640)
    z1 = _conv_fwd(u, p["dww"], p["dwb"])
    z3 = _ln_silu_fwd(z1, p["lng"], p["lnb"])
    ycv = _mm_x_wcols("mm_ycv", z3, p["wco4"], tn=256)
    ogs, lses = [], []
    for g in range(NG):
        og, lse = _attn_fwd(g, u, p["gq"], p["gk"], bm)
        ogs.append(og)
        lses.append(lse)
    o = _combine_fwd(ogs, lses)
    yat = _mm_x_wcols("mm_yat", o, p["wao4"], tn=256)
    m = _gate_fwd(u, ycv, yat)
    xm = _mm_x_wrows("mm_xmid", m, p["wout4"], x, tk=256, tn=512)
    h2 = _rms_fwd(xm, p["n2g"])
    f = _mm_x_wcols("mm_f", h2, p["wff14"], tn=512)
    r = _relu2_fwd(f)
    xo = _mm_x_wrows("mm_xout", r, p["wff24"], xm, tk=512, tn=512)
    saved = dict(x=x, h1=h1, u=u, z1=z1, ogs=ogs, lses=lses, o=o, ycv=ycv, yat=yat, m=m, xm=xm, h2=h2, f=f)
    return xo, saved


def _layer_bwd(dx, s, p, bm):
    u = s["u"]
    dr = _mm_g_wrows_t("mm_dr", dx, p["wff24"], tn=512)
    r, df = _relu2_bwd(s["f"], dr)
    g_ff2 = _mm_dw_rows("mm_dwff2", r, dx, ks=1024, tm=512, tn=512)
    g_ff1 = _mm_dw_cols("mm_dwff1", s["h2"], df, ns=1024, tm=512, tn=512)
    dh2 = _mm_g_wcols_t("mm_dh2", df, p["wff14"], tk=512, tn=512)
    dxm, d_n2g = _rms_bwd(s["xm"], p["n2g"], dh2, dx)

    dm = _mm_g_wrows_t("mm_dm", dxm, p["wout4"], tn=256)
    g_out = _mm_dw_rows("mm_dwout", s["m"], dxm, ks=256, tm=256, tn=512)
    dyc, dya, dgc, dga = _gate_bwd(u, s["ycv"], s["yat"], dm)

    dz3 = _mm_g_wcols_t("mm_dz3", dyc, p["wco4"], tk=256, tn=512)
    z3, dz1, d_lng, d_lnb = _ln_silu_bwd(s["z1"], p["lng"], p["lnb"], dz3)
    g_co = _mm_dw_cols("mm_dwco", z3, dyc, ns=256, tm=512, tn=256)
    da, dgt, d_dww, d_dwb = _conv_bwd(u, p["dww"], dz1)

    do = _mm_g_wcols_t("mm_do", dya, p["wao4"], tk=256, tn=512)
    g_ao = _mm_dw_cols("mm_dwao", s["o"], dya, ns=256, tm=512, tn=256)
    parts = _combine_bwd(s["ogs"], s["lses"], do)
    dqs, dks, dvs, d_gq, d_gk, dsas = [], [], [], [], [], []
    for g in range(NG):
        dq, dk, dv, dgq, dgk, dsa = _attn_bwd(g, u, p["gq"], p["gk"], bm, parts[g], parts[NG + g], s["lses"][g])
        dqs.append(dq)
        dks.append(dk)
        dvs.append(dv)
        d_gq.append(dgq)
        d_gk.append(dgk)
        dsas.append(dsa)
    du = jnp.concatenate([da, dgt] + dqs + dks + dvs + [dgc, dga], axis=1)
    g_in = _mm_dw_cols("mm_dwin", s["h1"], du, ns=1920, tm=512, tn=640)
    dh1 = _mm_g_wcols_t("mm_dh1", du, p["win4"], tk=640, tn=512)
    dxi, d_n1g = _rms_bwd(s["x"], p["n1g"], dh1, dxm)

    fold = lambda parts_: sum(v[0, :HD] + v[0, HD:] for v in parts_)
    big = dict(w_in=g_in, w_conv_out=g_co, w_attn_out=g_ao, w_out=g_out, w_ff1=g_ff1, w_ff2=g_ff2)
    small = dict(norm1_g=d_n1g[0], q_norm_g=fold(d_gq), k_norm_g=fold(d_gk), conv_dw_w=d_dww, conv_dw_b=d_dwb[0],
                 conv_ln_g=d_lng[0], conv_ln_b=d_lnb[0], norm2_g=d_n2g[0])
    return dxi, big, small, jnp.concatenate(dsas, axis=0)


def _local_step(x, target, layers, rel_bias):
    buckets = jnp.asarray(_bucket_tables())
    bm = _bias_table(rel_bias.T, buckets)
    saved = []
    for p in layers:
        x, s = _layer_fwd(x, p, bm)
        saved.append(s)
    loss_blk, dx = _loss_fwd_bwd(x, target)
    bigs, smalls, dsas = [None] * DEPTH, [None] * DEPTH, [None] * DEPTH
    for l in reversed(range(DEPTH)):
        dx, bigs[l], smalls[l], dsas[l] = _layer_bwd(dx, saved[l], layers[l], bm)
    d_rel = _bias_grad(jnp.stack(dsas), buckets)[:, 0, :NBUCKET].T
    return loss_blk[0, 0], dx, bigs, smalls, d_rel


MESH = pl.DeviceIdType.MESH


def _me():
    return lax.axis_index("x"), lax.axis_index("y"), lax.axis_index("c")


def _other_chips(mx, my):
    return [(1 - mx, my), (mx, 1 - my), (1 - mx, 1 - my)]


def _rcopy(src, dst, send_sems, recv_sems, k, dev):
    return pltpu.make_async_remote_copy(src_ref=src, dst_ref=dst, send_sem=send_sems.at[k], recv_sem=recv_sems.at[k],
                                        device_id=dev, device_id_type=MESH)


def _comm_call(body, name, out_shape, n_in, n_sems):
    return pl.pallas_call(
        body, name=name, out_shape=out_shape, in_specs=[HBM_SPEC] * n_in,
        out_specs=jax.tree.map(lambda _: HBM_SPEC, out_shape),
        scratch_shapes=[pltpu.SemaphoreType.DMA((n_sems,)), pltpu.SemaphoreType.DMA((n_sems,)),
                        pltpu.SemaphoreType.DMA(())],
        compiler_params=pltpu.CompilerParams(has_side_effects=True))


def _all_gather_chips(x, name):
    def body(x_ref, o_ref, send_sems, recv_sems, local_sem):
        mx, my, mc = _me()
        local = pltpu.make_async_copy(x_ref, o_ref.at[2 * mx + my], local_sem)
        local.start()
        sends = [_rcopy(x_ref, o_ref.at[2 * mx + my], send_sems, recv_sems, k, (px, py, mc))
                 for k, (px, py) in enumerate(_other_chips(mx, my))]
        for cp in sends:
            cp.start()
        for k, (px, py) in enumerate(_other_chips(mx, my)):
            _rcopy(x_ref, o_ref.at[2 * px + py], send_sems, recv_sems, k, (px, py, mc)).wait_recv()
        for cp in sends:
            cp.wait_send()
        local.wait()

    return _comm_call(body, name, _sds((NCHIP,) + x.shape, x.dtype), 1, 3)(x)


def _swap_half(g, name):
    def body(g_ref, o_ref, send_sems, recv_sems, local_sem):
        mx, my, mc = _me()
        cp = _rcopy(g_ref.at[:, 1 - mc], o_ref, send_sems, recv_sems, 0, (mx, my, 1 - mc))
        cp.start()
        cp.wait_recv()
        cp.wait_send()

    return _comm_call(body, name, _sds((g.shape[0],) + g.shape[2:], g.dtype), 1, 1)(g)


def _all_to_all_chips(q, name):
    def body(q_ref, o_ref, send_sems, recv_sems, local_sem):
        mx, my, mc = _me()
        sends = [_rcopy(q_ref.at[2 * px + py], o_ref.at[k], send_sems, recv_sems, k, (px, py, mc))
                 for k, (px, py) in enumerate(_other_chips(mx, my))]
        for cp in sends:
            cp.start()
        for cp in sends:
            cp.wait_recv()
        for cp in sends:
            cp.wait_send()

    return _comm_call(body, name, _sds((NCHIP - 1,) + q.shape[1:], q.dtype), 1, 3)(q)


def _swap_full(r, name):
    def body(r_ref, o_ref, send_sems, recv_sems, local_sem):
        mx, my, mc = _me()
        local = pltpu.make_async_copy(r_ref, o_ref.at[mc], local_sem)
        local.start()
        cp = _rcopy(r_ref, o_ref.at[mc], send_sems, recv_sems, 0, (mx, my, 1 - mc))
        cp.start()
        _rcopy(r_ref, o_ref.at[1 - mc], send_sems, recv_sems, 0, (mx, my, 1 - mc)).wait_recv()
        cp.wait_send()
        local.wait()

    return _comm_call(body, name, _sds((2,) + r.shape, r.dtype), 1, 1)(r)


def _all_gather_devices(v, name):
    def body(v_ref, o_ref, send_sems, recv_sems, local_sem):
        mx, my, mc = _me()
        flip = lambda m, b: 1 - m if b else m
        peers = [(flip(mx, k >> 2 & 1), flip(my, k >> 1 & 1), flip(mc, k & 1)) for k in range(1, 8)]
        slot = lambda d: 4 * d[0] + 2 * d[1] + d[2]
        local = pltpu.make_async_copy(v_ref, o_ref.at[slot((mx, my, mc))], local_sem)
        local.start()
        sends = [_rcopy(v_ref, o_ref.at[slot((mx, my, mc))], send_sems, recv_sems, k, dev)
                 for k, dev in enumerate(peers)]
        for cp in sends:
            cp.start()
        for k, dev in enumerate(peers):
            _rcopy(v_ref, o_ref.at[slot(dev)], send_sems, recv_sems, k, dev).wait_recv()
        for cp in sends:
            cp.wait_send()
        local.wait()

    return _comm_call(body, name, _sds((8,) + v.shape, v.dtype), 1, 7)(v)


def _row_tile(rows, cols):
    t = 8
    while t * 2 * cols * 4 <= (1 << 20) and rows % (t * 2) == 0:
        t *= 2
    return t


def _prefetch_call(body, name, out_shape, grid, in_specs, out_specs):
    return pl.pallas_call(
        body, name=name, out_shape=out_shape,
        grid_spec=pltpu.PrefetchScalarGridSpec(num_scalar_prefetch=1, grid=grid, in_specs=in_specs,
                                               out_specs=out_specs),
        compiler_params=pltpu.CompilerParams(vmem_limit_bytes=VMEM_LIMIT,
                                             dimension_semantics=("parallel",) * len(grid)))


def _sum_half(g, r1, core, name):
    _, _, rr, ns = g.shape
    tr = _row_tile(rr, ns)

    def body(c_ref, g_ref, r_ref, o_ref):
        o_ref[...] = g_ref[...] + r_ref[...]

    blk = pl.BlockSpec((None, tr, ns), lambda s, i, c: (s, i, 0))
    return _prefetch_call(body, name, _sds((NCHIP, rr, ns)), (NCHIP, rr // tr),
                          [pl.BlockSpec((None, None, tr, ns), lambda s, i, c: (s, c[0], i, 0)), blk], blk)(core, g, r1)


def _sum_recv(q, r2, chip, name):
    _, rr, ns = q.shape
    tr = _row_tile(rr, ns)

    def body(c_ref, q_ref, r_ref, o_ref):
        o_ref[...] = ((q_ref[...] + r_ref[0]) + r_ref[1]) + r_ref[2]

    return _prefetch_call(body, name, _sds((rr, ns)), (rr // tr,),
                          [pl.BlockSpec((None, tr, ns), lambda i, c: (c[0], i, 0)),
                           pl.BlockSpec((NCHIP - 1, tr, ns), lambda i, c: (0, i, 0))],
                          pl.BlockSpec((tr, ns), lambda i, c: (i, 0)))(chip, q, r2)


def _sum_devices(v8):
    def body(v_ref, o_ref):
        acc = v_ref[0]
        for dev in range(1, 8):
            acc = acc + v_ref[dev]
        o_ref[...] = acc

    return _pcall(body, name="sum_devices", out_shape=_sds(v8.shape[1:]))(v8)


def _adamw(w, g, m, v, name):
    rows, cols = w.shape
    tr = _row_tile(rows, cols)

    def body(w_ref, g_ref, m_ref, v_ref, d_ref, m2_ref, v2_ref):
        g = g_ref[...]
        m2 = ADAM_B1 * m_ref[...] + (1.0 - ADAM_B1) * g
        v2 = ADAM_B2 * v_ref[...] + (1.0 - ADAM_B2) * (g * g)
        m_hat = m2 / (1.0 - ADAM_B1 ** ADAM_STEP)
        v_hat = v2 / (1.0 - ADAM_B2 ** ADAM_STEP)
        d_ref[...] = -ADAM_LR * (m_hat / (jnp.sqrt(v_hat) + ADAM_EPS) + ADAM_WD * w_ref[...])
        m2_ref[...] = m2
        v2_ref[...] = v2

    blk = pl.BlockSpec((tr, cols), lambda i: (i, 0))
    return _pcall(body, name=name, out_shape=(_sds((rows, cols)),) * 3, grid=(rows // tr,), in_specs=[blk] * 4,
                  out_specs=(blk,) * 3, semantics=("parallel",))(w, g, m, v)


BIG = ("w_in", "w_conv_out", "w_attn_out", "w_out", "w_ff1", "w_ff2")
SMALL = ("rel_bias", "norm1_g", "q_norm_g", "k_norm_g", "conv_dw_w", "conv_dw_b", "conv_ln_g", "conv_ln_b", "norm2_g")
WEIGHTS = ("rel_bias", "norm1_g", "w_in", "q_norm_g", "k_norm_g", "conv_dw_w", "conv_dw_b", "conv_ln_g", "conv_ln_b",
           "w_conv_out", "w_attn_out", "w_out", "norm2_g", "w_ff1", "w_ff2")


def _pack(arrays):
    flat = jnp.concatenate([a.reshape(-1) for a in arrays])
    n = flat.shape[0]
    rows = -(-n // 1024) * 8
    return jnp.pad(flat, (0, rows * 128 - n)).reshape(rows, 128)


def _unpack(packed, shapes):
    flat = packed.reshape(-1)
    out, off = [], 0
    for shp in shapes:
        n = int(np.prod(shp))
        out.append(flat[off:off + n].reshape(shp))
        off += n
    return out


def _reduce_scatter(g_layers, core, chip, name):
    k, n = g_layers[0].shape[1:]
    g = jnp.stack(g_layers, axis=1).reshape(NCHIP, 2, DEPTH // 2 * k, n)
    r1 = _swap_half(g, f"rs1_{name}")
    q = _sum_half(g, r1, core, f"rs1_sum_{name}")
    r2 = _all_to_all_chips(q, f"rs2_{name}")
    mine = _sum_recv(q, r2, chip, f"rs2_sum_{name}")
    return _swap_full(mine, f"rs3_{name}").reshape(DEPTH * k, n)


def kernel(x, rel_bias, norm1_g, w_in, q_norm_g, k_norm_g, conv_dw_w, conv_dw_b, conv_ln_g, conv_ln_b, w_conv_out, w_attn_out, w_out, norm2_g, w_ff1, w_ff2, loss_target, m_rel_bias, m_norm1_g, m_w_in, m_q_norm_g, m_k_norm_g, m_conv_dw_w, m_conv_dw_b, m_conv_ln_g, m_conv_ln_b, m_w_conv_out, m_w_attn_out, m_w_out, m_norm2_g, m_w_ff1, m_w_ff2, v_rel_bias, v_norm1_g, v_w_in, v_q_norm_g, v_k_norm_g, v_conv_dw_w, v_conv_dw_b, v_conv_ln_g, v_conv_ln_b, v_w_conv_out, v_w_attn_out, v_w_out, v_norm2_g, v_w_ff1, v_w_ff2):
    w = dict(rel_bias=rel_bias, norm1_g=norm1_g, w_in=w_in, q_norm_g=q_norm_g, k_norm_g=k_norm_g, conv_dw_w=conv_dw_w,
             conv_dw_b=conv_dw_b, conv_ln_g=conv_ln_g, conv_ln_b=conv_ln_b, w_conv_out=w_conv_out,
             w_attn_out=w_attn_out, w_out=w_out, norm2_g=norm2_g, w_ff1=w_ff1, w_ff2=w_ff2)
    m = dict(rel_bias=m_rel_bias, norm1_g=m_norm1_g, w_in=m_w_in, q_norm_g=m_q_norm_g, k_norm_g=m_k_norm_g,
             conv_dw_w=m_conv_dw_w, conv_dw_b=m_conv_dw_b, conv_ln_g=m_conv_ln_g, conv_ln_b=m_conv_ln_b,
             w_conv_out=m_w_conv_out, w_attn_out=m_w_attn_out, w_out=m_w_out, norm2_g=m_norm2_g, w_ff1=m_w_ff1,
             w_ff2=m_w_ff2)
    v = dict(rel_bias=v_rel_bias, norm1_g=v_norm1_g, w_in=v_w_in, q_norm_g=v_q_norm_g, k_norm_g=v_k_norm_g,
             conv_dw_w=v_conv_dw_w, conv_dw_b=v_conv_dw_b, conv_ln_g=v_conv_ln_g, conv_ln_b=v_conv_ln_b,
             w_conv_out=v_w_conv_out, w_attn_out=v_w_attn_out, w_out=v_w_out, norm2_g=v_norm2_g, w_ff1=v_w_ff1,
             w_ff2=v_w_ff2)
    core = lax.axis_index("c").astype(jnp.int32).reshape(1)
    chip_id = 2 * lax.axis_index("x") + lax.axis_index("y")
    chip = chip_id.astype(jnp.int32).reshape(1)

    full = {n: _all_gather_chips(w[n].astype(BF16), f"ag_{n}") for n in BIG}
    dww = _all_gather_chips(conv_dw_w, "ag_conv_dw_w")
    dww = dww.transpose(1, 2, 0, 3).reshape(DEPTH, KW, CONV)
    layers = []
    for l in range(DEPTH):
        layers.append(dict(
            win4=full["w_in"][:, l], wco4=full["w_conv_out"][:, l], wao4=full["w_attn_out"][:, l],
            wout4=full["w_out"][:, l], wff14=full["w_ff1"][:, l], wff24=full["w_ff2"][:, l], dww=dww[l],
            dwb=conv_dw_b[l][None], lng=conv_ln_g[l][None], lnb=conv_ln_b[l][None], n1g=norm1_g[l][None],
            n2g=norm2_g[l][None], gq=jnp.tile(q_norm_g[l], 2)[None], gk=jnp.tile(k_norm_g[l], 2)[None]))

    loss_share, dx, bigs, smalls, d_rel = _local_step(x[0], loss_target[0], layers, rel_bias)
    loss = lax.psum(loss_share, ("x", "y", "c"))

    local_small = dict(rel_bias=d_rel)
    for n in SMALL[1:]:
        local_small[n] = jnp.stack([smalls[l][n] for l in range(DEPTH)])
    small_shapes = [local_small[n].shape for n in SMALL]
    summed = _sum_devices(_all_gather_devices(_pack([local_small[n] for n in SMALL]), "ag_small"))
    grads = dict(zip(SMALL, _unpack(summed, small_shapes)))
    grads["conv_dw_w"] = lax.dynamic_slice_in_dim(grads["conv_dw_w"], chip_id * 128, 128, axis=2)

    delta, new_m, new_v = {}, {}, {}
    small_w_shapes = [w[n].shape for n in SMALL]
    outs = _adamw(_pack([w[n] for n in SMALL]), _pack([grads[n] for n in SMALL]), _pack([m[n] for n in SMALL]),
                  _pack([v[n] for n in SMALL]), "adamw_small")
    for dst, packed in zip((delta, new_m, new_v), outs):
        dst.update(zip(SMALL, _unpack(packed, small_w_shapes)))

    for n in BIG:
        g2 = _reduce_scatter([bigs[l][n] for l in range(DEPTH)], core, chip, n)
        shp = w[n].shape
        d2, m2, v2 = _adamw(w[n].reshape(g2.shape), g2, m[n].reshape(g2.shape), v[n].reshape(g2.shape), f"adamw_{n}")
        grads[n], delta[n], new_m[n], new_v[n] = g2.reshape(shp), d2.reshape(shp), m2.reshape(shp), v2.reshape(shp)

    return (loss, dx[None], *[grads[n] for n in WEIGHTS], *[delta[n] for n in WEIGHTS],
            *[new_m[n] for n in WEIGHTS], *[new_v[n] for n in WEIGHTS])
```

```python
import functools
import math

import numpy as np
import jax
import jax.numpy as jnp
from jax import lax
from jax.experimental import pallas as pl
from jax.experimental.pallas import tpu as pltpu

F32 = jnp.float32
BF16 = jnp.bfloat16

T = 2048
D = 1024
DEPTH = 4
CONV = 512
KW = 31
NG = 3
HD = 64
AOUT = 512
DFF = 4096
INC = 7680
DIL = (1, 4, 16)
BLK = 128
NBUCKET = 32
EPS = 1e-6
NEG = -1e30
NCHIP = 4
UB_A, UB_GT, UB_Q, UB_K, UB_V, UB_GC, UB_GA = 0, 1, 2, 5, 8, 11, 13

ADAM_LR, ADAM_B1, ADAM_B2, ADAM_EPS, ADAM_WD, ADAM_STEP = 0.001, 0.9, 0.999, 1e-08, 0.01, 10

VMEM_LIMIT = 48 * 1024 * 1024
TB = 256
HBM_SPEC = pl.BlockSpec(memory_space=pltpu.HBM)
ANY_SPEC = pl.BlockSpec(memory_space=pl.ANY)
SEM_SPEC = pl.BlockSpec(memory_space=pltpu.SEMAPHORE)


def _pcall(body, *, name, out_shape, grid=(), in_specs=None, out_specs=None, scratch=(), aliases=None,
           semantics=None):
    kw = {}
    if in_specs is not None:
        kw["in_specs"] = in_specs
    if out_specs is not None:
        kw["out_specs"] = out_specs
    return pl.pallas_call(
        body, name=name, out_shape=out_shape, grid=grid, scratch_shapes=scratch,
        input_output_aliases=aliases or {},
        compiler_params=pltpu.CompilerParams(vmem_limit_bytes=VMEM_LIMIT, dimension_semantics=semantics),
        **kw)


def _sds(shape, dtype=F32):
    return jax.ShapeDtypeStruct(shape, dtype)


NN = (((1,), (0,)), ((), ()))
NT = (((1,), (1,)), ((), ()))
TN = (((0,), (0,)), ((), ()))


def _mm(name, a, b, *, out_shape, out_dtype, grid, a_spec, b_spec, o_spec, acc_shape, dims, add=None,
        add_spec=None, deps=()):
    nk = grid[2]
    deps = tuple(d for d in deps if d is not None)

    def body(*refs):
        refs = refs[:len(refs) - 2 - len(deps)] + refs[len(refs) - 2:]
        if add is None:
            a_ref, b_ref, o_ref, acc_ref = refs
            r_ref = None
        else:
            a_ref, b_ref, r_ref, o_ref, acc_ref = refs
        k = pl.program_id(2)

        @pl.when(k == 0)
        def _():
            acc_ref[...] = jnp.zeros_like(acc_ref)

        acc_ref[...] += lax.dot_general(a_ref[...].astype(BF16), b_ref[...].astype(BF16), dims,
                                        preferred_element_type=F32)

        @pl.when(k == nk - 1)
        def _():
            res = acc_ref[...]
            if r_ref is not None:
                res = res + r_ref[...]
            o_ref[...] = res.astype(out_dtype)

    ins = ([a, b] if add is None else [a, b, add]) + list(deps)
    specs = ([a_spec, b_spec] if add is None else [a_spec, b_spec, add_spec]) + [ANY_SPEC] * len(deps)
    return _pcall(body, name=name, out_shape=_sds(out_shape, out_dtype), grid=grid, in_specs=specs,
                  out_specs=o_spec, scratch=[pltpu.VMEM(acc_shape, F32)],
                  semantics=("parallel", "parallel", "arbitrary"))(*ins)


TM = 512


def _mm_x_wcols(name, a, w4, *, tn, out_dtype=F32, deps=()):
    _, k, ns = w4.shape
    nj = ns // tn
    return _mm(name, a, w4, out_shape=(T, NCHIP * ns), out_dtype=out_dtype, grid=(T // TM, NCHIP * nj, 1), deps=deps,
               a_spec=pl.BlockSpec((TM, k), lambda i, j, kk: (i, 0)),
               b_spec=pl.BlockSpec((None, k, tn), lambda i, j, kk: (j // nj, 0, j % nj)),
               o_spec=pl.BlockSpec((TM, tn), lambda i, j, kk: (i, j)), acc_shape=(TM, tn), dims=NN)


def _mm_x_wrows(name, a, w4, add, *, tk, tn):
    _, ks, n = w4.shape
    nkk = ks // tk
    return _mm(name, a, w4, out_shape=(T, n), out_dtype=F32, grid=(T // TM, n // tn, NCHIP * nkk),
               a_spec=pl.BlockSpec((TM, tk), lambda i, j, kk: (i, kk)),
               b_spec=pl.BlockSpec((None, tk, tn), lambda i, j, kk: (kk // nkk, kk % nkk, j)),
               o_spec=pl.BlockSpec((TM, tn), lambda i, j, kk: (i, j)), acc_shape=(TM, tn), dims=NN,
               add=add, add_spec=pl.BlockSpec((TM, tn), lambda i, j, kk: (i, j)))


def _mm_g_wcols_t(name, g, w4, *, tk, tn, out_dtype=F32, deps=()):
    _, k, ns = w4.shape
    nkk = ns // tk
    return _mm(name, g, w4, out_shape=(T, k), out_dtype=out_dtype, grid=(T // TM, k // tn, NCHIP * nkk), deps=deps,
               a_spec=pl.BlockSpec((TM, tk), lambda i, j, kk: (i, kk)),
               b_spec=pl.BlockSpec((None, tn, tk), lambda i, j, kk: (kk // nkk, j, kk % nkk)),
               o_spec=pl.BlockSpec((TM, tn), lambda i, j, kk: (i, j)), acc_shape=(TM, tn), dims=NT)


def _mm_g_wrows_t(name, g, w4, *, tn, out_dtype=F32, deps=()):
    _, ks, n = w4.shape
    nj = ks // tn
    return _mm(name, g, w4, out_shape=(T, NCHIP * ks), out_dtype=out_dtype, grid=(T // TM, NCHIP * nj, 1), deps=deps,
               a_spec=pl.BlockSpec((TM, n), lambda i, j, kk: (i, 0)),
               b_spec=pl.BlockSpec((None, tn, n), lambda i, j, kk: (j // nj, j % nj, 0)),
               o_spec=pl.BlockSpec((TM, tn), lambda i, j, kk: (i, j)), acc_shape=(TM, tn), dims=NT)


TT = 512


def _mm_dw_cols(name, a, g, *, ns, tm, tn, deps=()):
    k = a.shape[1]
    nj = ns // tn
    return _mm(name, a, g, out_shape=(NCHIP, k, ns), out_dtype=F32, grid=(k // tm, NCHIP * nj, T // TT), deps=deps,
               a_spec=pl.BlockSpec((TT, tm), lambda i, j, kk: (kk, i)),
               b_spec=pl.BlockSpec((TT, tn), lambda i, j, kk: (kk, j)),
               o_spec=pl.BlockSpec((None, tm, tn), lambda i, j, kk: (j // nj, i, j % nj)),
               acc_shape=(tm, tn), dims=TN)


def _mm_dw_rows(name, a, g, *, ks, tm, tn):
    n = g.shape[1]
    ni = ks // tm
    return _mm(name, a, g, out_shape=(NCHIP, ks, n), out_dtype=F32, grid=(NCHIP * ni, n // tn, T // TT),
               a_spec=pl.BlockSpec((TT, tm), lambda i, j, kk: (kk, i)),
               b_spec=pl.BlockSpec((TT, tn), lambda i, j, kk: (kk, j)),
               o_spec=pl.BlockSpec((None, tm, tn), lambda i, j, kk: (i // ni, i % ni, j)),
               acc_shape=(tm, tn), dims=TN)


def _row_spec(width, col=0):
    return pl.BlockSpec((TB, width), lambda i: (i, col))


def _vec_spec(width):
    return pl.BlockSpec((1, width), lambda i: (0, 0))


def _rms_fwd(x, g):
    def body(x_ref, g_ref, h_ref):
        x = x_ref[...]
        r = lax.rsqrt(jnp.mean(x * x, axis=-1, keepdims=True) + EPS)
        h_ref[...] = (x * r * g_ref[...]).astype(BF16)

    return _pcall(body, name="rms_fwd", out_shape=_sds((T, D), BF16), grid=(T // TB,),
                  in_specs=[_row_spec(D), _vec_spec(D)], out_specs=_row_spec(D), semantics=("parallel",))(x, g)


def _rms_bwd(x, g, dh, dres):
    def body(x_ref, g_ref, dh_ref, dres_ref, dx_ref, dg_ref):
        x = x_ref[...]
        r = lax.rsqrt(jnp.mean(x * x, axis=-1, keepdims=True) + EPS)
        y = x * r
        dh = dh_ref[...]
        dy = dh * g_ref[...]
        dx_ref[...] = dres_ref[...] + r * (dy - y * jnp.mean(dy * y, axis=-1, keepdims=True))

        @pl.when(pl.program_id(0) == 0)
        def _():
            dg_ref[...] = jnp.zeros_like(dg_ref)

        dg_ref[...] += jnp.sum(dh * y, axis=0, keepdims=True)

    return _pcall(body, name="rms_bwd", out_shape=(_sds((T, D)), _sds((1, D))), grid=(T // TB,),
                  in_specs=[_row_spec(D), _vec_spec(D), _row_spec(D), _row_spec(D)],
                  out_specs=(_row_spec(D), _vec_spec(D)), semantics=("arbitrary",))(x, g, dh, dres)


def _sigmoid(x):
    return 1.0 / (1.0 + jnp.exp(-x))


def _gate_fwd(u, ycv, yat):
    def body(gc_ref, ga_ref, yc_ref, ya_ref, m_ref):
        m_ref[...] = (_sigmoid(gc_ref[...]) * yc_ref[...] + _sigmoid(ga_ref[...]) * ya_ref[...]).astype(BF16)

    blk = lambda off: pl.BlockSpec((TB, 512), lambda i, j: (i, off + j))
    return _pcall(body, name="gate_fwd", out_shape=_sds((T, D), BF16), grid=(T // TB, 2),
                  in_specs=[blk(UB_GC), blk(UB_GA), blk(0), blk(0)], out_specs=blk(0),
                  semantics=("parallel", "parallel"))(u, u, ycv, yat)


def _gate_bwd(u, ycv, yat, dm):
    def body(gc_ref, ga_ref, yc_ref, ya_ref, dm_ref, dyc_ref, dya_ref, dgc_ref, dga_ref):
        dm = dm_ref[...]
        sc = _sigmoid(gc_ref[...])
        sa = _sigmoid(ga_ref[...])
        dyc_ref[...] = (dm * sc).astype(BF16)
        dya_ref[...] = (dm * sa).astype(BF16)
        dgc_ref[...] = dm * yc_ref[...] * sc * (1.0 - sc)
        dga_ref[...] = dm * ya_ref[...] * sa * (1.0 - sa)

    blk = lambda off: pl.BlockSpec((TB, 512), lambda i, j: (i, off + j))
    return _pcall(body, name="gate_bwd",
                  out_shape=(_sds((T, D), BF16), _sds((T, D), BF16), _sds((T, D)), _sds((T, D))),
                  grid=(T // TB, 2), in_specs=[blk(UB_GC), blk(UB_GA), blk(0), blk(0), blk(0)],
                  out_specs=(blk(0), blk(0), blk(0), blk(0)),
                  semantics=("parallel", "parallel"))(u, u, ycv, yat, dm)


def _relu2_fwd(f):
    def body(f_ref, r_ref):
        a = jnp.maximum(f_ref[...], 0.0)
        r_ref[...] = (a * a).astype(BF16)

    blk = pl.BlockSpec((TB, 1024), lambda i, j: (i, j))
    return _pcall(body, name="relu2_fwd", out_shape=_sds((T, DFF), BF16), grid=(T // TB, DFF // 1024),
                  in_specs=[blk], out_specs=blk, semantics=("parallel", "parallel"))(f)


def _relu2_bwd(f, dr):
    def body(f_ref, dr_ref, r_ref, df_ref):
        a = jnp.maximum(f_ref[...], 0.0)
        r_ref[...] = (a * a).astype(BF16)
        df_ref[...] = (dr_ref[...] * (2.0 * a)).astype(BF16)

    blk = pl.BlockSpec((TB, 1024), lambda i, j: (i, j))
    return _pcall(body, name="relu2_bwd", out_shape=(_sds((T, DFF), BF16), _sds((T, DFF), BF16)),
                  grid=(T // TB, DFF // 1024), in_specs=[blk, blk], out_specs=(blk, blk),
                  semantics=("parallel", "parallel"))(f, dr)


def _loss_fwd_bwd(y, target):
    def body(y_ref, t_ref, loss_ref, dy_ref):
        e = y_ref[...] - t_ref[...]
        dy_ref[...] = e * (1.0 / D)

        @pl.when(pl.program_id(0) == 0)
        def _():
            loss_ref[...] = jnp.zeros_like(loss_ref)

        loss_ref[...] += 0.5 * jnp.sum(jnp.mean(e * e, axis=-1, keepdims=True))

    return _pcall(body, name="loss", out_shape=(_sds((8, 128)), _sds((T, D))), grid=(T // TB,),
                  in_specs=[_row_spec(D), _row_spec(D)],
                  out_specs=(pl.BlockSpec((8, 128), lambda i: (0, 0)), _row_spec(D)),
                  semantics=("arbitrary",))(y, target)


PAD = 32
CCH = 256


def _conv_fwd(u, dw_w, dw_b):
    def body(a_ref, gt_ref, w_ref, b_ref, z1_ref, zp_ref):
        zp_ref[0:PAD, :] = jnp.zeros((PAD, 128), F32)
        zp_ref[PAD:PAD + T, :] = a_ref[...] * _sigmoid(gt_ref[...])
        for c in range(T // CCH):
            acc = jnp.broadcast_to(b_ref[...], (CCH, 128))
            for j in range(KW):
                acc = acc + w_ref[j:j + 1, :] * zp_ref[pl.ds(c * CCH + j + PAD - (KW - 1), CCH), :]
            z1_ref[c * CCH:(c + 1) * CCH, :] = acc

    col = lambda off: pl.BlockSpec((T, 128), lambda j: (0, off * 4 + j))
    return _pcall(body, name="conv_fwd", out_shape=_sds((T, CONV)), grid=(CONV // 128,),
                  in_specs=[col(UB_A), col(UB_GT), pl.BlockSpec((KW, 128), lambda j: (0, j)),
                            pl.BlockSpec((1, 128), lambda j: (0, j))],
                  out_specs=col(0), scratch=[pltpu.VMEM((T + PAD, 128), F32)],
                  semantics=("parallel",))(u, u, dw_w, dw_b)


def _ln_silu_fwd(z1, g, b):
    def body(z_ref, g_ref, b_ref, o_ref):
        z = z_ref[...]
        mu = jnp.mean(z, axis=-1, keepdims=True)
        zc = z - mu
        zh = zc * lax.rsqrt(jnp.mean(zc * zc, axis=-1, keepdims=True) + EPS)
        z2 = zh * g_ref[...] + b_ref[...]
        o_ref[...] = (z2 * _sigmoid(z2)).astype(BF16)

    return _pcall(body, name="ln_silu_fwd", out_shape=_sds((T, CONV), BF16), grid=(T // TB,),
                  in_specs=[_row_spec(CONV), _vec_spec(CONV), _vec_spec(CONV)], out_specs=_row_spec(CONV),
                  semantics=("parallel",))(z1, g, b)


def _ln_silu_bwd(z1, g, b, dz3):
    def body(z_ref, g_ref, b_ref, d_ref, z3_ref, dz1_ref, dg_ref, db_ref):
        z = z_ref[...]
        mu = jnp.mean(z, axis=-1, keepdims=True)
        zc = z - mu
        rs = lax.rsqrt(jnp.mean(zc * zc, axis=-1, keepdims=True) + EPS)
        zh = zc * rs
        z2 = zh * g_ref[...] + b_ref[...]
        s = _sigmoid(z2)
        z3_ref[...] = (z2 * s).astype(BF16)
        dz2 = d_ref[...] * (s * (1.0 + z2 * (1.0 - s)))
        dzh = dz2 * g_ref[...]
        dz1_ref[...] = rs * (dzh - jnp.mean(dzh, axis=-1, keepdims=True)
                             - zh * jnp.mean(dzh * zh, axis=-1, keepdims=True))

        @pl.when(pl.program_id(0) == 0)
        def _():
            dg_ref[...] = jnp.zeros_like(dg_ref)
            db_ref[...] = jnp.zeros_like(db_ref)

        dg_ref[...] += jnp.sum(dz2 * zh, axis=0, keepdims=True)
        db_ref[...] += jnp.sum(dz2, axis=0, keepdims=True)

    return _pcall(body, name="ln_silu_bwd",
                  out_shape=(_sds((T, CONV), BF16), _sds((T, CONV)), _sds((1, CONV)), _sds((1, CONV))),
                  grid=(T // TB,),
                  in_specs=[_row_spec(CONV), _vec_spec(CONV), _vec_spec(CONV), _row_spec(CONV)],
                  out_specs=(_row_spec(CONV), _row_spec(CONV), _vec_spec(CONV), _vec_spec(CONV)),
                  semantics=("arbitrary",))(z1, g, b, dz3)


def _conv_bwd(u, dw_w, dz1):
    def body(a_ref, gt_ref, w_ref, dz1_ref, da_ref, dgt_ref, dw_ref, db_ref, zp_ref, dp_ref):
        a = a_ref[...]
        s = _sigmoid(gt_ref[...])
        zp_ref[0:PAD, :] = jnp.zeros((PAD, 128), F32)
        zp_ref[PAD:PAD + T, :] = a * s
        dp_ref[0:T, :] = dz1_ref[...]
        dp_ref[T:T + PAD, :] = jnp.zeros((PAD, 128), F32)
        db_ref[...] = jnp.sum(dz1_ref[...], axis=0, keepdims=True)
        for j in range(KW):
            tot = jnp.zeros((1, 128), F32)
            for c in range(T // CCH):
                tot = tot + jnp.sum(dz1_ref[c * CCH:(c + 1) * CCH, :]
                                    * zp_ref[pl.ds(c * CCH + j + PAD - (KW - 1), CCH), :], axis=0, keepdims=True)
            dw_ref[j:j + 1, :] = tot
        for c in range(T // CCH):
            acc = jnp.zeros((CCH, 128), F32)
            for j in range(KW):
                acc = acc + w_ref[j:j + 1, :] * dp_ref[pl.ds(c * CCH + (KW - 1) - j, CCH), :]
            rows = slice(c * CCH, (c + 1) * CCH)
            sc = _sigmoid(gt_ref[rows, :])
            da_ref[rows, :] = acc * sc
            dgt_ref[rows, :] = acc * a_ref[rows, :] * sc * (1.0 - sc)

    col = lambda off: pl.BlockSpec((T, 128), lambda j: (0, off * 4 + j))
    wspec = pl.BlockSpec((KW, 128), lambda j: (0, j))
    return _pcall(body, name="conv_bwd",
                  out_shape=(_sds((T, CONV)), _sds((T, CONV)), _sds((KW, CONV)), _sds((1, CONV))),
                  grid=(CONV // 128,), in_specs=[col(UB_A), col(UB_GT), wspec, col(0)],
                  out_specs=(col(0), col(0), wspec, pl.BlockSpec((1, 128), lambda j: (0, j))),
                  scratch=[pltpu.VMEM((T + PAD, 128), F32), pltpu.VMEM((T + PAD, 128), F32)],
                  semantics=("parallel",))(u, u, dw_w, dz1)


def _bucket_tables():
    qi = np.arange(BLK)[:, None]
    kj = np.arange(2 * BLK)[None, :]
    off = np.clip(qi + BLK - kj, 0, BLK)
    out = []
    for d in DIL:
        dist = (off * d).astype(np.int32)
        nf = np.maximum(dist, 1).astype(np.float32)
        large = 16 + (np.log(nf / np.float32(16)) / np.float32(math.log(2048 / 16)) * np.float32(16)).astype(np.int32)
        large = np.minimum(large, NBUCKET - 1)
        out.append(np.where(dist < 16, dist, large))
    return np.stack(out).astype(np.int32)


def _band():
    off = lax.broadcasted_iota(jnp.int32, (BLK, 2 * BLK), 0) + BLK - lax.broadcasted_iota(jnp.int32, (BLK, 2 * BLK), 1)
    return (off >= 0) & (off <= BLK)


def _bias_table(rel_bias_t, buckets):
    def body(rb_ref, bk_ref, o_ref):
        h = pl.program_id(0)
        bk = bk_ref[...]
        acc = jnp.zeros((BLK, 2 * BLK), F32)
        for b in range(NBUCKET):
            acc = jnp.where(bk == b, rb_ref[h, b], acc)
        o_ref[...] = jnp.where(_band(), acc, NEG)

    return _pcall(body, name="bias_table", out_shape=_sds((3 * 8, BLK, 2 * BLK)), grid=(24,),
                  in_specs=[pl.BlockSpec(memory_space=pltpu.SMEM),
                            pl.BlockSpec((None, BLK, 2 * BLK), lambda h: (h // 8, 0, 0))],
                  out_specs=pl.BlockSpec((None, BLK, 2 * BLK), lambda h: (h, 0, 0)),
                  semantics=("parallel",))(rel_bias_t, buckets)


def _bias_grad(ds_acc, buckets):
    def body(a_ref, bk_ref, o_ref):
        acc = a_ref[0]
        for l in range(1, DEPTH):
            acc = acc + a_ref[l]
        bk = bk_ref[...]
        lane = lax.broadcasted_iota(jnp.int32, (1, 128), 1)
        row = jnp.zeros((1, 128), F32)
        for b in range(NBUCKET):
            row = jnp.where(lane == b, jnp.sum(jnp.where(bk == b, acc, 0.0)), row)
        o_ref[...] = row

    return _pcall(body, name="bias_grad", out_shape=_sds((24, 1, 128)), grid=(24,),
                  in_specs=[pl.BlockSpec((DEPTH, None, BLK, 2 * BLK), lambda h: (0, h, 0, 0)),
                            pl.BlockSpec((None, BLK, 2 * BLK), lambda h: (h // 8, 0, 0))],
                  out_specs=pl.BlockSpec((None, 1, 128), lambda h: (h, 0, 0)),
                  semantics=("parallel",))(ds_acc, buckets)


def _head_mask():
    return lax.broadcasted_iota(jnp.int32, (1, 128), 1) < HD


def _seg_ones(width):
    r = lax.broadcasted_iota(jnp.int32, (width, width), 0) >> 6
    c = lax.broadcasted_iota(jnp.int32, (width, width), 1) >> 6
    return (r == c).astype(BF16)


def _seg_sum(x, ones):
    hi = x.astype(BF16)
    lo = (x - hi.astype(F32)).astype(BF16)
    return (jnp.dot(hi, ones, preferred_element_type=F32) + jnp.dot(lo, ones, preferred_element_type=F32))


def _dot(a, b, dims):
    return lax.dot_general(a, b, dims, preferred_element_type=F32)


def _tile_rows(d, r, n):
    stride = None if d == 1 else d
    q_rows = pl.ds(r + d * n * BLK, BLK, stride=stride)
    if n == 0:
        return q_rows, q_rows, BLK
    return q_rows, pl.ds(r + d * (n - 1) * BLK, 2 * BLK, stride=stride), 2 * BLK


NCH = 256


def _qk_norm_prep(q_ref, k_ref, gq_ref, gk_ref, qn_ref, kn_ref, ones):
    def prep(i, carry):
        rows = pl.ds(pl.multiple_of(i * NCH, NCH), NCH)
        q = q_ref[rows, :]
        qn_ref[rows, :] = q * lax.rsqrt(_seg_sum(q * q, ones) * (1.0 / HD) + EPS) * gq_ref[...] * (HD ** -0.5)
        k = k_ref[rows, :]
        kn_ref[rows, :] = k * lax.rsqrt(_seg_sum(k * k, ones) * (1.0 / HD) + EPS) * gk_ref[...]
        return carry

    lax.fori_loop(0, T // NCH, prep, 0)


def _attn_specs(g):
    ucol = lambda base: pl.BlockSpec((T, 128), lambda hp: (0, (base + g) * 4 + hp))
    col = pl.BlockSpec((T, 128), lambda hp: (0, hp))
    vec = pl.BlockSpec((1, 128), lambda hp: (0, 0))
    bm = pl.BlockSpec((2, BLK, 2 * BLK), lambda hp: (g * 4 + hp, 0, 0))
    return ucol, col, vec, bm


def _attn_fwd(g, u, gq, gk, bm):
    d = DIL[g]

    def body(q_ref, k_ref, v_ref, gq_ref, gk_ref, bm_ref, o_ref, lse_ref, qn_ref, kn_ref):
        ones = _seg_ones(128)
        _qk_norm_prep(q_ref, k_ref, gq_ref, gk_ref, qn_ref, kn_ref, ones)
        m_a = _head_mask()
        for r in range(d):
            for n in range(T // d // BLK):
                q_rows, k_rows, nk = _tile_rows(d, r, n)
                qt = qn_ref[q_rows, :]
                kt = kn_ref[k_rows, :].astype(BF16)
                vt = v_ref[k_rows, :].astype(BF16)
                o_t = None
                for h in range(2):
                    mh = m_a if h == 0 else jnp.logical_not(m_a)
                    qh = jnp.where(mh, qt, 0.0).astype(BF16)
                    s = _dot(qh, kt, NT) + bm_ref[h, :, 2 * BLK - nk:]
                    mx = jnp.max(s, axis=1, keepdims=True)
                    p = jnp.exp(s - mx)
                    l = jnp.sum(p, axis=1, keepdims=True)
                    o_h = _dot(p.astype(BF16), vt, NN) / l
                    lse_h = jnp.broadcast_to(mx + jnp.log(l), (BLK, 128))
                    if h == 0:
                        o_t, lse_t = o_h, lse_h
                    else:
                        o_t = jnp.where(m_a, o_t, o_h)
                        lse_t = jnp.where(m_a, lse_t, lse_h)
                o_ref[q_rows, :] = o_t
                lse_ref[q_rows, :] = lse_t

    ucol, col, vec, bmspec = _attn_specs(g)
    return _pcall(body, name=f"attn_fwd_g{g}", out_shape=(_sds((T, AOUT)), _sds((T, AOUT))), grid=(4,),
                  in_specs=[ucol(UB_Q), ucol(UB_K), ucol(UB_V), vec, vec, bmspec], out_specs=(col, col),
                  scratch=[pltpu.VMEM((T, 128), F32), pltpu.VMEM((T, 128), F32)],
                  semantics=("parallel",))(u, u, u, gq, gk, bm)


def _attn_bwd(g, u, gq, gk, bm, dog, cb, lse):
    d = DIL[g]

    def body(q_ref, k_ref, v_ref, gq_ref, gk_ref, bm_ref, do_ref, cb_ref, lse_ref,
             dq_ref, dk_ref, dv_ref, dgq_ref, dgk_ref, dsa_ref, qn_ref, kn_ref):
        ones = _seg_ones(128)
        _qk_norm_prep(q_ref, k_ref, gq_ref, gk_ref, qn_ref, kn_ref, ones)
        m_a = _head_mask()
        dk_ref[...] = jnp.zeros_like(dk_ref)
        dv_ref[...] = jnp.zeros_like(dv_ref)
        dsa_ref[...] = jnp.zeros_like(dsa_ref)
        for r in range(d):
            for n in range(T // d // BLK):
                q_rows, k_rows, nk = _tile_rows(d, r, n)
                qt = qn_ref[q_rows, :]
                kt = kn_ref[k_rows, :]
                ktb = kt.astype(BF16)
                vtb = v_ref[k_rows, :].astype(BF16)
                do_t = do_ref[q_rows, :]
                c_t = cb_ref[q_rows, :]
                lse_t = lse_ref[q_rows, :]
                dq_t = jnp.zeros((BLK, 128), F32)
                dk_t = jnp.zeros((nk, 128), F32)
                dv_t = jnp.zeros((nk, 128), F32)
                for h in range(2):
                    mh = m_a if h == 0 else jnp.logical_not(m_a)
                    qh = jnp.where(mh, qt, 0.0).astype(BF16)
                    kh = jnp.where(mh, kt, 0.0).astype(BF16)
                    doh = jnp.where(mh, do_t, 0.0).astype(BF16)
                    lse_c = jnp.max(jnp.where(mh, lse_t, -3e38), axis=1, keepdims=True)
                    c_c = jnp.max(jnp.where(mh, c_t, -3e38), axis=1, keepdims=True)
                    s = _dot(qh, ktb, NT) + bm_ref[h, :, 2 * BLK - nk:]
                    p = jnp.exp(s - lse_c)
                    dp = _dot(doh, vtb, NT)
                    ds = p * (dp + c_c)
                    dsb = ds.astype(BF16)
                    dv_t = dv_t + _dot(p.astype(BF16), doh, TN)
                    dq_t = dq_t + _dot(dsb, kh, NN)
                    dk_t = dk_t + _dot(dsb, qh, TN)
                    dsa_ref[h, :, 2 * BLK - nk:] += ds
                dq_ref[q_rows, :] = dq_t
                dk_ref[k_rows, :] += dk_t
                dv_ref[k_rows, :] += dv_t

        @pl.when(pl.program_id(0) == 0)
        def _():
            dgq_ref[...] = jnp.zeros_like(dgq_ref)
            dgk_ref[...] = jnp.zeros_like(dgk_ref)

        def norm_bwd(i, carry):
            rows = pl.ds(pl.multiple_of(i * NCH, NCH), NCH)
            for x_ref, g_ref, dx_ref, dg_ref, scale in ((q_ref, gq_ref, dq_ref, dgq_ref, HD ** -0.5),
                                                       (k_ref, gk_ref, dk_ref, dgk_ref, 1.0)):
                x = x_ref[rows, :]
                rs = lax.rsqrt(_seg_sum(x * x, ones) * (1.0 / HD) + EPS)
                xh = x * rs
                dn = dx_ref[rows, :] * scale
                dxh = dn * g_ref[...]
                dx_ref[rows, :] = rs * (dxh - xh * (_seg_sum(dxh * xh, ones) * (1.0 / HD)))
                dg_ref[...] += jnp.sum(dn * xh, axis=0, keepdims=True)
            return carry

        lax.fori_loop(0, T // NCH, norm_bwd, 0)

    ucol, col, vec, bmspec = _attn_specs(g)
    return _pcall(body, name=f"attn_bwd_g{g}",
                  out_shape=(_sds((T, AOUT)), _sds((T, AOUT)), _sds((T, AOUT)), _sds((1, 128)), _sds((1, 128)),
                             _sds((8, BLK, 2 * BLK))),
                  grid=(4,),
                  in_specs=[ucol(UB_Q), ucol(UB_K), ucol(UB_V), vec, vec, bmspec, col, col, col],
                  out_specs=(col, col, col, vec, vec, pl.BlockSpec((2, BLK, 2 * BLK), lambda hp: (hp, 0, 0))),
                  scratch=[pltpu.VMEM((T, 128), F32), pltpu.VMEM((T, 128), F32)],
                  semantics=("arbitrary",))(u, u, u, gq, gk, bm, dog, cb, lse)


def _combine_fwd(ogs, lses):
    def body(o0, o1, o2, l0, l1, l2, o_ref):
        ls = [l0[...], l1[...], l2[...]]
        mx = jnp.maximum(jnp.maximum(ls[0], ls[1]), ls[2])
        es = [jnp.exp(l - mx) for l in ls]
        inv = 1.0 / (es[0] + es[1] + es[2])
        o_ref[...] = ((es[0] * o0[...] + es[1] * o1[...] + es[2] * o2[...]) * inv).astype(BF16)

    return _pcall(body, name="combine_fwd", out_shape=_sds((T, AOUT), BF16), grid=(T // TB,),
                  in_specs=[_row_spec(AOUT)] * 6, out_specs=_row_spec(AOUT), semantics=("parallel",))(*ogs, *lses)


def _combine_bwd(ogs, lses, do):
    def body(o0, o1, o2, l0, l1, l2, do_ref, d0, d1, d2, c0, c1, c2):
        ls = [l0[...], l1[...], l2[...]]
        mx = jnp.maximum(jnp.maximum(ls[0], ls[1]), ls[2])
        es = [jnp.exp(l - mx) for l in ls]
        inv = 1.0 / (es[0] + es[1] + es[2])
        ws = [e * inv for e in es]
        do = do_ref[...]
        o = ws[0] * o0[...] + ws[1] * o1[...] + ws[2] * o2[...]
        s = _seg_sum(do * o, _seg_ones(AOUT))
        for w, d_ref, c_ref in zip(ws, (d0, d1, d2), (c0, c1, c2)):
            d_ref[...] = w * do
            c_ref[...] = -(w * s)

    return _pcall(body, name="combine_bwd", out_shape=tuple(_sds((T, AOUT)) for _ in range(6)), grid=(T // TB,),
                  in_specs=[_row_spec(AOUT)] * 7, out_specs=tuple(_row_spec(AOUT) for _ in range(6)),
                  semantics=("parallel",))(*ogs, *lses, do)


def _layer_fwd(x, p, bm, deps=()):
    h1 = _rms_fwd(x, p["n1g"])
    u = _mm_x_wcols("mm_u", h1, p["win4"], tn=640, deps=deps)
    z1 = _conv_fwd(u, p["dww"], p["dwb"])
    z3 = _ln_silu_fwd(z1, p["lng"], p["lnb"])
    ycv = _mm_x_wcols("mm_ycv", z3, p["wco4"], tn=256)
    ogs, lses = [], []
    for g in range(NG):
        og, lse = _attn_fwd(g, u, p["gq"], p["gk"], bm)
        ogs.append(og)
        lses.append(lse)
    o = _combine_fwd(ogs, lses)
    yat = _mm_x_wcols("mm_yat", o, p["wao4"], tn=256)
    m = _gate_fwd(u, ycv, yat)
    xm = _mm_x_wrows("mm_xmid", m, p["wout4"], x, tk=256, tn=512)
    h2 = _rms_fwd(xm, p["n2g"])
    f = _mm_x_wcols("mm_f", h2, p["wff14"], tn=512)
    r = _relu2_fwd(f)
    xo = _mm_x_wrows("mm_xout", r, p["wff24"], xm, tk=512, tn=512)
    saved = dict(x=x, h1=h1, u=u, z1=z1, ogs=ogs, lses=lses, o=o, ycv=ycv, yat=yat, m=m, xm=xm, h2=h2, f=f)
    return xo, saved


def _layer_bwd(dx, s, p, bm, pipe=None):
    u = s["u"]
    tok = pipe.step0() if pipe else None
    dr = _mm_g_wrows_t("mm_dr", dx, p["wff24"], tn=512, deps=(tok,))
    r, df = _relu2_bwd(s["f"], dr)
    g_ff2 = _mm_dw_rows("mm_dwff2", r, dx, ks=1024, tm=512, tn=512)
    g_ff1 = _mm_dw_cols("mm_dwff1", s["h2"], df, ns=1024, tm=512, tn=512)
    tok = pipe.step1(g_ff1) if pipe else None
    dh2 = _mm_g_wcols_t("mm_dh2", df, p["wff14"], tk=512, tn=512, deps=(tok,))
    dxm, d_n2g = _rms_bwd(s["xm"], p["n2g"], dh2, dx)

    dm = _mm_g_wrows_t("mm_dm", dxm, p["wout4"], tn=256)
    g_out = _mm_dw_rows("mm_dwout", s["m"], dxm, ks=256, tm=256, tn=512)
    dyc, dya, dgc, dga = _gate_bwd(u, s["ycv"], s["yat"], dm)

    dz3 = _mm_g_wcols_t("mm_dz3", dyc, p["wco4"], tk=256, tn=512)
    z3, dz1, d_lng, d_lnb = _ln_silu_bwd(s["z1"], p["lng"], p["lnb"], dz3)
    g_co = _mm_dw_cols("mm_dwco", z3, dyc, ns=256, tm=512, tn=256)
    da, dgt, d_dww, d_dwb = _conv_bwd(u, p["dww"], dz1)

    do = _mm_g_wcols_t("mm_do", dya, p["wao4"], tk=256, tn=512)
    g_ao = _mm_dw_cols("mm_dwao", s["o"], dya, ns=256, tm=512, tn=256)
    parts = _combine_bwd(s["ogs"], s["lses"], do)
    dqs, dks, dvs, d_gq, d_gk, dsas = [], [], [], [], [], []
    for g in range(NG):
        dq, dk, dv, dgq, dgk, dsa = _attn_bwd(g, u, p["gq"], p["gk"], bm, parts[g], parts[NG + g], s["lses"][g])
        dqs.append(dq)
        dks.append(dk)
        dvs.append(dv)
        d_gq.append(dgq)
        d_gk.append(dgk)
        dsas.append(dsa)
    du = jnp.concatenate([da, dgt] + dqs + dks + dvs + [dgc, dga], axis=1)
    tok = pipe.step2(du) if pipe else None
    g_in = _mm_dw_cols("mm_dwin", s["h1"], du, ns=1920, tm=512, tn=640, deps=(tok,))
    dh1 = _mm_g_wcols_t("mm_dh1", du, p["win4"], tk=640, tn=512)
    dxi, d_n1g = _rms_bwd(s["x"], p["n1g"], dh1, dxm)
    if pipe:
        pipe.step3(dxi)

    fold = lambda parts_: sum(v[0, :HD] + v[0, HD:] for v in parts_)
    big = dict(w_in=g_in, w_conv_out=g_co, w_attn_out=g_ao, w_out=g_out, w_ff1=g_ff1, w_ff2=g_ff2)
    small = dict(norm1_g=d_n1g[0], q_norm_g=fold(d_gq), k_norm_g=fold(d_gk), conv_dw_w=d_dww, conv_dw_b=d_dwb[0],
                 conv_ln_g=d_lng[0], conv_ln_b=d_lnb[0], norm2_g=d_n2g[0])
    return dxi, big, small, jnp.concatenate(dsas, axis=0)


def _local_step(x, target, get_layer, rel_bias, make_pipe, first_deps=()):
    buckets = jnp.asarray(_bucket_tables())
    bm = _bias_table(rel_bias.T, buckets)
    saved, layers = [], []
    for l in range(DEPTH):
        layers.append(get_layer(l, x))
        x, s = _layer_fwd(x, layers[l], bm, deps=first_deps if l == 0 else ())
        saved.append(s)
    loss_blk, dx = _loss_fwd_bwd(x, target)
    smalls, dsas = [None] * DEPTH, [None] * DEPTH
    pipe = None
    for l in reversed(range(DEPTH)):
        dx, big, smalls[l], dsas[l] = _layer_bwd(dx, saved[l], layers[l], bm, pipe)
        pipe = make_pipe(l, big)
    pipe.finish()
    d_rel = _bias_grad(jnp.stack(dsas), buckets)[:, 0, :NBUCKET].T
    return loss_blk[0, 0], dx, smalls, d_rel


MESH = pl.DeviceIdType.MESH


def _me():
    return lax.axis_index("x"), lax.axis_index("y"), lax.axis_index("c")


def _other_chips(mx, my):
    return [(1 - mx, my), (mx, 1 - my), (1 - mx, 1 - my)]


def _rcopy(src, dst, send_sems, recv_sems, k, dev):
    return pltpu.make_async_remote_copy(src_ref=src, dst_ref=dst, send_sem=send_sems.at[k], recv_sem=recv_sems.at[k],
                                        device_id=dev, device_id_type=MESH)


def _comm_call(body, name, out_shape, n_in, n_sems):
    return pl.pallas_call(
        body, name=name, out_shape=out_shape, in_specs=[HBM_SPEC] * n_in,
        out_specs=jax.tree.map(lambda _: HBM_SPEC, out_shape),
        scratch_shapes=[pltpu.SemaphoreType.DMA((n_sems,)), pltpu.SemaphoreType.DMA((n_sems,)),
                        pltpu.SemaphoreType.DMA(())],
        compiler_params=pltpu.CompilerParams(has_side_effects=True))


def _all_gather_chips(x, name):
    def body(x_ref, o_ref, send_sems, recv_sems, local_sem):
        mx, my, mc = _me()
        local = pltpu.make_async_copy(x_ref, o_ref.at[2 * mx + my], local_sem)
        local.start()
        sends = [_rcopy(x_ref, o_ref.at[2 * mx + my], send_sems, recv_sems, k, (px, py, mc))
                 for k, (px, py) in enumerate(_other_chips(mx, my))]
        for cp in sends:
            cp.start()
        for k, (px, py) in enumerate(_other_chips(mx, my)):
            _rcopy(x_ref, o_ref.at[2 * px + py], send_sems, recv_sems, k, (px, py, mc)).wait_recv()
        for cp in sends:
            cp.wait_send()
        local.wait()

    return _comm_call(body, name, _sds((NCHIP,) + x.shape, x.dtype), 1, 3)(x)


EFFECT = pltpu.SideEffectType.DATAFLOW_SIDE_EFFECTING


def _hbm(a):
    return pltpu.with_memory_space_constraint(a, pltpu.HBM)


def _split_start(name, bufs, plan, n, after=None):
    nb = len(bufs)
    extra = [] if after is None else [after]
    ne = len(extra)

    def body(*refs):
        send_sems, recv_sems, token = refs[nb + ne], refs[nb + ne + 1], refs[-1]
        mx, my, mc = _me()
        for k, (src, dst, dev, _) in enumerate(plan(refs[:nb], mx, my, mc)):
            _rcopy(src, dst, send_sems, recv_sems, k, dev).start()
        token[...] = jnp.zeros_like(token)

    out = pl.pallas_call(
        body, name=name,
        out_shape=(pltpu.SemaphoreType.DMA((n,)), pltpu.SemaphoreType.DMA((n,)),
                   *[pltpu.HBM(b.shape, b.dtype) for b in bufs], _sds((8, 128))),
        in_specs=[HBM_SPEC] * nb + [ANY_SPEC] * ne,
        out_specs=(SEM_SPEC, SEM_SPEC, *[HBM_SPEC] * nb, pl.BlockSpec(memory_space=pltpu.VMEM)),
        input_output_aliases={i: 2 + i for i in range(nb)},
        compiler_params=pltpu.CompilerParams(has_side_effects=EFFECT))(*[_hbm(b) for b in bufs], *extra)
    return (out[0], out[1]), list(out[2:2 + nb]), out[-1]


def _split_wait(name, sems, bufs, plan, after):
    nb = len(bufs)

    def body(*refs):
        send_sems, recv_sems = refs[nb], refs[nb + 1]
        mx, my, mc = _me()
        for k, (src, dst, dev, land) in enumerate(plan(refs[:nb], mx, my, mc)):
            _rcopy(src, dst, send_sems, recv_sems, k, dev).wait_send()
            _rcopy(src, land, send_sems, recv_sems, k, dev).wait_recv()

    out = pl.pallas_call(
        body, name=name, out_shape=tuple(pltpu.HBM(b.shape, b.dtype) for b in bufs),
        in_specs=[HBM_SPEC] * nb + [SEM_SPEC, SEM_SPEC, ANY_SPEC], out_specs=(HBM_SPEC,) * nb,
        input_output_aliases={i: i for i in range(nb)},
        compiler_params=pltpu.CompilerParams(has_side_effects=EFFECT))(*bufs, sems[0], sems[1], after)
    return list(out)


def _plan_gather(refs, mx, my, mc):
    me = 2 * mx + my
    return [(r.at[me], r.at[me], (px, py, mc), r.at[2 * px + py]) for r in refs for px, py in _other_chips(mx, my)]


def _plan_pair_half(refs, mx, my, mc):
    n = len(refs) // 2
    return [(g.at[:, 1 - mc], r, (mx, my, 1 - mc), r) for g, r in zip(refs[:n], refs[n:])]


def _plan_scatter(refs, mx, my, mc):
    n = len(refs) // 2
    return [(q.at[2 * px + py], r.at[k], (px, py, mc), r.at[k])
            for q, r in zip(refs[:n], refs[n:]) for k, (px, py) in enumerate(_other_chips(mx, my))]


def _plan_pair_fill(refs, mx, my, mc):
    return [(r.at[mc], r.at[mc], (mx, my, 1 - mc), r.at[1 - mc]) for r in refs]


def _all_gather_devices(v, name):
    def body(v_ref, o_ref, send_sems, recv_sems, local_sem):
        mx, my, mc = _me()
        flip = lambda m, b: 1 - m if b else m
        peers = [(flip(mx, k >> 2 & 1), flip(my, k >> 1 & 1), flip(mc, k & 1)) for k in range(1, 8)]
        slot = lambda d: 4 * d[0] + 2 * d[1] + d[2]
        local = pltpu.make_async_copy(v_ref, o_ref.at[slot((mx, my, mc))], local_sem)
        local.start()
        sends = [_rcopy(v_ref, o_ref.at[slot((mx, my, mc))], send_sems, recv_sems, k, dev)
                 for k, dev in enumerate(peers)]
        for cp in sends:
            cp.start()
        for k, dev in enumerate(peers):
            _rcopy(v_ref, o_ref.at[slot(dev)], send_sems, recv_sems, k, dev).wait_recv()
        for cp in sends:
            cp.wait_send()
        local.wait()

    return _comm_call(body, name, _sds((8,) + v.shape, v.dtype), 1, 7)(v)


def _row_tile(rows, cols):
    t = 8
    while t * 2 * cols * 4 <= (1 << 20) and rows % (t * 2) == 0:
        t *= 2
    return t


def _prefetch_call(body, name, out_shape, grid, in_specs, out_specs):
    return pl.pallas_call(
        body, name=name, out_shape=out_shape,
        grid_spec=pltpu.PrefetchScalarGridSpec(num_scalar_prefetch=1, grid=grid, in_specs=in_specs,
                                               out_specs=out_specs),
        compiler_params=pltpu.CompilerParams(vmem_limit_bytes=VMEM_LIMIT,
                                             dimension_semantics=("parallel",) * len(grid)))


def _sum_half(g, r1, place, name):
    _, _, rr, ns = g.shape
    tr = _row_tile(rr, ns)

    def body(c_ref, g_ref, r_ref, o_ref, ob_ref):
        q = g_ref[...] + r_ref[...]
        o_ref[...] = q
        ob_ref[...] = q.astype(BF16)

    blk = pl.BlockSpec((None, tr, ns), lambda s, i, c: (s, i, 0))
    return _prefetch_call(body, name, (_sds((NCHIP, rr, ns)), _sds((NCHIP, rr, ns), BF16)), (NCHIP, rr // tr),
                          [pl.BlockSpec((None, None, tr, ns), lambda s, i, c: (s, c[1], i, 0)), blk],
                          (blk, blk))(place, g, r1)


def _sum_recv(q, r2, place, name):
    _, rr, ns = q.shape
    tr = _row_tile(rr, ns)

    def body(c_ref, q_ref, r_ref, o_ref):
        o_ref[...] = ((q_ref[...] + r_ref[0].astype(F32)) + r_ref[1].astype(F32)) + r_ref[2].astype(F32)

    return _prefetch_call(body, name, _sds((2, rr, ns)), (rr // tr,),
                          [pl.BlockSpec((None, tr, ns), lambda i, c: (c[0], i, 0)),
                           pl.BlockSpec((NCHIP - 1, tr, ns), lambda i, c: (0, i, 0))],
                          pl.BlockSpec((None, tr, ns), lambda i, c: (c[1], i, 0)))(place, q, r2)


def _sum_devices(v8):
    def body(v_ref, o_ref):
        acc = v_ref[0]
        for dev in range(1, 8):
            acc = acc + v_ref[dev]
        o_ref[...] = acc

    return _pcall(body, name="sum_devices", out_shape=_sds(v8.shape[1:]))(v8)


def _adamw(w, g, m, v, name):
    rows, cols = w.shape
    tr = _row_tile(rows, cols)

    def body(w_ref, g_ref, m_ref, v_ref, d_ref, m2_ref, v2_ref):
        g = g_ref[...]
        m2 = ADAM_B1 * m_ref[...] + (1.0 - ADAM_B1) * g
        v2 = ADAM_B2 * v_ref[...] + (1.0 - ADAM_B2) * (g * g)
        m_hat = m2 / (1.0 - ADAM_B1 ** ADAM_STEP)
        v_hat = v2 / (1.0 - ADAM_B2 ** ADAM_STEP)
        d_ref[...] = -ADAM_LR * (m_hat / (jnp.sqrt(v_hat) + ADAM_EPS) + ADAM_WD * w_ref[...])
        m2_ref[...] = m2
        v2_ref[...] = v2

    blk = pl.BlockSpec((tr, cols), lambda i: (i, 0))
    return _pcall(body, name=name, out_shape=(_sds((rows, cols)),) * 3, grid=(rows // tr,), in_specs=[blk] * 4,
                  out_specs=(blk,) * 3, semantics=("parallel",))(w, g, m, v)


BIG = ("w_in", "w_conv_out", "w_attn_out", "w_out", "w_ff1", "w_ff2")
SMALL = ("rel_bias", "norm1_g", "q_norm_g", "k_norm_g", "conv_dw_w", "conv_dw_b", "conv_ln_g", "conv_ln_b", "norm2_g")
WEIGHTS = ("rel_bias", "norm1_g", "w_in", "q_norm_g", "k_norm_g", "conv_dw_w", "conv_dw_b", "conv_ln_g", "conv_ln_b",
           "w_conv_out", "w_attn_out", "w_out", "norm2_g", "w_ff1", "w_ff2")


def _pack(arrays):
    flat = jnp.concatenate([a.reshape(-1) for a in arrays])
    n = flat.shape[0]
    rows = -(-n // 1024) * 8
    return jnp.pad(flat, (0, rows * 128 - n)).reshape(rows, 128)


def _unpack(packed, shapes):
    flat = packed.reshape(-1)
    out, off = [], 0
    for shp in shapes:
        n = int(np.prod(shp))
        out.append(flat[off:off + n].reshape(shp))
        off += n
    return out


def _adamw_layer(l, w, g, m, v, prev, name):
    _, k, n = w.shape
    tr = _row_tile(k, n)
    if prev is None:
        prev = tuple(lax.empty(w.shape, F32) for _ in range(4))

    def body(w_ref, g_ref, m_ref, v_ref, p0, p1, p2, p3, go_ref, d_ref, m2_ref, v2_ref):
        g = g_ref[...]
        m2 = ADAM_B1 * m_ref[...] + (1.0 - ADAM_B1) * g
        v2 = ADAM_B2 * v_ref[...] + (1.0 - ADAM_B2) * (g * g)
        m_hat = m2 / (1.0 - ADAM_B1 ** ADAM_STEP)
        v_hat = v2 / (1.0 - ADAM_B2 ** ADAM_STEP)
        go_ref[...] = g
        d_ref[...] = -ADAM_LR * (m_hat / (jnp.sqrt(v_hat) + ADAM_EPS) + ADAM_WD * w_ref[...])
        m2_ref[...] = m2
        v2_ref[...] = v2

    lay = pl.BlockSpec((None, tr, n), lambda i: (l, i, 0))
    return _pcall(body, name=name, out_shape=(_sds(w.shape),) * 4, grid=(k // tr,),
                  in_specs=[lay, pl.BlockSpec((tr, n), lambda i: (i, 0)), lay, lay] + [ANY_SPEC] * 4,
                  out_specs=(lay,) * 4, aliases={4: 0, 5: 1, 6: 2, 7: 3},
                  semantics=("parallel",))(w, g, m, v, *prev)


class _GradPipe:
    def __init__(self, l, big, place, w, m, v, results):
        self.l, self.place, self.w, self.m, self.v, self.results = l, place, w, m, v, results
        self.g = [big[n].reshape(NCHIP, 2, big[n].shape[1] // 2, big[n].shape[2]) for n in BIG]

    def step0(self):
        lands = [lax.empty((NCHIP,) + g.shape[2:], F32) for g in self.g]
        self.s1, self.b1, tok = _split_start(f"rs1_start_l{self.l}", self.g + lands, _plan_pair_half, len(BIG))
        return tok

    def step1(self, after):
        bufs = _split_wait(f"rs1_wait_l{self.l}", self.s1, self.b1, _plan_pair_half, after)
        g, r1 = bufs[:len(BIG)], bufs[len(BIG):]
        sums = [_sum_half(g[i], r1[i], self.place, f"rs1_sum_{n}") for i, n in enumerate(BIG)]
        self.q = [q for q, _ in sums]
        qb = [b for _, b in sums]
        lands = [lax.empty((NCHIP - 1,) + b.shape[1:], BF16) for b in qb]
        self.s2, self.b2, tok = _split_start(f"rs2_start_l{self.l}", qb + lands, _plan_scatter, 3 * len(BIG))
        return tok

    def step2(self, after):
        bufs = _split_wait(f"rs2_wait_l{self.l}", self.s2, self.b2, _plan_scatter, after)
        r2 = bufs[len(BIG):]
        fin = [_sum_recv(self.q[i], r2[i], self.place, f"rs2_sum_{n}") for i, n in enumerate(BIG)]
        self.s3, self.b3, tok = _split_start(f"rs3_start_l{self.l}", fin, _plan_pair_fill, len(BIG))
        return tok

    def step3(self, after):
        fin = _split_wait(f"rs3_wait_l{self.l}", self.s3, self.b3, _plan_pair_fill, after)
        for i, n in enumerate(BIG):
            g2 = fin[i].reshape(fin[i].shape[1] * 2, fin[i].shape[2])
            self.results[n] = _adamw_layer(self.l, self.w[n], g2, self.m[n], self.v[n], self.results.get(n),
                                           f"adamw_{n}_l{self.l}")

    def finish(self):
        self.step3(self.step2(self.step1(self.step0())))


def kernel(x, rel_bias, norm1_g, w_in, q_norm_g, k_norm_g, conv_dw_w, conv_dw_b, conv_ln_g, conv_ln_b, w_conv_out, w_attn_out, w_out, norm2_g, w_ff1, w_ff2, loss_target, m_rel_bias, m_norm1_g, m_w_in, m_q_norm_g, m_k_norm_g, m_conv_dw_w, m_conv_dw_b, m_conv_ln_g, m_conv_ln_b, m_w_conv_out, m_w_attn_out, m_w_out, m_norm2_g, m_w_ff1, m_w_ff2, v_rel_bias, v_norm1_g, v_w_in, v_q_norm_g, v_k_norm_g, v_conv_dw_w, v_conv_dw_b, v_conv_ln_g, v_conv_ln_b, v_w_conv_out, v_w_attn_out, v_w_out, v_norm2_g, v_w_ff1, v_w_ff2):
    w = dict(rel_bias=rel_bias, norm1_g=norm1_g, w_in=w_in, q_norm_g=q_norm_g, k_norm_g=k_norm_g, conv_dw_w=conv_dw_w,
             conv_dw_b=conv_dw_b, conv_ln_g=conv_ln_g, conv_ln_b=conv_ln_b, w_conv_out=w_conv_out,
             w_attn_out=w_attn_out, w_out=w_out, norm2_g=norm2_g, w_ff1=w_ff1, w_ff2=w_ff2)
    m = dict(rel_bias=m_rel_bias, norm1_g=m_norm1_g, w_in=m_w_in, q_norm_g=m_q_norm_g, k_norm_g=m_k_norm_g,
             conv_dw_w=m_conv_dw_w, conv_dw_b=m_conv_dw_b, conv_ln_g=m_conv_ln_g, conv_ln_b=m_conv_ln_b,
             w_conv_out=m_w_conv_out, w_attn_out=m_w_attn_out, w_out=m_w_out, norm2_g=m_norm2_g, w_ff1=m_w_ff1,
             w_ff2=m_w_ff2)
    v = dict(rel_bias=v_rel_bias, norm1_g=v_norm1_g, w_in=v_w_in, q_norm_g=v_q_norm_g, k_norm_g=v_k_norm_g,
             conv_dw_w=v_conv_dw_w, conv_dw_b=v_conv_dw_b, conv_ln_g=v_conv_ln_g, conv_ln_b=v_conv_ln_b,
             w_conv_out=v_w_conv_out, w_attn_out=v_w_attn_out, w_out=v_w_out, norm2_g=v_norm2_g, w_ff1=v_w_ff1,
             w_ff2=v_w_ff2)
    chip_id = 2 * lax.axis_index("x") + lax.axis_index("y")
    place = jnp.stack([chip_id, lax.axis_index("c")]).astype(jnp.int32)

    starts, tok = [], None
    for l in range(DEPTH):
        lands = [lax.dynamic_update_slice(lax.empty((NCHIP,) + w[n].shape[1:], BF16), w[n][l].astype(BF16)[None],
                                          (chip_id, 0, 0)) for n in BIG]
        sems, bufs, tok = _split_start(f"ag_start_l{l}", lands, _plan_gather, 3 * len(BIG), after=tok)
        starts.append((sems, bufs))
    dww = _all_gather_chips(conv_dw_w, "ag_conv_dw_w")
    dww = dww.transpose(1, 2, 0, 3).reshape(DEPTH, KW, CONV)

    def get_layer(l, after):
        sems, bufs = starts[l]
        full = dict(zip(BIG, _split_wait(f"ag_wait_l{l}", sems, bufs, _plan_gather, tok if l == 0 else after)))
        return dict(
            win4=full["w_in"], wco4=full["w_conv_out"], wao4=full["w_attn_out"], wout4=full["w_out"],
            wff14=full["w_ff1"], wff24=full["w_ff2"], dww=dww[l],
            dwb=conv_dw_b[l][None], lng=conv_ln_g[l][None], lnb=conv_ln_b[l][None], n1g=norm1_g[l][None],
            n2g=norm2_g[l][None], gq=jnp.tile(q_norm_g[l], 2)[None], gk=jnp.tile(k_norm_g[l], 2)[None])

    results = {}
    make_pipe = lambda l, big: _GradPipe(l, big, place, w, m, v, results)
    loss_share, dx, smalls, d_rel = _local_step(x[0], loss_target[0], get_layer, rel_bias, make_pipe)
    loss = lax.psum(loss_share, ("x", "y", "c"))

    local_small = dict(rel_bias=d_rel)
    for n in SMALL[1:]:
        local_small[n] = jnp.stack([smalls[l][n] for l in range(DEPTH)])
    small_shapes = [local_small[n].shape for n in SMALL]
    summed = _sum_devices(_all_gather_devices(_pack([local_small[n] for n in SMALL]), "ag_small"))
    grads = dict(zip(SMALL, _unpack(summed, small_shapes)))
    grads["conv_dw_w"] = lax.dynamic_slice_in_dim(grads["conv_dw_w"], chip_id * 128, 128, axis=2)

    delta, new_m, new_v = {}, {}, {}
    small_w_shapes = [w[n].shape for n in SMALL]
    outs = _adamw(_pack([w[n] for n in SMALL]), _pack([grads[n] for n in SMALL]), _pack([m[n] for n in SMALL]),
                  _pack([v[n] for n in SMALL]), "adamw_small")
    for dst, packed in zip((delta, new_m, new_v), outs):
        dst.update(zip(SMALL, _unpack(packed, small_w_shapes)))

    for n in BIG:
        grads[n], delta[n], new_m[n], new_v[n] = results[n]

    return (loss, dx[None], *[grads[n] for n in WEIGHTS], *[delta[n] for n in WEIGHTS],
            *[new_m[n] for n in WEIGHTS], *[new_v[n] for n in WEIGHTS])
```

```python
import functools
import math

import numpy as np
import jax
import jax.numpy as jnp
from jax import lax
from jax.experimental import pallas as pl
from jax.experimental.pallas import tpu as pltpu

F32 = jnp.float32
BF16 = jnp.bfloat16

T = 2048
D = 1024
DEPTH = 4
CONV = 512
KW = 31
NG = 3
HD = 64
AOUT = 512
DFF = 4096
INC = 7680
DIL = (1, 4, 16)
BLK = 128
NBUCKET = 32
EPS = 1e-6
NEG = -1e30
NCHIP = 4
UB_A, UB_GT, UB_Q, UB_K, UB_V, UB_GC, UB_GA = 0, 1, 2, 5, 8, 11, 13

ADAM_LR, ADAM_B1, ADAM_B2, ADAM_EPS, ADAM_WD, ADAM_STEP = 0.001, 0.9, 0.999, 1e-08, 0.01, 10

VMEM_LIMIT = 48 * 1024 * 1024
TB = 256
HBM_SPEC = pl.BlockSpec(memory_space=pltpu.HBM)
ANY_SPEC = pl.BlockSpec(memory_space=pl.ANY)
SEM_SPEC = pl.BlockSpec(memory_space=pltpu.SEMAPHORE)


def _pcall(body, *, name, out_shape, grid=(), in_specs=None, out_specs=None, scratch=(), aliases=None,
           semantics=None):
    kw = {}
    if in_specs is not None:
        kw["in_specs"] = in_specs
    if out_specs is not None:
        kw["out_specs"] = out_specs
    return pl.pallas_call(
        body, name=name, out_shape=out_shape, grid=grid, scratch_shapes=scratch,
        input_output_aliases=aliases or {},
        compiler_params=pltpu.CompilerParams(vmem_limit_bytes=VMEM_LIMIT, dimension_semantics=semantics),
        **kw)


def _sds(shape, dtype=F32):
    return jax.ShapeDtypeStruct(shape, dtype)


NN = (((1,), (0,)), ((), ()))
NT = (((1,), (1,)), ((), ()))
TN = (((0,), (0,)), ((), ()))


def _mm(name, a, b, *, out_shape, out_dtype, grid, a_spec, b_spec, o_spec, acc_shape, dims, add=None,
        add_spec=None, deps=()):
    nk = grid[2]
    deps = tuple(d for d in deps if d is not None)
    n_scratch = 1 if nk > 1 else 0

    def body(*refs):
        n_out = 1 + n_scratch
        refs = refs[:len(refs) - n_out - len(deps)] + refs[len(refs) - n_out:]
        a_ref, b_ref = refs[0], refs[1]
        r_ref = refs[2] if add is not None else None
        o_ref = refs[-n_out]
        prod = lax.dot_general(a_ref[...].astype(BF16), b_ref[...].astype(BF16), dims, preferred_element_type=F32)
        if nk == 1:
            o_ref[...] = (prod if r_ref is None else prod + r_ref[...]).astype(out_dtype)
            return
        acc_ref = refs[-1]
        k = pl.program_id(2)

        @pl.when(k == 0)
        def _():
            acc_ref[...] = prod

        @pl.when(k > 0)
        def _():
            acc_ref[...] += prod

        @pl.when(k == nk - 1)
        def _():
            res = acc_ref[...]
            if r_ref is not None:
                res = res + r_ref[...]
            o_ref[...] = res.astype(out_dtype)

    ins = ([a, b] if add is None else [a, b, add]) + list(deps)
    specs = ([a_spec, b_spec] if add is None else [a_spec, b_spec, add_spec]) + [ANY_SPEC] * len(deps)
    return _pcall(body, name=name, out_shape=_sds(out_shape, out_dtype), grid=grid, in_specs=specs,
                  out_specs=o_spec, scratch=[pltpu.VMEM(acc_shape, F32)] * n_scratch,
                  semantics=("parallel", "parallel", "arbitrary"))(*ins)


def _mm_x_wcols(name, a, w4, *, tm, tn, out_dtype=F32, deps=()):
    _, k, ns = w4.shape
    nj = ns // tn
    return _mm(name, a, w4, out_shape=(T, NCHIP * ns), out_dtype=out_dtype, grid=(T // tm, NCHIP * nj, 1), deps=deps,
               a_spec=pl.BlockSpec((tm, k), lambda i, j, kk: (i, 0)),
               b_spec=pl.BlockSpec((None, k, tn), lambda i, j, kk: (j // nj, 0, j % nj)),
               o_spec=pl.BlockSpec((tm, tn), lambda i, j, kk: (i, j)), acc_shape=(tm, tn), dims=NN)


def _mm_ff1(a, w4, *, tm, tn):
    _, k, ns = w4.shape
    nj = ns // tn

    def body(a_ref, b_ref, f_ref, r_ref):
        f = jnp.dot(a_ref[...], b_ref[...], preferred_element_type=F32)
        f_ref[...] = f
        p = jnp.maximum(f, 0.0)
        r_ref[...] = (p * p).astype(BF16)

    out = pl.BlockSpec((tm, tn), lambda i, j: (i, j))
    return _pcall(body, name="mm_f", out_shape=(_sds((T, DFF)), _sds((T, DFF), BF16)), grid=(T // tm, NCHIP * nj),
                  in_specs=[pl.BlockSpec((tm, k), lambda i, j: (i, 0)),
                            pl.BlockSpec((None, k, tn), lambda i, j: (j // nj, 0, j % nj))],
                  out_specs=(out, out), semantics=("parallel", "parallel"))(a, w4)


def _mm_x_wrows(name, a, w4, add, *, tm, tk, tn):
    _, ks, n = w4.shape
    nkk = ks // tk
    return _mm(name, a, w4, out_shape=(T, n), out_dtype=F32, grid=(T // tm, n // tn, NCHIP * nkk),
               a_spec=pl.BlockSpec((tm, tk), lambda i, j, kk: (i, kk)),
               b_spec=pl.BlockSpec((None, tk, tn), lambda i, j, kk: (kk // nkk, kk % nkk, j)),
               o_spec=pl.BlockSpec((tm, tn), lambda i, j, kk: (i, j)), acc_shape=(tm, tn), dims=NN,
               add=add, add_spec=pl.BlockSpec((tm, tn), lambda i, j, kk: (i, j)))


def _mm_g_wcols_t(name, g, w4, *, tm, tk, tn, out_dtype=F32, deps=()):
    _, k, ns = w4.shape
    nkk = ns // tk
    return _mm(name, g, w4, out_shape=(T, k), out_dtype=out_dtype, grid=(T // tm, k // tn, NCHIP * nkk), deps=deps,
               a_spec=pl.BlockSpec((tm, tk), lambda i, j, kk: (i, kk)),
               b_spec=pl.BlockSpec((None, tn, tk), lambda i, j, kk: (kk // nkk, j, kk % nkk)),
               o_spec=pl.BlockSpec((tm, tn), lambda i, j, kk: (i, j)), acc_shape=(tm, tn), dims=NT)


def _mm_g_wrows_t(name, g, w4, *, tm, tn, out_dtype=F32, deps=()):
    _, ks, n = w4.shape
    nj = ks // tn
    return _mm(name, g, w4, out_shape=(T, NCHIP * ks), out_dtype=out_dtype, grid=(T // tm, NCHIP * nj, 1), deps=deps,
               a_spec=pl.BlockSpec((tm, n), lambda i, j, kk: (i, 0)),
               b_spec=pl.BlockSpec((None, tn, n), lambda i, j, kk: (j // nj, j % nj, 0)),
               o_spec=pl.BlockSpec((tm, tn), lambda i, j, kk: (i, j)), acc_shape=(tm, tn), dims=NT)


def _mm_dff2(dx, w4, f, *, tm, tn, deps=()):
    _, ks, n = w4.shape
    nj = ks // tn
    deps = tuple(d for d in deps if d is not None)

    def body(*refs):
        dx_ref, b_ref, f_ref = refs[:3]
        r_ref, df_ref = refs[-2:]
        dr = lax.dot_general(dx_ref[...].astype(BF16), b_ref[...], NT, preferred_element_type=F32)
        p = jnp.maximum(f_ref[...], 0.0)
        r_ref[...] = (p * p).astype(BF16)
        df_ref[...] = (dr * (2.0 * p)).astype(BF16)

    out = pl.BlockSpec((tm, tn), lambda i, j: (i, j))
    return _pcall(body, name="mm_dr", out_shape=(_sds((T, DFF), BF16), _sds((T, DFF), BF16)),
                  grid=(T // tm, NCHIP * nj),
                  in_specs=[pl.BlockSpec((tm, n), lambda i, j: (i, 0)),
                            pl.BlockSpec((None, tn, n), lambda i, j: (j // nj, j % nj, 0)), out]
                  + [ANY_SPEC] * len(deps),
                  out_specs=(out, out), semantics=("parallel", "parallel"))(dx, w4, f, *deps)


TCH = 512


def _mm_dw(name, a, g, *, out_shape, out_map, tm, tn, deps=()):
    deps = tuple(d for d in deps if d is not None)

    def body(*refs):
        a_ref, g_ref = refs[:2]
        o_ref, at_ref = refs[-2:]

        @pl.when(pl.program_id(1) == 0)
        def _():
            for c in range(T // TCH):
                at_ref[:, c * TCH:(c + 1) * TCH] = a_ref[c * TCH:(c + 1) * TCH, :].T

        o_ref[...] = jnp.dot(at_ref[...], g_ref[...].astype(BF16), preferred_element_type=F32)

    return _pcall(body, name=name, out_shape=_sds(out_shape), grid=(a.shape[1] // tm, g.shape[1] // tn),
                  in_specs=[pl.BlockSpec((T, tm), lambda i, j: (0, i)), pl.BlockSpec((T, tn), lambda i, j: (0, j))]
                  + [ANY_SPEC] * len(deps),
                  out_specs=pl.BlockSpec((None, tm, tn), out_map), scratch=[pltpu.VMEM((tm, T), BF16)],
                  semantics=("parallel", "arbitrary"))(a, g, *deps)


def _mm_dw_cols(name, a, g, *, ns, tm, tn, deps=()):
    nj = ns // tn
    return _mm_dw(name, a, g, out_shape=(NCHIP, a.shape[1], ns), out_map=lambda i, j: (j // nj, i, j % nj),
                  tm=tm, tn=tn, deps=deps)


def _mm_dw_rows(name, a, g, *, ks, tm, tn):
    ni = ks // tm
    return _mm_dw(name, a, g, out_shape=(NCHIP, ks, g.shape[1]), out_map=lambda i, j: (i // ni, i % ni, j),
                  tm=tm, tn=tn)


def _row_spec(width, col=0):
    return pl.BlockSpec((TB, width), lambda i: (i, col))


def _vec_spec(width):
    return pl.BlockSpec((1, width), lambda i: (0, 0))


def _rms_fwd(x, g):
    def body(x_ref, g_ref, h_ref):
        x = x_ref[...]
        r = lax.rsqrt(jnp.mean(x * x, axis=-1, keepdims=True) + EPS)
        h_ref[...] = (x * r * g_ref[...]).astype(BF16)

    return _pcall(body, name="rms_fwd", out_shape=_sds((T, D), BF16), grid=(T // TB,),
                  in_specs=[_row_spec(D), _vec_spec(D)], out_specs=_row_spec(D), semantics=("parallel",))(x, g)


def _rms_bwd(x, g, dh, dres):
    def body(x_ref, g_ref, dh_ref, dres_ref, dx_ref, dg_ref):
        x = x_ref[...]
        r = lax.rsqrt(jnp.mean(x * x, axis=-1, keepdims=True) + EPS)
        y = x * r
        dh = dh_ref[...]
        dy = dh * g_ref[...]
        dx_ref[...] = dres_ref[...] + r * (dy - y * jnp.mean(dy * y, axis=-1, keepdims=True))

        @pl.when(pl.program_id(0) == 0)
        def _():
            dg_ref[...] = jnp.zeros_like(dg_ref)

        dg_ref[...] += jnp.sum(dh * y, axis=0, keepdims=True)

    return _pcall(body, name="rms_bwd", out_shape=(_sds((T, D)), _sds((1, D))), grid=(T // TB,),
                  in_specs=[_row_spec(D), _vec_spec(D), _row_spec(D), _row_spec(D)],
                  out_specs=(_row_spec(D), _vec_spec(D)), semantics=("arbitrary",))(x, g, dh, dres)


def _sigmoid(x):
    return 1.0 / (1.0 + jnp.exp(-x))


def _gate_fwd(u, ycv, yat):
    def body(gc_ref, ga_ref, yc_ref, ya_ref, m_ref):
        m_ref[...] = (_sigmoid(gc_ref[...]) * yc_ref[...] + _sigmoid(ga_ref[...]) * ya_ref[...]).astype(BF16)

    blk = lambda off: pl.BlockSpec((TB, 512), lambda i, j: (i, off + j))
    return _pcall(body, name="gate_fwd", out_shape=_sds((T, D), BF16), grid=(T // TB, 2),
                  in_specs=[blk(UB_GC), blk(UB_GA), blk(0), blk(0)], out_specs=blk(0),
                  semantics=("parallel", "parallel"))(u, u, ycv, yat)


def _gate_bwd(u, ycv, yat, dm):
    def body(gc_ref, ga_ref, yc_ref, ya_ref, dm_ref, dyc_ref, dya_ref, dgc_ref, dga_ref):
        dm = dm_ref[...]
        sc = _sigmoid(gc_ref[...])
        sa = _sigmoid(ga_ref[...])
        dyc_ref[...] = (dm * sc).astype(BF16)
        dya_ref[...] = (dm * sa).astype(BF16)
        dgc_ref[...] = (dm * yc_ref[...] * sc * (1.0 - sc)).astype(BF16)
        dga_ref[...] = (dm * ya_ref[...] * sa * (1.0 - sa)).astype(BF16)

    blk = lambda off: pl.BlockSpec((TB, 512), lambda i, j: (i, off + j))
    return _pcall(body, name="gate_bwd",
                  out_shape=(_sds((T, D), BF16), _sds((T, D), BF16), _sds((T, D), BF16), _sds((T, D), BF16)),
                  grid=(T // TB, 2), in_specs=[blk(UB_GC), blk(UB_GA), blk(0), blk(0), blk(0)],
                  out_specs=(blk(0), blk(0), blk(0), blk(0)),
                  semantics=("parallel", "parallel"))(u, u, ycv, yat, dm)


def _relu2_fwd(f):
    def body(f_ref, r_ref):
        a = jnp.maximum(f_ref[...], 0.0)
        r_ref[...] = (a * a).astype(BF16)

    blk = pl.BlockSpec((TB, 1024), lambda i, j: (i, j))
    return _pcall(body, name="relu2_fwd", out_shape=_sds((T, DFF), BF16), grid=(T // TB, DFF // 1024),
                  in_specs=[blk], out_specs=blk, semantics=("parallel", "parallel"))(f)


def _relu2_bwd(f, dr):
    def body(f_ref, dr_ref, r_ref, df_ref):
        a = jnp.maximum(f_ref[...], 0.0)
        r_ref[...] = (a * a).astype(BF16)
        df_ref[...] = (dr_ref[...] * (2.0 * a)).astype(BF16)

    blk = pl.BlockSpec((TB, 1024), lambda i, j: (i, j))
    return _pcall(body, name="relu2_bwd", out_shape=(_sds((T, DFF), BF16), _sds((T, DFF), BF16)),
                  grid=(T // TB, DFF // 1024), in_specs=[blk, blk], out_specs=(blk, blk),
                  semantics=("parallel", "parallel"))(f, dr)


def _loss_fwd_bwd(y, target):
    def body(y_ref, t_ref, loss_ref, dy_ref):
        e = y_ref[...] - t_ref[...]
        dy_ref[...] = e * (1.0 / D)

        @pl.when(pl.program_id(0) == 0)
        def _():
            loss_ref[...] = jnp.zeros_like(loss_ref)

        loss_ref[...] += 0.5 * jnp.sum(jnp.mean(e * e, axis=-1, keepdims=True))

    return _pcall(body, name="loss", out_shape=(_sds((8, 128)), _sds((T, D))), grid=(T // TB,),
                  in_specs=[_row_spec(D), _row_spec(D)],
                  out_specs=(pl.BlockSpec((8, 128), lambda i: (0, 0)), _row_spec(D)),
                  semantics=("arbitrary",))(y, target)


PAD = 32
CCH = 256


def _conv_fwd(u, dw_w, dw_b):
    def body(a_ref, gt_ref, w_ref, b_ref, z1_ref, zp_ref):
        zp_ref[0:PAD, :] = jnp.zeros((PAD, 128), F32)
        zp_ref[PAD:PAD + T, :] = a_ref[...] * _sigmoid(gt_ref[...])
        for c in range(T // CCH):
            acc = jnp.broadcast_to(b_ref[...], (CCH, 128))
            for j in range(KW):
                acc = acc + w_ref[j:j + 1, :] * zp_ref[pl.ds(c * CCH + j + PAD - (KW - 1), CCH), :]
            z1_ref[c * CCH:(c + 1) * CCH, :] = acc

    col = lambda off: pl.BlockSpec((T, 128), lambda j: (0, off * 4 + j))
    return _pcall(body, name="conv_fwd", out_shape=_sds((T, CONV)), grid=(CONV // 128,),
                  in_specs=[col(UB_A), col(UB_GT), pl.BlockSpec((KW, 128), lambda j: (0, j)),
                            pl.BlockSpec((1, 128), lambda j: (0, j))],
                  out_specs=col(0), scratch=[pltpu.VMEM((T + PAD, 128), F32)],
                  semantics=("parallel",))(u, u, dw_w, dw_b)


def _ln_silu_fwd(z1, g, b):
    def body(z_ref, g_ref, b_ref, o_ref):
        z = z_ref[...]
        mu = jnp.mean(z, axis=-1, keepdims=True)
        zc = z - mu
        zh = zc * lax.rsqrt(jnp.mean(zc * zc, axis=-1, keepdims=True) + EPS)
        z2 = zh * g_ref[...] + b_ref[...]
        o_ref[...] = (z2 * _sigmoid(z2)).astype(BF16)

    return _pcall(body, name="ln_silu_fwd", out_shape=_sds((T, CONV), BF16), grid=(T // TB,),
                  in_specs=[_row_spec(CONV), _vec_spec(CONV), _vec_spec(CONV)], out_specs=_row_spec(CONV),
                  semantics=("parallel",))(z1, g, b)


def _ln_silu_bwd(z1, g, b, dz3):
    def body(z_ref, g_ref, b_ref, d_ref, z3_ref, dz1_ref, dg_ref, db_ref):
        z = z_ref[...]
        mu = jnp.mean(z, axis=-1, keepdims=True)
        zc = z - mu
        rs = lax.rsqrt(jnp.mean(zc * zc, axis=-1, keepdims=True) + EPS)
        zh = zc * rs
        z2 = zh * g_ref[...] + b_ref[...]
        s = _sigmoid(z2)
        z3_ref[...] = (z2 * s).astype(BF16)
        dz2 = d_ref[...] * (s * (1.0 + z2 * (1.0 - s)))
        dzh = dz2 * g_ref[...]
        dz1_ref[...] = rs * (dzh - jnp.mean(dzh, axis=-1, keepdims=True)
                             - zh * jnp.mean(dzh * zh, axis=-1, keepdims=True))

        @pl.when(pl.program_id(0) == 0)
        def _():
            dg_ref[...] = jnp.zeros_like(dg_ref)
            db_ref[...] = jnp.zeros_like(db_ref)

        dg_ref[...] += jnp.sum(dz2 * zh, axis=0, keepdims=True)
        db_ref[...] += jnp.sum(dz2, axis=0, keepdims=True)

    return _pcall(body, name="ln_silu_bwd",
                  out_shape=(_sds((T, CONV), BF16), _sds((T, CONV)), _sds((1, CONV)), _sds((1, CONV))),
                  grid=(T // TB,),
                  in_specs=[_row_spec(CONV), _vec_spec(CONV), _vec_spec(CONV), _row_spec(CONV)],
                  out_specs=(_row_spec(CONV), _row_spec(CONV), _vec_spec(CONV), _vec_spec(CONV)),
                  semantics=("arbitrary",))(z1, g, b, dz3)


def _conv_bwd(u, dw_w, dz1):
    def body(a_ref, gt_ref, w_ref, dz1_ref, da_ref, dgt_ref, dw_ref, db_ref, zp_ref, dp_ref):
        a = a_ref[...]
        s = _sigmoid(gt_ref[...])
        zp_ref[0:PAD, :] = jnp.zeros((PAD, 128), F32)
        zp_ref[PAD:PAD + T, :] = a * s
        dp_ref[0:T, :] = dz1_ref[...]
        dp_ref[T:T + PAD, :] = jnp.zeros((PAD, 128), F32)
        db_ref[...] = jnp.sum(dz1_ref[...], axis=0, keepdims=True)
        for j in range(KW):
            tot = jnp.zeros((1, 128), F32)
            for c in range(T // CCH):
                tot = tot + jnp.sum(dz1_ref[c * CCH:(c + 1) * CCH, :]
                                    * zp_ref[pl.ds(c * CCH + j + PAD - (KW - 1), CCH), :], axis=0, keepdims=True)
            dw_ref[j:j + 1, :] = tot
        for c in range(T // CCH):
            acc = jnp.zeros((CCH, 128), F32)
            for j in range(KW):
                acc = acc + w_ref[j:j + 1, :] * dp_ref[pl.ds(c * CCH + (KW - 1) - j, CCH), :]
            rows = slice(c * CCH, (c + 1) * CCH)
            sc = _sigmoid(gt_ref[rows, :])
            da_ref[rows, :] = (acc * sc).astype(BF16)
            dgt_ref[rows, :] = (acc * a_ref[rows, :] * sc * (1.0 - sc)).astype(BF16)

    col = lambda off: pl.BlockSpec((T, 128), lambda j: (0, off * 4 + j))
    wspec = pl.BlockSpec((KW, 128), lambda j: (0, j))
    return _pcall(body, name="conv_bwd",
                  out_shape=(_sds((T, CONV), BF16), _sds((T, CONV), BF16), _sds((KW, CONV)), _sds((1, CONV))),
                  grid=(CONV // 128,), in_specs=[col(UB_A), col(UB_GT), wspec, col(0)],
                  out_specs=(col(0), col(0), wspec, pl.BlockSpec((1, 128), lambda j: (0, j))),
                  scratch=[pltpu.VMEM((T + PAD, 128), F32), pltpu.VMEM((T + PAD, 128), F32)],
                  semantics=("parallel",))(u, u, dw_w, dz1)


def _bucket_tables():
    qi = np.arange(BLK)[:, None]
    kj = np.arange(2 * BLK)[None, :]
    off = np.clip(qi + BLK - kj, 0, BLK)
    out = []
    for d in DIL:
        dist = (off * d).astype(np.int32)
        nf = np.maximum(dist, 1).astype(np.float32)
        large = 16 + (np.log(nf / np.float32(16)) / np.float32(math.log(2048 / 16)) * np.float32(16)).astype(np.int32)
        large = np.minimum(large, NBUCKET - 1)
        out.append(np.where(dist < 16, dist, large))
    return np.stack(out).astype(np.int32)


def _band():
    off = lax.broadcasted_iota(jnp.int32, (BLK, 2 * BLK), 0) + BLK - lax.broadcasted_iota(jnp.int32, (BLK, 2 * BLK), 1)
    return (off >= 0) & (off <= BLK)


def _bias_table(rel_bias_t, buckets):
    def body(rb_ref, bk_ref, o_ref):
        h = pl.program_id(0)
        bk = bk_ref[...]
        acc = jnp.zeros((BLK, 2 * BLK), F32)
        for b in range(NBUCKET):
            acc = jnp.where(bk == b, rb_ref[h, b], acc)
        o_ref[...] = jnp.where(_band(), acc, NEG)

    return _pcall(body, name="bias_table", out_shape=_sds((3 * 8, BLK, 2 * BLK)), grid=(24,),
                  in_specs=[pl.BlockSpec(memory_space=pltpu.SMEM),
                            pl.BlockSpec((None, BLK, 2 * BLK), lambda h: (h // 8, 0, 0))],
                  out_specs=pl.BlockSpec((None, BLK, 2 * BLK), lambda h: (h, 0, 0)),
                  semantics=("parallel",))(rel_bias_t, buckets)


def _bias_grad(ds_acc, buckets):
    def body(a_ref, bk_ref, o_ref):
        acc = a_ref[0]
        for l in range(1, DEPTH):
            acc = acc + a_ref[l]
        bk = bk_ref[...]
        lane = lax.broadcasted_iota(jnp.int32, (1, 128), 1)
        row = jnp.zeros((1, 128), F32)
        for b in range(NBUCKET):
            row = jnp.where(lane == b, jnp.sum(jnp.where(bk == b, acc, 0.0)), row)
        o_ref[...] = row

    return _pcall(body, name="bias_grad", out_shape=_sds((24, 1, 128)), grid=(24,),
                  in_specs=[pl.BlockSpec((DEPTH, None, BLK, 2 * BLK), lambda h: (0, h, 0, 0)),
                            pl.BlockSpec((None, BLK, 2 * BLK), lambda h: (h // 8, 0, 0))],
                  out_specs=pl.BlockSpec((None, 1, 128), lambda h: (h, 0, 0)),
                  semantics=("parallel",))(ds_acc, buckets)


def _head_mask():
    return lax.broadcasted_iota(jnp.int32, (1, 128), 1) < HD


def _seg_ones(width):
    r = lax.broadcasted_iota(jnp.int32, (width, width), 0) >> 6
    c = lax.broadcasted_iota(jnp.int32, (width, width), 1) >> 6
    return (r == c).astype(BF16)


def _seg_sum(x, ones):
    hi = x.astype(BF16)
    lo = (x - hi.astype(F32)).astype(BF16)
    return (jnp.dot(hi, ones, preferred_element_type=F32) + jnp.dot(lo, ones, preferred_element_type=F32))


def _dot(a, b, dims):
    return lax.dot_general(a, b, dims, preferred_element_type=F32)


def _tile_rows(d, r, n):
    stride = None if d == 1 else d
    q_rows = pl.ds(r + d * n * BLK, BLK, stride=stride)
    if n == 0:
        return q_rows, q_rows, BLK
    return q_rows, pl.ds(r + d * (n - 1) * BLK, 2 * BLK, stride=stride), 2 * BLK


NCH = 256


def _qk_norm_prep(q_ref, k_ref, gq_ref, gk_ref, qn_ref, kn_ref, ones):
    def prep(i, carry):
        rows = pl.ds(pl.multiple_of(i * NCH, NCH), NCH)
        q = q_ref[rows, :]
        qn_ref[rows, :] = q * lax.rsqrt(_seg_sum(q * q, ones) * (1.0 / HD) + EPS) * gq_ref[...] * (HD ** -0.5)
        k = k_ref[rows, :]
        kn_ref[rows, :] = k * lax.rsqrt(_seg_sum(k * k, ones) * (1.0 / HD) + EPS) * gk_ref[...]
        return carry

    lax.fori_loop(0, T // NCH, prep, 0)


def _attn_specs(g):
    ucol = lambda base: pl.BlockSpec((T, 128), lambda hp: (0, (base + g) * 4 + hp))
    col = pl.BlockSpec((T, 128), lambda hp: (0, hp))
    vec = pl.BlockSpec((1, 128), lambda hp: (0, 0))
    bm = pl.BlockSpec((2, BLK, 2 * BLK), lambda hp: (g * 4 + hp, 0, 0))
    return ucol, col, vec, bm


def _attn_fwd(g, u, gq, gk, bm):
    d = DIL[g]

    def body(q_ref, k_ref, v_ref, gq_ref, gk_ref, bm_ref, o_ref, lse_ref, qn_ref, kn_ref):
        ones = _seg_ones(128)
        _qk_norm_prep(q_ref, k_ref, gq_ref, gk_ref, qn_ref, kn_ref, ones)
        m_a = _head_mask()
        for r in range(d):
            for n in range(T // d // BLK):
                q_rows, k_rows, nk = _tile_rows(d, r, n)
                qt = qn_ref[q_rows, :]
                kt = kn_ref[k_rows, :].astype(BF16)
                vt = v_ref[k_rows, :].astype(BF16)
                o_t = None
                for h in range(2):
                    mh = m_a if h == 0 else jnp.logical_not(m_a)
                    qh = jnp.where(mh, qt, 0.0).astype(BF16)
                    s = _dot(qh, kt, NT) + bm_ref[h, :, 2 * BLK - nk:]
                    mx = jnp.max(s, axis=1, keepdims=True)
                    p = jnp.exp(s - mx)
                    l = jnp.sum(p, axis=1, keepdims=True)
                    o_h = _dot(p.astype(BF16), vt, NN) / l
                    lse_h = jnp.broadcast_to(mx + jnp.log(l), (BLK, 128))
                    if h == 0:
                        o_t, lse_t = o_h, lse_h
                    else:
                        o_t = jnp.where(m_a, o_t, o_h)
                        lse_t = jnp.where(m_a, lse_t, lse_h)
                o_ref[q_rows, :] = o_t
                lse_ref[q_rows, :] = lse_t

    ucol, col, vec, bmspec = _attn_specs(g)
    return _pcall(body, name=f"attn_fwd_g{g}", out_shape=(_sds((T, AOUT)), _sds((T, AOUT))), grid=(4,),
                  in_specs=[ucol(UB_Q), ucol(UB_K), ucol(UB_V), vec, vec, bmspec], out_specs=(col, col),
                  scratch=[pltpu.VMEM((T, 128), F32), pltpu.VMEM((T, 128), F32)],
                  semantics=("parallel",))(u, u, u, gq, gk, bm)


def _attn_bwd(g, u, gq, gk, bm, dog, cb, lse):
    d = DIL[g]

    def body(q_ref, k_ref, v_ref, gq_ref, gk_ref, bm_ref, do_ref, cb_ref, lse_ref,
             dqo_ref, dko_ref, dvo_ref, dgq_ref, dgk_ref, dsa_ref, qn_ref, kn_ref, dq_ref, dk_ref, dv_ref):
        ones = _seg_ones(128)
        _qk_norm_prep(q_ref, k_ref, gq_ref, gk_ref, qn_ref, kn_ref, ones)
        m_a = _head_mask()
        dk_ref[...] = jnp.zeros_like(dk_ref)
        dv_ref[...] = jnp.zeros_like(dv_ref)
        dsa_ref[...] = jnp.zeros_like(dsa_ref)
        for r in range(d):
            for n in range(T // d // BLK):
                q_rows, k_rows, nk = _tile_rows(d, r, n)
                qt = qn_ref[q_rows, :]
                kt = kn_ref[k_rows, :]
                ktb = kt.astype(BF16)
                vtb = v_ref[k_rows, :].astype(BF16)
                do_t = do_ref[q_rows, :]
                c_t = cb_ref[q_rows, :]
                lse_t = lse_ref[q_rows, :]
                dq_t = jnp.zeros((BLK, 128), F32)
                dk_t = jnp.zeros((nk, 128), F32)
                dv_t = jnp.zeros((nk, 128), F32)
                for h in range(2):
                    mh = m_a if h == 0 else jnp.logical_not(m_a)
                    qh = jnp.where(mh, qt, 0.0).astype(BF16)
                    kh = jnp.where(mh, kt, 0.0).astype(BF16)
                    doh = jnp.where(mh, do_t, 0.0).astype(BF16)
                    lse_c = jnp.max(jnp.where(mh, lse_t, -3e38), axis=1, keepdims=True)
                    c_c = jnp.max(jnp.where(mh, c_t, -3e38), axis=1, keepdims=True)
                    s = _dot(qh, ktb, NT) + bm_ref[h, :, 2 * BLK - nk:]
                    p = jnp.exp(s - lse_c)
                    dp = _dot(doh, vtb, NT)
                    ds = p * (dp + c_c)
                    dsb = ds.astype(BF16)
                    dv_t = dv_t + _dot(p.astype(BF16), doh, TN)
                    dq_t = dq_t + _dot(dsb, kh, NN)
                    dk_t = dk_t + _dot(dsb, qh, TN)
                    dsa_ref[h, :, 2 * BLK - nk:] += ds
                dq_ref[q_rows, :] = dq_t
                dk_ref[k_rows, :] += dk_t
                dv_ref[k_rows, :] += dv_t

        @pl.when(pl.program_id(0) == 0)
        def _():
            dgq_ref[...] = jnp.zeros_like(dgq_ref)
            dgk_ref[...] = jnp.zeros_like(dgk_ref)

        def norm_bwd(i, carry):
            rows = pl.ds(pl.multiple_of(i * NCH, NCH), NCH)
            for x_ref, g_ref, dx_ref, dxo_ref, dg_ref, scale in (
                    (q_ref, gq_ref, dq_ref, dqo_ref, dgq_ref, HD ** -0.5), (k_ref, gk_ref, dk_ref, dko_ref, dgk_ref, 1.0)):
                x = x_ref[rows, :]
                rs = lax.rsqrt(_seg_sum(x * x, ones) * (1.0 / HD) + EPS)
                xh = x * rs
                dn = dx_ref[rows, :] * scale
                dxh = dn * g_ref[...]
                dxo_ref[rows, :] = (rs * (dxh - xh * (_seg_sum(dxh * xh, ones) * (1.0 / HD)))).astype(BF16)
                dg_ref[...] += jnp.sum(dn * xh, axis=0, keepdims=True)
            dvo_ref[rows, :] = dv_ref[rows, :].astype(BF16)
            return carry

        lax.fori_loop(0, T // NCH, norm_bwd, 0)

    ucol, col, vec, bmspec = _attn_specs(g)
    return _pcall(body, name=f"attn_bwd_g{g}",
                  out_shape=(_sds((T, AOUT), BF16), _sds((T, AOUT), BF16), _sds((T, AOUT), BF16), _sds((1, 128)),
                             _sds((1, 128)), _sds((8, BLK, 2 * BLK))),
                  grid=(4,),
                  in_specs=[ucol(UB_Q), ucol(UB_K), ucol(UB_V), vec, vec, bmspec, col, col, col],
                  out_specs=(col, col, col, vec, vec, pl.BlockSpec((2, BLK, 2 * BLK), lambda hp: (hp, 0, 0))),
                  scratch=[pltpu.VMEM((T, 128), F32)] * 5,
                  semantics=("arbitrary",))(u, u, u, gq, gk, bm, dog, cb, lse)


def _combine_fwd(ogs, lses):
    def body(o0, o1, o2, l0, l1, l2, o_ref):
        ls = [l0[...], l1[...], l2[...]]
        mx = jnp.maximum(jnp.maximum(ls[0], ls[1]), ls[2])
        es = [jnp.exp(l - mx) for l in ls]
        inv = 1.0 / (es[0] + es[1] + es[2])
        o_ref[...] = ((es[0] * o0[...] + es[1] * o1[...] + es[2] * o2[...]) * inv).astype(BF16)

    return _pcall(body, name="combine_fwd", out_shape=_sds((T, AOUT), BF16), grid=(T // TB,),
                  in_specs=[_row_spec(AOUT)] * 6, out_specs=_row_spec(AOUT), semantics=("parallel",))(*ogs, *lses)


def _combine_bwd(ogs, lses, do):
    def body(o0, o1, o2, l0, l1, l2, do_ref, d0, d1, d2, c0, c1, c2):
        ls = [l0[...], l1[...], l2[...]]
        mx = jnp.maximum(jnp.maximum(ls[0], ls[1]), ls[2])
        es = [jnp.exp(l - mx) for l in ls]
        inv = 1.0 / (es[0] + es[1] + es[2])
        ws = [e * inv for e in es]
        do = do_ref[...]
        o = ws[0] * o0[...] + ws[1] * o1[...] + ws[2] * o2[...]
        s = _seg_sum(do * o, _seg_ones(AOUT))
        for w, d_ref, c_ref in zip(ws, (d0, d1, d2), (c0, c1, c2)):
            d_ref[...] = w * do
            c_ref[...] = -(w * s)

    return _pcall(body, name="combine_bwd", out_shape=tuple(_sds((T, AOUT)) for _ in range(6)), grid=(T // TB,),
                  in_specs=[_row_spec(AOUT)] * 7, out_specs=tuple(_row_spec(AOUT) for _ in range(6)),
                  semantics=("parallel",))(*ogs, *lses, do)


def _layer_fwd(x, p, bm, deps=()):
    h1 = _rms_fwd(x, p["n1g"])
    u = _mm_x_wcols("mm_u", h1, p["win4"], tm=T, tn=640, deps=deps)
    z1 = _conv_fwd(u, p["dww"], p["dwb"])
    z3 = _ln_silu_fwd(z1, p["lng"], p["lnb"])
    ycv = _mm_x_wcols("mm_ycv", z3, p["wco4"], tm=T, tn=256)
    ogs, lses = [], []
    for g in range(NG):
        og, lse = _attn_fwd(g, u, p["gq"], p["gk"], bm)
        ogs.append(og)
        lses.append(lse)
    o = _combine_fwd(ogs, lses)
    yat = _mm_x_wcols("mm_yat", o, p["wao4"], tm=T, tn=256)
    m = _gate_fwd(u, ycv, yat)
    xm = _mm_x_wrows("mm_xmid", m, p["wout4"], x, tm=1024, tk=256, tn=1024)
    h2 = _rms_fwd(xm, p["n2g"])
    f, r = _mm_ff1(h2, p["wff14"], tm=T, tn=512)
    xo = _mm_x_wrows("mm_xout", r, p["wff24"], xm, tm=1024, tk=1024, tn=1024)
    saved = dict(x=x, h1=h1, u=u, z1=z1, ogs=ogs, lses=lses, o=o, ycv=ycv, yat=yat, m=m, xm=xm, h2=h2, f=f)
    return xo, saved


def _layer_bwd(dx, s, p, bm, pipe=None):
    u = s["u"]
    tok = pipe.step0() if pipe else None
    r, df = _mm_dff2(dx, p["wff24"], s["f"], tm=1024, tn=512, deps=(tok,))
    g_ff2 = _mm_dw_rows("mm_dwff2", r, dx, ks=1024, tm=1024, tn=512)
    g_ff1 = _mm_dw_cols("mm_dwff1", s["h2"], df, ns=1024, tm=1024, tn=512)
    tok = pipe.step1(g_ff1) if pipe else None
    dh2 = _mm_g_wcols_t("mm_dh2", df, p["wff14"], tm=1024, tk=512, tn=1024, deps=(tok,))
    dxm, d_n2g = _rms_bwd(s["xm"], p["n2g"], dh2, dx)

    dm = _mm_g_wrows_t("mm_dm", dxm, p["wout4"], tm=1024, tn=256)
    g_out = _mm_dw_rows("mm_dwout", s["m"], dxm, ks=256, tm=256, tn=512)
    dyc, dya, dgc, dga = _gate_bwd(u, s["ycv"], s["yat"], dm)

    dz3 = _mm_g_wcols_t("mm_dz3", dyc, p["wco4"], tm=T, tk=256, tn=512)
    z3, dz1, d_lng, d_lnb = _ln_silu_bwd(s["z1"], p["lng"], p["lnb"], dz3)
    g_co = _mm_dw_cols("mm_dwco", z3, dyc, ns=256, tm=512, tn=256)
    da, dgt, d_dww, d_dwb = _conv_bwd(u, p["dww"], dz1)

    do = _mm_g_wcols_t("mm_do", dya, p["wao4"], tm=T, tk=256, tn=512)
    g_ao = _mm_dw_cols("mm_dwao", s["o"], dya, ns=256, tm=512, tn=256)
    parts = _combine_bwd(s["ogs"], s["lses"], do)
    dqs, dks, dvs, d_gq, d_gk, dsas = [], [], [], [], [], []
    for g in range(NG):
        dq, dk, dv, dgq, dgk, dsa = _attn_bwd(g, u, p["gq"], p["gk"], bm, parts[g], parts[NG + g], s["lses"][g])
        dqs.append(dq)
        dks.append(dk)
        dvs.append(dv)
        d_gq.append(dgq)
        d_gk.append(dgk)
        dsas.append(dsa)
    du = jnp.concatenate([da, dgt] + dqs + dks + dvs + [dgc, dga], axis=1)
    tok = pipe.step2(du) if pipe else None
    g_in = _mm_dw_cols("mm_dwin", s["h1"], du, ns=1920, tm=1024, tn=640, deps=(tok,))
    dh1 = _mm_g_wcols_t("mm_dh1", du, p["win4"], tm=1024, tk=640, tn=1024)
    dxi, d_n1g = _rms_bwd(s["x"], p["n1g"], dh1, dxm)
    if pipe:
        pipe.step3(dxi)

    fold = lambda parts_: sum(v[0, :HD] + v[0, HD:] for v in parts_)
    big = dict(w_in=g_in, w_conv_out=g_co, w_attn_out=g_ao, w_out=g_out, w_ff1=g_ff1, w_ff2=g_ff2)
    small = dict(norm1_g=d_n1g[0], q_norm_g=fold(d_gq), k_norm_g=fold(d_gk), conv_dw_w=d_dww, conv_dw_b=d_dwb[0],
                 conv_ln_g=d_lng[0], conv_ln_b=d_lnb[0], norm2_g=d_n2g[0])
    return dxi, big, small, jnp.concatenate(dsas, axis=0)


def _local_step(x, target, get_layer, rel_bias, make_pipe):
    buckets = jnp.asarray(_bucket_tables())
    bm = _bias_table(rel_bias.T, buckets)
    saved, layers = [], []
    for l in range(DEPTH):
        p, deps = get_layer(l, x)
        layers.append(p)
        x, s = _layer_fwd(x, p, bm, deps=deps)
        saved.append(s)
    loss_blk, dx = _loss_fwd_bwd(x, target)
    smalls, dsas = [None] * DEPTH, [None] * DEPTH
    pipe = None
    for l in reversed(range(DEPTH)):
        dx, big, smalls[l], dsas[l] = _layer_bwd(dx, saved[l], layers[l], bm, pipe)
        pipe = make_pipe(l, big)
    pipe.finish()
    d_rel = _bias_grad(jnp.stack(dsas), buckets)[:, 0, :NBUCKET].T
    return loss_blk[0, 0], dx, smalls, d_rel


MESH = pl.DeviceIdType.MESH


def _me():
    return lax.axis_index("x"), lax.axis_index("y"), lax.axis_index("c")


def _other_chips(mx, my):
    return [(1 - mx, my), (mx, 1 - my), (1 - mx, 1 - my)]


def _rcopy(src, dst, send_sems, recv_sems, k, dev):
    return pltpu.make_async_remote_copy(src_ref=src, dst_ref=dst, send_sem=send_sems.at[k], recv_sem=recv_sems.at[k],
                                        device_id=dev, device_id_type=MESH)


def _comm_call(body, name, out_shape, n_in, n_sems):
    return pl.pallas_call(
        body, name=name, out_shape=out_shape, in_specs=[HBM_SPEC] * n_in,
        out_specs=jax.tree.map(lambda _: HBM_SPEC, out_shape),
        scratch_shapes=[pltpu.SemaphoreType.DMA((n_sems,)), pltpu.SemaphoreType.DMA((n_sems,)),
                        pltpu.SemaphoreType.DMA(())],
        compiler_params=pltpu.CompilerParams(has_side_effects=True))


def _all_gather_chips(x, name):
    def body(x_ref, o_ref, send_sems, recv_sems, local_sem):
        mx, my, mc = _me()
        local = pltpu.make_async_copy(x_ref, o_ref.at[2 * mx + my], local_sem)
        local.start()
        sends = [_rcopy(x_ref, o_ref.at[2 * mx + my], send_sems, recv_sems, k, (px, py, mc))
                 for k, (px, py) in enumerate(_other_chips(mx, my))]
        for cp in sends:
            cp.start()
        for k, (px, py) in enumerate(_other_chips(mx, my)):
            _rcopy(x_ref, o_ref.at[2 * px + py], send_sems, recv_sems, k, (px, py, mc)).wait_recv()
        for cp in sends:
            cp.wait_send()
        local.wait()

    return _comm_call(body, name, _sds((NCHIP,) + x.shape, x.dtype), 1, 3)(x)


EFFECT = pltpu.SideEffectType.DATAFLOW_SIDE_EFFECTING


def _hbm(a):
    return pltpu.with_memory_space_constraint(a, pltpu.HBM)


def _split_start(name, bufs, plan, n, after=None):
    nb = len(bufs)
    extra = [] if after is None else [after]
    ne = len(extra)

    def body(*refs):
        send_sems, recv_sems, token = refs[nb + ne], refs[nb + ne + 1], refs[-1]
        mx, my, mc = _me()
        for k, (src, dst, dev, _) in enumerate(plan(refs[:nb], mx, my, mc)):
            _rcopy(src, dst, send_sems, recv_sems, k, dev).start()
        token[...] = jnp.zeros_like(token)

    out = pl.pallas_call(
        body, name=name,
        out_shape=(pltpu.SemaphoreType.DMA((n,)), pltpu.SemaphoreType.DMA((n,)),
                   *[pltpu.HBM(b.shape, b.dtype) for b in bufs], _sds((8, 128))),
        in_specs=[HBM_SPEC] * nb + [ANY_SPEC] * ne,
        out_specs=(SEM_SPEC, SEM_SPEC, *[HBM_SPEC] * nb, pl.BlockSpec(memory_space=pltpu.VMEM)),
        input_output_aliases={i: 2 + i for i in range(nb)},
        compiler_params=pltpu.CompilerParams(has_side_effects=EFFECT))(*[_hbm(b) for b in bufs], *extra)
    return (out[0], out[1]), list(out[2:2 + nb]), out[-1]


def _split_wait(name, sems, bufs, plan, after):
    nb = len(bufs)

    def body(*refs):
        send_sems, recv_sems = refs[nb], refs[nb + 1]
        mx, my, mc = _me()
        for k, (src, dst, dev, land) in enumerate(plan(refs[:nb], mx, my, mc)):
            _rcopy(src, dst, send_sems, recv_sems, k, dev).wait_send()
            _rcopy(src, land, send_sems, recv_sems, k, dev).wait_recv()

    out = pl.pallas_call(
        body, name=name, out_shape=tuple(pltpu.HBM(b.shape, b.dtype) for b in bufs),
        in_specs=[HBM_SPEC] * nb + [SEM_SPEC, SEM_SPEC, ANY_SPEC], out_specs=(HBM_SPEC,) * nb,
        input_output_aliases={i: i for i in range(nb)},
        compiler_params=pltpu.CompilerParams(has_side_effects=EFFECT))(*bufs, sems[0], sems[1], after)
    return list(out)


def _plan_gather(refs, mx, my, mc):
    me = 2 * mx + my
    return [(r.at[me], r.at[me], (px, py, mc), r.at[2 * px + py]) for r in refs for px, py in _other_chips(mx, my)]


def _plan_pair_half(refs, mx, my, mc):
    n = len(refs) // 2
    return [(g.at[:, 1 - mc], r, (mx, my, 1 - mc), r) for g, r in zip(refs[:n], refs[n:])]


def _plan_scatter(refs, mx, my, mc):
    n = len(refs) // 2
    return [(q.at[2 * px + py], r.at[k], (px, py, mc), r.at[k])
            for q, r in zip(refs[:n], refs[n:]) for k, (px, py) in enumerate(_other_chips(mx, my))]


def _plan_pair_fill(refs, mx, my, mc):
    return [(r.at[mc], r.at[mc], (mx, my, 1 - mc), r.at[1 - mc]) for r in refs]


def _all_gather_devices(v, name):
    def body(v_ref, o_ref, send_sems, recv_sems, local_sem):
        mx, my, mc = _me()
        flip = lambda m, b: 1 - m if b else m
        peers = [(flip(mx, k >> 2 & 1), flip(my, k >> 1 & 1), flip(mc, k & 1)) for k in range(1, 8)]
        slot = lambda d: 4 * d[0] + 2 * d[1] + d[2]
        local = pltpu.make_async_copy(v_ref, o_ref.at[slot((mx, my, mc))], local_sem)
        local.start()
        sends = [_rcopy(v_ref, o_ref.at[slot((mx, my, mc))], send_sems, recv_sems, k, dev)
                 for k, dev in enumerate(peers)]
        for cp in sends:
            cp.start()
        for k, dev in enumerate(peers):
            _rcopy(v_ref, o_ref.at[slot(dev)], send_sems, recv_sems, k, dev).wait_recv()
        for cp in sends:
            cp.wait_send()
        local.wait()

    return _comm_call(body, name, _sds((8,) + v.shape, v.dtype), 1, 7)(v)


def _row_tile(rows, cols):
    t = 8
    while t * 2 * cols * 4 <= (1 << 20) and rows % (t * 2) == 0:
        t *= 2
    return t


def _prefetch_call(body, name, out_shape, grid, in_specs, out_specs):
    return pl.pallas_call(
        body, name=name, out_shape=out_shape,
        grid_spec=pltpu.PrefetchScalarGridSpec(num_scalar_prefetch=1, grid=grid, in_specs=in_specs,
                                               out_specs=out_specs),
        compiler_params=pltpu.CompilerParams(vmem_limit_bytes=VMEM_LIMIT,
                                             dimension_semantics=("parallel",) * len(grid)))


def _sum_half(g, r1, place, name):
    _, _, rr, ns = g.shape
    tr = _row_tile(rr, ns)

    def body(c_ref, g_ref, r_ref, o_ref, ob_ref):
        q = g_ref[...] + r_ref[...]
        o_ref[...] = q
        ob_ref[...] = q.astype(BF16)

    blk = pl.BlockSpec((None, tr, ns), lambda s, i, c: (s, i, 0))
    return _prefetch_call(body, name, (_sds((NCHIP, rr, ns)), _sds((NCHIP, rr, ns), BF16)), (NCHIP, rr // tr),
                          [pl.BlockSpec((None, None, tr, ns), lambda s, i, c: (s, c[1], i, 0)), blk],
                          (blk, blk))(place, g, r1)


def _sum_recv(q, r2, place, name):
    _, rr, ns = q.shape
    tr = _row_tile(rr, ns)

    def body(c_ref, q_ref, r_ref, o_ref):
        o_ref[...] = ((q_ref[...] + r_ref[0].astype(F32)) + r_ref[1].astype(F32)) + r_ref[2].astype(F32)

    return _prefetch_call(body, name, _sds((2, rr, ns)), (rr // tr,),
                          [pl.BlockSpec((None, tr, ns), lambda i, c: (c[0], i, 0)),
                           pl.BlockSpec((NCHIP - 1, tr, ns), lambda i, c: (0, i, 0))],
                          pl.BlockSpec((None, tr, ns), lambda i, c: (c[1], i, 0)))(place, q, r2)


def _sum_devices(v8):
    def body(v_ref, o_ref):
        acc = v_ref[0]
        for dev in range(1, 8):
            acc = acc + v_ref[dev]
        o_ref[...] = acc

    return _pcall(body, name="sum_devices", out_shape=_sds(v8.shape[1:]))(v8)


def _adamw(w, g, m, v, name):
    rows, cols = w.shape
    tr = _row_tile(rows, cols)

    def body(w_ref, g_ref, m_ref, v_ref, d_ref, m2_ref, v2_ref):
        g = g_ref[...]
        m2 = ADAM_B1 * m_ref[...] + (1.0 - ADAM_B1) * g
        v2 = ADAM_B2 * v_ref[...] + (1.0 - ADAM_B2) * (g * g)
        m_hat = m2 / (1.0 - ADAM_B1 ** ADAM_STEP)
        v_hat = v2 / (1.0 - ADAM_B2 ** ADAM_STEP)
        d_ref[...] = -ADAM_LR * (m_hat / (jnp.sqrt(v_hat) + ADAM_EPS) + ADAM_WD * w_ref[...])
        m2_ref[...] = m2
        v2_ref[...] = v2

    blk = pl.BlockSpec((tr, cols), lambda i: (i, 0))
    return _pcall(body, name=name, out_shape=(_sds((rows, cols)),) * 3, grid=(rows // tr,), in_specs=[blk] * 4,
                  out_specs=(blk,) * 3, semantics=("parallel",))(w, g, m, v)


BIG = ("w_in", "w_conv_out", "w_attn_out", "w_out", "w_ff1", "w_ff2")
SMALL = ("rel_bias", "norm1_g", "q_norm_g", "k_norm_g", "conv_dw_w", "conv_dw_b", "conv_ln_g", "conv_ln_b", "norm2_g")
WEIGHTS = ("rel_bias", "norm1_g", "w_in", "q_norm_g", "k_norm_g", "conv_dw_w", "conv_dw_b", "conv_ln_g", "conv_ln_b",
           "w_conv_out", "w_attn_out", "w_out", "norm2_g", "w_ff1", "w_ff2")


def _pack(arrays):
    flat = jnp.concatenate([a.reshape(-1) for a in arrays])
    n = flat.shape[0]
    rows = -(-n // 1024) * 8
    return jnp.pad(flat, (0, rows * 128 - n)).reshape(rows, 128)


def _unpack(packed, shapes):
    flat = packed.reshape(-1)
    out, off = [], 0
    for shp in shapes:
        n = int(np.prod(shp))
        out.append(flat[off:off + n].reshape(shp))
        off += n
    return out


def _adamw_layer(l, w, g, m, v, prev, name):
    _, k, n = w.shape
    tr = _row_tile(k, n)
    if prev is None:
        prev = tuple(lax.empty(w.shape, F32) for _ in range(4))

    def body(w_ref, g_ref, m_ref, v_ref, p0, p1, p2, p3, go_ref, d_ref, m2_ref, v2_ref):
        g = g_ref[...]
        m2 = ADAM_B1 * m_ref[...] + (1.0 - ADAM_B1) * g
        v2 = ADAM_B2 * v_ref[...] + (1.0 - ADAM_B2) * (g * g)
        m_hat = m2 / (1.0 - ADAM_B1 ** ADAM_STEP)
        v_hat = v2 / (1.0 - ADAM_B2 ** ADAM_STEP)
        go_ref[...] = g
        d_ref[...] = -ADAM_LR * (m_hat / (jnp.sqrt(v_hat) + ADAM_EPS) + ADAM_WD * w_ref[...])
        m2_ref[...] = m2
        v2_ref[...] = v2

    lay = pl.BlockSpec((None, tr, n), lambda i: (l, i, 0))
    return _pcall(body, name=name, out_shape=(_sds(w.shape),) * 4, grid=(k // tr,),
                  in_specs=[lay, pl.BlockSpec((tr, n), lambda i: (i, 0)), lay, lay] + [ANY_SPEC] * 4,
                  out_specs=(lay,) * 4, aliases={4: 0, 5: 1, 6: 2, 7: 3},
                  semantics=("parallel",))(w, g, m, v, *prev)


class _GradPipe:
    def __init__(self, l, big, place, w, m, v, results):
        self.l, self.place, self.w, self.m, self.v, self.results = l, place, w, m, v, results
        self.g = [big[n].reshape(NCHIP, 2, big[n].shape[1] // 2, big[n].shape[2]) for n in BIG]

    def step0(self):
        lands = [lax.empty((NCHIP,) + g.shape[2:], F32) for g in self.g]
        self.s1, self.b1, tok = _split_start(f"rs1_start_l{self.l}", self.g + lands, _plan_pair_half, len(BIG))
        return tok

    def step1(self, after):
        bufs = _split_wait(f"rs1_wait_l{self.l}", self.s1, self.b1, _plan_pair_half, after)
        g, r1 = bufs[:len(BIG)], bufs[len(BIG):]
        sums = [_sum_half(g[i], r1[i], self.place, f"rs1_sum_{n}") for i, n in enumerate(BIG)]
        self.q = [q for q, _ in sums]
        qb = [b for _, b in sums]
        lands = [lax.empty((NCHIP - 1,) + b.shape[1:], BF16) for b in qb]
        self.s2, self.b2, tok = _split_start(f"rs2_start_l{self.l}", qb + lands, _plan_scatter, 3 * len(BIG))
        return tok

    def step2(self, after):
        bufs = _split_wait(f"rs2_wait_l{self.l}", self.s2, self.b2, _plan_scatter, after)
        r2 = bufs[len(BIG):]
        fin = [_sum_recv(self.q[i], r2[i], self.place, f"rs2_sum_{n}") for i, n in enumerate(BIG)]
        self.s3, self.b3, tok = _split_start(f"rs3_start_l{self.l}", fin, _plan_pair_fill, len(BIG))
        return tok

    def step3(self, after):
        fin = _split_wait(f"rs3_wait_l{self.l}", self.s3, self.b3, _plan_pair_fill, after)
        for i, n in enumerate(BIG):
            g2 = fin[i].reshape(fin[i].shape[1] * 2, fin[i].shape[2])
            self.results[n] = _adamw_layer(self.l, self.w[n], g2, self.m[n], self.v[n], self.results.get(n),
                                           f"adamw_{n}_l{self.l}")

    def finish(self):
        self.step3(self.step2(self.step1(self.step0())))


def kernel(x, rel_bias, norm1_g, w_in, q_norm_g, k_norm_g, conv_dw_w, conv_dw_b, conv_ln_g, conv_ln_b, w_conv_out, w_attn_out, w_out, norm2_g, w_ff1, w_ff2, loss_target, m_rel_bias, m_norm1_g, m_w_in, m_q_norm_g, m_k_norm_g, m_conv_dw_w, m_conv_dw_b, m_conv_ln_g, m_conv_ln_b, m_w_conv_out, m_w_attn_out, m_w_out, m_norm2_g, m_w_ff1, m_w_ff2, v_rel_bias, v_norm1_g, v_w_in, v_q_norm_g, v_k_norm_g, v_conv_dw_w, v_conv_dw_b, v_conv_ln_g, v_conv_ln_b, v_w_conv_out, v_w_attn_out, v_w_out, v_norm2_g, v_w_ff1, v_w_ff2):
    w = dict(rel_bias=rel_bias, norm1_g=norm1_g, w_in=w_in, q_norm_g=q_norm_g, k_norm_g=k_norm_g, conv_dw_w=conv_dw_w,
             conv_dw_b=conv_dw_b, conv_ln_g=conv_ln_g, conv_ln_b=conv_ln_b, w_conv_out=w_conv_out,
             w_attn_out=w_attn_out, w_out=w_out, norm2_g=norm2_g, w_ff1=w_ff1, w_ff2=w_ff2)
    m = dict(rel_bias=m_rel_bias, norm1_g=m_norm1_g, w_in=m_w_in, q_norm_g=m_q_norm_g, k_norm_g=m_k_norm_g,
             conv_dw_w=m_conv_dw_w, conv_dw_b=m_conv_dw_b, conv_ln_g=m_conv_ln_g, conv_ln_b=m_conv_ln_b,
             w_conv_out=m_w_conv_out, w_attn_out=m_w_attn_out, w_out=m_w_out, norm2_g=m_norm2_g, w_ff1=m_w_ff1,
             w_ff2=m_w_ff2)
    v = dict(rel_bias=v_rel_bias, norm1_g=v_norm1_g, w_in=v_w_in, q_norm_g=v_q_norm_g, k_norm_g=v_k_norm_g,
             conv_dw_w=v_conv_dw_w, conv_dw_b=v_conv_dw_b, conv_ln_g=v_conv_ln_g, conv_ln_b=v_conv_ln_b,
             w_conv_out=v_w_conv_out, w_attn_out=v_w_attn_out, w_out=v_w_out, norm2_g=v_norm2_g, w_ff1=v_w_ff1,
             w_ff2=v_w_ff2)
    chip_id = 2 * lax.axis_index("x") + lax.axis_index("y")
    place = jnp.stack([chip_id, lax.axis_index("c")]).astype(jnp.int32)

    dww4 = _all_gather_chips(conv_dw_w, "ag_conv_dw_w")
    dww = dww4.transpose(1, 2, 0, 3).reshape(DEPTH, KW, CONV)

    def start(l, after):
        lands = [lax.dynamic_update_slice(lax.empty((NCHIP,) + w[n].shape[1:], BF16), w[n][l].astype(BF16)[None],
                                          (chip_id, 0, 0)) for n in BIG]
        return _split_start(f"ag_start_l{l}", lands, _plan_gather, 3 * len(BIG), after=after)

    started = {0: start(0, dww4)}

    def get_layer(l, after):
        sems, bufs, tok = started[l]
        full = dict(zip(BIG, _split_wait(f"ag_wait_l{l}", sems, bufs, _plan_gather, tok if l == 0 else after)))
        deps = ()
        if l + 1 < DEPTH:
            started[l + 1] = start(l + 1, full["w_in"])
            deps = (started[l + 1][2],)
        return dict(
            win4=full["w_in"], wco4=full["w_conv_out"], wao4=full["w_attn_out"], wout4=full["w_out"],
            wff14=full["w_ff1"], wff24=full["w_ff2"], dww=dww[l],
            dwb=conv_dw_b[l][None], lng=conv_ln_g[l][None], lnb=conv_ln_b[l][None], n1g=norm1_g[l][None],
            n2g=norm2_g[l][None], gq=jnp.tile(q_norm_g[l], 2)[None], gk=jnp.tile(k_norm_g[l], 2)[None]), deps

    results = {}
    make_pipe = lambda l, big: _GradPipe(l, big, place, w, m, v, results)
    loss_share, dx, smalls, d_rel = _local_step(x[0], loss_target[0], get_layer, rel_bias, make_pipe)
    loss = lax.psum(loss_share, ("x", "y", "c"))

    local_small = dict(rel_bias=d_rel)
    for n in SMALL[1:]:
        local_small[n] = jnp.stack([smalls[l][n] for l in range(DEPTH)])
    small_shapes = [local_small[n].shape for n in SMALL]
    summed = _sum_devices(_all_gather_devices(_pack([local_small[n] for n in SMALL]), "ag_small"))
    grads = dict(zip(SMALL, _unpack(summed, small_shapes)))
    grads["conv_dw_w"] = lax.dynamic_slice_in_dim(grads["conv_dw_w"], chip_id * 128, 128, axis=2)

    delta, new_m, new_v = {}, {}, {}
    small_w_shapes = [w[n].shape for n in SMALL]
    outs = _adamw(_pack([w[n] for n in SMALL]), _pack([grads[n] for n in SMALL]), _pack([m[n] for n in SMALL]),
                  _pack([v[n] for n in SMALL]), "adamw_small")
    for dst, packed in zip((delta, new_m, new_v), outs):
        dst.update(zip(SMALL, _unpack(packed, small_w_shapes)))

    for n in BIG:
        grads[n], delta[n], new_m[n], new_v[n] = results[n]

    return (loss, dx[None], *[grads[n] for n in WEIGHTS], *[delta[n] for n in WEIGHTS],
            *[new_m[n] for n in WEIGHTS], *[new_v[n] for n in WEIGHTS])
```

```python
import functools
import math

import numpy as np
import jax
import jax.numpy as jnp
from jax import lax
from jax.experimental import pallas as pl
from jax.experimental.pallas import tpu as pltpu

F32 = jnp.float32
BF16 = jnp.bfloat16

T = 2048
D = 1024
DEPTH = 4
CONV = 512
KW = 31
NG = 3
HD = 64
AOUT = 512
DFF = 4096
INC = 7680
DIL = (1, 4, 16)
BLK = 128
NBUCKET = 32
EPS = 1e-6
NEG = -1e30
NCHIP = 4
UB_A, UB_GT, UB_Q, UB_K, UB_V, UB_GC, UB_GA = 0, 1, 2, 5, 8, 11, 13

ADAM_LR, ADAM_B1, ADAM_B2, ADAM_EPS, ADAM_WD, ADAM_STEP = 0.001, 0.9, 0.999, 1e-08, 0.01, 10

VMEM_LIMIT = 48 * 1024 * 1024
TB = 256
HBM_SPEC = pl.BlockSpec(memory_space=pltpu.HBM)
ANY_SPEC = pl.BlockSpec(memory_space=pl.ANY)
SEM_SPEC = pl.BlockSpec(memory_space=pltpu.SEMAPHORE)


def _pcall(body, *, name, out_shape, grid=(), in_specs=None, out_specs=None, scratch=(), aliases=None,
           semantics=None):
    kw = {}
    if in_specs is not None:
        kw["in_specs"] = in_specs
    if out_specs is not None:
        kw["out_specs"] = out_specs
    return pl.pallas_call(
        body, name=name, out_shape=out_shape, grid=grid, scratch_shapes=scratch,
        input_output_aliases=aliases or {},
        compiler_params=pltpu.CompilerParams(vmem_limit_bytes=VMEM_LIMIT, dimension_semantics=semantics),
        **kw)


def _sds(shape, dtype=F32):
    return jax.ShapeDtypeStruct(shape, dtype)


NN = (((1,), (0,)), ((), ()))
NT = (((1,), (1,)), ((), ()))
TN = (((0,), (0,)), ((), ()))


def _mm(name, a, b, *, out_shape, out_dtype, grid, a_spec, b_spec, o_spec, acc_shape, dims, add=None,
        add_spec=None, deps=()):
    nk = grid[2]
    deps = tuple(d for d in deps if d is not None)
    n_scratch = 1 if nk > 1 else 0

    def body(*refs):
        n_out = 1 + n_scratch
        refs = refs[:len(refs) - n_out - len(deps)] + refs[len(refs) - n_out:]
        a_ref, b_ref = refs[0], refs[1]
        r_ref = refs[2] if add is not None else None
        o_ref = refs[-n_out]
        prod = lax.dot_general(a_ref[...].astype(BF16), b_ref[...].astype(BF16), dims, preferred_element_type=F32)
        if nk == 1:
            o_ref[...] = (prod if r_ref is None else prod + r_ref[...]).astype(out_dtype)
            return
        acc_ref = refs[-1]
        k = pl.program_id(2)

        @pl.when(k == 0)
        def _():
            acc_ref[...] = prod

        @pl.when(k > 0)
        def _():
            acc_ref[...] += prod

        @pl.when(k == nk - 1)
        def _():
            res = acc_ref[...]
            if r_ref is not None:
                res = res + r_ref[...]
            o_ref[...] = res.astype(out_dtype)

    ins = ([a, b] if add is None else [a, b, add]) + list(deps)
    specs = ([a_spec, b_spec] if add is None else [a_spec, b_spec, add_spec]) + [ANY_SPEC] * len(deps)
    return _pcall(body, name=name, out_shape=_sds(out_shape, out_dtype), grid=grid, in_specs=specs,
                  out_specs=o_spec, scratch=[pltpu.VMEM(acc_shape, F32)] * n_scratch,
                  semantics=("parallel", "parallel", "arbitrary"))(*ins)


def _mm_x_wcols(name, a, w4, *, tm, tn, out_dtype=F32, deps=()):
    _, k, ns = w4.shape
    nj = ns // tn
    return _mm(name, a, w4, out_shape=(T, NCHIP * ns), out_dtype=out_dtype, grid=(T // tm, NCHIP * nj, 1), deps=deps,
               a_spec=pl.BlockSpec((tm, k), lambda i, j, kk: (i, 0)),
               b_spec=pl.BlockSpec((None, k, tn), lambda i, j, kk: (j // nj, 0, j % nj)),
               o_spec=pl.BlockSpec((tm, tn), lambda i, j, kk: (i, j)), acc_shape=(tm, tn), dims=NN)


def _mm_ff1(a, w4, *, tm, tn):
    _, k, ns = w4.shape
    nj = ns // tn

    def body(a_ref, b_ref, f_ref, r_ref):
        f = jnp.dot(a_ref[...], b_ref[...], preferred_element_type=F32)
        f_ref[...] = f
        p = jnp.maximum(f, 0.0)
        r_ref[...] = (p * p).astype(BF16)

    out = pl.BlockSpec((tm, tn), lambda i, j: (i, j))
    return _pcall(body, name="mm_f", out_shape=(_sds((T, DFF)), _sds((T, DFF), BF16)), grid=(T // tm, NCHIP * nj),
                  in_specs=[pl.BlockSpec((tm, k), lambda i, j: (i, 0)),
                            pl.BlockSpec((None, k, tn), lambda i, j: (j // nj, 0, j % nj))],
                  out_specs=(out, out), semantics=("parallel", "parallel"))(a, w4)


def _mm_x_wrows(name, a, w4, add, *, tm, tk, tn, deps=()):
    _, ks, n = w4.shape
    nkk = ks // tk
    return _mm(name, a, w4, out_shape=(T, n), out_dtype=F32, grid=(T // tm, n // tn, NCHIP * nkk), deps=deps,
               a_spec=pl.BlockSpec((tm, tk), lambda i, j, kk: (i, kk)),
               b_spec=pl.BlockSpec((None, tk, tn), lambda i, j, kk: (kk // nkk, kk % nkk, j)),
               o_spec=pl.BlockSpec((tm, tn), lambda i, j, kk: (i, j)), acc_shape=(tm, tn), dims=NN,
               add=add, add_spec=pl.BlockSpec((tm, tn), lambda i, j, kk: (i, j)))


def _mm_g_wcols_t(name, g, w4, *, tm, tk, tn, out_dtype=F32, deps=()):
    _, k, ns = w4.shape
    nkk = ns // tk
    return _mm(name, g, w4, out_shape=(T, k), out_dtype=out_dtype, grid=(T // tm, k // tn, NCHIP * nkk), deps=deps,
               a_spec=pl.BlockSpec((tm, tk), lambda i, j, kk: (i, kk)),
               b_spec=pl.BlockSpec((None, tn, tk), lambda i, j, kk: (kk // nkk, j, kk % nkk)),
               o_spec=pl.BlockSpec((tm, tn), lambda i, j, kk: (i, j)), acc_shape=(tm, tn), dims=NT)


def _mm_g_wrows_t(name, g, w4, *, tm, tn, out_dtype=F32, deps=()):
    _, ks, n = w4.shape
    nj = ks // tn
    return _mm(name, g, w4, out_shape=(T, NCHIP * ks), out_dtype=out_dtype, grid=(T // tm, NCHIP * nj, 1), deps=deps,
               a_spec=pl.BlockSpec((tm, n), lambda i, j, kk: (i, 0)),
               b_spec=pl.BlockSpec((None, tn, n), lambda i, j, kk: (j // nj, j % nj, 0)),
               o_spec=pl.BlockSpec((tm, tn), lambda i, j, kk: (i, j)), acc_shape=(tm, tn), dims=NT)


def _mm_dff2(dx, w4, f, *, tm, tn, deps=()):
    _, ks, n = w4.shape
    nj = ks // tn
    deps = tuple(d for d in deps if d is not None)

    def body(*refs):
        dx_ref, b_ref, f_ref = refs[:3]
        r_ref, df_ref = refs[-2:]
        dr = lax.dot_general(dx_ref[...].astype(BF16), b_ref[...], NT, preferred_element_type=F32)
        p = jnp.maximum(f_ref[...], 0.0)
        r_ref[...] = (p * p).astype(BF16)
        df_ref[...] = (dr * (2.0 * p)).astype(BF16)

    out = pl.BlockSpec((tm, tn), lambda i, j: (i, j))
    return _pcall(body, name="mm_dr", out_shape=(_sds((T, DFF), BF16), _sds((T, DFF), BF16)),
                  grid=(T // tm, NCHIP * nj),
                  in_specs=[pl.BlockSpec((tm, n), lambda i, j: (i, 0)),
                            pl.BlockSpec((None, tn, n), lambda i, j: (j // nj, j % nj, 0)), out]
                  + [ANY_SPEC] * len(deps),
                  out_specs=(out, out), semantics=("parallel", "parallel"))(dx, w4, f, *deps)


TCH = 512


def _mm_dw(name, a, g, *, out_shape, out_map, tm, tn, deps=()):
    deps = tuple(d for d in deps if d is not None)

    def body(*refs):
        a_ref, g_ref = refs[:2]
        o_ref, at_ref = refs[-2:]

        @pl.when(pl.program_id(1) == 0)
        def _():
            for c in range(T // TCH):
                at_ref[:, c * TCH:(c + 1) * TCH] = a_ref[c * TCH:(c + 1) * TCH, :].T

        o_ref[...] = jnp.dot(at_ref[...], g_ref[...].astype(BF16), preferred_element_type=F32)

    return _pcall(body, name=name, out_shape=_sds(out_shape), grid=(a.shape[1] // tm, g.shape[1] // tn),
                  in_specs=[pl.BlockSpec((T, tm), lambda i, j: (0, i)), pl.BlockSpec((T, tn), lambda i, j: (0, j))]
                  + [ANY_SPEC] * len(deps),
                  out_specs=pl.BlockSpec((None, tm, tn), out_map), scratch=[pltpu.VMEM((tm, T), BF16)],
                  semantics=("parallel", "arbitrary"))(a, g, *deps)


def _mm_dw_cols(name, a, g, *, ns, tm, tn, deps=()):
    nj = ns // tn
    return _mm_dw(name, a, g, out_shape=(NCHIP, a.shape[1], ns), out_map=lambda i, j: (j // nj, i, j % nj),
                  tm=tm, tn=tn, deps=deps)


def _mm_dw_rows(name, a, g, *, ks, tm, tn):
    ni = ks // tm
    return _mm_dw(name, a, g, out_shape=(NCHIP, ks, g.shape[1]), out_map=lambda i, j: (i // ni, i % ni, j),
                  tm=tm, tn=tn)


def _row_spec(width, col=0):
    return pl.BlockSpec((TB, width), lambda i: (i, col))


def _vec_spec(width):
    return pl.BlockSpec((1, width), lambda i: (0, 0))


def _rms_fwd(x, g):
    def body(x_ref, g_ref, h_ref):
        x = x_ref[...]
        r = lax.rsqrt(jnp.mean(x * x, axis=-1, keepdims=True) + EPS)
        h_ref[...] = (x * r * g_ref[...]).astype(BF16)

    return _pcall(body, name="rms_fwd", out_shape=_sds((T, D), BF16), grid=(T // TB,),
                  in_specs=[_row_spec(D), _vec_spec(D)], out_specs=_row_spec(D), semantics=("parallel",))(x, g)


def _rms_bwd(x, g, dh, dres):
    def body(x_ref, g_ref, dh_ref, dres_ref, dx_ref, dg_ref):
        x = x_ref[...]
        r = lax.rsqrt(jnp.mean(x * x, axis=-1, keepdims=True) + EPS)
        y = x * r
        dh = dh_ref[...]
        dy = dh * g_ref[...]
        dx_ref[...] = dres_ref[...] + r * (dy - y * jnp.mean(dy * y, axis=-1, keepdims=True))

        @pl.when(pl.program_id(0) == 0)
        def _():
            dg_ref[...] = jnp.zeros_like(dg_ref)

        dg_ref[...] += jnp.sum(dh * y, axis=0, keepdims=True)

    return _pcall(body, name="rms_bwd", out_shape=(_sds((T, D)), _sds((1, D))), grid=(T // TB,),
                  in_specs=[_row_spec(D), _vec_spec(D), _row_spec(D), _row_spec(D)],
                  out_specs=(_row_spec(D), _vec_spec(D)), semantics=("arbitrary",))(x, g, dh, dres)


def _sigmoid(x):
    return 1.0 / (1.0 + jnp.exp(-x))


def _gate_fwd(u, ycv, yat):
    def body(gc_ref, ga_ref, yc_ref, ya_ref, m_ref):
        m_ref[...] = (_sigmoid(gc_ref[...]) * yc_ref[...] + _sigmoid(ga_ref[...]) * ya_ref[...]).astype(BF16)

    blk = lambda off: pl.BlockSpec((TB, 512), lambda i, j: (i, off + j))
    return _pcall(body, name="gate_fwd", out_shape=_sds((T, D), BF16), grid=(T // TB, 2),
                  in_specs=[blk(UB_GC), blk(UB_GA), blk(0), blk(0)], out_specs=blk(0),
                  semantics=("parallel", "parallel"))(u, u, ycv, yat)


def _gate_bwd(u, ycv, yat, dm):
    def body(gc_ref, ga_ref, yc_ref, ya_ref, dm_ref, dyc_ref, dya_ref, dgc_ref, dga_ref):
        dm = dm_ref[...]
        sc = _sigmoid(gc_ref[...])
        sa = _sigmoid(ga_ref[...])
        dyc_ref[...] = (dm * sc).astype(BF16)
        dya_ref[...] = (dm * sa).astype(BF16)
        dgc_ref[...] = (dm * yc_ref[...] * sc * (1.0 - sc)).astype(BF16)
        dga_ref[...] = (dm * ya_ref[...] * sa * (1.0 - sa)).astype(BF16)

    blk = lambda off: pl.BlockSpec((TB, 512), lambda i, j: (i, off + j))
    return _pcall(body, name="gate_bwd",
                  out_shape=(_sds((T, D), BF16), _sds((T, D), BF16), _sds((T, D), BF16), _sds((T, D), BF16)),
                  grid=(T // TB, 2), in_specs=[blk(UB_GC), blk(UB_GA), blk(0), blk(0), blk(0)],
                  out_specs=(blk(0), blk(0), blk(0), blk(0)),
                  semantics=("parallel", "parallel"))(u, u, ycv, yat, dm)


def _relu2_fwd(f):
    def body(f_ref, r_ref):
        a = jnp.maximum(f_ref[...], 0.0)
        r_ref[...] = (a * a).astype(BF16)

    blk = pl.BlockSpec((TB, 1024), lambda i, j: (i, j))
    return _pcall(body, name="relu2_fwd", out_shape=_sds((T, DFF), BF16), grid=(T // TB, DFF // 1024),
                  in_specs=[blk], out_specs=blk, semantics=("parallel", "parallel"))(f)


def _relu2_bwd(f, dr):
    def body(f_ref, dr_ref, r_ref, df_ref):
        a = jnp.maximum(f_ref[...], 0.0)
        r_ref[...] = (a * a).astype(BF16)
        df_ref[...] = (dr_ref[...] * (2.0 * a)).astype(BF16)

    blk = pl.BlockSpec((TB, 1024), lambda i, j: (i, j))
    return _pcall(body, name="relu2_bwd", out_shape=(_sds((T, DFF), BF16), _sds((T, DFF), BF16)),
                  grid=(T // TB, DFF // 1024), in_specs=[blk, blk], out_specs=(blk, blk),
                  semantics=("parallel", "parallel"))(f, dr)


def _loss_fwd_bwd(y, target):
    def body(y_ref, t_ref, loss_ref, dy_ref):
        e = y_ref[...] - t_ref[...]
        dy_ref[...] = e * (1.0 / D)

        @pl.when(pl.program_id(0) == 0)
        def _():
            loss_ref[...] = jnp.zeros_like(loss_ref)

        loss_ref[...] += 0.5 * jnp.sum(jnp.mean(e * e, axis=-1, keepdims=True))

    return _pcall(body, name="loss", out_shape=(_sds((8, 128)), _sds((T, D))), grid=(T // TB,),
                  in_specs=[_row_spec(D), _row_spec(D)],
                  out_specs=(pl.BlockSpec((8, 128), lambda i: (0, 0)), _row_spec(D)),
                  semantics=("arbitrary",))(y, target)


PAD = 32
CCH = 256


def _conv_fwd(u, dw_w, dw_b):
    def body(a_ref, gt_ref, w_ref, b_ref, z1_ref, zp_ref):
        zp_ref[0:PAD, :] = jnp.zeros((PAD, 128), F32)
        zp_ref[PAD:PAD + T, :] = a_ref[...] * _sigmoid(gt_ref[...])
        for c in range(T // CCH):
            acc = jnp.broadcast_to(b_ref[...], (CCH, 128))
            for j in range(KW):
                acc = acc + w_ref[j:j + 1, :] * zp_ref[pl.ds(c * CCH + j + PAD - (KW - 1), CCH), :]
            z1_ref[c * CCH:(c + 1) * CCH, :] = acc

    col = lambda off: pl.BlockSpec((T, 128), lambda j: (0, off * 4 + j))
    return _pcall(body, name="conv_fwd", out_shape=_sds((T, CONV)), grid=(CONV // 128,),
                  in_specs=[col(UB_A), col(UB_GT), pl.BlockSpec((KW, 128), lambda j: (0, j)),
                            pl.BlockSpec((1, 128), lambda j: (0, j))],
                  out_specs=col(0), scratch=[pltpu.VMEM((T + PAD, 128), F32)],
                  semantics=("parallel",))(u, u, dw_w, dw_b)


def _ln_silu_fwd(z1, g, b):
    def body(z_ref, g_ref, b_ref, o_ref):
        z = z_ref[...]
        mu = jnp.mean(z, axis=-1, keepdims=True)
        zc = z - mu
        zh = zc * lax.rsqrt(jnp.mean(zc * zc, axis=-1, keepdims=True) + EPS)
        z2 = zh * g_ref[...] + b_ref[...]
        o_ref[...] = (z2 * _sigmoid(z2)).astype(BF16)

    return _pcall(body, name="ln_silu_fwd", out_shape=_sds((T, CONV), BF16), grid=(T // TB,),
                  in_specs=[_row_spec(CONV), _vec_spec(CONV), _vec_spec(CONV)], out_specs=_row_spec(CONV),
                  semantics=("parallel",))(z1, g, b)


def _ln_silu_bwd(z1, g, b, dz3):
    def body(z_ref, g_ref, b_ref, d_ref, z3_ref, dz1_ref, dg_ref, db_ref):
        z = z_ref[...]
        mu = jnp.mean(z, axis=-1, keepdims=True)
        zc = z - mu
        rs = lax.rsqrt(jnp.mean(zc * zc, axis=-1, keepdims=True) + EPS)
        zh = zc * rs
        z2 = zh * g_ref[...] + b_ref[...]
        s = _sigmoid(z2)
        z3_ref[...] = (z2 * s).astype(BF16)
        dz2 = d_ref[...] * (s * (1.0 + z2 * (1.0 - s)))
        dzh = dz2 * g_ref[...]
        dz1_ref[...] = rs * (dzh - jnp.mean(dzh, axis=-1, keepdims=True)
                             - zh * jnp.mean(dzh * zh, axis=-1, keepdims=True))

        @pl.when(pl.program_id(0) == 0)
        def _():
            dg_ref[...] = jnp.zeros_like(dg_ref)
            db_ref[...] = jnp.zeros_like(db_ref)

        dg_ref[...] += jnp.sum(dz2 * zh, axis=0, keepdims=True)
        db_ref[...] += jnp.sum(dz2, axis=0, keepdims=True)

    return _pcall(body, name="ln_silu_bwd",
                  out_shape=(_sds((T, CONV), BF16), _sds((T, CONV)), _sds((1, CONV)), _sds((1, CONV))),
                  grid=(T // TB,),
                  in_specs=[_row_spec(CONV), _vec_spec(CONV), _vec_spec(CONV), _row_spec(CONV)],
                  out_specs=(_row_spec(CONV), _row_spec(CONV), _vec_spec(CONV), _vec_spec(CONV)),
                  semantics=("arbitrary",))(z1, g, b, dz3)


def _conv_bwd(u, dw_w, dz1):
    def body(a_ref, gt_ref, w_ref, dz1_ref, da_ref, dgt_ref, dw_ref, db_ref, zp_ref, dp_ref):
        a = a_ref[...]
        s = _sigmoid(gt_ref[...])
        zp_ref[0:PAD, :] = jnp.zeros((PAD, 128), F32)
        zp_ref[PAD:PAD + T, :] = a * s
        dp_ref[0:T, :] = dz1_ref[...]
        dp_ref[T:T + PAD, :] = jnp.zeros((PAD, 128), F32)
        db_ref[...] = jnp.sum(dz1_ref[...], axis=0, keepdims=True)
        for j in range(KW):
            tot = jnp.zeros((1, 128), F32)
            for c in range(T // CCH):
                tot = tot + jnp.sum(dz1_ref[c * CCH:(c + 1) * CCH, :]
                                    * zp_ref[pl.ds(c * CCH + j + PAD - (KW - 1), CCH), :], axis=0, keepdims=True)
            dw_ref[j:j + 1, :] = tot
        for c in range(T // CCH):
            acc = jnp.zeros((CCH, 128), F32)
            for j in range(KW):
                acc = acc + w_ref[j:j + 1, :] * dp_ref[pl.ds(c * CCH + (KW - 1) - j, CCH), :]
            rows = slice(c * CCH, (c + 1) * CCH)
            sc = _sigmoid(gt_ref[rows, :])
            da_ref[rows, :] = (acc * sc).astype(BF16)
            dgt_ref[rows, :] = (acc * a_ref[rows, :] * sc * (1.0 - sc)).astype(BF16)

    col = lambda off: pl.BlockSpec((T, 128), lambda j: (0, off * 4 + j))
    wspec = pl.BlockSpec((KW, 128), lambda j: (0, j))
    return _pcall(body, name="conv_bwd",
                  out_shape=(_sds((T, CONV), BF16), _sds((T, CONV), BF16), _sds((KW, CONV)), _sds((1, CONV))),
                  grid=(CONV // 128,), in_specs=[col(UB_A), col(UB_GT), wspec, col(0)],
                  out_specs=(col(0), col(0), wspec, pl.BlockSpec((1, 128), lambda j: (0, j))),
                  scratch=[pltpu.VMEM((T + PAD, 128), F32), pltpu.VMEM((T + PAD, 128), F32)],
                  semantics=("parallel",))(u, u, dw_w, dz1)


def _bucket_tables():
    qi = np.arange(BLK)[:, None]
    kj = np.arange(2 * BLK)[None, :]
    off = np.clip(qi + BLK - kj, 0, BLK)
    out = []
    for d in DIL:
        dist = (off * d).astype(np.int32)
        nf = np.maximum(dist, 1).astype(np.float32)
        large = 16 + (np.log(nf / np.float32(16)) / np.float32(math.log(2048 / 16)) * np.float32(16)).astype(np.int32)
        large = np.minimum(large, NBUCKET - 1)
        out.append(np.where(dist < 16, dist, large))
    return np.stack(out).astype(np.int32)


def _band():
    off = lax.broadcasted_iota(jnp.int32, (BLK, 2 * BLK), 0) + BLK - lax.broadcasted_iota(jnp.int32, (BLK, 2 * BLK), 1)
    return (off >= 0) & (off <= BLK)


def _bias_table(rel_bias_t, buckets):
    def body(rb_ref, bk_ref, o_ref):
        h = pl.program_id(0)
        bk = bk_ref[...]
        acc = jnp.zeros((BLK, 2 * BLK), F32)
        for b in range(NBUCKET):
            acc = jnp.where(bk == b, rb_ref[h, b], acc)
        o_ref[...] = jnp.where(_band(), acc, NEG)

    return _pcall(body, name="bias_table", out_shape=_sds((3 * 8, BLK, 2 * BLK)), grid=(24,),
                  in_specs=[pl.BlockSpec(memory_space=pltpu.SMEM),
                            pl.BlockSpec((None, BLK, 2 * BLK), lambda h: (h // 8, 0, 0))],
                  out_specs=pl.BlockSpec((None, BLK, 2 * BLK), lambda h: (h, 0, 0)),
                  semantics=("parallel",))(rel_bias_t, buckets)


def _bias_grad(ds_acc, buckets):
    def body(a_ref, bk_ref, o_ref):
        acc = a_ref[0]
        for l in range(1, DEPTH):
            acc = acc + a_ref[l]
        bk = bk_ref[...]
        lane = lax.broadcasted_iota(jnp.int32, (1, 128), 1)
        row = jnp.zeros((1, 128), F32)
        for b in range(NBUCKET):
            row = jnp.where(lane == b, jnp.sum(jnp.where(bk == b, acc, 0.0)), row)
        o_ref[...] = row

    return _pcall(body, name="bias_grad", out_shape=_sds((24, 1, 128)), grid=(24,),
                  in_specs=[pl.BlockSpec((DEPTH, None, BLK, 2 * BLK), lambda h: (0, h, 0, 0)),
                            pl.BlockSpec((None, BLK, 2 * BLK), lambda h: (h // 8, 0, 0))],
                  out_specs=pl.BlockSpec((None, 1, 128), lambda h: (h, 0, 0)),
                  semantics=("parallel",))(ds_acc, buckets)


def _head_mask():
    return lax.broadcasted_iota(jnp.int32, (1, 128), 1) < HD


def _seg_ones(width):
    r = lax.broadcasted_iota(jnp.int32, (width, width), 0) >> 6
    c = lax.broadcasted_iota(jnp.int32, (width, width), 1) >> 6
    return (r == c).astype(BF16)


def _seg_sum(x, ones):
    hi = x.astype(BF16)
    lo = (x - hi.astype(F32)).astype(BF16)
    return (jnp.dot(hi, ones, preferred_element_type=F32) + jnp.dot(lo, ones, preferred_element_type=F32))


def _dot(a, b, dims):
    return lax.dot_general(a, b, dims, preferred_element_type=F32)


def _tile_rows(d, r, n):
    stride = None if d == 1 else d
    q_rows = pl.ds(r + d * n * BLK, BLK, stride=stride)
    if n == 0:
        return q_rows, q_rows, BLK
    return q_rows, pl.ds(r + d * (n - 1) * BLK, 2 * BLK, stride=stride), 2 * BLK


NCH = 256


def _qk_norm_prep(q_ref, k_ref, gq_ref, gk_ref, qn_ref, kn_ref, ones):
    def prep(i, carry):
        rows = pl.ds(pl.multiple_of(i * NCH, NCH), NCH)
        q = q_ref[rows, :]
        qn_ref[rows, :] = q * lax.rsqrt(_seg_sum(q * q, ones) * (1.0 / HD) + EPS) * gq_ref[...] * (HD ** -0.5)
        k = k_ref[rows, :]
        kn_ref[rows, :] = k * lax.rsqrt(_seg_sum(k * k, ones) * (1.0 / HD) + EPS) * gk_ref[...]
        return carry

    lax.fori_loop(0, T // NCH, prep, 0)


def _attn_specs(g):
    ucol = lambda base: pl.BlockSpec((T, 128), lambda hp: (0, (base + g) * 4 + hp))
    col = pl.BlockSpec((T, 128), lambda hp: (0, hp))
    vec = pl.BlockSpec((1, 128), lambda hp: (0, 0))
    bm = pl.BlockSpec((2, BLK, 2 * BLK), lambda hp: (g * 4 + hp, 0, 0))
    return ucol, col, vec, bm


def _attn_fwd(g, u, gq, gk, bm):
    d = DIL[g]

    def body(q_ref, k_ref, v_ref, gq_ref, gk_ref, bm_ref, o_ref, lse_ref, qn_ref, kn_ref):
        ones = _seg_ones(128)
        _qk_norm_prep(q_ref, k_ref, gq_ref, gk_ref, qn_ref, kn_ref, ones)
        m_a = _head_mask()
        for r in range(d):
            for n in range(T // d // BLK):
                q_rows, k_rows, nk = _tile_rows(d, r, n)
                qt = qn_ref[q_rows, :]
                kt = kn_ref[k_rows, :].astype(BF16)
                vt = v_ref[k_rows, :].astype(BF16)
                o_t = None
                for h in range(2):
                    mh = m_a if h == 0 else jnp.logical_not(m_a)
                    qh = jnp.where(mh, qt, 0.0).astype(BF16)
                    s = _dot(qh, kt, NT) + bm_ref[h, :, 2 * BLK - nk:]
                    mx = jnp.max(s, axis=1, keepdims=True)
                    p = jnp.exp(s - mx)
                    l = jnp.sum(p, axis=1, keepdims=True)
                    o_h = _dot(p.astype(BF16), vt, NN) / l
                    lse_h = jnp.broadcast_to(mx + jnp.log(l), (BLK, 128))
                    if h == 0:
                        o_t, lse_t = o_h, lse_h
                    else:
                        o_t = jnp.where(m_a, o_t, o_h)
                        lse_t = jnp.where(m_a, lse_t, lse_h)
                o_ref[q_rows, :] = o_t
                lse_ref[q_rows, :] = lse_t

    ucol, col, vec, bmspec = _attn_specs(g)
    return _pcall(body, name=f"attn_fwd_g{g}", out_shape=(_sds((T, AOUT)), _sds((T, AOUT))), grid=(4,),
                  in_specs=[ucol(UB_Q), ucol(UB_K), ucol(UB_V), vec, vec, bmspec], out_specs=(col, col),
                  scratch=[pltpu.VMEM((T, 128), F32), pltpu.VMEM((T, 128), F32)],
                  semantics=("parallel",))(u, u, u, gq, gk, bm)


def _attn_bwd(g, u, gq, gk, bm, dog, cb, lse):
    d = DIL[g]

    def body(q_ref, k_ref, v_ref, gq_ref, gk_ref, bm_ref, do_ref, cb_ref, lse_ref,
             dqo_ref, dko_ref, dvo_ref, dgq_ref, dgk_ref, dsa_ref, qn_ref, kn_ref, dq_ref, dk_ref, dv_ref):
        ones = _seg_ones(128)
        _qk_norm_prep(q_ref, k_ref, gq_ref, gk_ref, qn_ref, kn_ref, ones)
        m_a = _head_mask()
        dk_ref[...] = jnp.zeros_like(dk_ref)
        dv_ref[...] = jnp.zeros_like(dv_ref)
        dsa_ref[...] = jnp.zeros_like(dsa_ref)
        for r in range(d):
            for n in range(T // d // BLK):
                q_rows, k_rows, nk = _tile_rows(d, r, n)
                qt = qn_ref[q_rows, :]
                kt = kn_ref[k_rows, :]
                ktb = kt.astype(BF16)
                vtb = v_ref[k_rows, :].astype(BF16)
                do_t = do_ref[q_rows, :]
                c_t = cb_ref[q_rows, :]
                lse_t = lse_ref[q_rows, :]
                dq_t = jnp.zeros((BLK, 128), F32)
                dk_t = jnp.zeros((nk, 128), F32)
                dv_t = jnp.zeros((nk, 128), F32)
                for h in range(2):
                    mh = m_a if h == 0 else jnp.logical_not(m_a)
                    qh = jnp.where(mh, qt, 0.0).astype(BF16)
                    kh = jnp.where(mh, kt, 0.0).astype(BF16)
                    doh = jnp.where(mh, do_t, 0.0).astype(BF16)
                    lse_c = jnp.max(jnp.where(mh, lse_t, -3e38), axis=1, keepdims=True)
                    c_c = jnp.max(jnp.where(mh, c_t, -3e38), axis=1, keepdims=True)
                    s = _dot(qh, ktb, NT) + bm_ref[h, :, 2 * BLK - nk:]
                    p = jnp.exp(s - lse_c)
                    dp = _dot(doh, vtb, NT)
                    ds = p * (dp + c_c)
                    dsb = ds.astype(BF16)
                    dv_t = dv_t + _dot(p.astype(BF16), doh, TN)
                    dq_t = dq_t + _dot(dsb, kh, NN)
                    dk_t = dk_t + _dot(dsb, qh, TN)
                    dsa_ref[h, :, 2 * BLK - nk:] += ds
                dq_ref[q_rows, :] = dq_t
                dk_ref[k_rows, :] += dk_t
                dv_ref[k_rows, :] += dv_t

        @pl.when(pl.program_id(0) == 0)
        def _():
            dgq_ref[...] = jnp.zeros_like(dgq_ref)
            dgk_ref[...] = jnp.zeros_like(dgk_ref)

        def norm_bwd(i, carry):
            rows = pl.ds(pl.multiple_of(i * NCH, NCH), NCH)
            for x_ref, g_ref, dx_ref, dxo_ref, dg_ref, scale in (
                    (q_ref, gq_ref, dq_ref, dqo_ref, dgq_ref, HD ** -0.5), (k_ref, gk_ref, dk_ref, dko_ref, dgk_ref, 1.0)):
                x = x_ref[rows, :]
                rs = lax.rsqrt(_seg_sum(x * x, ones) * (1.0 / HD) + EPS)
                xh = x * rs
                dn = dx_ref[rows, :] * scale
                dxh = dn * g_ref[...]
                dxo_ref[rows, :] = (rs * (dxh - xh * (_seg_sum(dxh * xh, ones) * (1.0 / HD)))).astype(BF16)
                dg_ref[...] += jnp.sum(dn * xh, axis=0, keepdims=True)
            dvo_ref[rows, :] = dv_ref[rows, :].astype(BF16)
            return carry

        lax.fori_loop(0, T // NCH, norm_bwd, 0)

    ucol, col, vec, bmspec = _attn_specs(g)
    return _pcall(body, name=f"attn_bwd_g{g}",
                  out_shape=(_sds((T, AOUT), BF16), _sds((T, AOUT), BF16), _sds((T, AOUT), BF16), _sds((1, 128)),
                             _sds((1, 128)), _sds((8, BLK, 2 * BLK))),
                  grid=(4,),
                  in_specs=[ucol(UB_Q), ucol(UB_K), ucol(UB_V), vec, vec, bmspec, col, col, col],
                  out_specs=(col, col, col, vec, vec, pl.BlockSpec((2, BLK, 2 * BLK), lambda hp: (hp, 0, 0))),
                  scratch=[pltpu.VMEM((T, 128), F32)] * 5,
                  semantics=("arbitrary",))(u, u, u, gq, gk, bm, dog, cb, lse)


def _combine_fwd(ogs, lses):
    def body(o0, o1, o2, l0, l1, l2, o_ref):
        ls = [l0[...], l1[...], l2[...]]
        mx = jnp.maximum(jnp.maximum(ls[0], ls[1]), ls[2])
        es = [jnp.exp(l - mx) for l in ls]
        inv = 1.0 / (es[0] + es[1] + es[2])
        o_ref[...] = ((es[0] * o0[...] + es[1] * o1[...] + es[2] * o2[...]) * inv).astype(BF16)

    return _pcall(body, name="combine_fwd", out_shape=_sds((T, AOUT), BF16), grid=(T // TB,),
                  in_specs=[_row_spec(AOUT)] * 6, out_specs=_row_spec(AOUT), semantics=("parallel",))(*ogs, *lses)


def _combine_bwd(ogs, lses, do):
    def body(o0, o1, o2, l0, l1, l2, do_ref, d0, d1, d2, c0, c1, c2):
        ls = [l0[...], l1[...], l2[...]]
        mx = jnp.maximum(jnp.maximum(ls[0], ls[1]), ls[2])
        es = [jnp.exp(l - mx) for l in ls]
        inv = 1.0 / (es[0] + es[1] + es[2])
        ws = [e * inv for e in es]
        do = do_ref[...]
        o = ws[0] * o0[...] + ws[1] * o1[...] + ws[2] * o2[...]
        s = _seg_sum(do * o, _seg_ones(AOUT))
        for w, d_ref, c_ref in zip(ws, (d0, d1, d2), (c0, c1, c2)):
            d_ref[...] = w * do
            c_ref[...] = -(w * s)

    return _pcall(body, name="combine_bwd", out_shape=tuple(_sds((T, AOUT)) for _ in range(6)), grid=(T // TB,),
                  in_specs=[_row_spec(AOUT)] * 7, out_specs=tuple(_row_spec(AOUT) for _ in range(6)),
                  semantics=("parallel",))(*ogs, *lses, do)


def _layer_fwd(x, p, bm, deps=(), mid=None):
    h1 = _rms_fwd(x, p["n1g"])
    u = _mm_x_wcols("mm_u", h1, p["win4"], tm=T, tn=640, deps=deps)
    z1 = _conv_fwd(u, p["dww"], p["dwb"])
    z3 = _ln_silu_fwd(z1, p["lng"], p["lnb"])
    ycv = _mm_x_wcols("mm_ycv", z3, p["wco4"], tm=T, tn=256)
    ogs, lses = [], []
    for g in range(NG):
        og, lse = _attn_fwd(g, u, p["gq"], p["gk"], bm)
        ogs.append(og)
        lses.append(lse)
    o = _combine_fwd(ogs, lses)
    yat = _mm_x_wcols("mm_yat", o, p["wao4"], tm=T, tn=256)
    m = _gate_fwd(u, ycv, yat)
    tok = mid(m) if mid else None
    xm = _mm_x_wrows("mm_xmid", m, p["wout4"], x, tm=1024, tk=256, tn=1024, deps=(tok,))
    h2 = _rms_fwd(xm, p["n2g"])
    f, r = _mm_ff1(h2, p["wff14"], tm=T, tn=512)
    xo = _mm_x_wrows("mm_xout", r, p["wff24"], xm, tm=1024, tk=1024, tn=1024)
    saved = dict(x=x, h1=h1, u=u, z1=z1, ogs=ogs, lses=lses, o=o, ycv=ycv, yat=yat, m=m, xm=xm, h2=h2, f=f)
    return xo, saved


EARLY = ("w_ff2", "w_ff1", "w_out")
LATE = ("w_conv_out", "w_attn_out", "w_in")


def _layer_bwd(dx, s, p, bm, pipe=None, own_early=None):
    u = s["u"]
    tok = pipe.step0() if pipe else None
    r, df = _mm_dff2(dx, p["wff24"], s["f"], tm=1024, tn=512, deps=(tok,))
    g_ff2 = _mm_dw_rows("mm_dwff2", r, dx, ks=1024, tm=1024, tn=512)
    g_ff1 = _mm_dw_cols("mm_dwff1", s["h2"], df, ns=1024, tm=1024, tn=512)
    tok = pipe.step1(g_ff1) if pipe else None
    dh2 = _mm_g_wcols_t("mm_dh2", df, p["wff14"], tm=1024, tk=512, tn=1024, deps=(tok,))
    dxm, d_n2g = _rms_bwd(s["xm"], p["n2g"], dh2, dx)

    dm = _mm_g_wrows_t("mm_dm", dxm, p["wout4"], tm=1024, tn=256)
    g_out = _mm_dw_rows("mm_dwout", s["m"], dxm, ks=256, tm=256, tn=512)
    early = own_early(dict(w_ff2=g_ff2, w_ff1=g_ff1, w_out=g_out)) if own_early else None
    tok_e = early.step0() if early else None
    dyc, dya, dgc, dga = _gate_bwd(u, s["ycv"], s["yat"], dm)

    dz3 = _mm_g_wcols_t("mm_dz3", dyc, p["wco4"], tm=T, tk=256, tn=512, deps=(tok_e,))
    z3, dz1, d_lng, d_lnb = _ln_silu_bwd(s["z1"], p["lng"], p["lnb"], dz3)
    g_co = _mm_dw_cols("mm_dwco", z3, dyc, ns=256, tm=512, tn=256)
    da, dgt, d_dww, d_dwb = _conv_bwd(u, p["dww"], dz1)

    tok_e = early.step1(da) if early else None
    do = _mm_g_wcols_t("mm_do", dya, p["wao4"], tm=T, tk=256, tn=512, deps=(tok_e,))
    g_ao = _mm_dw_cols("mm_dwao", s["o"], dya, ns=256, tm=512, tn=256)
    parts = _combine_bwd(s["ogs"], s["lses"], do)
    dqs, dks, dvs, d_gq, d_gk, dsas = [], [], [], [], [], []
    for g in range(NG):
        dq, dk, dv, dgq, dgk, dsa = _attn_bwd(g, u, p["gq"], p["gk"], bm, parts[g], parts[NG + g], s["lses"][g])
        dqs.append(dq)
        dks.append(dk)
        dvs.append(dv)
        d_gq.append(dgq)
        d_gk.append(dgk)
        dsas.append(dsa)
    du = jnp.concatenate([da, dgt] + dqs + dks + dvs + [dgc, dga], axis=1)
    tok = pipe.step2(du) if pipe else None
    tok_e = early.step2(du) if early else None
    g_in = _mm_dw_cols("mm_dwin", s["h1"], du, ns=1920, tm=1024, tn=640, deps=(tok, tok_e))
    dh1 = _mm_g_wcols_t("mm_dh1", du, p["win4"], tm=1024, tk=640, tn=1024)
    dxi, d_n1g = _rms_bwd(s["x"], p["n1g"], dh1, dxm)
    if pipe:
        pipe.step3(dxi)
    if early:
        early.step3(dxi)

    fold = lambda parts_: sum(v[0, :HD] + v[0, HD:] for v in parts_)
    big = dict(w_in=g_in, w_conv_out=g_co, w_attn_out=g_ao, w_out=g_out, w_ff1=g_ff1, w_ff2=g_ff2)
    small = dict(norm1_g=d_n1g[0], q_norm_g=fold(d_gq), k_norm_g=fold(d_gk), conv_dw_w=d_dww, conv_dw_b=d_dwb[0],
                 conv_ln_g=d_lng[0], conv_ln_b=d_lnb[0], norm2_g=d_n2g[0])
    return dxi, big, small, jnp.concatenate(dsas, axis=0)


def _local_step(x, target, get_layer, rel_bias, make_pipe):
    buckets = jnp.asarray(_bucket_tables())
    bm = _bias_table(rel_bias.T, buckets)
    saved, layers = [], []
    for l in range(DEPTH):
        p, deps, mid = get_layer(l, x)
        layers.append(p)
        x, s = _layer_fwd(x, p, bm, deps=deps, mid=mid)
        saved.append(s)
    loss_blk, dx = _loss_fwd_bwd(x, target)
    smalls, dsas = [None] * DEPTH, [None] * DEPTH
    pipe = None
    for l in reversed(range(DEPTH)):
        own = (lambda big: make_pipe(0, EARLY, "e", big)) if l == 0 else None
        dx, big, smalls[l], dsas[l] = _layer_bwd(dx, saved[l], layers[l], bm, pipe, own)
        pipe = make_pipe(l, LATE if l == 0 else BIG, "", big)
    pipe.finish()
    d_rel = _bias_grad(jnp.stack(dsas), buckets)[:, 0, :NBUCKET].T
    return loss_blk[0, 0], dx, smalls, d_rel


MESH = pl.DeviceIdType.MESH


def _me():
    return lax.axis_index("x"), lax.axis_index("y"), lax.axis_index("c")


def _other_chips(mx, my):
    return [(1 - mx, my), (mx, 1 - my), (1 - mx, 1 - my)]


def _rcopy(src, dst, send_sems, recv_sems, k, dev):
    return pltpu.make_async_remote_copy(src_ref=src, dst_ref=dst, send_sem=send_sems.at[k], recv_sem=recv_sems.at[k],
                                        device_id=dev, device_id_type=MESH)


def _comm_call(body, name, out_shape, n_in, n_sems):
    return pl.pallas_call(
        body, name=name, out_shape=out_shape, in_specs=[HBM_SPEC] * n_in,
        out_specs=jax.tree.map(lambda _: HBM_SPEC, out_shape),
        scratch_shapes=[pltpu.SemaphoreType.DMA((n_sems,)), pltpu.SemaphoreType.DMA((n_sems,)),
                        pltpu.SemaphoreType.DMA(())],
        compiler_params=pltpu.CompilerParams(has_side_effects=True))


def _all_gather_chips(x, name):
    def body(x_ref, o_ref, send_sems, recv_sems, local_sem):
        mx, my, mc = _me()
        local = pltpu.make_async_copy(x_ref, o_ref.at[2 * mx + my], local_sem)
        local.start()
        sends = [_rcopy(x_ref, o_ref.at[2 * mx + my], send_sems, recv_sems, k, (px, py, mc))
                 for k, (px, py) in enumerate(_other_chips(mx, my))]
        for cp in sends:
            cp.start()
        for k, (px, py) in enumerate(_other_chips(mx, my)):
            _rcopy(x_ref, o_ref.at[2 * px + py], send_sems, recv_sems, k, (px, py, mc)).wait_recv()
        for cp in sends:
            cp.wait_send()
        local.wait()

    return _comm_call(body, name, _sds((NCHIP,) + x.shape, x.dtype), 1, 3)(x)


EFFECT = pltpu.SideEffectType.DATAFLOW_SIDE_EFFECTING


def _hbm(a):
    return pltpu.with_memory_space_constraint(a, pltpu.HBM)


def _split_start(name, bufs, plan, n, after=None):
    nb = len(bufs)
    extra = [] if after is None else [after]
    ne = len(extra)

    def body(*refs):
        send_sems, recv_sems, token = refs[nb + ne], refs[nb + ne + 1], refs[-1]
        mx, my, mc = _me()
        for k, (src, dst, dev, _) in enumerate(plan(refs[:nb], mx, my, mc)):
            _rcopy(src, dst, send_sems, recv_sems, k, dev).start()
        token[...] = jnp.zeros_like(token)

    out = pl.pallas_call(
        body, name=name,
        out_shape=(pltpu.SemaphoreType.DMA((n,)), pltpu.SemaphoreType.DMA((n,)),
                   *[pltpu.HBM(b.shape, b.dtype) for b in bufs], _sds((8, 128))),
        in_specs=[HBM_SPEC] * nb + [ANY_SPEC] * ne,
        out_specs=(SEM_SPEC, SEM_SPEC, *[HBM_SPEC] * nb, pl.BlockSpec(memory_space=pltpu.VMEM)),
        input_output_aliases={i: 2 + i for i in range(nb)},
        compiler_params=pltpu.CompilerParams(has_side_effects=EFFECT))(*[_hbm(b) for b in bufs], *extra)
    return (out[0], out[1]), list(out[2:2 + nb]), out[-1]


def _split_wait(name, sems, bufs, plan, after):
    nb = len(bufs)

    def body(*refs):
        send_sems, recv_sems = refs[nb], refs[nb + 1]
        mx, my, mc = _me()
        for k, (src, dst, dev, land) in enumerate(plan(refs[:nb], mx, my, mc)):
            _rcopy(src, dst, send_sems, recv_sems, k, dev).wait_send()
            _rcopy(src, land, send_sems, recv_sems, k, dev).wait_recv()

    out = pl.pallas_call(
        body, name=name, out_shape=tuple(pltpu.HBM(b.shape, b.dtype) for b in bufs),
        in_specs=[HBM_SPEC] * nb + [SEM_SPEC, SEM_SPEC, ANY_SPEC], out_specs=(HBM_SPEC,) * nb,
        input_output_aliases={i: i for i in range(nb)},
        compiler_params=pltpu.CompilerParams(has_side_effects=EFFECT))(*bufs, sems[0], sems[1], after)
    return list(out)


def _plan_gather_chips(refs, mx, my, mc):
    me = 2 * mx + my
    return [(r.at[me, mc], r.at[me, mc], (px, py, mc), r.at[2 * px + py, mc])
            for r in refs for px, py in _other_chips(mx, my)]


def _plan_gather_pair(refs, mx, my, mc):
    return [(r.at[2 * px + py, mc], r.at[2 * px + py, mc], (mx, my, 1 - mc), r.at[2 * px + py, 1 - mc])
            for r in refs for px, py in _other_chips(mx, my)]


def _plan_pair_half(refs, mx, my, mc):
    n = len(refs) // 2
    return [(g.at[:, 1 - mc], r, (mx, my, 1 - mc), r) for g, r in zip(refs[:n], refs[n:])]


def _plan_scatter(refs, mx, my, mc):
    n = len(refs) // 2
    return [(q.at[2 * px + py], r.at[k], (px, py, mc), r.at[k])
            for q, r in zip(refs[:n], refs[n:]) for k, (px, py) in enumerate(_other_chips(mx, my))]


def _plan_pair_fill(refs, mx, my, mc):
    return [(r.at[mc], r.at[mc], (mx, my, 1 - mc), r.at[1 - mc]) for r in refs]


def _all_gather_devices(v, name):
    def body(v_ref, o_ref, send_sems, recv_sems, local_sem):
        mx, my, mc = _me()
        flip = lambda m, b: 1 - m if b else m
        peers = [(flip(mx, k >> 2 & 1), flip(my, k >> 1 & 1), flip(mc, k & 1)) for k in range(1, 8)]
        slot = lambda d: 4 * d[0] + 2 * d[1] + d[2]
        local = pltpu.make_async_copy(v_ref, o_ref.at[slot((mx, my, mc))], local_sem)
        local.start()
        sends = [_rcopy(v_ref, o_ref.at[slot((mx, my, mc))], send_sems, recv_sems, k, dev)
                 for k, dev in enumerate(peers)]
        for cp in sends:
            cp.start()
        for k, dev in enumerate(peers):
            _rcopy(v_ref, o_ref.at[slot(dev)], send_sems, recv_sems, k, dev).wait_recv()
        for cp in sends:
            cp.wait_send()
        local.wait()

    return _comm_call(body, name, _sds((8,) + v.shape, v.dtype), 1, 7)(v)


def _row_tile(rows, cols):
    t = 8
    while t * 2 * cols * 4 <= (1 << 20) and rows % (t * 2) == 0:
        t *= 2
    return t


def _prefetch_call(body, name, out_shape, grid, in_specs, out_specs):
    return pl.pallas_call(
        body, name=name, out_shape=out_shape,
        grid_spec=pltpu.PrefetchScalarGridSpec(num_scalar_prefetch=1, grid=grid, in_specs=in_specs,
                                               out_specs=out_specs),
        compiler_params=pltpu.CompilerParams(vmem_limit_bytes=VMEM_LIMIT,
                                             dimension_semantics=("parallel",) * len(grid)))


def _sum_half(g, r1, place, name):
    _, _, rr, ns = g.shape
    tr = _row_tile(rr, ns)

    def body(c_ref, g_ref, r_ref, o_ref, ob_ref):
        q = g_ref[...] + r_ref[...]
        o_ref[...] = q
        ob_ref[...] = q.astype(BF16)

    blk = pl.BlockSpec((None, tr, ns), lambda s, i, c: (s, i, 0))
    return _prefetch_call(body, name, (_sds((NCHIP, rr, ns)), _sds((NCHIP, rr, ns), BF16)), (NCHIP, rr // tr),
                          [pl.BlockSpec((None, None, tr, ns), lambda s, i, c: (s, c[1], i, 0)), blk],
                          (blk, blk))(place, g, r1)


def _sum_recv(q, r2, place, name):
    _, rr, ns = q.shape
    tr = _row_tile(rr, ns)

    def body(c_ref, q_ref, r_ref, o_ref):
        o_ref[...] = ((q_ref[...] + r_ref[0].astype(F32)) + r_ref[1].astype(F32)) + r_ref[2].astype(F32)

    return _prefetch_call(body, name, _sds((2, rr, ns)), (rr // tr,),
                          [pl.BlockSpec((None, tr, ns), lambda i, c: (c[0], i, 0)),
                           pl.BlockSpec((NCHIP - 1, tr, ns), lambda i, c: (0, i, 0))],
                          pl.BlockSpec((None, tr, ns), lambda i, c: (c[1], i, 0)))(place, q, r2)


def _sum_devices(v8):
    def body(v_ref, o_ref):
        acc = v_ref[0]
        for dev in range(1, 8):
            acc = acc + v_ref[dev]
        o_ref[...] = acc

    return _pcall(body, name="sum_devices", out_shape=_sds(v8.shape[1:]))(v8)


def _adamw(w, g, m, v, name):
    rows, cols = w.shape
    tr = _row_tile(rows, cols)

    def body(w_ref, g_ref, m_ref, v_ref, d_ref, m2_ref, v2_ref):
        g = g_ref[...]
        m2 = ADAM_B1 * m_ref[...] + (1.0 - ADAM_B1) * g
        v2 = ADAM_B2 * v_ref[...] + (1.0 - ADAM_B2) * (g * g)
        m_hat = m2 / (1.0 - ADAM_B1 ** ADAM_STEP)
        v_hat = v2 / (1.0 - ADAM_B2 ** ADAM_STEP)
        d_ref[...] = -ADAM_LR * (m_hat / (jnp.sqrt(v_hat) + ADAM_EPS) + ADAM_WD * w_ref[...])
        m2_ref[...] = m2
        v2_ref[...] = v2

    blk = pl.BlockSpec((tr, cols), lambda i: (i, 0))
    return _pcall(body, name=name, out_shape=(_sds((rows, cols)),) * 3, grid=(rows // tr,), in_specs=[blk] * 4,
                  out_specs=(blk,) * 3, semantics=("parallel",))(w, g, m, v)


BIG = ("w_in", "w_conv_out", "w_attn_out", "w_out", "w_ff1", "w_ff2")
SMALL = ("rel_bias", "norm1_g", "q_norm_g", "k_norm_g", "conv_dw_w", "conv_dw_b", "conv_ln_g", "conv_ln_b", "norm2_g")
WEIGHTS = ("rel_bias", "norm1_g", "w_in", "q_norm_g", "k_norm_g", "conv_dw_w", "conv_dw_b", "conv_ln_g", "conv_ln_b",
           "w_conv_out", "w_attn_out", "w_out", "norm2_g", "w_ff1", "w_ff2")


def _pack(arrays):
    flat = jnp.concatenate([a.reshape(-1) for a in arrays])
    n = flat.shape[0]
    rows = -(-n // 1024) * 8
    return jnp.pad(flat, (0, rows * 128 - n)).reshape(rows, 128)


def _unpack(packed, shapes):
    flat = packed.reshape(-1)
    out, off = [], 0
    for shp in shapes:
        n = int(np.prod(shp))
        out.append(flat[off:off + n].reshape(shp))
        off += n
    return out


def _adamw_layer(l, w, g, m, v, prev, name):
    _, k, n = w.shape
    tr = _row_tile(k, n)
    if prev is None:
        prev = tuple(lax.empty(w.shape, F32) for _ in range(4))

    def body(w_ref, g_ref, m_ref, v_ref, p0, p1, p2, p3, go_ref, d_ref, m2_ref, v2_ref):
        g = g_ref[...]
        m2 = ADAM_B1 * m_ref[...] + (1.0 - ADAM_B1) * g
        v2 = ADAM_B2 * v_ref[...] + (1.0 - ADAM_B2) * (g * g)
        m_hat = m2 / (1.0 - ADAM_B1 ** ADAM_STEP)
        v_hat = v2 / (1.0 - ADAM_B2 ** ADAM_STEP)
        go_ref[...] = g
        d_ref[...] = -ADAM_LR * (m_hat / (jnp.sqrt(v_hat) + ADAM_EPS) + ADAM_WD * w_ref[...])
        m2_ref[...] = m2
        v2_ref[...] = v2

    lay = pl.BlockSpec((None, tr, n), lambda i: (l, i, 0))
    return _pcall(body, name=name, out_shape=(_sds(w.shape),) * 4, grid=(k // tr,),
                  in_specs=[lay, pl.BlockSpec((tr, n), lambda i: (i, 0)), lay, lay] + [ANY_SPEC] * 4,
                  out_specs=(lay,) * 4, aliases={4: 0, 5: 1, 6: 2, 7: 3},
                  semantics=("parallel",))(w, g, m, v, *prev)


class _GradPipe:
    def __init__(self, l, kinds, tag, big, place, w, m, v, results):
        self.l, self.kinds, self.place, self.w, self.m, self.v, self.results = l, kinds, place, w, m, v, results
        self.id = f"l{l}{tag}"
        self.g = [big[n].reshape(NCHIP, 2, big[n].shape[1] // 2, big[n].shape[2]) for n in kinds]

    def step0(self):
        lands = [lax.empty((NCHIP,) + g.shape[2:], F32) for g in self.g]
        self.s1, self.b1, tok = _split_start(f"rs1_start_{self.id}", self.g + lands, _plan_pair_half, len(self.kinds))
        return tok

    def step1(self, after):
        nk = len(self.kinds)
        bufs = _split_wait(f"rs1_wait_{self.id}", self.s1, self.b1, _plan_pair_half, after)
        sums = [_sum_half(bufs[i], bufs[nk + i], self.place, f"rs1_sum_{n}") for i, n in enumerate(self.kinds)]
        self.q = [q for q, _ in sums]
        qb = [b for _, b in sums]
        lands = [lax.empty((NCHIP - 1,) + b.shape[1:], BF16) for b in qb]
        self.s2, self.b2, tok = _split_start(f"rs2_start_{self.id}", qb + lands, _plan_scatter, 3 * nk)
        return tok

    def step2(self, after):
        nk = len(self.kinds)
        bufs = _split_wait(f"rs2_wait_{self.id}", self.s2, self.b2, _plan_scatter, after)
        fin = [_sum_recv(self.q[i], bufs[nk + i], self.place, f"rs2_sum_{n}") for i, n in enumerate(self.kinds)]
        self.s3, self.b3, tok = _split_start(f"rs3_start_{self.id}", fin, _plan_pair_fill, nk)
        return tok

    def step3(self, after):
        fin = _split_wait(f"rs3_wait_{self.id}", self.s3, self.b3, _plan_pair_fill, after)
        for i, n in enumerate(self.kinds):
            g2 = fin[i].reshape(fin[i].shape[1] * 2, fin[i].shape[2])
            self.results[n] = _adamw_layer(self.l, self.w[n], g2, self.m[n], self.v[n], self.results.get(n),
                                           f"adamw_{n}_l{self.l}")

    def finish(self):
        self.step3(self.step2(self.step1(self.step0())))


def kernel(x, rel_bias, norm1_g, w_in, q_norm_g, k_norm_g, conv_dw_w, conv_dw_b, conv_ln_g, conv_ln_b, w_conv_out, w_attn_out, w_out, norm2_g, w_ff1, w_ff2, loss_target, m_rel_bias, m_norm1_g, m_w_in, m_q_norm_g, m_k_norm_g, m_conv_dw_w, m_conv_dw_b, m_conv_ln_g, m_conv_ln_b, m_w_conv_out, m_w_attn_out, m_w_out, m_norm2_g, m_w_ff1, m_w_ff2, v_rel_bias, v_norm1_g, v_w_in, v_q_norm_g, v_k_norm_g, v_conv_dw_w, v_conv_dw_b, v_conv_ln_g, v_conv_ln_b, v_w_conv_out, v_w_attn_out, v_w_out, v_norm2_g, v_w_ff1, v_w_ff2):
    w = dict(rel_bias=rel_bias, norm1_g=norm1_g, w_in=w_in, q_norm_g=q_norm_g, k_norm_g=k_norm_g, conv_dw_w=conv_dw_w,
             conv_dw_b=conv_dw_b, conv_ln_g=conv_ln_g, conv_ln_b=conv_ln_b, w_conv_out=w_conv_out,
             w_attn_out=w_attn_out, w_out=w_out, norm2_g=norm2_g, w_ff1=w_ff1, w_ff2=w_ff2)
    m = dict(rel_bias=m_rel_bias, norm1_g=m_norm1_g, w_in=m_w_in, q_norm_g=m_q_norm_g, k_norm_g=m_k_norm_g,
             conv_dw_w=m_conv_dw_w, conv_dw_b=m_conv_dw_b, conv_ln_g=m_conv_ln_g, conv_ln_b=m_conv_ln_b,
             w_conv_out=m_w_conv_out, w_attn_out=m_w_attn_out, w_out=m_w_out, norm2_g=m_norm2_g, w_ff1=m_w_ff1,
             w_ff2=m_w_ff2)
    v = dict(rel_bias=v_rel_bias, norm1_g=v_norm1_g, w_in=v_w_in, q_norm_g=v_q_norm_g, k_norm_g=v_k_norm_g,
             conv_dw_w=v_conv_dw_w, conv_dw_b=v_conv_dw_b, conv_ln_g=v_conv_ln_g, conv_ln_b=v_conv_ln_b,
             w_conv_out=v_w_conv_out, w_attn_out=v_w_attn_out, w_out=v_w_out, norm2_g=v_norm2_g, w_ff1=v_w_ff1,
             w_ff2=v_w_ff2)
    chip_id = 2 * lax.axis_index("x") + lax.axis_index("y")
    place = jnp.stack([chip_id, lax.axis_index("c")]).astype(jnp.int32)

    dww4 = _all_gather_chips(conv_dw_w, "ag_conv_dw_w")
    dww = dww4.transpose(1, 2, 0, 3).reshape(DEPTH, KW, CONV)

    ncopy = 3 * len(BIG)

    def start_chips(l, after):
        lands = []
        for n in BIG:
            k, ns = w[n].shape[1:]
            land = lax.dynamic_update_slice(lax.empty((NCHIP, k, ns), BF16), w[n][l].astype(BF16)[None], (chip_id, 0, 0))
            lands.append(land.reshape(NCHIP, 2, k // 2, ns))
        return _split_start(f"ag_chips_start_l{l}", lands, _plan_gather_chips, ncopy, after=after)

    def start_pair(l, after):
        sems, bufs, _ = chips[l]
        bufs = _split_wait(f"ag_chips_wait_l{l}", sems, bufs, _plan_gather_chips, after)
        pair[l] = _split_start(f"ag_pair_start_l{l}", bufs, _plan_gather_pair, ncopy)
        return pair[l][2]

    chips, pair = {0: start_chips(0, dww4)}, {}
    start_pair(0, chips[0][2])

    def get_layer(l, after):
        sems, bufs, tok = pair[l]
        bufs = _split_wait(f"ag_pair_wait_l{l}", sems, bufs, _plan_gather_pair, tok if l == 0 else after)
        full = {n: b.reshape(NCHIP, 2 * b.shape[2], b.shape[3]) for n, b in zip(BIG, bufs)}
        deps, mid = (), None
        if l + 1 < DEPTH:
            chips[l + 1] = start_chips(l + 1, full["w_in"])
            deps = (chips[l + 1][2],)
            mid = lambda after_: start_pair(l + 1, after_)
        return dict(
            win4=full["w_in"], wco4=full["w_conv_out"], wao4=full["w_attn_out"], wout4=full["w_out"],
            wff14=full["w_ff1"], wff24=full["w_ff2"], dww=dww[l],
            dwb=conv_dw_b[l][None], lng=conv_ln_g[l][None], lnb=conv_ln_b[l][None], n1g=norm1_g[l][None],
            n2g=norm2_g[l][None], gq=jnp.tile(q_norm_g[l], 2)[None], gk=jnp.tile(k_norm_g[l], 2)[None]), deps, mid

    results = {}
    make_pipe = lambda l, kinds, tag, big: _GradPipe(l, kinds, tag, big, place, w, m, v, results)
    loss_share, dx, smalls, d_rel = _local_step(x[0], loss_target[0], get_layer, rel_bias, make_pipe)
    loss = lax.psum(loss_share, ("x", "y", "c"))

    local_small = dict(rel_bias=d_rel)
    for n in SMALL[1:]:
        local_small[n] = jnp.stack([smalls[l][n] for l in range(DEPTH)])
    small_shapes = [local_small[n].shape for n in SMALL]
    summed = _sum_devices(_all_gather_devices(_pack([local_small[n] for n in SMALL]), "ag_small"))
    grads = dict(zip(SMALL, _unpack(summed, small_shapes)))
    grads["conv_dw_w"] = lax.dynamic_slice_in_dim(grads["conv_dw_w"], chip_id * 128, 128, axis=2)

    delta, new_m, new_v = {}, {}, {}
    small_w_shapes = [w[n].shape for n in SMALL]
    outs = _adamw(_pack([w[n] for n in SMALL]), _pack([grads[n] for n in SMALL]), _pack([m[n] for n in SMALL]),
                  _pack([v[n] for n in SMALL]), "adamw_small")
    for dst, packed in zip((delta, new_m, new_v), outs):
        dst.update(zip(SMALL, _unpack(packed, small_w_shapes)))

    for n in BIG:
        grads[n], delta[n], new_m[n], new_v[n] = results[n]

    return (loss, dx[None], *[grads[n] for n in WEIGHTS], *[delta[n] for n in WEIGHTS],
            *[new_m[n] for n in WEIGHTS], *[new_v[n] for n in WEIGHTS])
```

```python
import functools
import math

import numpy as np
import jax
import jax.numpy as jnp
from jax import lax
from jax.experimental import pallas as pl
from jax.experimental.pallas import tpu as pltpu

F32 = jnp.float32
BF16 = jnp.bfloat16

T = 2048
D = 1024
DEPTH = 4
CONV = 512
KW = 31
NG = 3
HD = 64
AOUT = 512
DFF = 4096
INC = 7680
DIL = (1, 4, 16)
BLK = 128
NBUCKET = 32
EPS = 1e-6
NEG = -1e30
NCHIP = 4
UB_A, UB_GT, UB_Q, UB_K, UB_V, UB_GC, UB_GA = 0, 1, 2, 5, 8, 11, 13

ADAM_LR, ADAM_B1, ADAM_B2, ADAM_EPS, ADAM_WD, ADAM_STEP = 0.001, 0.9, 0.999, 1e-08, 0.01, 10

VMEM_LIMIT = 48 * 1024 * 1024
TB = 256
HBM_SPEC = pl.BlockSpec(memory_space=pltpu.HBM)
ANY_SPEC = pl.BlockSpec(memory_space=pl.ANY)
SEM_SPEC = pl.BlockSpec(memory_space=pltpu.SEMAPHORE)


def _pcall(body, *, name, out_shape, grid=(), in_specs=None, out_specs=None, scratch=(), aliases=None,
           semantics=None):
    kw = {}
    if in_specs is not None:
        kw["in_specs"] = in_specs
    if out_specs is not None:
        kw["out_specs"] = out_specs
    return pl.pallas_call(
        body, name=name, out_shape=out_shape, grid=grid, scratch_shapes=scratch,
        input_output_aliases=aliases or {},
        compiler_params=pltpu.CompilerParams(vmem_limit_bytes=VMEM_LIMIT, dimension_semantics=semantics),
        **kw)


def _sds(shape, dtype=F32):
    return jax.ShapeDtypeStruct(shape, dtype)


NN = (((1,), (0,)), ((), ()))
NT = (((1,), (1,)), ((), ()))
TN = (((0,), (0,)), ((), ()))


def _mm(name, a, b, *, out_shape, out_dtype, grid, a_spec, b_spec, o_spec, acc_shape, dims, add=None,
        add_spec=None, deps=()):
    nk = grid[2]
    deps = tuple(d for d in deps if d is not None)
    n_scratch = 1 if nk > 1 else 0

    def body(*refs):
        n_out = 1 + n_scratch
        refs = refs[:len(refs) - n_out - len(deps)] + refs[len(refs) - n_out:]
        a_ref, b_ref = refs[0], refs[1]
        r_ref = refs[2] if add is not None else None
        o_ref = refs[-n_out]
        prod = lax.dot_general(a_ref[...].astype(BF16), b_ref[...].astype(BF16), dims, preferred_element_type=F32)
        if nk == 1:
            o_ref[...] = (prod if r_ref is None else prod + r_ref[...]).astype(out_dtype)
            return
        acc_ref = refs[-1]
        k = pl.program_id(2)

        @pl.when(k == 0)
        def _():
            acc_ref[...] = prod

        @pl.when(k > 0)
        def _():
            acc_ref[...] += prod

        @pl.when(k == nk - 1)
        def _():
            res = acc_ref[...]
            if r_ref is not None:
                res = res + r_ref[...]
            o_ref[...] = res.astype(out_dtype)

    ins = ([a, b] if add is None else [a, b, add]) + list(deps)
    specs = ([a_spec, b_spec] if add is None else [a_spec, b_spec, add_spec]) + [ANY_SPEC] * len(deps)
    return _pcall(body, name=name, out_shape=_sds(out_shape, out_dtype), grid=grid, in_specs=specs,
                  out_specs=o_spec, scratch=[pltpu.VMEM(acc_shape, F32)] * n_scratch,
                  semantics=("parallel", "parallel", "arbitrary"))(*ins)


def _mm_x_wcols(name, a, w4, *, tm, tn, out_dtype=F32, deps=()):
    _, k, ns = w4.shape
    nj = ns // tn
    return _mm(name, a, w4, out_shape=(T, NCHIP * ns), out_dtype=out_dtype, grid=(T // tm, NCHIP * nj, 1), deps=deps,
               a_spec=pl.BlockSpec((tm, k), lambda i, j, kk: (i, 0)),
               b_spec=pl.BlockSpec((None, k, tn), lambda i, j, kk: (j // nj, 0, j % nj)),
               o_spec=pl.BlockSpec((tm, tn), lambda i, j, kk: (i, j)), acc_shape=(tm, tn), dims=NN)


def _mm_ff1(a, w4, *, tm, tn):
    _, k, ns = w4.shape
    nj = ns // tn

    def body(a_ref, b_ref, f_ref, r_ref):
        p = jnp.maximum(jnp.dot(a_ref[...], b_ref[...], preferred_element_type=F32), 0.0)
        f_ref[...] = p.astype(BF16)
        r_ref[...] = (p * p).astype(BF16)

    out = pl.BlockSpec((tm, tn), lambda i, j: (i, j))
    return _pcall(body, name="mm_f", out_shape=(_sds((T, DFF), BF16), _sds((T, DFF), BF16)), grid=(T // tm, NCHIP * nj),
                  in_specs=[pl.BlockSpec((tm, k), lambda i, j: (i, 0)),
                            pl.BlockSpec((None, k, tn), lambda i, j: (j // nj, 0, j % nj))],
                  out_specs=(out, out), semantics=("parallel", "parallel"))(a, w4)


def _mm_x_wrows(name, a, w4, add, *, tm, tk, tn, deps=()):
    _, ks, n = w4.shape
    nkk = ks // tk
    return _mm(name, a, w4, out_shape=(T, n), out_dtype=F32, grid=(T // tm, n // tn, NCHIP * nkk), deps=deps,
               a_spec=pl.BlockSpec((tm, tk), lambda i, j, kk: (i, kk)),
               b_spec=pl.BlockSpec((None, tk, tn), lambda i, j, kk: (kk // nkk, kk % nkk, j)),
               o_spec=pl.BlockSpec((tm, tn), lambda i, j, kk: (i, j)), acc_shape=(tm, tn), dims=NN,
               add=add, add_spec=pl.BlockSpec((tm, tn), lambda i, j, kk: (i, j)))


def _mm_g_wcols_t(name, g, w4, *, tm, tk, tn, out_dtype=F32, deps=()):
    _, k, ns = w4.shape
    nkk = ns // tk
    return _mm(name, g, w4, out_shape=(T, k), out_dtype=out_dtype, grid=(T // tm, k // tn, NCHIP * nkk), deps=deps,
               a_spec=pl.BlockSpec((tm, tk), lambda i, j, kk: (i, kk)),
               b_spec=pl.BlockSpec((None, tn, tk), lambda i, j, kk: (kk // nkk, j, kk % nkk)),
               o_spec=pl.BlockSpec((tm, tn), lambda i, j, kk: (i, j)), acc_shape=(tm, tn), dims=NT)


def _mm_g_wrows_t(name, g, w4, *, tm, tn, out_dtype=F32, deps=()):
    _, ks, n = w4.shape
    nj = ks // tn
    return _mm(name, g, w4, out_shape=(T, NCHIP * ks), out_dtype=out_dtype, grid=(T // tm, NCHIP * nj, 1), deps=deps,
               a_spec=pl.BlockSpec((tm, n), lambda i, j, kk: (i, 0)),
               b_spec=pl.BlockSpec((None, tn, n), lambda i, j, kk: (j // nj, j % nj, 0)),
               o_spec=pl.BlockSpec((tm, tn), lambda i, j, kk: (i, j)), acc_shape=(tm, tn), dims=NT)


def _mm_dff2(dx, w4, fa, *, tm, tn, deps=()):
    _, ks, n = w4.shape
    nj = ks // tn
    deps = tuple(d for d in deps if d is not None)

    def body(*refs):
        dx_ref, b_ref, f_ref = refs[:3]
        df_ref = refs[-1]
        dr = lax.dot_general(dx_ref[...].astype(BF16), b_ref[...], NT, preferred_element_type=F32)
        df_ref[...] = (dr * (2.0 * f_ref[...].astype(F32))).astype(BF16)

    out = pl.BlockSpec((tm, tn), lambda i, j: (i, j))
    return _pcall(body, name="mm_dr", out_shape=_sds((T, DFF), BF16), grid=(T // tm, NCHIP * nj),
                  in_specs=[pl.BlockSpec((tm, n), lambda i, j: (i, 0)),
                            pl.BlockSpec((None, tn, n), lambda i, j: (j // nj, j % nj, 0)), out]
                  + [ANY_SPEC] * len(deps),
                  out_specs=out, semantics=("parallel", "parallel"))(dx, w4, fa, *deps)


TCH = 512


def _mm_dw(name, a, g, *, out_shape, out_map, tm, tn, deps=()):
    deps = tuple(d for d in deps if d is not None)

    def body(*refs):
        a_ref, g_ref = refs[:2]
        o_ref, at_ref = refs[-2:]

        @pl.when(pl.program_id(1) == 0)
        def _():
            for c in range(T // TCH):
                at_ref[:, c * TCH:(c + 1) * TCH] = a_ref[c * TCH:(c + 1) * TCH, :].T

        o_ref[...] = jnp.dot(at_ref[...], g_ref[...].astype(BF16), preferred_element_type=F32)

    return _pcall(body, name=name, out_shape=_sds(out_shape), grid=(a.shape[1] // tm, g.shape[1] // tn),
                  in_specs=[pl.BlockSpec((T, tm), lambda i, j: (0, i)), pl.BlockSpec((T, tn), lambda i, j: (0, j))]
                  + [ANY_SPEC] * len(deps),
                  out_specs=pl.BlockSpec((None, tm, tn), out_map), scratch=[pltpu.VMEM((tm, T), BF16)],
                  semantics=("parallel", "arbitrary"))(a, g, *deps)


def _mm_dw_cols(name, a, g, *, ns, tm, tn, deps=()):
    nj = ns // tn
    return _mm_dw(name, a, g, out_shape=(NCHIP, a.shape[1], ns), out_map=lambda i, j: (j // nj, i, j % nj),
                  tm=tm, tn=tn, deps=deps)


def _mm_dw_rows(name, a, g, *, ks, tm, tn):
    ni = ks // tm
    return _mm_dw(name, a, g, out_shape=(NCHIP, ks, g.shape[1]), out_map=lambda i, j: (i // ni, i % ni, j),
                  tm=tm, tn=tn)


def _row_spec(width, col=0):
    return pl.BlockSpec((TB, width), lambda i: (i, col))


def _vec_spec(width):
    return pl.BlockSpec((1, width), lambda i: (0, 0))


def _rms_fwd(x, g):
    def body(x_ref, g_ref, h_ref):
        x = x_ref[...]
        r = lax.rsqrt(jnp.mean(x * x, axis=-1, keepdims=True) + EPS)
        h_ref[...] = (x * r * g_ref[...]).astype(BF16)

    return _pcall(body, name="rms_fwd", out_shape=_sds((T, D), BF16), grid=(T // TB,),
                  in_specs=[_row_spec(D), _vec_spec(D)], out_specs=_row_spec(D), semantics=("parallel",))(x, g)


def _rms_bwd(x, g, dh, dres, deps=()):
    deps = tuple(d for d in deps if d is not None)

    def body(*refs):
        x_ref, g_ref, dh_ref, dres_ref = refs[:4]
        dx_ref, dg_ref = refs[-2:]
        x = x_ref[...]
        r = lax.rsqrt(jnp.mean(x * x, axis=-1, keepdims=True) + EPS)
        y = x * r
        dh = dh_ref[...]
        dy = dh * g_ref[...]
        dx_ref[...] = dres_ref[...] + r * (dy - y * jnp.mean(dy * y, axis=-1, keepdims=True))

        @pl.when(pl.program_id(0) == 0)
        def _():
            dg_ref[...] = jnp.zeros_like(dg_ref)

        dg_ref[...] += jnp.sum(dh * y, axis=0, keepdims=True)

    return _pcall(body, name="rms_bwd", out_shape=(_sds((T, D)), _sds((1, D))), grid=(T // TB,),
                  in_specs=[_row_spec(D), _vec_spec(D), _row_spec(D), _row_spec(D)] + [ANY_SPEC] * len(deps),
                  out_specs=(_row_spec(D), _vec_spec(D)), semantics=("arbitrary",))(x, g, dh, dres, *deps)


def _sigmoid(x):
    return 1.0 / (1.0 + jnp.exp(-x))


def _gate_fwd(u, ycv, yat):
    def body(gc_ref, ga_ref, yc_ref, ya_ref, m_ref):
        m_ref[...] = (_sigmoid(gc_ref[...].astype(F32)) * yc_ref[...]
                      + _sigmoid(ga_ref[...].astype(F32)) * ya_ref[...]).astype(BF16)

    blk = lambda off: pl.BlockSpec((TB, 512), lambda i, j: (i, off + j))
    return _pcall(body, name="gate_fwd", out_shape=_sds((T, D), BF16), grid=(T // TB, 2),
                  in_specs=[blk(UB_GC), blk(UB_GA), blk(0), blk(0)], out_specs=blk(0),
                  semantics=("parallel", "parallel"))(u, u, ycv, yat)


def _gate_bwd(u, ycv, yat, dm):
    def body(gc_ref, ga_ref, yc_ref, ya_ref, dm_ref, dyc_ref, dya_ref, dgc_ref, dga_ref):
        dm = dm_ref[...]
        sc = _sigmoid(gc_ref[...].astype(F32))
        sa = _sigmoid(ga_ref[...].astype(F32))
        dyc_ref[...] = (dm * sc).astype(BF16)
        dya_ref[...] = (dm * sa).astype(BF16)
        dgc_ref[...] = (dm * yc_ref[...] * sc * (1.0 - sc)).astype(BF16)
        dga_ref[...] = (dm * ya_ref[...] * sa * (1.0 - sa)).astype(BF16)

    blk = lambda off: pl.BlockSpec((TB, 512), lambda i, j: (i, off + j))
    return _pcall(body, name="gate_bwd",
                  out_shape=(_sds((T, D), BF16), _sds((T, D), BF16), _sds((T, D), BF16), _sds((T, D), BF16)),
                  grid=(T // TB, 2), in_specs=[blk(UB_GC), blk(UB_GA), blk(0), blk(0), blk(0)],
                  out_specs=(blk(0), blk(0), blk(0), blk(0)),
                  semantics=("parallel", "parallel"))(u, u, ycv, yat, dm)


def _relu2_fwd(f):
    def body(f_ref, r_ref):
        a = jnp.maximum(f_ref[...], 0.0)
        r_ref[...] = (a * a).astype(BF16)

    blk = pl.BlockSpec((TB, 1024), lambda i, j: (i, j))
    return _pcall(body, name="relu2_fwd", out_shape=_sds((T, DFF), BF16), grid=(T // TB, DFF // 1024),
                  in_specs=[blk], out_specs=blk, semantics=("parallel", "parallel"))(f)


def _relu2_bwd(f, dr):
    def body(f_ref, dr_ref, r_ref, df_ref):
        a = jnp.maximum(f_ref[...], 0.0)
        r_ref[...] = (a * a).astype(BF16)
        df_ref[...] = (dr_ref[...] * (2.0 * a)).astype(BF16)

    blk = pl.BlockSpec((TB, 1024), lambda i, j: (i, j))
    return _pcall(body, name="relu2_bwd", out_shape=(_sds((T, DFF), BF16), _sds((T, DFF), BF16)),
                  grid=(T // TB, DFF // 1024), in_specs=[blk, blk], out_specs=(blk, blk),
                  semantics=("parallel", "parallel"))(f, dr)


def _loss_fwd_bwd(y, target):
    def body(y_ref, t_ref, loss_ref, dy_ref):
        e = y_ref[...] - t_ref[...]
        dy_ref[...] = e * (1.0 / D)

        @pl.when(pl.program_id(0) == 0)
        def _():
            loss_ref[...] = jnp.zeros_like(loss_ref)

        loss_ref[...] += 0.5 * jnp.sum(jnp.mean(e * e, axis=-1, keepdims=True))

    return _pcall(body, name="loss", out_shape=(_sds((8, 128)), _sds((T, D))), grid=(T // TB,),
                  in_specs=[_row_spec(D), _row_spec(D)],
                  out_specs=(pl.BlockSpec((8, 128), lambda i: (0, 0)), _row_spec(D)),
                  semantics=("arbitrary",))(y, target)


PAD = 32
CCH = 256


def _conv_fwd(u, dw_w, dw_b):
    def body(a_ref, gt_ref, w_ref, b_ref, z1_ref, zp_ref):
        zp_ref[0:PAD, :] = jnp.zeros((PAD, 128), F32)
        zp_ref[PAD:PAD + T, :] = a_ref[...].astype(F32) * _sigmoid(gt_ref[...].astype(F32))
        for c in range(T // CCH):
            acc = jnp.broadcast_to(b_ref[...], (CCH, 128))
            for j in range(KW):
                acc = acc + w_ref[j:j + 1, :] * zp_ref[pl.ds(c * CCH + j + PAD - (KW - 1), CCH), :]
            z1_ref[c * CCH:(c + 1) * CCH, :] = acc

    col = lambda off: pl.BlockSpec((T, 128), lambda j: (0, off * 4 + j))
    return _pcall(body, name="conv_fwd", out_shape=_sds((T, CONV)), grid=(CONV // 128,),
                  in_specs=[col(UB_A), col(UB_GT), pl.BlockSpec((KW, 128), lambda j: (0, j)),
                            pl.BlockSpec((1, 128), lambda j: (0, j))],
                  out_specs=col(0), scratch=[pltpu.VMEM((T + PAD, 128), F32)],
                  semantics=("parallel",))(u, u, dw_w, dw_b)


def _ln_silu_fwd(z1, g, b):
    def body(z_ref, g_ref, b_ref, o_ref):
        z = z_ref[...]
        mu = jnp.mean(z, axis=-1, keepdims=True)
        zc = z - mu
        zh = zc * lax.rsqrt(jnp.mean(zc * zc, axis=-1, keepdims=True) + EPS)
        z2 = zh * g_ref[...] + b_ref[...]
        o_ref[...] = (z2 * _sigmoid(z2)).astype(BF16)

    return _pcall(body, name="ln_silu_fwd", out_shape=_sds((T, CONV), BF16), grid=(T // TB,),
                  in_specs=[_row_spec(CONV), _vec_spec(CONV), _vec_spec(CONV)], out_specs=_row_spec(CONV),
                  semantics=("parallel",))(z1, g, b)


def _ln_silu_bwd(z1, g, b, dz3):
    def body(z_ref, g_ref, b_ref, d_ref, z3_ref, dz1_ref, dg_ref, db_ref):
        z = z_ref[...]
        mu = jnp.mean(z, axis=-1, keepdims=True)
        zc = z - mu
        rs = lax.rsqrt(jnp.mean(zc * zc, axis=-1, keepdims=True) + EPS)
        zh = zc * rs
        z2 = zh * g_ref[...] + b_ref[...]
        s = _sigmoid(z2)
        z3_ref[...] = (z2 * s).astype(BF16)
        dz2 = d_ref[...] * (s * (1.0 + z2 * (1.0 - s)))
        dzh = dz2 * g_ref[...]
        dz1_ref[...] = rs * (dzh - jnp.mean(dzh, axis=-1, keepdims=True)
                             - zh * jnp.mean(dzh * zh, axis=-1, keepdims=True))

        @pl.when(pl.program_id(0) == 0)
        def _():
            dg_ref[...] = jnp.zeros_like(dg_ref)
            db_ref[...] = jnp.zeros_like(db_ref)

        dg_ref[...] += jnp.sum(dz2 * zh, axis=0, keepdims=True)
        db_ref[...] += jnp.sum(dz2, axis=0, keepdims=True)

    return _pcall(body, name="ln_silu_bwd",
                  out_shape=(_sds((T, CONV), BF16), _sds((T, CONV)), _sds((1, CONV)), _sds((1, CONV))),
                  grid=(T // TB,),
                  in_specs=[_row_spec(CONV), _vec_spec(CONV), _vec_spec(CONV), _row_spec(CONV)],
                  out_specs=(_row_spec(CONV), _row_spec(CONV), _vec_spec(CONV), _vec_spec(CONV)),
                  semantics=("arbitrary",))(z1, g, b, dz3)


def _conv_bwd(u, dw_w, dz1):
    def body(a_ref, gt_ref, w_ref, dz1_ref, da_ref, dgt_ref, dw_ref, db_ref, zp_ref, dp_ref):
        a = a_ref[...].astype(F32)
        s = _sigmoid(gt_ref[...].astype(F32))
        zp_ref[0:PAD, :] = jnp.zeros((PAD, 128), F32)
        zp_ref[PAD:PAD + T, :] = a * s
        dp_ref[0:T, :] = dz1_ref[...]
        dp_ref[T:T + PAD, :] = jnp.zeros((PAD, 128), F32)
        db_ref[...] = jnp.sum(dz1_ref[...], axis=0, keepdims=True)
        for j in range(KW):
            tot = jnp.zeros((1, 128), F32)
            for c in range(T // CCH):
                tot = tot + jnp.sum(dz1_ref[c * CCH:(c + 1) * CCH, :]
                                    * zp_ref[pl.ds(c * CCH + j + PAD - (KW - 1), CCH), :], axis=0, keepdims=True)
            dw_ref[j:j + 1, :] = tot
        for c in range(T // CCH):
            acc = jnp.zeros((CCH, 128), F32)
            for j in range(KW):
                acc = acc + w_ref[j:j + 1, :] * dp_ref[pl.ds(c * CCH + (KW - 1) - j, CCH), :]
            rows = slice(c * CCH, (c + 1) * CCH)
            sc = _sigmoid(gt_ref[rows, :].astype(F32))
            da_ref[rows, :] = (acc * sc).astype(BF16)
            dgt_ref[rows, :] = (acc * a_ref[rows, :].astype(F32) * sc * (1.0 - sc)).astype(BF16)

    col = lambda off: pl.BlockSpec((T, 128), lambda j: (0, off * 4 + j))
    wspec = pl.BlockSpec((KW, 128), lambda j: (0, j))
    return _pcall(body, name="conv_bwd",
                  out_shape=(_sds((T, CONV), BF16), _sds((T, CONV), BF16), _sds((KW, CONV)), _sds((1, CONV))),
                  grid=(CONV // 128,), in_specs=[col(UB_A), col(UB_GT), wspec, col(0)],
                  out_specs=(col(0), col(0), wspec, pl.BlockSpec((1, 128), lambda j: (0, j))),
                  scratch=[pltpu.VMEM((T + PAD, 128), F32), pltpu.VMEM((T + PAD, 128), F32)],
                  semantics=("parallel",))(u, u, dw_w, dz1)


def _bucket_tables():
    qi = np.arange(BLK)[:, None]
    kj = np.arange(2 * BLK)[None, :]
    off = np.clip(qi + BLK - kj, 0, BLK)
    out = []
    for d in DIL:
        dist = (off * d).astype(np.int32)
        nf = np.maximum(dist, 1).astype(np.float32)
        large = 16 + (np.log(nf / np.float32(16)) / np.float32(math.log(2048 / 16)) * np.float32(16)).astype(np.int32)
        large = np.minimum(large, NBUCKET - 1)
        out.append(np.where(dist < 16, dist, large))
    return np.stack(out).astype(np.int32)


def _band():
    off = lax.broadcasted_iota(jnp.int32, (BLK, 2 * BLK), 0) + BLK - lax.broadcasted_iota(jnp.int32, (BLK, 2 * BLK), 1)
    return (off >= 0) & (off <= BLK)


def _bias_table(rel_bias_t, buckets):
    def body(rb_ref, bk_ref, o_ref):
        h = pl.program_id(0)
        bk = bk_ref[...]
        acc = jnp.zeros((BLK, 2 * BLK), F32)
        for b in range(NBUCKET):
            acc = jnp.where(bk == b, rb_ref[h, b], acc)
        o_ref[...] = jnp.where(_band(), acc, NEG)

    return _pcall(body, name="bias_table", out_shape=_sds((3 * 8, BLK, 2 * BLK)), grid=(24,),
                  in_specs=[pl.BlockSpec(memory_space=pltpu.SMEM),
                            pl.BlockSpec((None, BLK, 2 * BLK), lambda h: (h // 8, 0, 0))],
                  out_specs=pl.BlockSpec((None, BLK, 2 * BLK), lambda h: (h, 0, 0)),
                  semantics=("parallel",))(rel_bias_t, buckets)


def _bias_grad(ds_acc, buckets):
    def body(a_ref, bk_ref, o_ref):
        acc = a_ref[0]
        for l in range(1, DEPTH):
            acc = acc + a_ref[l]
        bk = bk_ref[...]
        lane = lax.broadcasted_iota(jnp.int32, (1, 128), 1)
        row = jnp.zeros((1, 128), F32)
        for b in range(NBUCKET):
            row = jnp.where(lane == b, jnp.sum(jnp.where(bk == b, acc, 0.0)), row)
        o_ref[...] = row

    return _pcall(body, name="bias_grad", out_shape=_sds((24, 1, 128)), grid=(24,),
                  in_specs=[pl.BlockSpec((DEPTH, None, BLK, 2 * BLK), lambda h: (0, h, 0, 0)),
                            pl.BlockSpec((None, BLK, 2 * BLK), lambda h: (h // 8, 0, 0))],
                  out_specs=pl.BlockSpec((None, 1, 128), lambda h: (h, 0, 0)),
                  semantics=("parallel",))(ds_acc, buckets)


def _head_mask():
    return lax.broadcasted_iota(jnp.int32, (1, 128), 1) < HD


def _seg_ones(width):
    r = lax.broadcasted_iota(jnp.int32, (width, width), 0) >> 6
    c = lax.broadcasted_iota(jnp.int32, (width, width), 1) >> 6
    return (r == c).astype(BF16)


def _seg_sum(x, ones):
    hi = x.astype(BF16)
    lo = (x - hi.astype(F32)).astype(BF16)
    return (jnp.dot(hi, ones, preferred_element_type=F32) + jnp.dot(lo, ones, preferred_element_type=F32))


def _dot(a, b, dims):
    return lax.dot_general(a, b, dims, preferred_element_type=F32)


def _tile_rows(d, r, n):
    stride = None if d == 1 else d
    q_rows = pl.ds(r + d * n * BLK, BLK, stride=stride)
    if n == 0:
        return q_rows, q_rows, BLK
    return q_rows, pl.ds(r + d * (n - 1) * BLK, 2 * BLK, stride=stride), 2 * BLK


NCH = 256


def _qk_norm_prep(q_ref, k_ref, v_ref, gq_ref, gk_ref, qn_ref, kn_ref, vn_ref, ones):
    def prep(i, carry):
        rows = pl.ds(pl.multiple_of(i * NCH, NCH), NCH)
        q = q_ref[rows, :].astype(F32)
        qn_ref[rows, :] = q * lax.rsqrt(_seg_sum(q * q, ones) * (1.0 / HD) + EPS) * gq_ref[...] * (HD ** -0.5)
        k = k_ref[rows, :].astype(F32)
        kn_ref[rows, :] = k * lax.rsqrt(_seg_sum(k * k, ones) * (1.0 / HD) + EPS) * gk_ref[...]
        vn_ref[rows, :] = v_ref[rows, :].astype(F32)
        return carry

    lax.fori_loop(0, T // NCH, prep, 0)


def _attn_specs(g):
    ucol = lambda base: pl.BlockSpec((T, 128), lambda hp: (0, (base + g) * 4 + hp))
    col = pl.BlockSpec((T, 128), lambda hp: (0, hp))
    vec = pl.BlockSpec((1, 128), lambda hp: (0, 0))
    bm = pl.BlockSpec((2, BLK, 2 * BLK), lambda hp: (g * 4 + hp, 0, 0))
    return ucol, col, vec, bm


def _attn_fwd(g, u, gq, gk, bm, deps=()):
    d = DIL[g]

    def body(*refs):
        q_ref, k_ref, v_ref, gq_ref, gk_ref, bm_ref = refs[:6]
        o_ref, lse_ref, qn_ref, kn_ref, vn_ref = refs[-5:]
        ones = _seg_ones(128)
        _qk_norm_prep(q_ref, k_ref, v_ref, gq_ref, gk_ref, qn_ref, kn_ref, vn_ref, ones)
        m_a = _head_mask()
        for r in range(d):
            for n in range(T // d // BLK):
                q_rows, k_rows, nk = _tile_rows(d, r, n)
                qt = qn_ref[q_rows, :]
                kt = kn_ref[k_rows, :].astype(BF16)
                vt = vn_ref[k_rows, :].astype(BF16)
                o_t = None
                for h in range(2):
                    mh = m_a if h == 0 else jnp.logical_not(m_a)
                    qh = jnp.where(mh, qt, 0.0).astype(BF16)
                    s = _dot(qh, kt, NT) + bm_ref[h, :, 2 * BLK - nk:]
                    mx = jnp.max(s, axis=1, keepdims=True)
                    p = jnp.exp(s - mx)
                    l = jnp.sum(p, axis=1, keepdims=True)
                    o_h = _dot(p.astype(BF16), vt, NN) / l
                    lse_h = jnp.broadcast_to(mx + jnp.log(l), (BLK, 128))
                    if h == 0:
                        o_t, lse_t = o_h, lse_h
                    else:
                        o_t = jnp.where(m_a, o_t, o_h)
                        lse_t = jnp.where(m_a, lse_t, lse_h)
                o_ref[q_rows, :] = o_t
                lse_ref[q_rows, :] = lse_t

    ucol, col, vec, bmspec = _attn_specs(g)
    return _pcall(body, name=f"attn_fwd_g{g}", out_shape=(_sds((T, AOUT)), _sds((T, AOUT))), grid=(4,),
                  in_specs=[ucol(UB_Q), ucol(UB_K), ucol(UB_V), vec, vec, bmspec] + [ANY_SPEC] * len(deps),
                  out_specs=(col, col), scratch=[pltpu.VMEM((T, 128), F32)] * 3,
                  semantics=("parallel",))(u, u, u, gq, gk, bm, *deps)


def _attn_bwd(g, u, gq, gk, bm, dog, cb, lse):
    d = DIL[g]

    def body(q_ref, k_ref, v_ref, gq_ref, gk_ref, bm_ref, do_ref, cb_ref, lse_ref,
             dqo_ref, dko_ref, dvo_ref, dgq_ref, dgk_ref, dsa_ref, qn_ref, kn_ref, vn_ref, dq_ref, dk_ref, dv_ref):
        ones = _seg_ones(128)
        _qk_norm_prep(q_ref, k_ref, v_ref, gq_ref, gk_ref, qn_ref, kn_ref, vn_ref, ones)
        m_a = _head_mask()
        dk_ref[...] = jnp.zeros_like(dk_ref)
        dv_ref[...] = jnp.zeros_like(dv_ref)
        dsa_ref[...] = jnp.zeros_like(dsa_ref)
        for r in range(d):
            for n in range(T // d // BLK):
                q_rows, k_rows, nk = _tile_rows(d, r, n)
                qt = qn_ref[q_rows, :]
                kt = kn_ref[k_rows, :]
                ktb = kt.astype(BF16)
                vtb = vn_ref[k_rows, :].astype(BF16)
                do_t = do_ref[q_rows, :]
                c_t = cb_ref[q_rows, :]
                lse_t = lse_ref[q_rows, :]
                dq_t = jnp.zeros((BLK, 128), F32)
                dk_t = jnp.zeros((nk, 128), F32)
                dv_t = jnp.zeros((nk, 128), F32)
                for h in range(2):
                    mh = m_a if h == 0 else jnp.logical_not(m_a)
                    qh = jnp.where(mh, qt, 0.0).astype(BF16)
                    kh = jnp.where(mh, kt, 0.0).astype(BF16)
                    doh = jnp.where(mh, do_t, 0.0).astype(BF16)
                    lse_c = jnp.max(jnp.where(mh, lse_t, -3e38), axis=1, keepdims=True)
                    c_c = jnp.max(jnp.where(mh, c_t, -3e38), axis=1, keepdims=True)
                    s = _dot(qh, ktb, NT) + bm_ref[h, :, 2 * BLK - nk:]
                    p = jnp.exp(s - lse_c)
                    dp = _dot(doh, vtb, NT)
                    ds = p * (dp + c_c)
                    dsb = ds.astype(BF16)
                    dv_t = dv_t + _dot(p.astype(BF16), doh, TN)
                    dq_t = dq_t + _dot(dsb, kh, NN)
                    dk_t = dk_t + _dot(dsb, qh, TN)
                    dsa_ref[h, :, 2 * BLK - nk:] += ds
                dq_ref[q_rows, :] = dq_t
                dk_ref[k_rows, :] += dk_t
                dv_ref[k_rows, :] += dv_t

        @pl.when(pl.program_id(0) == 0)
        def _():
            dgq_ref[...] = jnp.zeros_like(dgq_ref)
            dgk_ref[...] = jnp.zeros_like(dgk_ref)

        def norm_bwd(i, carry):
            rows = pl.ds(pl.multiple_of(i * NCH, NCH), NCH)
            for x_ref, g_ref, dx_ref, dxo_ref, dg_ref, scale in (
                    (q_ref, gq_ref, dq_ref, dqo_ref, dgq_ref, HD ** -0.5), (k_ref, gk_ref, dk_ref, dko_ref, dgk_ref, 1.0)):
                x = x_ref[rows, :].astype(F32)
                rs = lax.rsqrt(_seg_sum(x * x, ones) * (1.0 / HD) + EPS)
                xh = x * rs
                dn = dx_ref[rows, :] * scale
                dxh = dn * g_ref[...]
                dxo_ref[rows, :] = (rs * (dxh - xh * (_seg_sum(dxh * xh, ones) * (1.0 / HD)))).astype(BF16)
                dg_ref[...] += jnp.sum(dn * xh, axis=0, keepdims=True)
            dvo_ref[rows, :] = dv_ref[rows, :].astype(BF16)
            return carry

        lax.fori_loop(0, T // NCH, norm_bwd, 0)

    ucol, col, vec, bmspec = _attn_specs(g)
    return _pcall(body, name=f"attn_bwd_g{g}",
                  out_shape=(_sds((T, AOUT), BF16), _sds((T, AOUT), BF16), _sds((T, AOUT), BF16), _sds((1, 128)),
                             _sds((1, 128)), _sds((8, BLK, 2 * BLK))),
                  grid=(4,),
                  in_specs=[ucol(UB_Q), ucol(UB_K), ucol(UB_V), vec, vec, bmspec, col, col, col],
                  out_specs=(col, col, col, vec, vec, pl.BlockSpec((2, BLK, 2 * BLK), lambda hp: (hp, 0, 0))),
                  scratch=[pltpu.VMEM((T, 128), F32)] * 6,
                  semantics=("arbitrary",))(u, u, u, gq, gk, bm, dog, cb, lse)


def _combine_fwd(ogs, lses):
    def body(o0, o1, o2, l0, l1, l2, o_ref):
        ls = [l0[...], l1[...], l2[...]]
        mx = jnp.maximum(jnp.maximum(ls[0], ls[1]), ls[2])
        es = [jnp.exp(l - mx) for l in ls]
        inv = 1.0 / (es[0] + es[1] + es[2])
        o_ref[...] = ((es[0] * o0[...] + es[1] * o1[...] + es[2] * o2[...]) * inv).astype(BF16)

    return _pcall(body, name="combine_fwd", out_shape=_sds((T, AOUT), BF16), grid=(T // TB,),
                  in_specs=[_row_spec(AOUT)] * 6, out_specs=_row_spec(AOUT), semantics=("parallel",))(*ogs, *lses)


def _combine_bwd(ogs, lses, do):
    def body(o0, o1, o2, l0, l1, l2, do_ref, d0, d1, d2, c0, c1, c2):
        ls = [l0[...], l1[...], l2[...]]
        mx = jnp.maximum(jnp.maximum(ls[0], ls[1]), ls[2])
        es = [jnp.exp(l - mx) for l in ls]
        inv = 1.0 / (es[0] + es[1] + es[2])
        ws = [e * inv for e in es]
        do = do_ref[...]
        o = ws[0] * o0[...] + ws[1] * o1[...] + ws[2] * o2[...]
        s = _seg_sum(do * o, _seg_ones(AOUT))
        for w, d_ref, c_ref in zip(ws, (d0, d1, d2), (c0, c1, c2)):
            d_ref[...] = w * do
            c_ref[...] = -(w * s)

    return _pcall(body, name="combine_bwd", out_shape=tuple(_sds((T, AOUT)) for _ in range(6)), grid=(T // TB,),
                  in_specs=[_row_spec(AOUT)] * 7, out_specs=tuple(_row_spec(AOUT) for _ in range(6)),
                  semantics=("parallel",))(*ogs, *lses, do)


def _layer_fwd(x, p, bm, deps=(), mid=None):
    h1 = _rms_fwd(x, p["n1g"])
    u = _mm_x_wcols("mm_u", h1, p["win4"], tm=T, tn=640, out_dtype=BF16, deps=deps)
    ogs, lses = [], []
    for g in range(NG):
        gdeps = (p["hook"](ogs[-1]),) if g == NG - 1 and "hook" in p else ()
        og, lse = _attn_fwd(g, u, p["gq"], p["gk"], bm, deps=gdeps)
        ogs.append(og)
        lses.append(lse)
    o = _combine_fwd(ogs, lses)
    z1 = _conv_fwd(u, p["dww"], p["dwb"])
    z3 = _ln_silu_fwd(z1, p["lng"], p["lnb"])
    if "rest" in p:
        p = {**p, **p["rest"](z3)}
    ycv = _mm_x_wcols("mm_ycv", z3, p["wco4"], tm=T, tn=256)
    yat = _mm_x_wcols("mm_yat", o, p["wao4"], tm=T, tn=256)
    m = _gate_fwd(u, ycv, yat)
    tok = mid(m) if mid else None
    xm = _mm_x_wrows("mm_xmid", m, p["wout4"], x, tm=1024, tk=256, tn=1024, deps=(tok,))
    h2 = _rms_fwd(xm, p["n2g"])
    fa, r = _mm_ff1(h2, p["wff14"], tm=T, tn=512)
    xo = _mm_x_wrows("mm_xout", r, p["wff24"], xm, tm=1024, tk=1024, tn=1024)
    saved = dict(x=x, h1=h1, u=u, z1=z1, ogs=ogs, lses=lses, o=o, ycv=ycv, yat=yat, m=m, xm=xm, h2=h2, fa=fa, r=r)
    return xo, p, saved


EARLY = ("w_ff2", "w_ff1", "w_out")
LATE = ("w_conv_out", "w_attn_out", "w_in")


def _layer_bwd(dx, s, p, bm, pipe=None, own_early=None, own_late=None):
    u = s["u"]
    tok = pipe.step0() if pipe else None
    df = _mm_dff2(dx, p["wff24"], s["fa"], tm=1024, tn=512, deps=(tok,))
    g_ff2 = _mm_dw_rows("mm_dwff2", s["r"], dx, ks=1024, tm=1024, tn=512)
    g_ff1 = _mm_dw_cols("mm_dwff1", s["h2"], df, ns=1024, tm=1024, tn=512)
    tok = pipe.step1(g_ff1) if pipe else None
    dh2 = _mm_g_wcols_t("mm_dh2", df, p["wff14"], tm=1024, tk=512, tn=1024, deps=(tok,))
    dxm, d_n2g = _rms_bwd(s["xm"], p["n2g"], dh2, dx)

    dm = _mm_g_wrows_t("mm_dm", dxm, p["wout4"], tm=1024, tn=256)
    g_out = _mm_dw_rows("mm_dwout", s["m"], dxm, ks=256, tm=256, tn=512)
    early = own_early(dict(w_ff2=g_ff2, w_ff1=g_ff1, w_out=g_out)) if own_early else None
    tok_e = early.step0() if early else None
    dyc, dya, dgc, dga = _gate_bwd(u, s["ycv"], s["yat"], dm)

    dz3 = _mm_g_wcols_t("mm_dz3", dyc, p["wco4"], tm=T, tk=256, tn=512, deps=(tok_e,))
    z3, dz1, d_lng, d_lnb = _ln_silu_bwd(s["z1"], p["lng"], p["lnb"], dz3)
    g_co = _mm_dw_cols("mm_dwco", z3, dyc, ns=256, tm=512, tn=256)
    da, dgt, d_dww, d_dwb = _conv_bwd(u, p["dww"], dz1)

    tok_e = early.step1(da) if early else None
    do = _mm_g_wcols_t("mm_do", dya, p["wao4"], tm=T, tk=256, tn=512, deps=(tok_e,))
    g_ao = _mm_dw_cols("mm_dwao", s["o"], dya, ns=256, tm=512, tn=256)
    parts = _combine_bwd(s["ogs"], s["lses"], do)
    dqs, dks, dvs, d_gq, d_gk, dsas = [], [], [], [], [], []
    for g in range(NG):
        dq, dk, dv, dgq, dgk, dsa = _attn_bwd(g, u, p["gq"], p["gk"], bm, parts[g], parts[NG + g], s["lses"][g])
        dqs.append(dq)
        dks.append(dk)
        dvs.append(dv)
        d_gq.append(dgq)
        d_gk.append(dgk)
        dsas.append(dsa)
    du = jnp.concatenate([da, dgt] + dqs + dks + dvs + [dgc, dga], axis=1)
    tok = pipe.step2(du) if pipe else None
    tok_e = early.step2(du) if early else None
    g_in = _mm_dw_cols("mm_dwin", s["h1"], du, ns=1920, tm=1024, tn=640, deps=(tok, tok_e))
    late = own_late(dict(w_in=g_in, w_conv_out=g_co, w_attn_out=g_ao)) if own_late else None
    tok_l = late.step0() if late else None
    dh1 = _mm_g_wcols_t("mm_dh1", du, p["win4"], tm=1024, tk=640, tn=1024, deps=(tok_l,))
    tok_l = late.step1(dh1) if late else None
    dxi, d_n1g = _rms_bwd(s["x"], p["n1g"], dh1, dxm, deps=(tok_l,))
    if pipe:
        pipe.step3(dxi)
    if early:
        early.step3(dxi)
    if late:
        late.step3(late.step2(dxi))

    fold = lambda parts_: sum(v[0, :HD] + v[0, HD:] for v in parts_)
    big = dict(w_in=g_in, w_conv_out=g_co, w_attn_out=g_ao, w_out=g_out, w_ff1=g_ff1, w_ff2=g_ff2)
    small = dict(norm1_g=d_n1g[0], q_norm_g=fold(d_gq), k_norm_g=fold(d_gk), conv_dw_w=d_dww, conv_dw_b=d_dwb[0],
                 conv_ln_g=d_lng[0], conv_ln_b=d_lnb[0], norm2_g=d_n2g[0])
    return dxi, big, small, jnp.concatenate(dsas, axis=0)


def _local_step(x, target, get_layer, rel_bias, make_pipe):
    buckets = jnp.asarray(_bucket_tables())
    bm = _bias_table(rel_bias.T, buckets)
    saved, layers = [], []
    for l in range(DEPTH):
        p, deps, mid = get_layer(l, x)
        x, p, s = _layer_fwd(x, p, bm, deps=deps, mid=mid)
        layers.append(p)
        saved.append(s)
    loss_blk, dx = _loss_fwd_bwd(x, target)
    smalls, dsas = [None] * DEPTH, [None] * DEPTH
    pipe = None
    for l in reversed(range(DEPTH)):
        if l > 0:
            dx, big, smalls[l], dsas[l] = _layer_bwd(dx, saved[l], layers[l], bm, pipe)
            pipe = make_pipe(l, BIG, "", big)
        else:
            dx, big, smalls[l], dsas[l] = _layer_bwd(
                dx, saved[l], layers[l], bm, pipe, lambda big_: make_pipe(0, EARLY, "e", big_),
                lambda big_: make_pipe(0, LATE, "", big_))
    d_rel = _bias_grad(jnp.stack(dsas), buckets)[:, 0, :NBUCKET].T
    return loss_blk[0, 0], dx, smalls, d_rel


MESH = pl.DeviceIdType.MESH


def _me():
    return lax.axis_index("x"), lax.axis_index("y"), lax.axis_index("c")


def _other_chips(mx, my):
    return [(1 - mx, my), (mx, 1 - my), (1 - mx, 1 - my)]


def _rcopy(src, dst, send_sems, recv_sems, k, dev):
    return pltpu.make_async_remote_copy(src_ref=src, dst_ref=dst, send_sem=send_sems.at[k], recv_sem=recv_sems.at[k],
                                        device_id=dev, device_id_type=MESH)


def _comm_call(body, name, out_shape, n_in, n_sems):
    return pl.pallas_call(
        body, name=name, out_shape=out_shape, in_specs=[HBM_SPEC] * n_in,
        out_specs=jax.tree.map(lambda _: HBM_SPEC, out_shape),
        scratch_shapes=[pltpu.SemaphoreType.DMA((n_sems,)), pltpu.SemaphoreType.DMA((n_sems,)),
                        pltpu.SemaphoreType.DMA(())],
        compiler_params=pltpu.CompilerParams(has_side_effects=True))


def _all_gather_chips(x, name):
    def body(x_ref, o_ref, send_sems, recv_sems, local_sem):
        mx, my, mc = _me()
        local = pltpu.make_async_copy(x_ref, o_ref.at[2 * mx + my], local_sem)
        local.start()
        sends = [_rcopy(x_ref, o_ref.at[2 * mx + my], send_sems, recv_sems, k, (px, py, mc))
                 for k, (px, py) in enumerate(_other_chips(mx, my))]
        for cp in sends:
            cp.start()
        for k, (px, py) in enumerate(_other_chips(mx, my)):
            _rcopy(x_ref, o_ref.at[2 * px + py], send_sems, recv_sems, k, (px, py, mc)).wait_recv()
        for cp in sends:
            cp.wait_send()
        local.wait()

    return _comm_call(body, name, _sds((NCHIP,) + x.shape, x.dtype), 1, 3)(x)


EFFECT = pltpu.SideEffectType.DATAFLOW_SIDE_EFFECTING


def _hbm(a):
    return pltpu.with_memory_space_constraint(a, pltpu.HBM)


def _split_start(name, bufs, plan, n, after=None):
    nb = len(bufs)
    extra = [] if after is None else [after]
    ne = len(extra)

    def body(*refs):
        send_sems, recv_sems, token = refs[nb + ne], refs[nb + ne + 1], refs[-1]
        mx, my, mc = _me()
        for k, (src, dst, dev, _) in enumerate(plan(refs[:nb], mx, my, mc)):
            _rcopy(src, dst, send_sems, recv_sems, k, dev).start()
        token[...] = jnp.zeros_like(token)

    out = pl.pallas_call(
        body, name=name,
        out_shape=(pltpu.SemaphoreType.DMA((n,)), pltpu.SemaphoreType.DMA((n,)),
                   *[pltpu.HBM(b.shape, b.dtype) for b in bufs], _sds((8, 128))),
        in_specs=[HBM_SPEC] * nb + [ANY_SPEC] * ne,
        out_specs=(SEM_SPEC, SEM_SPEC, *[HBM_SPEC] * nb, pl.BlockSpec(memory_space=pltpu.VMEM)),
        input_output_aliases={i: 2 + i for i in range(nb)},
        compiler_params=pltpu.CompilerParams(has_side_effects=EFFECT))(*[_hbm(b) for b in bufs], *extra)
    return (out[0], out[1]), list(out[2:2 + nb]), out[-1]


def _split_wait(name, sems, bufs, plan, after):
    nb = len(bufs)

    def body(*refs):
        send_sems, recv_sems = refs[nb], refs[nb + 1]
        mx, my, mc = _me()
        for k, (src, dst, dev, land) in enumerate(plan(refs[:nb], mx, my, mc)):
            _rcopy(src, dst, send_sems, recv_sems, k, dev).wait_send()
            _rcopy(src, land, send_sems, recv_sems, k, dev).wait_recv()

    out = pl.pallas_call(
        body, name=name, out_shape=tuple(pltpu.HBM(b.shape, b.dtype) for b in bufs),
        in_specs=[HBM_SPEC] * nb + [SEM_SPEC, SEM_SPEC, ANY_SPEC], out_specs=(HBM_SPEC,) * nb,
        input_output_aliases={i: i for i in range(nb)},
        compiler_params=pltpu.CompilerParams(has_side_effects=EFFECT))(*bufs, sems[0], sems[1], after)
    return list(out)


def _plan_gather_chips(refs, mx, my, mc):
    me = 2 * mx + my
    return [(r.at[me, mc], r.at[me, mc], (px, py, mc), r.at[2 * px + py, mc])
            for r in refs for px, py in _other_chips(mx, my)]


def _plan_gather_pair(refs, mx, my, mc):
    return [(r.at[2 * px + py, mc], r.at[2 * px + py, mc], (mx, my, 1 - mc), r.at[2 * px + py, 1 - mc])
            for r in refs for px, py in _other_chips(mx, my)]


def _plan_pair_half(refs, mx, my, mc):
    n = len(refs) // 2
    return [(g.at[:, 1 - mc], r, (mx, my, 1 - mc), r) for g, r in zip(refs[:n], refs[n:])]


def _plan_scatter(refs, mx, my, mc):
    n = len(refs) // 2
    return [(q.at[2 * px + py], r.at[k], (px, py, mc), r.at[k])
            for q, r in zip(refs[:n], refs[n:]) for k, (px, py) in enumerate(_other_chips(mx, my))]


def _plan_pair_fill(refs, mx, my, mc):
    return [(r.at[mc], r.at[mc], (mx, my, 1 - mc), r.at[1 - mc]) for r in refs]


def _all_gather_devices(v, name):
    def body(v_ref, o_ref, send_sems, recv_sems, local_sem):
        mx, my, mc = _me()
        flip = lambda m, b: 1 - m if b else m
        peers = [(flip(mx, k >> 2 & 1), flip(my, k >> 1 & 1), flip(mc, k & 1)) for k in range(1, 8)]
        slot = lambda d: 4 * d[0] + 2 * d[1] + d[2]
        local = pltpu.make_async_copy(v_ref, o_ref.at[slot((mx, my, mc))], local_sem)
        local.start()
        sends = [_rcopy(v_ref, o_ref.at[slot((mx, my, mc))], send_sems, recv_sems, k, dev)
                 for k, dev in enumerate(peers)]
        for cp in sends:
            cp.start()
        for k, dev in enumerate(peers):
            _rcopy(v_ref, o_ref.at[slot(dev)], send_sems, recv_sems, k, dev).wait_recv()
        for cp in sends:
            cp.wait_send()
        local.wait()

    return _comm_call(body, name, _sds((8,) + v.shape, v.dtype), 1, 7)(v)


def _row_tile(rows, cols):
    t = 8
    while t * 2 * cols * 4 <= (1 << 20) and rows % (t * 2) == 0:
        t *= 2
    return t


def _prefetch_call(body, name, out_shape, grid, in_specs, out_specs):
    return pl.pallas_call(
        body, name=name, out_shape=out_shape,
        grid_spec=pltpu.PrefetchScalarGridSpec(num_scalar_prefetch=1, grid=grid, in_specs=in_specs,
                                               out_specs=out_specs),
        compiler_params=pltpu.CompilerParams(vmem_limit_bytes=VMEM_LIMIT,
                                             dimension_semantics=("parallel",) * len(grid)))


def _sum_half(g, r1, place, name):
    _, _, rr, ns = g.shape
    tr = _row_tile(rr, ns)

    def body(c_ref, g_ref, r_ref, o_ref, ob_ref):
        q = g_ref[...] + r_ref[...]
        ob_ref[...] = q.astype(BF16)

        @pl.when(pl.program_id(1) == c_ref[0])
        def _():
            o_ref[...] = q

    blk = pl.BlockSpec((None, tr, ns), lambda i, s, c: (s, i, 0))
    return pl.pallas_call(
        body, name=name, out_shape=(_sds((rr, ns)), _sds((NCHIP, rr, ns), BF16)),
        grid_spec=pltpu.PrefetchScalarGridSpec(
            num_scalar_prefetch=1, grid=(rr // tr, NCHIP),
            in_specs=[pl.BlockSpec((None, None, tr, ns), lambda i, s, c: (s, c[1], i, 0)), blk],
            out_specs=(pl.BlockSpec((tr, ns), lambda i, s, c: (i, 0)), blk)),
        compiler_params=pltpu.CompilerParams(vmem_limit_bytes=VMEM_LIMIT,
                                             dimension_semantics=("parallel", "arbitrary")))(place, g, r1)


def _sum_recv(q, r2, place, name):
    rr, ns = q.shape
    tr = _row_tile(rr, ns)

    def body(c_ref, q_ref, r_ref, o_ref):
        o_ref[...] = ((q_ref[...] + r_ref[0].astype(F32)) + r_ref[1].astype(F32)) + r_ref[2].astype(F32)

    return _prefetch_call(body, name, _sds((2, rr, ns)), (rr // tr,),
                          [pl.BlockSpec((tr, ns), lambda i, c: (i, 0)),
                           pl.BlockSpec((NCHIP - 1, tr, ns), lambda i, c: (0, i, 0))],
                          pl.BlockSpec((None, tr, ns), lambda i, c: (c[1], i, 0)))(place, q, r2)


def _sum_devices(v8):
    def body(v_ref, o_ref):
        acc = v_ref[0]
        for dev in range(1, 8):
            acc = acc + v_ref[dev]
        o_ref[...] = acc

    return _pcall(body, name="sum_devices", out_shape=_sds(v8.shape[1:]))(v8)


def _adamw(w, g, m, v, name):
    rows, cols = w.shape
    tr = _row_tile(rows, cols)

    def body(w_ref, g_ref, m_ref, v_ref, d_ref, m2_ref, v2_ref):
        g = g_ref[...]
        m2 = ADAM_B1 * m_ref[...] + (1.0 - ADAM_B1) * g
        v2 = ADAM_B2 * v_ref[...] + (1.0 - ADAM_B2) * (g * g)
        m_hat = m2 / (1.0 - ADAM_B1 ** ADAM_STEP)
        v_hat = v2 / (1.0 - ADAM_B2 ** ADAM_STEP)
        d_ref[...] = -ADAM_LR * (m_hat / (jnp.sqrt(v_hat) + ADAM_EPS) + ADAM_WD * w_ref[...])
        m2_ref[...] = m2
        v2_ref[...] = v2

    blk = pl.BlockSpec((tr, cols), lambda i: (i, 0))
    return _pcall(body, name=name, out_shape=(_sds((rows, cols)),) * 3, grid=(rows // tr,), in_specs=[blk] * 4,
                  out_specs=(blk,) * 3, semantics=("parallel",))(w, g, m, v)


BIG = ("w_in", "w_conv_out", "w_attn_out", "w_out", "w_ff1", "w_ff2")
SMALL = ("rel_bias", "norm1_g", "q_norm_g", "k_norm_g", "conv_dw_w", "conv_dw_b", "conv_ln_g", "conv_ln_b", "norm2_g")
WEIGHTS = ("rel_bias", "norm1_g", "w_in", "q_norm_g", "k_norm_g", "conv_dw_w", "conv_dw_b", "conv_ln_g", "conv_ln_b",
           "w_conv_out", "w_attn_out", "w_out", "norm2_g", "w_ff1", "w_ff2")


def _pack(arrays):
    flat = jnp.concatenate([a.reshape(-1) for a in arrays])
    n = flat.shape[0]
    rows = -(-n // 1024) * 8
    return jnp.pad(flat, (0, rows * 128 - n)).reshape(rows, 128)


def _unpack(packed, shapes):
    flat = packed.reshape(-1)
    out, off = [], 0
    for shp in shapes:
        n = int(np.prod(shp))
        out.append(flat[off:off + n].reshape(shp))
        off += n
    return out


def _adamw_layer(l, w, g, m, v, prev, name):
    _, k, n = w.shape
    tr = _row_tile(k, n)
    if prev is None:
        prev = tuple(lax.empty(w.shape, F32) for _ in range(4))

    def body(w_ref, g_ref, m_ref, v_ref, p0, p1, p2, p3, go_ref, d_ref, m2_ref, v2_ref):
        g = g_ref[...]
        m2 = ADAM_B1 * m_ref[...] + (1.0 - ADAM_B1) * g
        v2 = ADAM_B2 * v_ref[...] + (1.0 - ADAM_B2) * (g * g)
        m_hat = m2 / (1.0 - ADAM_B1 ** ADAM_STEP)
        v_hat = v2 / (1.0 - ADAM_B2 ** ADAM_STEP)
        go_ref[...] = g
        d_ref[...] = -ADAM_LR * (m_hat / (jnp.sqrt(v_hat) + ADAM_EPS) + ADAM_WD * w_ref[...])
        m2_ref[...] = m2
        v2_ref[...] = v2

    lay = pl.BlockSpec((None, tr, n), lambda i: (l, i, 0))
    return _pcall(body, name=name, out_shape=(_sds(w.shape),) * 4, grid=(k // tr,),
                  in_specs=[lay, pl.BlockSpec((tr, n), lambda i: (i, 0)), lay, lay] + [ANY_SPEC] * 4,
                  out_specs=(lay,) * 4, aliases={4: 0, 5: 1, 6: 2, 7: 3},
                  semantics=("parallel",))(w, g, m, v, *prev)


class _GradPipe:
    def __init__(self, l, kinds, tag, big, place, w, m, v, results):
        self.l, self.kinds, self.place, self.w, self.m, self.v, self.results = l, kinds, place, w, m, v, results
        self.id = f"l{l}{tag}"
        self.g = [big[n].reshape(NCHIP, 2, big[n].shape[1] // 2, big[n].shape[2]) for n in kinds]

    def step0(self):
        lands = [lax.empty((NCHIP,) + g.shape[2:], F32) for g in self.g]
        self.s1, self.b1, tok = _split_start(f"rs1_start_{self.id}", self.g + lands, _plan_pair_half, len(self.kinds))
        return tok

    def step1(self, after):
        nk = len(self.kinds)
        bufs = _split_wait(f"rs1_wait_{self.id}", self.s1, self.b1, _plan_pair_half, after)
        sums = [_sum_half(bufs[i], bufs[nk + i], self.place, f"rs1_sum_{n}") for i, n in enumerate(self.kinds)]
        self.q = [q for q, _ in sums]
        qb = [b for _, b in sums]
        lands = [lax.empty((NCHIP - 1,) + b.shape[1:], BF16) for b in qb]
        self.s2, self.b2, tok = _split_start(f"rs2_start_{self.id}", qb + lands, _plan_scatter, 3 * nk)
        return tok

    def step2(self, after):
        nk = len(self.kinds)
        bufs = _split_wait(f"rs2_wait_{self.id}", self.s2, self.b2, _plan_scatter, after)
        fin = [_sum_recv(self.q[i], bufs[nk + i], self.place, f"rs2_sum_{n}") for i, n in enumerate(self.kinds)]
        self.s3, self.b3, tok = _split_start(f"rs3_start_{self.id}", fin, _plan_pair_fill, nk)
        return tok

    def step3(self, after):
        fin = _split_wait(f"rs3_wait_{self.id}", self.s3, self.b3, _plan_pair_fill, after)
        for i, n in enumerate(self.kinds):
            g2 = fin[i].reshape(fin[i].shape[1] * 2, fin[i].shape[2])
            self.results[n] = _adamw_layer(self.l, self.w[n], g2, self.m[n], self.v[n], self.results.get(n),
                                           f"adamw_{n}_l{self.l}")

    def finish(self):
        self.step3(self.step2(self.step1(self.step0())))


def kernel(x, rel_bias, norm1_g, w_in, q_norm_g, k_norm_g, conv_dw_w, conv_dw_b, conv_ln_g, conv_ln_b, w_conv_out, w_attn_out, w_out, norm2_g, w_ff1, w_ff2, loss_target, m_rel_bias, m_norm1_g, m_w_in, m_q_norm_g, m_k_norm_g, m_conv_dw_w, m_conv_dw_b, m_conv_ln_g, m_conv_ln_b, m_w_conv_out, m_w_attn_out, m_w_out, m_norm2_g, m_w_ff1, m_w_ff2, v_rel_bias, v_norm1_g, v_w_in, v_q_norm_g, v_k_norm_g, v_conv_dw_w, v_conv_dw_b, v_conv_ln_g, v_conv_ln_b, v_w_conv_out, v_w_attn_out, v_w_out, v_norm2_g, v_w_ff1, v_w_ff2):
    w = dict(rel_bias=rel_bias, norm1_g=norm1_g, w_in=w_in, q_norm_g=q_norm_g, k_norm_g=k_norm_g, conv_dw_w=conv_dw_w,
             conv_dw_b=conv_dw_b, conv_ln_g=conv_ln_g, conv_ln_b=conv_ln_b, w_conv_out=w_conv_out,
             w_attn_out=w_attn_out, w_out=w_out, norm2_g=norm2_g, w_ff1=w_ff1, w_ff2=w_ff2)
    m = dict(rel_bias=m_rel_bias, norm1_g=m_norm1_g, w_in=m_w_in, q_norm_g=m_q_norm_g, k_norm_g=m_k_norm_g,
             conv_dw_w=m_conv_dw_w, conv_dw_b=m_conv_dw_b, conv_ln_g=m_conv_ln_g, conv_ln_b=m_conv_ln_b,
             w_conv_out=m_w_conv_out, w_attn_out=m_w_attn_out, w_out=m_w_out, norm2_g=m_norm2_g, w_ff1=m_w_ff1,
             w_ff2=m_w_ff2)
    v = dict(rel_bias=v_rel_bias, norm1_g=v_norm1_g, w_in=v_w_in, q_norm_g=v_q_norm_g, k_norm_g=v_k_norm_g,
             conv_dw_w=v_conv_dw_w, conv_dw_b=v_conv_dw_b, conv_ln_g=v_conv_ln_g, conv_ln_b=v_conv_ln_b,
             w_conv_out=v_w_conv_out, w_attn_out=v_w_attn_out, w_out=v_w_out, norm2_g=v_norm2_g, w_ff1=v_w_ff1,
             w_ff2=v_w_ff2)
    chip_id = 2 * lax.axis_index("x") + lax.axis_index("y")
    place = jnp.stack([chip_id, lax.axis_index("c")]).astype(jnp.int32)

    dww4 = _all_gather_chips(conv_dw_w, "ag_conv_dw_w")
    dww = dww4.transpose(1, 2, 0, 3).reshape(DEPTH, KW, CONV)

    names = dict(w_in="win4", w_conv_out="wco4", w_attn_out="wao4", w_out="wout4", w_ff1="wff14", w_ff2="wff24")
    chips, pair = {}, {}

    def start_chips(key, l, kinds, after):
        lands = []
        for n in kinds:
            k, ns = w[n].shape[1:]
            land = lax.dynamic_update_slice(lax.empty((NCHIP, k, ns), BF16), w[n][l].astype(BF16)[None], (chip_id, 0, 0))
            lands.append(land.reshape(NCHIP, 2, k // 2, ns))
        chips[key] = _split_start(f"ag_chips_start_{key}", lands, _plan_gather_chips, 3 * len(kinds), after=after)
        return chips[key][2]

    def start_pair(key, after):
        sems, bufs, _ = chips[key]
        bufs = _split_wait(f"ag_chips_wait_{key}", sems, bufs, _plan_gather_chips, after)
        pair[key] = _split_start(f"ag_pair_start_{key}", bufs, _plan_gather_pair, len(bufs) * 3)
        return pair[key][2]

    def landed(key, kinds, after):
        sems, bufs, _ = pair[key]
        bufs = _split_wait(f"ag_pair_wait_{key}", sems, bufs, _plan_gather_pair, after)
        return {names[n]: b.reshape(NCHIP, 2 * b.shape[2], b.shape[3]) for n, b in zip(kinds, bufs)}

    rest = tuple(n for n in BIG if n != "w_in")
    tok0 = start_pair("l0a", start_chips("l0a", 0, ("w_in",), dww4))
    start_chips("l0b", 0, rest, tok0)

    def get_layer(l, after):
        p = dict(dww=dww[l], dwb=conv_dw_b[l][None], lng=conv_ln_g[l][None], lnb=conv_ln_b[l][None],
                 n1g=norm1_g[l][None], n2g=norm2_g[l][None], gq=jnp.tile(q_norm_g[l], 2)[None],
                 gk=jnp.tile(k_norm_g[l], 2)[None])
        if l == 0:
            p.update(landed("l0a", ("w_in",), chips["l0b"][2]))
            p["hook"] = lambda after_: start_chips("l1", 1, BIG, start_pair("l0b", after_))
            p["rest"] = lambda after_: landed("l0b", rest, after_)
            return p, (), None
        if l == 1:
            after = start_pair("l1", after)
        p.update(landed(f"l{l}", BIG, after))
        deps, mid = (), None
        if l + 1 < DEPTH:
            deps = (start_chips(f"l{l + 1}", l + 1, BIG, p["win4"]),)
            mid = lambda after_: start_pair(f"l{l + 1}", after_)
        return p, deps, mid

    results = {}
    make_pipe = lambda l, kinds, tag, big: _GradPipe(l, kinds, tag, big, place, w, m, v, results)
    loss_share, dx, smalls, d_rel = _local_step(x[0], loss_target[0], get_layer, rel_bias, make_pipe)
    loss = lax.psum(loss_share, ("x", "y", "c"))

    local_small = dict(rel_bias=d_rel)
    for n in SMALL[1:]:
        local_small[n] = jnp.stack([smalls[l][n] for l in range(DEPTH)])
    small_shapes = [local_small[n].shape for n in SMALL]
    summed = _sum_devices(_all_gather_devices(_pack([local_small[n] for n in SMALL]), "ag_small"))
    grads = dict(zip(SMALL, _unpack(summed, small_shapes)))
    grads["conv_dw_w"] = lax.dynamic_slice_in_dim(grads["conv_dw_w"], chip_id * 128, 128, axis=2)

    delta, new_m, new_v = {}, {}, {}
    small_w_shapes = [w[n].shape for n in SMALL]
    outs = _adamw(_pack([w[n] for n in SMALL]), _pack([grads[n] for n in SMALL]), _pack([m[n] for n in SMALL]),
                  _pack([v[n] for n in SMALL]), "adamw_small")
    for dst, packed in zip((delta, new_m, new_v), outs):
        dst.update(zip(SMALL, _unpack(packed, small_w_shapes)))

    for n in BIG:
        grads[n], delta[n], new_m[n], new_v[n] = results[n]

    return (loss, dx[None], *[grads[n] for n in WEIGHTS], *[delta[n] for n in WEIGHTS],
            *[new_m[n] for n in WEIGHTS], *[new_v[n] for n in WEIGHTS])
```

```python
import functools
import math

import numpy as np
import jax
import jax.numpy as jnp
from jax import lax
from jax.experimental import pallas as pl
from jax.experimental.pallas import tpu as pltpu

F32 = jnp.float32
BF16 = jnp.bfloat16

T = 2048
D = 1024
DEPTH = 4
CONV = 512
KW = 31
NG = 3
HD = 64
AOUT = 512
DFF = 4096
INC = 7680
DIL = (1, 4, 16)
BLK = 128
NBUCKET = 32
EPS = 1e-6
NEG = -1e30
NCHIP = 4
UB_A, UB_GT, UB_Q, UB_K, UB_V, UB_GC, UB_GA = 0, 1, 2, 5, 8, 11, 13

ADAM_LR, ADAM_B1, ADAM_B2, ADAM_EPS, ADAM_WD, ADAM_STEP = 0.001, 0.9, 0.999, 1e-08, 0.01, 10

VMEM_LIMIT = 48 * 1024 * 1024
TB = 256
HBM_SPEC = pl.BlockSpec(memory_space=pltpu.HBM)
ANY_SPEC = pl.BlockSpec(memory_space=pl.ANY)
SEM_SPEC = pl.BlockSpec(memory_space=pltpu.SEMAPHORE)


def _pcall(body, *, name, out_shape, grid=(), in_specs=None, out_specs=None, scratch=(), aliases=None,
           semantics=None):
    kw = {}
    if in_specs is not None:
        kw["in_specs"] = in_specs
    if out_specs is not None:
        kw["out_specs"] = out_specs
    return pl.pallas_call(
        body, name=name, out_shape=out_shape, grid=grid, scratch_shapes=scratch,
        input_output_aliases=aliases or {},
        compiler_params=pltpu.CompilerParams(vmem_limit_bytes=VMEM_LIMIT, dimension_semantics=semantics),
        **kw)


def _sds(shape, dtype=F32):
    return jax.ShapeDtypeStruct(shape, dtype)


NN = (((1,), (0,)), ((), ()))
NT = (((1,), (1,)), ((), ()))
TN = (((0,), (0,)), ((), ()))


def _mm(name, a, b, *, out_shape, out_dtype, grid, a_spec, b_spec, o_spec, acc_shape, dims, add=None,
        add_spec=None, deps=()):
    nk = grid[2]
    deps = tuple(d for d in deps if d is not None)
    n_scratch = 1 if nk > 1 else 0

    def body(*refs):
        n_out = 1 + n_scratch
        refs = refs[:len(refs) - n_out - len(deps)] + refs[len(refs) - n_out:]
        a_ref, b_ref = refs[0], refs[1]
        r_ref = refs[2] if add is not None else None
        o_ref = refs[-n_out]
        prod = lax.dot_general(a_ref[...].astype(BF16), b_ref[...].astype(BF16), dims, preferred_element_type=F32)
        if nk == 1:
            o_ref[...] = (prod if r_ref is None else prod + r_ref[...]).astype(out_dtype)
            return
        acc_ref = refs[-1]
        k = pl.program_id(2)

        @pl.when(k == 0)
        def _():
            acc_ref[...] = prod

        @pl.when(k > 0)
        def _():
            acc_ref[...] += prod

        @pl.when(k == nk - 1)
        def _():
            res = acc_ref[...]
            if r_ref is not None:
                res = res + r_ref[...]
            o_ref[...] = res.astype(out_dtype)

    ins = ([a, b] if add is None else [a, b, add]) + list(deps)
    specs = ([a_spec, b_spec] if add is None else [a_spec, b_spec, add_spec]) + [ANY_SPEC] * len(deps)
    return _pcall(body, name=name, out_shape=_sds(out_shape, out_dtype), grid=grid, in_specs=specs,
                  out_specs=o_spec, scratch=[pltpu.VMEM(acc_shape, F32)] * n_scratch,
                  semantics=("parallel", "parallel", "arbitrary"))(*ins)


def _mm_x_wcols(name, a, w4, *, tm, tn, out_dtype=F32, deps=()):
    _, k, ns = w4.shape
    nj = ns // tn
    return _mm(name, a, w4, out_shape=(T, NCHIP * ns), out_dtype=out_dtype, grid=(T // tm, NCHIP * nj, 1), deps=deps,
               a_spec=pl.BlockSpec((tm, k), lambda i, j, kk: (i, 0)),
               b_spec=pl.BlockSpec((None, k, tn), lambda i, j, kk: (j // nj, 0, j % nj)),
               o_spec=pl.BlockSpec((tm, tn), lambda i, j, kk: (i, j)), acc_shape=(tm, tn), dims=NN)


def _mm_ff1(a, w4, *, tm, tn):
    _, k, ns = w4.shape
    nj = ns // tn

    def body(a_ref, b_ref, f_ref, r_ref):
        p = jnp.maximum(jnp.dot(a_ref[...], b_ref[...], preferred_element_type=F32), 0.0)
        f_ref[...] = p.astype(BF16)
        r_ref[...] = (p * p).astype(BF16)

    out = pl.BlockSpec((tm, tn), lambda i, j: (i, j))
    return _pcall(body, name="mm_f", out_shape=(_sds((T, DFF), BF16), _sds((T, DFF), BF16)), grid=(T // tm, NCHIP * nj),
                  in_specs=[pl.BlockSpec((tm, k), lambda i, j: (i, 0)),
                            pl.BlockSpec((None, k, tn), lambda i, j: (j // nj, 0, j % nj))],
                  out_specs=(out, out), semantics=("parallel", "parallel"))(a, w4)


def _mm_x_wrows(name, a, w4, add, *, tm, tk, tn, deps=()):
    _, ks, n = w4.shape
    nkk = ks // tk
    return _mm(name, a, w4, out_shape=(T, n), out_dtype=F32, grid=(T // tm, n // tn, NCHIP * nkk), deps=deps,
               a_spec=pl.BlockSpec((tm, tk), lambda i, j, kk: (i, kk)),
               b_spec=pl.BlockSpec((None, tk, tn), lambda i, j, kk: (kk // nkk, kk % nkk, j)),
               o_spec=pl.BlockSpec((tm, tn), lambda i, j, kk: (i, j)), acc_shape=(tm, tn), dims=NN,
               add=add, add_spec=pl.BlockSpec((tm, tn), lambda i, j, kk: (i, j)))


def _mm_g_wcols_t(name, g, w4, *, tm, tk, tn, out_dtype=F32, deps=()):
    _, k, ns = w4.shape
    nkk = ns // tk
    return _mm(name, g, w4, out_shape=(T, k), out_dtype=out_dtype, grid=(T // tm, k // tn, NCHIP * nkk), deps=deps,
               a_spec=pl.BlockSpec((tm, tk), lambda i, j, kk: (i, kk)),
               b_spec=pl.BlockSpec((None, tn, tk), lambda i, j, kk: (kk // nkk, j, kk % nkk)),
               o_spec=pl.BlockSpec((tm, tn), lambda i, j, kk: (i, j)), acc_shape=(tm, tn), dims=NT)


def _mm_g_wrows_t(name, g, w4, *, tm, tn, out_dtype=F32, deps=()):
    _, ks, n = w4.shape
    nj = ks // tn
    return _mm(name, g, w4, out_shape=(T, NCHIP * ks), out_dtype=out_dtype, grid=(T // tm, NCHIP * nj, 1), deps=deps,
               a_spec=pl.BlockSpec((tm, n), lambda i, j, kk: (i, 0)),
               b_spec=pl.BlockSpec((None, tn, n), lambda i, j, kk: (j // nj, j % nj, 0)),
               o_spec=pl.BlockSpec((tm, tn), lambda i, j, kk: (i, j)), acc_shape=(tm, tn), dims=NT)


def _mm_dff2(dx, w4, fa, *, tm, tn, deps=()):
    _, ks, n = w4.shape
    nj = ks // tn
    deps = tuple(d for d in deps if d is not None)

    def body(*refs):
        dx_ref, b_ref, f_ref = refs[:3]
        df_ref = refs[-1]
        dr = lax.dot_general(dx_ref[...].astype(BF16), b_ref[...], NT, preferred_element_type=F32)
        df_ref[...] = (dr * (2.0 * f_ref[...].astype(F32))).astype(BF16)

    out = pl.BlockSpec((tm, tn), lambda i, j: (i, j))
    return _pcall(body, name="mm_dr", out_shape=_sds((T, DFF), BF16), grid=(T // tm, NCHIP * nj),
                  in_specs=[pl.BlockSpec((tm, n), lambda i, j: (i, 0)),
                            pl.BlockSpec((None, tn, n), lambda i, j: (j // nj, j % nj, 0)), out]
                  + [ANY_SPEC] * len(deps),
                  out_specs=out, semantics=("parallel", "parallel"))(dx, w4, fa, *deps)


TCH = 512


def _mm_dw(name, a, g, *, out_shape, out_map, tm, tn, deps=()):
    deps = tuple(d for d in deps if d is not None)

    def body(*refs):
        a_ref, g_ref = refs[:2]
        o_ref, at_ref = refs[-2:]

        @pl.when(pl.program_id(1) == 0)
        def _():
            for c in range(T // TCH):
                at_ref[:, c * TCH:(c + 1) * TCH] = a_ref[c * TCH:(c + 1) * TCH, :].T

        o_ref[...] = jnp.dot(at_ref[...], g_ref[...].astype(BF16), preferred_element_type=F32)

    return _pcall(body, name=name, out_shape=_sds(out_shape), grid=(a.shape[1] // tm, g.shape[1] // tn),
                  in_specs=[pl.BlockSpec((T, tm), lambda i, j: (0, i)), pl.BlockSpec((T, tn), lambda i, j: (0, j))]
                  + [ANY_SPEC] * len(deps),
                  out_specs=pl.BlockSpec((None, tm, tn), out_map), scratch=[pltpu.VMEM((tm, T), BF16)],
                  semantics=("parallel", "arbitrary"))(a, g, *deps)


def _mm_dw_cols(name, a, g, *, ns, tm, tn, deps=()):
    nj = ns // tn
    return _mm_dw(name, a, g, out_shape=(NCHIP, a.shape[1], ns), out_map=lambda i, j: (j // nj, i, j % nj),
                  tm=tm, tn=tn, deps=deps)


def _mm_dw_rows(name, a, g, *, ks, tm, tn):
    ni = ks // tm
    return _mm_dw(name, a, g, out_shape=(NCHIP, ks, g.shape[1]), out_map=lambda i, j: (i // ni, i % ni, j),
                  tm=tm, tn=tn)


def _row_spec(width, col=0):
    return pl.BlockSpec((TB, width), lambda i: (i, col))


def _vec_spec(width):
    return pl.BlockSpec((1, width), lambda i: (0, 0))


def _rms_fwd(x, g):
    def body(x_ref, g_ref, h_ref):
        x = x_ref[...]
        r = lax.rsqrt(jnp.mean(x * x, axis=-1, keepdims=True) + EPS)
        h_ref[...] = (x * r * g_ref[...]).astype(BF16)

    return _pcall(body, name="rms_fwd", out_shape=_sds((T, D), BF16), grid=(T // TB,),
                  in_specs=[_row_spec(D), _vec_spec(D)], out_specs=_row_spec(D), semantics=("parallel",))(x, g)


def _rms_bwd(x, g, dh, dres, deps=()):
    deps = tuple(d for d in deps if d is not None)

    def body(*refs):
        x_ref, g_ref, dh_ref, dres_ref = refs[:4]
        dx_ref, dg_ref = refs[-2:]
        x = x_ref[...]
        r = lax.rsqrt(jnp.mean(x * x, axis=-1, keepdims=True) + EPS)
        y = x * r
        dh = dh_ref[...]
        dy = dh * g_ref[...]
        dx_ref[...] = dres_ref[...] + r * (dy - y * jnp.mean(dy * y, axis=-1, keepdims=True))

        @pl.when(pl.program_id(0) == 0)
        def _():
            dg_ref[...] = jnp.zeros_like(dg_ref)

        dg_ref[...] += jnp.sum(dh * y, axis=0, keepdims=True)

    return _pcall(body, name="rms_bwd", out_shape=(_sds((T, D)), _sds((1, D))), grid=(T // TB,),
                  in_specs=[_row_spec(D), _vec_spec(D), _row_spec(D), _row_spec(D)] + [ANY_SPEC] * len(deps),
                  out_specs=(_row_spec(D), _vec_spec(D)), semantics=("arbitrary",))(x, g, dh, dres, *deps)


def _sigmoid(x):
    return 1.0 / (1.0 + jnp.exp(-x))


def _gate_fwd(u, ycv, yat):
    def body(gc_ref, ga_ref, yc_ref, ya_ref, m_ref):
        m_ref[...] = (_sigmoid(gc_ref[...].astype(F32)) * yc_ref[...]
                      + _sigmoid(ga_ref[...].astype(F32)) * ya_ref[...]).astype(BF16)

    blk = lambda off: pl.BlockSpec((TB, 512), lambda i, j: (i, off + j))
    return _pcall(body, name="gate_fwd", out_shape=_sds((T, D), BF16), grid=(T // TB, 2),
                  in_specs=[blk(UB_GC), blk(UB_GA), blk(0), blk(0)], out_specs=blk(0),
                  semantics=("parallel", "parallel"))(u, u, ycv, yat)


def _gate_bwd(u, ycv, yat, dm):
    def body(gc_ref, ga_ref, yc_ref, ya_ref, dm_ref, dyc_ref, dya_ref, dgc_ref, dga_ref):
        dm = dm_ref[...]
        sc = _sigmoid(gc_ref[...].astype(F32))
        sa = _sigmoid(ga_ref[...].astype(F32))
        dyc_ref[...] = (dm * sc).astype(BF16)
        dya_ref[...] = (dm * sa).astype(BF16)
        dgc_ref[...] = (dm * yc_ref[...] * sc * (1.0 - sc)).astype(BF16)
        dga_ref[...] = (dm * ya_ref[...] * sa * (1.0 - sa)).astype(BF16)

    blk = lambda off: pl.BlockSpec((TB, 512), lambda i, j: (i, off + j))
    return _pcall(body, name="gate_bwd",
                  out_shape=(_sds((T, D), BF16), _sds((T, D), BF16), _sds((T, D), BF16), _sds((T, D), BF16)),
                  grid=(T // TB, 2), in_specs=[blk(UB_GC), blk(UB_GA), blk(0), blk(0), blk(0)],
                  out_specs=(blk(0), blk(0), blk(0), blk(0)),
                  semantics=("parallel", "parallel"))(u, u, ycv, yat, dm)


def _relu2_fwd(f):
    def body(f_ref, r_ref):
        a = jnp.maximum(f_ref[...], 0.0)
        r_ref[...] = (a * a).astype(BF16)

    blk = pl.BlockSpec((TB, 1024), lambda i, j: (i, j))
    return _pcall(body, name="relu2_fwd", out_shape=_sds((T, DFF), BF16), grid=(T // TB, DFF // 1024),
                  in_specs=[blk], out_specs=blk, semantics=("parallel", "parallel"))(f)


def _relu2_bwd(f, dr):
    def body(f_ref, dr_ref, r_ref, df_ref):
        a = jnp.maximum(f_ref[...], 0.0)
        r_ref[...] = (a * a).astype(BF16)
        df_ref[...] = (dr_ref[...] * (2.0 * a)).astype(BF16)

    blk = pl.BlockSpec((TB, 1024), lambda i, j: (i, j))
    return _pcall(body, name="relu2_bwd", out_shape=(_sds((T, DFF), BF16), _sds((T, DFF), BF16)),
                  grid=(T // TB, DFF // 1024), in_specs=[blk, blk], out_specs=(blk, blk),
                  semantics=("parallel", "parallel"))(f, dr)


def _loss_fwd_bwd(y, target):
    def body(y_ref, t_ref, loss_ref, dy_ref):
        e = y_ref[...] - t_ref[...]
        dy_ref[...] = e * (1.0 / D)

        @pl.when(pl.program_id(0) == 0)
        def _():
            loss_ref[...] = jnp.zeros_like(loss_ref)

        loss_ref[...] += 0.5 * jnp.sum(jnp.mean(e * e, axis=-1, keepdims=True))

    return _pcall(body, name="loss", out_shape=(_sds((8, 128)), _sds((T, D))), grid=(T // TB,),
                  in_specs=[_row_spec(D), _row_spec(D)],
                  out_specs=(pl.BlockSpec((8, 128), lambda i: (0, 0)), _row_spec(D)),
                  semantics=("arbitrary",))(y, target)


PAD = 32
CCH = 256


def _conv_fwd(u, dw_w, dw_b):
    def body(a_ref, gt_ref, w_ref, b_ref, z1_ref, zp_ref):
        zp_ref[0:PAD, :] = jnp.zeros((PAD, 128), F32)
        zp_ref[PAD:PAD + T, :] = a_ref[...].astype(F32) * _sigmoid(gt_ref[...].astype(F32))
        for c in range(T // CCH):
            acc = jnp.broadcast_to(b_ref[...], (CCH, 128))
            for j in range(KW):
                acc = acc + w_ref[j:j + 1, :] * zp_ref[pl.ds(c * CCH + j + PAD - (KW - 1), CCH), :]
            z1_ref[c * CCH:(c + 1) * CCH, :] = acc

    col = lambda off: pl.BlockSpec((T, 128), lambda j: (0, off * 4 + j))
    return _pcall(body, name="conv_fwd", out_shape=_sds((T, CONV)), grid=(CONV // 128,),
                  in_specs=[col(UB_A), col(UB_GT), pl.BlockSpec((KW, 128), lambda j: (0, j)),
                            pl.BlockSpec((1, 128), lambda j: (0, j))],
                  out_specs=col(0), scratch=[pltpu.VMEM((T + PAD, 128), F32)],
                  semantics=("parallel",))(u, u, dw_w, dw_b)


def _ln_silu_fwd(z1, g, b):
    def body(z_ref, g_ref, b_ref, o_ref):
        z = z_ref[...]
        mu = jnp.mean(z, axis=-1, keepdims=True)
        zc = z - mu
        zh = zc * lax.rsqrt(jnp.mean(zc * zc, axis=-1, keepdims=True) + EPS)
        z2 = zh * g_ref[...] + b_ref[...]
        o_ref[...] = (z2 * _sigmoid(z2)).astype(BF16)

    return _pcall(body, name="ln_silu_fwd", out_shape=_sds((T, CONV), BF16), grid=(T // TB,),
                  in_specs=[_row_spec(CONV), _vec_spec(CONV), _vec_spec(CONV)], out_specs=_row_spec(CONV),
                  semantics=("parallel",))(z1, g, b)


def _ln_silu_bwd(z1, g, b, dz3):
    def body(z_ref, g_ref, b_ref, d_ref, z3_ref, dz1_ref, dg_ref, db_ref):
        z = z_ref[...]
        mu = jnp.mean(z, axis=-1, keepdims=True)
        zc = z - mu
        rs = lax.rsqrt(jnp.mean(zc * zc, axis=-1, keepdims=True) + EPS)
        zh = zc * rs
        z2 = zh * g_ref[...] + b_ref[...]
        s = _sigmoid(z2)
        z3_ref[...] = (z2 * s).astype(BF16)
        dz2 = d_ref[...] * (s * (1.0 + z2 * (1.0 - s)))
        dzh = dz2 * g_ref[...]
        dz1_ref[...] = rs * (dzh - jnp.mean(dzh, axis=-1, keepdims=True)
                             - zh * jnp.mean(dzh * zh, axis=-1, keepdims=True))

        @pl.when(pl.program_id(0) == 0)
        def _():
            dg_ref[...] = jnp.zeros_like(dg_ref)
            db_ref[...] = jnp.zeros_like(db_ref)

        dg_ref[...] += jnp.sum(dz2 * zh, axis=0, keepdims=True)
        db_ref[...] += jnp.sum(dz2, axis=0, keepdims=True)

    return _pcall(body, name="ln_silu_bwd",
                  out_shape=(_sds((T, CONV), BF16), _sds((T, CONV)), _sds((1, CONV)), _sds((1, CONV))),
                  grid=(T // TB,),
                  in_specs=[_row_spec(CONV), _vec_spec(CONV), _vec_spec(CONV), _row_spec(CONV)],
                  out_specs=(_row_spec(CONV), _row_spec(CONV), _vec_spec(CONV), _vec_spec(CONV)),
                  semantics=("arbitrary",))(z1, g, b, dz3)


def _conv_bwd(u, dw_w, dz1):
    def body(a_ref, gt_ref, w_ref, dz1_ref, da_ref, dgt_ref, dw_ref, db_ref, zp_ref, dp_ref):
        a = a_ref[...].astype(F32)
        s = _sigmoid(gt_ref[...].astype(F32))
        zp_ref[0:PAD, :] = jnp.zeros((PAD, 128), F32)
        zp_ref[PAD:PAD + T, :] = a * s
        dp_ref[0:T, :] = dz1_ref[...]
        dp_ref[T:T + PAD, :] = jnp.zeros((PAD, 128), F32)
        db_ref[...] = jnp.sum(dz1_ref[...], axis=0, keepdims=True)
        for j in range(KW):
            tot = jnp.zeros((1, 128), F32)
            for c in range(T // CCH):
                tot = tot + jnp.sum(dz1_ref[c * CCH:(c + 1) * CCH, :]
                                    * zp_ref[pl.ds(c * CCH + j + PAD - (KW - 1), CCH), :], axis=0, keepdims=True)
            dw_ref[j:j + 1, :] = tot
        for c in range(T // CCH):
            acc = jnp.zeros((CCH, 128), F32)
            for j in range(KW):
                acc = acc + w_ref[j:j + 1, :] * dp_ref[pl.ds(c * CCH + (KW - 1) - j, CCH), :]
            rows = slice(c * CCH, (c + 1) * CCH)
            sc = _sigmoid(gt_ref[rows, :].astype(F32))
            da_ref[rows, :] = (acc * sc).astype(BF16)
            dgt_ref[rows, :] = (acc * a_ref[rows, :].astype(F32) * sc * (1.0 - sc)).astype(BF16)

    col = lambda off: pl.BlockSpec((T, 128), lambda j: (0, off * 4 + j))
    wspec = pl.BlockSpec((KW, 128), lambda j: (0, j))
    return _pcall(body, name="conv_bwd",
                  out_shape=(_sds((T, CONV), BF16), _sds((T, CONV), BF16), _sds((KW, CONV)), _sds((1, CONV))),
                  grid=(CONV // 128,), in_specs=[col(UB_A), col(UB_GT), wspec, col(0)],
                  out_specs=(col(0), col(0), wspec, pl.BlockSpec((1, 128), lambda j: (0, j))),
                  scratch=[pltpu.VMEM((T + PAD, 128), F32), pltpu.VMEM((T + PAD, 128), F32)],
                  semantics=("parallel",))(u, u, dw_w, dz1)


def _bucket_tables():
    qi = np.arange(BLK)[:, None]
    kj = np.arange(2 * BLK)[None, :]
    off = np.clip(qi + BLK - kj, 0, BLK)
    out = []
    for d in DIL:
        dist = (off * d).astype(np.int32)
        nf = np.maximum(dist, 1).astype(np.float32)
        large = 16 + (np.log(nf / np.float32(16)) / np.float32(math.log(2048 / 16)) * np.float32(16)).astype(np.int32)
        large = np.minimum(large, NBUCKET - 1)
        out.append(np.where(dist < 16, dist, large))
    return np.stack(out).astype(np.int32)


def _band():
    off = lax.broadcasted_iota(jnp.int32, (BLK, 2 * BLK), 0) + BLK - lax.broadcasted_iota(jnp.int32, (BLK, 2 * BLK), 1)
    return (off >= 0) & (off <= BLK)


def _bias_table(rel_bias_t, buckets):
    def body(rb_ref, bk_ref, o_ref):
        h = pl.program_id(0)
        bk = bk_ref[...]
        acc = jnp.zeros((BLK, 2 * BLK), F32)
        for b in range(NBUCKET):
            acc = jnp.where(bk == b, rb_ref[h, b], acc)
        o_ref[...] = jnp.where(_band(), acc, NEG)

    return _pcall(body, name="bias_table", out_shape=_sds((3 * 8, BLK, 2 * BLK)), grid=(24,),
                  in_specs=[pl.BlockSpec(memory_space=pltpu.SMEM),
                            pl.BlockSpec((None, BLK, 2 * BLK), lambda h: (h // 8, 0, 0))],
                  out_specs=pl.BlockSpec((None, BLK, 2 * BLK), lambda h: (h, 0, 0)),
                  semantics=("parallel",))(rel_bias_t, buckets)


def _bias_grad(ds_acc, buckets):
    def body(a_ref, bk_ref, o_ref):
        acc = a_ref[0]
        for l in range(1, DEPTH):
            acc = acc + a_ref[l]
        bk = bk_ref[...]
        lane = lax.broadcasted_iota(jnp.int32, (1, 128), 1)
        row = jnp.zeros((1, 128), F32)
        for b in range(NBUCKET):
            row = jnp.where(lane == b, jnp.sum(jnp.where(bk == b, acc, 0.0)), row)
        o_ref[...] = row

    return _pcall(body, name="bias_grad", out_shape=_sds((24, 1, 128)), grid=(24,),
                  in_specs=[pl.BlockSpec((DEPTH, None, BLK, 2 * BLK), lambda h: (0, h, 0, 0)),
                            pl.BlockSpec((None, BLK, 2 * BLK), lambda h: (h // 8, 0, 0))],
                  out_specs=pl.BlockSpec((None, 1, 128), lambda h: (h, 0, 0)),
                  semantics=("parallel",))(ds_acc, buckets)


def _head_mask():
    return lax.broadcasted_iota(jnp.int32, (1, 128), 1) < HD


def _seg_ones(width):
    r = lax.broadcasted_iota(jnp.int32, (width, width), 0) >> 6
    c = lax.broadcasted_iota(jnp.int32, (width, width), 1) >> 6
    return (r == c).astype(BF16)


def _seg_sum(x, ones):
    hi = x.astype(BF16)
    lo = (x - hi.astype(F32)).astype(BF16)
    return (jnp.dot(hi, ones, preferred_element_type=F32) + jnp.dot(lo, ones, preferred_element_type=F32))


def _dot(a, b, dims):
    return lax.dot_general(a, b, dims, preferred_element_type=F32)


def _tile_rows(d, r, n):
    stride = None if d == 1 else d
    q_rows = pl.ds(r + d * n * BLK, BLK, stride=stride)
    if n == 0:
        return q_rows, q_rows, BLK
    return q_rows, pl.ds(r + d * (n - 1) * BLK, 2 * BLK, stride=stride), 2 * BLK


def _stack_heads(x, m_a):
    return jnp.concatenate([jnp.where(m_a, x, 0.0), jnp.where(m_a, 0.0, x)], axis=0)


def _stack_cols(x, m_a):
    return jnp.concatenate([jnp.max(jnp.where(m_a, x, -3e38), axis=1, keepdims=True),
                            jnp.max(jnp.where(m_a, -3e38, x), axis=1, keepdims=True)], axis=0)


NCH = 256


def _qk_norm_prep(q_ref, k_ref, v_ref, gq_ref, gk_ref, qn_ref, kn_ref, vn_ref, ones):
    def prep(i, carry):
        rows = pl.ds(pl.multiple_of(i * NCH, NCH), NCH)
        q = q_ref[rows, :].astype(F32)
        qn_ref[rows, :] = q * lax.rsqrt(_seg_sum(q * q, ones) * (1.0 / HD) + EPS) * gq_ref[...] * (HD ** -0.5)
        k = k_ref[rows, :].astype(F32)
        kn_ref[rows, :] = k * lax.rsqrt(_seg_sum(k * k, ones) * (1.0 / HD) + EPS) * gk_ref[...]
        vn_ref[rows, :] = v_ref[rows, :].astype(F32)
        return carry

    lax.fori_loop(0, T // NCH, prep, 0)


def _attn_specs(g):
    ucol = lambda base: pl.BlockSpec((T, 128), lambda hp: (0, (base + g) * 4 + hp))
    col = pl.BlockSpec((T, 128), lambda hp: (0, hp))
    vec = pl.BlockSpec((1, 128), lambda hp: (0, 0))
    bm = pl.BlockSpec((2, BLK, 2 * BLK), lambda hp: (g * 4 + hp, 0, 0))
    return ucol, col, vec, bm


def _attn_fwd(g, u, gq, gk, bm, deps=()):
    d = DIL[g]

    def body(*refs):
        q_ref, k_ref, v_ref, gq_ref, gk_ref, bm_ref = refs[:6]
        o_ref, lse_ref, qn_ref, kn_ref, vn_ref = refs[-5:]
        ones = _seg_ones(128)
        _qk_norm_prep(q_ref, k_ref, v_ref, gq_ref, gk_ref, qn_ref, kn_ref, vn_ref, ones)
        m_a = _head_mask()
        for r in range(d):
            for n in range(T // d // BLK):
                q_rows, k_rows, nk = _tile_rows(d, r, n)
                qt = qn_ref[q_rows, :]
                kt = kn_ref[k_rows, :].astype(BF16)
                vt = vn_ref[k_rows, :].astype(BF16)
                q2 = _stack_heads(qt, m_a).astype(BF16)
                s = _dot(q2, kt, NT) + bm_ref[...].reshape(2 * BLK, 2 * BLK)[:, 2 * BLK - nk:]
                mx = jnp.max(s, axis=1, keepdims=True)
                p = jnp.exp(s - mx)
                l = jnp.sum(p, axis=1, keepdims=True)
                o2 = _dot(p.astype(BF16), vt, NN) / l
                lse2 = jnp.broadcast_to(mx + jnp.log(l), (2 * BLK, 128))
                o_ref[q_rows, :] = jnp.where(m_a, o2[:BLK], o2[BLK:])
                lse_ref[q_rows, :] = jnp.where(m_a, lse2[:BLK], lse2[BLK:])

    ucol, col, vec, bmspec = _attn_specs(g)
    return _pcall(body, name=f"attn_fwd_g{g}", out_shape=(_sds((T, AOUT)), _sds((T, AOUT))), grid=(4,),
                  in_specs=[ucol(UB_Q), ucol(UB_K), ucol(UB_V), vec, vec, bmspec] + [ANY_SPEC] * len(deps),
                  out_specs=(col, col), scratch=[pltpu.VMEM((T, 128), F32)] * 3,
                  semantics=("parallel",))(u, u, u, gq, gk, bm, *deps)


def _attn_bwd(g, u, gq, gk, bm, dog, cb, lse):
    d = DIL[g]

    def body(q_ref, k_ref, v_ref, gq_ref, gk_ref, bm_ref, do_ref, cb_ref, lse_ref,
             dqo_ref, dko_ref, dvo_ref, dgq_ref, dgk_ref, dsa_ref, qn_ref, kn_ref, vn_ref, dq_ref, dk_ref, dv_ref):
        ones = _seg_ones(128)
        _qk_norm_prep(q_ref, k_ref, v_ref, gq_ref, gk_ref, qn_ref, kn_ref, vn_ref, ones)
        m_a = _head_mask()
        dk_ref[...] = jnp.zeros_like(dk_ref)
        dv_ref[...] = jnp.zeros_like(dv_ref)
        dsa_ref[...] = jnp.zeros_like(dsa_ref)
        for r in range(d):
            for n in range(T // d // BLK):
                q_rows, k_rows, nk = _tile_rows(d, r, n)
                ktb = kn_ref[k_rows, :].astype(BF16)
                vtb = vn_ref[k_rows, :].astype(BF16)
                q2 = _stack_heads(qn_ref[q_rows, :], m_a).astype(BF16)
                do2 = _stack_heads(do_ref[q_rows, :], m_a).astype(BF16)
                lse_c = _stack_cols(lse_ref[q_rows, :], m_a)
                c_c = _stack_cols(cb_ref[q_rows, :], m_a)
                s = _dot(q2, ktb, NT) + bm_ref[...].reshape(2 * BLK, 2 * BLK)[:, 2 * BLK - nk:]
                p = jnp.exp(s - lse_c)
                ds = p * (_dot(do2, vtb, NT) + c_c)
                dsb = ds.astype(BF16)
                dq2 = _dot(dsb, ktb, NN)
                dq_ref[q_rows, :] = jnp.where(m_a, dq2[:BLK], dq2[BLK:])
                dk_ref[k_rows, :] += _dot(dsb, q2, TN)
                dv_ref[k_rows, :] += _dot(p.astype(BF16), do2, TN)
                dsa_ref[:, :, 2 * BLK - nk:] += ds.reshape(2, BLK, nk)

        @pl.when(pl.program_id(0) == 0)
        def _():
            dgq_ref[...] = jnp.zeros_like(dgq_ref)
            dgk_ref[...] = jnp.zeros_like(dgk_ref)

        def norm_bwd(i, carry):
            rows = pl.ds(pl.multiple_of(i * NCH, NCH), NCH)
            for x_ref, g_ref, dx_ref, dxo_ref, dg_ref, scale in (
                    (q_ref, gq_ref, dq_ref, dqo_ref, dgq_ref, HD ** -0.5), (k_ref, gk_ref, dk_ref, dko_ref, dgk_ref, 1.0)):
                x = x_ref[rows, :].astype(F32)
                rs = lax.rsqrt(_seg_sum(x * x, ones) * (1.0 / HD) + EPS)
                xh = x * rs
                dn = dx_ref[rows, :] * scale
                dxh = dn * g_ref[...]
                dxo_ref[rows, :] = (rs * (dxh - xh * (_seg_sum(dxh * xh, ones) * (1.0 / HD)))).astype(BF16)
                dg_ref[...] += jnp.sum(dn * xh, axis=0, keepdims=True)
            dvo_ref[rows, :] = dv_ref[rows, :].astype(BF16)
            return carry

        lax.fori_loop(0, T // NCH, norm_bwd, 0)

    ucol, col, vec, bmspec = _attn_specs(g)
    return _pcall(body, name=f"attn_bwd_g{g}",
                  out_shape=(_sds((T, AOUT), BF16), _sds((T, AOUT), BF16), _sds((T, AOUT), BF16), _sds((1, 128)),
                             _sds((1, 128)), _sds((8, BLK, 2 * BLK))),
                  grid=(4,),
                  in_specs=[ucol(UB_Q), ucol(UB_K), ucol(UB_V), vec, vec, bmspec, col, col, col],
                  out_specs=(col, col, col, vec, vec, pl.BlockSpec((2, BLK, 2 * BLK), lambda hp: (hp, 0, 0))),
                  scratch=[pltpu.VMEM((T, 128), F32)] * 6,
                  semantics=("arbitrary",))(u, u, u, gq, gk, bm, dog, cb, lse)


def _combine_fwd(ogs, lses):
    def body(o0, o1, o2, l0, l1, l2, o_ref):
        ls = [l0[...], l1[...], l2[...]]
        mx = jnp.maximum(jnp.maximum(ls[0], ls[1]), ls[2])
        es = [jnp.exp(l - mx) for l in ls]
        inv = 1.0 / (es[0] + es[1] + es[2])
        o_ref[...] = ((es[0] * o0[...] + es[1] * o1[...] + es[2] * o2[...]) * inv).astype(BF16)

    return _pcall(body, name="combine_fwd", out_shape=_sds((T, AOUT), BF16), grid=(T // TB,),
                  in_specs=[_row_spec(AOUT)] * 6, out_specs=_row_spec(AOUT), semantics=("parallel",))(*ogs, *lses)


def _combine_bwd(ogs, lses, do):
    def body(o0, o1, o2, l0, l1, l2, do_ref, d0, d1, d2, c0, c1, c2):
        ls = [l0[...], l1[...], l2[...]]
        mx = jnp.maximum(jnp.maximum(ls[0], ls[1]), ls[2])
        es = [jnp.exp(l - mx) for l in ls]
        inv = 1.0 / (es[0] + es[1] + es[2])
        ws = [e * inv for e in es]
        do = do_ref[...]
        o = ws[0] * o0[...] + ws[1] * o1[...] + ws[2] * o2[...]
        s = _seg_sum(do * o, _seg_ones(AOUT))
        for w, d_ref, c_ref in zip(ws, (d0, d1, d2), (c0, c1, c2)):
            d_ref[...] = w * do
            c_ref[...] = -(w * s)

    return _pcall(body, name="combine_bwd", out_shape=tuple(_sds((T, AOUT)) for _ in range(6)), grid=(T // TB,),
                  in_specs=[_row_spec(AOUT)] * 7, out_specs=tuple(_row_spec(AOUT) for _ in range(6)),
                  semantics=("parallel",))(*ogs, *lses, do)


def _layer_fwd(x, p, bm, deps=(), mid=None):
    h1 = _rms_fwd(x, p["n1g"])
    u = _mm_x_wcols("mm_u", h1, p["win4"], tm=T, tn=640, out_dtype=BF16, deps=deps)
    ogs, lses = [], []
    for g in range(NG):
        gdeps = (p["hook"](ogs[-1]),) if g == NG - 1 and "hook" in p else ()
        og, lse = _attn_fwd(g, u, p["gq"], p["gk"], bm, deps=gdeps)
        ogs.append(og)
        lses.append(lse)
    o = _combine_fwd(ogs, lses)
    z1 = _conv_fwd(u, p["dww"], p["dwb"])
    z3 = _ln_silu_fwd(z1, p["lng"], p["lnb"])
    if "rest" in p:
        p = {**p, **p["rest"](z3)}
    ycv = _mm_x_wcols("mm_ycv", z3, p["wco4"], tm=T, tn=256)
    yat = _mm_x_wcols("mm_yat", o, p["wao4"], tm=T, tn=256)
    m = _gate_fwd(u, ycv, yat)
    tok = mid(m) if mid else None
    xm = _mm_x_wrows("mm_xmid", m, p["wout4"], x, tm=1024, tk=256, tn=1024, deps=(tok,))
    h2 = _rms_fwd(xm, p["n2g"])
    fa, r = _mm_ff1(h2, p["wff14"], tm=T, tn=512)
    xo = _mm_x_wrows("mm_xout", r, p["wff24"], xm, tm=1024, tk=1024, tn=1024)
    saved = dict(x=x, h1=h1, u=u, z1=z1, ogs=ogs, lses=lses, o=o, ycv=ycv, yat=yat, m=m, xm=xm, h2=h2, fa=fa, r=r)
    return xo, p, saved


EARLY = ("w_ff2", "w_ff1", "w_out")
LATE = ("w_conv_out", "w_attn_out", "w_in")


def _layer_bwd(dx, s, p, bm, pipe=None, own_early=None, own_late=None):
    u = s["u"]
    tok = pipe.step0() if pipe else None
    df = _mm_dff2(dx, p["wff24"], s["fa"], tm=1024, tn=512, deps=(tok,))
    g_ff2 = _mm_dw_rows("mm_dwff2", s["r"], dx, ks=1024, tm=1024, tn=512)
    g_ff1 = _mm_dw_cols("mm_dwff1", s["h2"], df, ns=1024, tm=1024, tn=512)
    tok = pipe.step1(g_ff1) if pipe else None
    dh2 = _mm_g_wcols_t("mm_dh2", df, p["wff14"], tm=1024, tk=1024, tn=1024, deps=(tok,))
    dxm, d_n2g = _rms_bwd(s["xm"], p["n2g"], dh2, dx)

    dm = _mm_g_wrows_t("mm_dm", dxm, p["wout4"], tm=1024, tn=256)
    g_out = _mm_dw_rows("mm_dwout", s["m"], dxm, ks=256, tm=256, tn=512)
    early = own_early(dict(w_ff2=g_ff2, w_ff1=g_ff1, w_out=g_out)) if own_early else None
    tok_e = early.step0() if early else None
    dyc, dya, dgc, dga = _gate_bwd(u, s["ycv"], s["yat"], dm)

    dz3 = _mm_g_wcols_t("mm_dz3", dyc, p["wco4"], tm=T, tk=256, tn=512, deps=(tok_e,))
    z3, dz1, d_lng, d_lnb = _ln_silu_bwd(s["z1"], p["lng"], p["lnb"], dz3)
    g_co = _mm_dw_cols("mm_dwco", z3, dyc, ns=256, tm=512, tn=256)
    da, dgt, d_dww, d_dwb = _conv_bwd(u, p["dww"], dz1)

    tok_e = early.step1(da) if early else None
    do = _mm_g_wcols_t("mm_do", dya, p["wao4"], tm=T, tk=256, tn=512, deps=(tok_e,))
    g_ao = _mm_dw_cols("mm_dwao", s["o"], dya, ns=256, tm=512, tn=256)
    parts = _combine_bwd(s["ogs"], s["lses"], do)
    dqs, dks, dvs, d_gq, d_gk, dsas = [], [], [], [], [], []
    for g in range(NG):
        dq, dk, dv, dgq, dgk, dsa = _attn_bwd(g, u, p["gq"], p["gk"], bm, parts[g], parts[NG + g], s["lses"][g])
        dqs.append(dq)
        dks.append(dk)
        dvs.append(dv)
        d_gq.append(dgq)
        d_gk.append(dgk)
        dsas.append(dsa)
    du = jnp.concatenate([da, dgt] + dqs + dks + dvs + [dgc, dga], axis=1)
    tok = pipe.step2(du) if pipe else None
    tok_e = early.step2(du) if early else None
    g_in = _mm_dw_cols("mm_dwin", s["h1"], du, ns=1920, tm=1024, tn=640, deps=(tok, tok_e))
    late = own_late(dict(w_in=g_in, w_conv_out=g_co, w_attn_out=g_ao)) if own_late else None
    tok_l = late.step0() if late else None
    dh1 = _mm_g_wcols_t("mm_dh1", du, p["win4"], tm=1024, tk=1920, tn=1024, deps=(tok_l,))
    tok_l = late.step1(dh1) if late else None
    dxi, d_n1g = _rms_bwd(s["x"], p["n1g"], dh1, dxm, deps=(tok_l,))
    if pipe:
        pipe.step3(dxi)
    if early:
        early.step3(dxi)
    if late:
        late.step3(late.step2(dxi))

    fold = lambda parts_: sum(v[0, :HD] + v[0, HD:] for v in parts_)
    big = dict(w_in=g_in, w_conv_out=g_co, w_attn_out=g_ao, w_out=g_out, w_ff1=g_ff1, w_ff2=g_ff2)
    small = dict(norm1_g=d_n1g[0], q_norm_g=fold(d_gq), k_norm_g=fold(d_gk), conv_dw_w=d_dww, conv_dw_b=d_dwb[0],
                 conv_ln_g=d_lng[0], conv_ln_b=d_lnb[0], norm2_g=d_n2g[0])
    return dxi, big, small, jnp.concatenate(dsas, axis=0)


def _local_step(x, target, get_layer, rel_bias, make_pipe):
    buckets = jnp.asarray(_bucket_tables())
    bm = _bias_table(rel_bias.T, buckets)
    saved, layers = [], []
    for l in range(DEPTH):
        p, deps, mid = get_layer(l, x)
        x, p, s = _layer_fwd(x, p, bm, deps=deps, mid=mid)
        layers.append(p)
        saved.append(s)
    loss_blk, dx = _loss_fwd_bwd(x, target)
    smalls, dsas = [None] * DEPTH, [None] * DEPTH
    pipe = None
    for l in reversed(range(DEPTH)):
        if l > 0:
            dx, big, smalls[l], dsas[l] = _layer_bwd(dx, saved[l], layers[l], bm, pipe)
            pipe = make_pipe(l, BIG, "", big)
        else:
            dx, big, smalls[l], dsas[l] = _layer_bwd(
                dx, saved[l], layers[l], bm, pipe, lambda big_: make_pipe(0, EARLY, "e", big_),
                lambda big_: make_pipe(0, LATE, "", big_))
    d_rel = _bias_grad(jnp.stack(dsas), buckets)[:, 0, :NBUCKET].T
    return loss_blk[0, 0], dx, smalls, d_rel


MESH = pl.DeviceIdType.MESH


def _me():
    return lax.axis_index("x"), lax.axis_index("y"), lax.axis_index("c")


def _other_chips(mx, my):
    return [(1 - mx, my), (mx, 1 - my), (1 - mx, 1 - my)]


def _rcopy(src, dst, send_sems, recv_sems, k, dev):
    return pltpu.make_async_remote_copy(src_ref=src, dst_ref=dst, send_sem=send_sems.at[k], recv_sem=recv_sems.at[k],
                                        device_id=dev, device_id_type=MESH)


def _comm_call(body, name, out_shape, n_in, n_sems):
    return pl.pallas_call(
        body, name=name, out_shape=out_shape, in_specs=[HBM_SPEC] * n_in,
        out_specs=jax.tree.map(lambda _: HBM_SPEC, out_shape),
        scratch_shapes=[pltpu.SemaphoreType.DMA((n_sems,)), pltpu.SemaphoreType.DMA((n_sems,)),
                        pltpu.SemaphoreType.DMA(())],
        compiler_params=pltpu.CompilerParams(has_side_effects=True))


def _all_gather_chips(x, name):
    def body(x_ref, o_ref, send_sems, recv_sems, local_sem):
        mx, my, mc = _me()
        local = pltpu.make_async_copy(x_ref, o_ref.at[2 * mx + my], local_sem)
        local.start()
        sends = [_rcopy(x_ref, o_ref.at[2 * mx + my], send_sems, recv_sems, k, (px, py, mc))
                 for k, (px, py) in enumerate(_other_chips(mx, my))]
        for cp in sends:
            cp.start()
        for k, (px, py) in enumerate(_other_chips(mx, my)):
            _rcopy(x_ref, o_ref.at[2 * px + py], send_sems, recv_sems, k, (px, py, mc)).wait_recv()
        for cp in sends:
            cp.wait_send()
        local.wait()

    return _comm_call(body, name, _sds((NCHIP,) + x.shape, x.dtype), 1, 3)(x)


EFFECT = pltpu.SideEffectType.DATAFLOW_SIDE_EFFECTING


def _hbm(a):
    return pltpu.with_memory_space_constraint(a, pltpu.HBM)


def _split_start(name, bufs, plan, n, after=None):
    nb = len(bufs)
    extra = [] if after is None else [after]
    ne = len(extra)

    def body(*refs):
        send_sems, recv_sems, token = refs[nb + ne], refs[nb + ne + 1], refs[-1]
        mx, my, mc = _me()
        for k, (src, dst, dev, _) in enumerate(plan(refs[:nb], mx, my, mc)):
            _rcopy(src, dst, send_sems, recv_sems, k, dev).start()
        token[...] = jnp.zeros_like(token)

    out = pl.pallas_call(
        body, name=name,
        out_shape=(pltpu.SemaphoreType.DMA((n,)), pltpu.SemaphoreType.DMA((n,)),
                   *[pltpu.HBM(b.shape, b.dtype) for b in bufs], _sds((8, 128))),
        in_specs=[HBM_SPEC] * nb + [ANY_SPEC] * ne,
        out_specs=(SEM_SPEC, SEM_SPEC, *[HBM_SPEC] * nb, pl.BlockSpec(memory_space=pltpu.VMEM)),
        input_output_aliases={i: 2 + i for i in range(nb)},
        compiler_params=pltpu.CompilerParams(has_side_effects=EFFECT))(*[_hbm(b) for b in bufs], *extra)
    return (out[0], out[1]), list(out[2:2 + nb]), out[-1]


def _split_wait(name, sems, bufs, plan, after):
    nb = len(bufs)

    def body(*refs):
        send_sems, recv_sems = refs[nb], refs[nb + 1]
        mx, my, mc = _me()
        for k, (src, dst, dev, land) in enumerate(plan(refs[:nb], mx, my, mc)):
            _rcopy(src, dst, send_sems, recv_sems, k, dev).wait_send()
            _rcopy(src, land, send_sems, recv_sems, k, dev).wait_recv()

    out = pl.pallas_call(
        body, name=name, out_shape=tuple(pltpu.HBM(b.shape, b.dtype) for b in bufs),
        in_specs=[HBM_SPEC] * nb + [SEM_SPEC, SEM_SPEC, ANY_SPEC], out_specs=(HBM_SPEC,) * nb,
        input_output_aliases={i: i for i in range(nb)},
        compiler_params=pltpu.CompilerParams(has_side_effects=EFFECT))(*bufs, sems[0], sems[1], after)
    return list(out)


def _plan_gather_chips(refs, mx, my, mc):
    me = 2 * mx + my
    return [(r.at[me, mc], r.at[me, mc], (px, py, mc), r.at[2 * px + py, mc])
            for r in refs for px, py in _other_chips(mx, my)]


def _plan_gather_pair(refs, mx, my, mc):
    return [(r.at[2 * px + py, mc], r.at[2 * px + py, mc], (mx, my, 1 - mc), r.at[2 * px + py, 1 - mc])
            for r in refs for px, py in _other_chips(mx, my)]


def _plan_pair_half(refs, mx, my, mc):
    n = len(refs) // 2
    return [(g.at[:, 1 - mc], r, (mx, my, 1 - mc), r) for g, r in zip(refs[:n], refs[n:])]


def _plan_scatter(refs, mx, my, mc):
    n = len(refs) // 2
    return [(q.at[2 * px + py], r.at[k], (px, py, mc), r.at[k])
            for q, r in zip(refs[:n], refs[n:]) for k, (px, py) in enumerate(_other_chips(mx, my))]


def _plan_pair_fill(refs, mx, my, mc):
    return [(r.at[mc], r.at[mc], (mx, my, 1 - mc), r.at[1 - mc]) for r in refs]


def _all_gather_devices(v, name):
    def body(v_ref, o_ref, send_sems, recv_sems, local_sem):
        mx, my, mc = _me()
        flip = lambda m, b: 1 - m if b else m
        peers = [(flip(mx, k >> 2 & 1), flip(my, k >> 1 & 1), flip(mc, k & 1)) for k in range(1, 8)]
        slot = lambda d: 4 * d[0] + 2 * d[1] + d[2]
        local = pltpu.make_async_copy(v_ref, o_ref.at[slot((mx, my, mc))], local_sem)
        local.start()
        sends = [_rcopy(v_ref, o_ref.at[slot((mx, my, mc))], send_sems, recv_sems, k, dev)
                 for k, dev in enumerate(peers)]
        for cp in sends:
            cp.start()
        for k, dev in enumerate(peers):
            _rcopy(v_ref, o_ref.at[slot(dev)], send_sems, recv_sems, k, dev).wait_recv()
        for cp in sends:
            cp.wait_send()
        local.wait()

    return _comm_call(body, name, _sds((8,) + v.shape, v.dtype), 1, 7)(v)


def _row_tile(rows, cols):
    t = 8
    while t * 2 * cols * 4 <= (1 << 20) and rows % (t * 2) == 0:
        t *= 2
    return t


def _prefetch_call(body, name, out_shape, grid, in_specs, out_specs):
    return pl.pallas_call(
        body, name=name, out_shape=out_shape,
        grid_spec=pltpu.PrefetchScalarGridSpec(num_scalar_prefetch=1, grid=grid, in_specs=in_specs,
                                               out_specs=out_specs),
        compiler_params=pltpu.CompilerParams(vmem_limit_bytes=VMEM_LIMIT,
                                             dimension_semantics=("parallel",) * len(grid)))


def _sum_half(g, r1, place, name):
    _, _, rr, ns = g.shape
    tr = _row_tile(rr, ns)

    def body(c_ref, g_ref, r_ref, o_ref, ob_ref):
        q = g_ref[...] + r_ref[...]
        ob_ref[...] = q.astype(BF16)

        @pl.when(pl.program_id(1) == c_ref[0])
        def _():
            o_ref[...] = q

    blk = pl.BlockSpec((None, tr, ns), lambda i, s, c: (s, i, 0))
    return pl.pallas_call(
        body, name=name, out_shape=(_sds((rr, ns)), _sds((NCHIP, rr, ns), BF16)),
        grid_spec=pltpu.PrefetchScalarGridSpec(
            num_scalar_prefetch=1, grid=(rr // tr, NCHIP),
            in_specs=[pl.BlockSpec((None, None, tr, ns), lambda i, s, c: (s, c[1], i, 0)), blk],
            out_specs=(pl.BlockSpec((tr, ns), lambda i, s, c: (i, 0)), blk)),
        compiler_params=pltpu.CompilerParams(vmem_limit_bytes=VMEM_LIMIT,
                                             dimension_semantics=("parallel", "arbitrary")))(place, g, r1)


def _sum_recv(q, r2, place, name):
    rr, ns = q.shape
    tr = _row_tile(rr, ns)

    def body(c_ref, q_ref, r_ref, o_ref):
        o_ref[...] = ((q_ref[...] + r_ref[0].astype(F32)) + r_ref[1].astype(F32)) + r_ref[2].astype(F32)

    return _prefetch_call(body, name, _sds((2, rr, ns)), (rr // tr,),
                          [pl.BlockSpec((tr, ns), lambda i, c: (i, 0)),
                           pl.BlockSpec((NCHIP - 1, tr, ns), lambda i, c: (0, i, 0))],
                          pl.BlockSpec((None, tr, ns), lambda i, c: (c[1], i, 0)))(place, q, r2)


def _sum_devices(v8):
    def body(v_ref, o_ref):
        acc = v_ref[0]
        for dev in range(1, 8):
            acc = acc + v_ref[dev]
        o_ref[...] = acc

    return _pcall(body, name="sum_devices", out_shape=_sds(v8.shape[1:]))(v8)


def _adamw(w, g, m, v, name):
    rows, cols = w.shape
    tr = _row_tile(rows, cols)

    def body(w_ref, g_ref, m_ref, v_ref, d_ref, m2_ref, v2_ref):
        g = g_ref[...]
        m2 = ADAM_B1 * m_ref[...] + (1.0 - ADAM_B1) * g
        v2 = ADAM_B2 * v_ref[...] + (1.0 - ADAM_B2) * (g * g)
        m_hat = m2 / (1.0 - ADAM_B1 ** ADAM_STEP)
        v_hat = v2 / (1.0 - ADAM_B2 ** ADAM_STEP)
        d_ref[...] = -ADAM_LR * (m_hat / (jnp.sqrt(v_hat) + ADAM_EPS) + ADAM_WD * w_ref[...])
        m2_ref[...] = m2
        v2_ref[...] = v2

    blk = pl.BlockSpec((tr, cols), lambda i: (i, 0))
    return _pcall(body, name=name, out_shape=(_sds((rows, cols)),) * 3, grid=(rows // tr,), in_specs=[blk] * 4,
                  out_specs=(blk,) * 3, semantics=("parallel",))(w, g, m, v)


BIG = ("w_in", "w_conv_out", "w_attn_out", "w_out", "w_ff1", "w_ff2")
SMALL = ("rel_bias", "norm1_g", "q_norm_g", "k_norm_g", "conv_dw_w", "conv_dw_b", "conv_ln_g", "conv_ln_b", "norm2_g")
WEIGHTS = ("rel_bias", "norm1_g", "w_in", "q_norm_g", "k_norm_g", "conv_dw_w", "conv_dw_b", "conv_ln_g", "conv_ln_b",
           "w_conv_out", "w_attn_out", "w_out", "norm2_g", "w_ff1", "w_ff2")


def _pack(arrays):
    flat = jnp.concatenate([a.reshape(-1) for a in arrays])
    n = flat.shape[0]
    rows = -(-n // 1024) * 8
    return jnp.pad(flat, (0, rows * 128 - n)).reshape(rows, 128)


def _unpack(packed, shapes):
    flat = packed.reshape(-1)
    out, off = [], 0
    for shp in shapes:
        n = int(np.prod(shp))
        out.append(flat[off:off + n].reshape(shp))
        off += n
    return out


def _adamw_layer(l, w, g, m, v, prev, name):
    _, k, n = w.shape
    tr = _row_tile(k, n)
    if prev is None:
        prev = tuple(lax.empty(w.shape, F32) for _ in range(4))

    def body(w_ref, g_ref, m_ref, v_ref, p0, p1, p2, p3, go_ref, d_ref, m2_ref, v2_ref):
        g = g_ref[...]
        m2 = ADAM_B1 * m_ref[...] + (1.0 - ADAM_B1) * g
        v2 = ADAM_B2 * v_ref[...] + (1.0 - ADAM_B2) * (g * g)
        m_hat = m2 / (1.0 - ADAM_B1 ** ADAM_STEP)
        v_hat = v2 / (1.0 - ADAM_B2 ** ADAM_STEP)
        go_ref[...] = g
        d_ref[...] = -ADAM_LR * (m_hat / (jnp.sqrt(v_hat) + ADAM_EPS) + ADAM_WD * w_ref[...])
        m2_ref[...] = m2
        v2_ref[...] = v2

    lay = pl.BlockSpec((None, tr, n), lambda i: (l, i, 0))
    return _pcall(body, name=name, out_shape=(_sds(w.shape),) * 4, grid=(k // tr,),
                  in_specs=[lay, pl.BlockSpec((tr, n), lambda i: (i, 0)), lay, lay] + [ANY_SPEC] * 4,
                  out_specs=(lay,) * 4, aliases={4: 0, 5: 1, 6: 2, 7: 3},
                  semantics=("parallel",))(w, g, m, v, *prev)


class _GradPipe:
    def __init__(self, l, kinds, tag, big, place, w, m, v, results):
        self.l, self.kinds, self.place, self.w, self.m, self.v, self.results = l, kinds, place, w, m, v, results
        self.id = f"l{l}{tag}"
        self.g = [big[n].reshape(NCHIP, 2, big[n].shape[1] // 2, big[n].shape[2]) for n in kinds]

    def step0(self):
        lands = [lax.empty((NCHIP,) + g.shape[2:], F32) for g in self.g]
        self.s1, self.b1, tok = _split_start(f"rs1_start_{self.id}", self.g + lands, _plan_pair_half, len(self.kinds))
        return tok

    def step1(self, after):
        nk = len(self.kinds)
        bufs = _split_wait(f"rs1_wait_{self.id}", self.s1, self.b1, _plan_pair_half, after)
        sums = [_sum_half(bufs[i], bufs[nk + i], self.place, f"rs1_sum_{n}") for i, n in enumerate(self.kinds)]
        self.q = [q for q, _ in sums]
        qb = [b for _, b in sums]
        lands = [lax.empty((NCHIP - 1,) + b.shape[1:], BF16) for b in qb]
        self.s2, self.b2, tok = _split_start(f"rs2_start_{self.id}", qb + lands, _plan_scatter, 3 * nk)
        return tok

    def step2(self, after):
        nk = len(self.kinds)
        bufs = _split_wait(f"rs2_wait_{self.id}", self.s2, self.b2, _plan_scatter, after)
        fin = [_sum_recv(self.q[i], bufs[nk + i], self.place, f"rs2_sum_{n}") for i, n in enumerate(self.kinds)]
        self.s3, self.b3, tok = _split_start(f"rs3_start_{self.id}", fin, _plan_pair_fill, nk)
        return tok

    def step3(self, after):
        fin = _split_wait(f"rs3_wait_{self.id}", self.s3, self.b3, _plan_pair_fill, after)
        for i, n in enumerate(self.kinds):
            g2 = fin[i].reshape(fin[i].shape[1] * 2, fin[i].shape[2])
            self.results[n] = _adamw_layer(self.l, self.w[n], g2, self.m[n], self.v[n], self.results.get(n),
                                           f"adamw_{n}_l{self.l}")

    def finish(self):
        self.step3(self.step2(self.step1(self.step0())))


def kernel(x, rel_bias, norm1_g, w_in, q_norm_g, k_norm_g, conv_dw_w, conv_dw_b, conv_ln_g, conv_ln_b, w_conv_out, w_attn_out, w_out, norm2_g, w_ff1, w_ff2, loss_target, m_rel_bias, m_norm1_g, m_w_in, m_q_norm_g, m_k_norm_g, m_conv_dw_w, m_conv_dw_b, m_conv_ln_g, m_conv_ln_b, m_w_conv_out, m_w_attn_out, m_w_out, m_norm2_g, m_w_ff1, m_w_ff2, v_rel_bias, v_norm1_g, v_w_in, v_q_norm_g, v_k_norm_g, v_conv_dw_w, v_conv_dw_b, v_conv_ln_g, v_conv_ln_b, v_w_conv_out, v_w_attn_out, v_w_out, v_norm2_g, v_w_ff1, v_w_ff2):
    w = dict(rel_bias=rel_bias, norm1_g=norm1_g, w_in=w_in, q_norm_g=q_norm_g, k_norm_g=k_norm_g, conv_dw_w=conv_dw_w,
             conv_dw_b=conv_dw_b, conv_ln_g=conv_ln_g, conv_ln_b=conv_ln_b, w_conv_out=w_conv_out,
             w_attn_out=w_attn_out, w_out=w_out, norm2_g=norm2_g, w_ff1=w_ff1, w_ff2=w_ff2)
    m = dict(rel_bias=m_rel_bias, norm1_g=m_norm1_g, w_in=m_w_in, q_norm_g=m_q_norm_g, k_norm_g=m_k_norm_g,
             conv_dw_w=m_conv_dw_w, conv_dw_b=m_conv_dw_b, conv_ln_g=m_conv_ln_g, conv_ln_b=m_conv_ln_b,
             w_conv_out=m_w_conv_out, w_attn_out=m_w_attn_out, w_out=m_w_out, norm2_g=m_norm2_g, w_ff1=m_w_ff1,
             w_ff2=m_w_ff2)
    v = dict(rel_bias=v_rel_bias, norm1_g=v_norm1_g, w_in=v_w_in, q_norm_g=v_q_norm_g, k_norm_g=v_k_norm_g,
             conv_dw_w=v_conv_dw_w, conv_dw_b=v_conv_dw_b, conv_ln_g=v_conv_ln_g, conv_ln_b=v_conv_ln_b,
             w_conv_out=v_w_conv_out, w_attn_out=v_w_attn_out, w_out=v_w_out, norm2_g=v_norm2_g, w_ff1=v_w_ff1,
             w_ff2=v_w_ff2)
    chip_id = 2 * lax.axis_index("x") + lax.axis_index("y")
    place = jnp.stack([chip_id, lax.axis_index("c")]).astype(jnp.int32)

    dww4 = _all_gather_chips(conv_dw_w, "ag_conv_dw_w")
    dww = dww4.transpose(1, 2, 0, 3).reshape(DEPTH, KW, CONV)

    names = dict(w_in="win4", w_conv_out="wco4", w_attn_out="wao4", w_out="wout4", w_ff1="wff14", w_ff2="wff24")
    chips, pair = {}, {}

    def start_chips(key, l, kinds, after):
        lands = []
        for n in kinds:
            k, ns = w[n].shape[1:]
            land = lax.dynamic_update_slice(lax.empty((NCHIP, k, ns), BF16), w[n][l].astype(BF16)[None], (chip_id, 0, 0))
            lands.append(land.reshape(NCHIP, 2, k // 2, ns))
        chips[key] = _split_start(f"ag_chips_start_{key}", lands, _plan_gather_chips, 3 * len(kinds), after=after)
        return chips[key][2]

    def start_pair(key, after):
        sems, bufs, _ = chips[key]
        bufs = _split_wait(f"ag_chips_wait_{key}", sems, bufs, _plan_gather_chips, after)
        pair[key] = _split_start(f"ag_pair_start_{key}", bufs, _plan_gather_pair, len(bufs) * 3)
        return pair[key][2]

    def landed(key, kinds, after):
        sems, bufs, _ = pair[key]
        bufs = _split_wait(f"ag_pair_wait_{key}", sems, bufs, _plan_gather_pair, after)
        return {names[n]: b.reshape(NCHIP, 2 * b.shape[2], b.shape[3]) for n, b in zip(kinds, bufs)}

    rest = tuple(n for n in BIG if n != "w_in")
    tok0 = start_pair("l0a", start_chips("l0a", 0, ("w_in",), dww4))
    start_chips("l0b", 0, rest, tok0)

    def get_layer(l, after):
        p = dict(dww=dww[l], dwb=conv_dw_b[l][None], lng=conv_ln_g[l][None], lnb=conv_ln_b[l][None],
                 n1g=norm1_g[l][None], n2g=norm2_g[l][None], gq=jnp.tile(q_norm_g[l], 2)[None],
                 gk=jnp.tile(k_norm_g[l], 2)[None])
        if l == 0:
            p.update(landed("l0a", ("w_in",), chips["l0b"][2]))
            p["hook"] = lambda after_: start_chips("l1", 1, BIG, start_pair("l0b", after_))
            p["rest"] = lambda after_: landed("l0b", rest, after_)
            return p, (), None
        if l == 1:
            after = start_pair("l1", after)
        p.update(landed(f"l{l}", BIG, after))
        deps, mid = (), None
        if l + 1 < DEPTH:
            deps = (start_chips(f"l{l + 1}", l + 1, BIG, p["win4"]),)
            mid = lambda after_: start_pair(f"l{l + 1}", after_)
        return p, deps, mid

    results = {}
    make_pipe = lambda l, kinds, tag, big: _GradPipe(l, kinds, tag, big, place, w, m, v, results)
    loss_share, dx, smalls, d_rel = _local_step(x[0], loss_target[0], get_layer, rel_bias, make_pipe)
    loss = lax.psum(loss_share, ("x", "y", "c"))

    local_small = dict(rel_bias=d_rel)
    for n in SMALL[1:]:
        local_small[n] = jnp.stack([smalls[l][n] for l in range(DEPTH)])
    small_shapes = [local_small[n].shape for n in SMALL]
    summed = _sum_devices(_all_gather_devices(_pack([local_small[n] for n in SMALL]), "ag_small"))
    grads = dict(zip(SMALL, _unpack(summed, small_shapes)))
    grads["conv_dw_w"] = lax.dynamic_slice_in_dim(grads["conv_dw_w"], chip_id * 128, 128, axis=2)

    delta, new_m, new_v = {}, {}, {}
    small_w_shapes = [w[n].shape for n in SMALL]
    outs = _adamw(_pack([w[n] for n in SMALL]), _pack([grads[n] for n in SMALL]), _pack([m[n] for n in SMALL]),
                  _pack([v[n] for n in SMALL]), "adamw_small")
    for dst, packed in zip((delta, new_m, new_v), outs):
        dst.update(zip(SMALL, _unpack(packed, small_w_shapes)))

    for n in BIG:
        grads[n], delta[n], new_m[n], new_v[n] = results[n]

    return (loss, dx[None], *[grads[n] for n in WEIGHTS], *[delta[n] for n in WEIGHTS],
            *[new_m[n] for n in WEIGHTS], *[new_v[n] for n in WEIGHTS])
```

```python
import functools
import math

import numpy as np
import jax
import jax.numpy as jnp
from jax import lax
from jax.experimental import pallas as pl
from jax.experimental.pallas import tpu as pltpu

F32 = jnp.float32
BF16 = jnp.bfloat16

T = 2048
D = 1024
DEPTH = 4
CONV = 512
KW = 31
NG = 3
HD = 64
AOUT = 512
DFF = 4096
INC = 7680
DIL = (1, 4, 16)
BLK = 128
NBUCKET = 32
EPS = 1e-6
NEG = -1e30
NCHIP = 4
UB_A, UB_GT, UB_Q, UB_K, UB_V, UB_GC, UB_GA = 0, 1, 2, 5, 8, 11, 13

ADAM_LR, ADAM_B1, ADAM_B2, ADAM_EPS, ADAM_WD, ADAM_STEP = 0.001, 0.9, 0.999, 1e-08, 0.01, 10

VMEM_LIMIT = 48 * 1024 * 1024
TB = 256
HBM_SPEC = pl.BlockSpec(memory_space=pltpu.HBM)
ANY_SPEC = pl.BlockSpec(memory_space=pl.ANY)
SEM_SPEC = pl.BlockSpec(memory_space=pltpu.SEMAPHORE)


def _pcall(body, *, name, out_shape, grid=(), in_specs=None, out_specs=None, scratch=(), aliases=None,
           semantics=None):
    kw = {}
    if in_specs is not None:
        kw["in_specs"] = in_specs
    if out_specs is not None:
        kw["out_specs"] = out_specs
    return pl.pallas_call(
        body, name=name, out_shape=out_shape, grid=grid, scratch_shapes=scratch,
        input_output_aliases=aliases or {},
        compiler_params=pltpu.CompilerParams(vmem_limit_bytes=VMEM_LIMIT, dimension_semantics=semantics),
        **kw)


def _sds(shape, dtype=F32):
    return jax.ShapeDtypeStruct(shape, dtype)


NN = (((1,), (0,)), ((), ()))
NT = (((1,), (1,)), ((), ()))
TN = (((0,), (0,)), ((), ()))


def _mm(name, a, b, *, out_shape, out_dtype, grid, a_spec, b_spec, o_spec, acc_shape, dims, add=None,
        add_spec=None, deps=()):
    nk = grid[2]
    deps = tuple(d for d in deps if d is not None)
    n_scratch = 1 if nk > 1 else 0

    def body(*refs):
        n_out = 1 + n_scratch
        refs = refs[:len(refs) - n_out - len(deps)] + refs[len(refs) - n_out:]
        a_ref, b_ref = refs[0], refs[1]
        r_ref = refs[2] if add is not None else None
        o_ref = refs[-n_out]
        prod = lax.dot_general(a_ref[...].astype(BF16), b_ref[...].astype(BF16), dims, preferred_element_type=F32)
        if nk == 1:
            o_ref[...] = (prod if r_ref is None else prod + r_ref[...]).astype(out_dtype)
            return
        acc_ref = refs[-1]
        k = pl.program_id(2)

        @pl.when(k == 0)
        def _():
            acc_ref[...] = prod

        @pl.when(k > 0)
        def _():
            acc_ref[...] += prod

        @pl.when(k == nk - 1)
        def _():
            res = acc_ref[...]
            if r_ref is not None:
                res = res + r_ref[...]
            o_ref[...] = res.astype(out_dtype)

    ins = ([a, b] if add is None else [a, b, add]) + list(deps)
    specs = ([a_spec, b_spec] if add is None else [a_spec, b_spec, add_spec]) + [ANY_SPEC] * len(deps)
    return _pcall(body, name=name, out_shape=_sds(out_shape, out_dtype), grid=grid, in_specs=specs,
                  out_specs=o_spec, scratch=[pltpu.VMEM(acc_shape, F32)] * n_scratch,
                  semantics=("parallel", "parallel", "arbitrary"))(*ins)


def _mm_x_wcols(name, a, w4, *, tm, tn, out_dtype=F32, deps=()):
    _, k, ns = w4.shape
    nj = ns // tn
    return _mm(name, a, w4, out_shape=(T, NCHIP * ns), out_dtype=out_dtype, grid=(T // tm, NCHIP * nj, 1), deps=deps,
               a_spec=pl.BlockSpec((tm, k), lambda i, j, kk: (i, 0)),
               b_spec=pl.BlockSpec((None, k, tn), lambda i, j, kk: (j // nj, 0, j % nj)),
               o_spec=pl.BlockSpec((tm, tn), lambda i, j, kk: (i, j)), acc_shape=(tm, tn), dims=NN)


def _mm_ff1(a, w4, *, tm, tn):
    _, k, ns = w4.shape
    nj = ns // tn

    def body(a_ref, b_ref, f_ref, r_ref):
        p = jnp.maximum(jnp.dot(a_ref[...], b_ref[...], preferred_element_type=F32), 0.0)
        f_ref[...] = p.astype(BF16)
        r_ref[...] = (p * p).astype(BF16)

    out = pl.BlockSpec((tm, tn), lambda i, j: (i, j))
    return _pcall(body, name="mm_f", out_shape=(_sds((T, DFF), BF16), _sds((T, DFF), BF16)), grid=(T // tm, NCHIP * nj),
                  in_specs=[pl.BlockSpec((tm, k), lambda i, j: (i, 0)),
                            pl.BlockSpec((None, k, tn), lambda i, j: (j // nj, 0, j % nj))],
                  out_specs=(out, out), semantics=("parallel", "parallel"))(a, w4)


def _mm_x_wrows(name, a, w4, add, *, tm, tk, tn, deps=()):
    _, ks, n = w4.shape
    nkk = ks // tk
    return _mm(name, a, w4, out_shape=(T, n), out_dtype=F32, grid=(T // tm, n // tn, NCHIP * nkk), deps=deps,
               a_spec=pl.BlockSpec((tm, tk), lambda i, j, kk: (i, kk)),
               b_spec=pl.BlockSpec((None, tk, tn), lambda i, j, kk: (kk // nkk, kk % nkk, j)),
               o_spec=pl.BlockSpec((tm, tn), lambda i, j, kk: (i, j)), acc_shape=(tm, tn), dims=NN,
               add=add, add_spec=pl.BlockSpec((tm, tn), lambda i, j, kk: (i, j)))


def _mm_g_wcols_t(name, g, w4, *, tm, tk, tn, out_dtype=F32, deps=()):
    _, k, ns = w4.shape
    nkk = ns // tk
    return _mm(name, g, w4, out_shape=(T, k), out_dtype=out_dtype, grid=(T // tm, k // tn, NCHIP * nkk), deps=deps,
               a_spec=pl.BlockSpec((tm, tk), lambda i, j, kk: (i, kk)),
               b_spec=pl.BlockSpec((None, tn, tk), lambda i, j, kk: (kk // nkk, j, kk % nkk)),
               o_spec=pl.BlockSpec((tm, tn), lambda i, j, kk: (i, j)), acc_shape=(tm, tn), dims=NT)


def _mm_g_wrows_t(name, g, w4, *, tm, tn, out_dtype=F32, deps=()):
    _, ks, n = w4.shape
    nj = ks // tn
    return _mm(name, g, w4, out_shape=(T, NCHIP * ks), out_dtype=out_dtype, grid=(T // tm, NCHIP * nj, 1), deps=deps,
               a_spec=pl.BlockSpec((tm, n), lambda i, j, kk: (i, 0)),
               b_spec=pl.BlockSpec((None, tn, n), lambda i, j, kk: (j // nj, j % nj, 0)),
               o_spec=pl.BlockSpec((tm, tn), lambda i, j, kk: (i, j)), acc_shape=(tm, tn), dims=NT)


def _mm_dff2(dx, w4, fa, *, tm, tn, deps=()):
    _, ks, n = w4.shape
    nj = ks // tn
    deps = tuple(d for d in deps if d is not None)

    def body(*refs):
        dx_ref, b_ref, f_ref = refs[:3]
        df_ref = refs[-1]
        dr = lax.dot_general(dx_ref[...].astype(BF16), b_ref[...], NT, preferred_element_type=F32)
        df_ref[...] = (dr * (2.0 * f_ref[...].astype(F32))).astype(BF16)

    out = pl.BlockSpec((tm, tn), lambda i, j: (i, j))
    return _pcall(body, name="mm_dr", out_shape=_sds((T, DFF), BF16), grid=(T // tm, NCHIP * nj),
                  in_specs=[pl.BlockSpec((tm, n), lambda i, j: (i, 0)),
                            pl.BlockSpec((None, tn, n), lambda i, j: (j // nj, j % nj, 0)), out]
                  + [ANY_SPEC] * len(deps),
                  out_specs=out, semantics=("parallel", "parallel"))(dx, w4, fa, *deps)


TCH = 512


def _mm_dw(name, a, g, *, out_shape, out_map, tm, tn, deps=()):
    deps = tuple(d for d in deps if d is not None)

    def body(*refs):
        a_ref, g_ref = refs[:2]
        o_ref, at_ref = refs[-2:]

        @pl.when(pl.program_id(1) == 0)
        def _():
            for c in range(T // TCH):
                at_ref[:, c * TCH:(c + 1) * TCH] = a_ref[c * TCH:(c + 1) * TCH, :].T

        o_ref[...] = jnp.dot(at_ref[...], g_ref[...].astype(BF16), preferred_element_type=F32)

    return _pcall(body, name=name, out_shape=_sds(out_shape), grid=(a.shape[1] // tm, g.shape[1] // tn),
                  in_specs=[pl.BlockSpec((T, tm), lambda i, j: (0, i)), pl.BlockSpec((T, tn), lambda i, j: (0, j))]
                  + [ANY_SPEC] * len(deps),
                  out_specs=pl.BlockSpec((None, tm, tn), out_map), scratch=[pltpu.VMEM((tm, T), BF16)],
                  semantics=("parallel", "arbitrary"))(a, g, *deps)


def _mm_dw_cols(name, a, g, *, ns, tm, tn, deps=()):
    nj = ns // tn
    return _mm_dw(name, a, g, out_shape=(NCHIP, a.shape[1], ns), out_map=lambda i, j: (j // nj, i, j % nj),
                  tm=tm, tn=tn, deps=deps)


def _mm_dw_rows(name, a, g, *, ks, tm, tn):
    ni = ks // tm
    return _mm_dw(name, a, g, out_shape=(NCHIP, ks, g.shape[1]), out_map=lambda i, j: (i // ni, i % ni, j),
                  tm=tm, tn=tn)


def _row_spec(width, col=0):
    return pl.BlockSpec((TB, width), lambda i: (i, col))


def _vec_spec(width):
    return pl.BlockSpec((1, width), lambda i: (0, 0))


def _rms_fwd(x, g):
    def body(x_ref, g_ref, h_ref):
        x = x_ref[...]
        r = lax.rsqrt(jnp.mean(x * x, axis=-1, keepdims=True) + EPS)
        h_ref[...] = (x * r * g_ref[...]).astype(BF16)

    return _pcall(body, name="rms_fwd", out_shape=_sds((T, D), BF16), grid=(T // TB,),
                  in_specs=[_row_spec(D), _vec_spec(D)], out_specs=_row_spec(D), semantics=("parallel",))(x, g)


def _rms_bwd(x, g, dh, dres, deps=()):
    deps = tuple(d for d in deps if d is not None)

    def body(*refs):
        x_ref, g_ref, dh_ref, dres_ref = refs[:4]
        dx_ref, dg_ref = refs[-2:]
        x = x_ref[...]
        r = lax.rsqrt(jnp.mean(x * x, axis=-1, keepdims=True) + EPS)
        y = x * r
        dh = dh_ref[...]
        dy = dh * g_ref[...]
        dx_ref[...] = dres_ref[...] + r * (dy - y * jnp.mean(dy * y, axis=-1, keepdims=True))

        @pl.when(pl.program_id(0) == 0)
        def _():
            dg_ref[...] = jnp.zeros_like(dg_ref)

        dg_ref[...] += jnp.sum(dh * y, axis=0, keepdims=True)

    return _pcall(body, name="rms_bwd", out_shape=(_sds((T, D)), _sds((1, D))), grid=(T // TB,),
                  in_specs=[_row_spec(D), _vec_spec(D), _row_spec(D), _row_spec(D)] + [ANY_SPEC] * len(deps),
                  out_specs=(_row_spec(D), _vec_spec(D)), semantics=("arbitrary",))(x, g, dh, dres, *deps)


def _sigmoid(x):
    return 1.0 / (1.0 + jnp.exp(-x))


def _gate_fwd(u, ycv, yat):
    def body(gc_ref, ga_ref, yc_ref, ya_ref, m_ref):
        m_ref[...] = (_sigmoid(gc_ref[...].astype(F32)) * yc_ref[...]
                      + _sigmoid(ga_ref[...].astype(F32)) * ya_ref[...]).astype(BF16)

    blk = lambda off: pl.BlockSpec((TB, 512), lambda i, j: (i, off + j))
    return _pcall(body, name="gate_fwd", out_shape=_sds((T, D), BF16), grid=(T // TB, 2),
                  in_specs=[blk(UB_GC), blk(UB_GA), blk(0), blk(0)], out_specs=blk(0),
                  semantics=("parallel", "parallel"))(u, u, ycv, yat)


def _gate_bwd(u, ycv, yat, dm):
    def body(gc_ref, ga_ref, yc_ref, ya_ref, dm_ref, dyc_ref, dya_ref, dgc_ref, dga_ref):
        dm = dm_ref[...]
        sc = _sigmoid(gc_ref[...].astype(F32))
        sa = _sigmoid(ga_ref[...].astype(F32))
        dyc_ref[...] = (dm * sc).astype(BF16)
        dya_ref[...] = (dm * sa).astype(BF16)
        dgc_ref[...] = (dm * yc_ref[...] * sc * (1.0 - sc)).astype(BF16)
        dga_ref[...] = (dm * ya_ref[...] * sa * (1.0 - sa)).astype(BF16)

    blk = lambda off: pl.BlockSpec((TB, 512), lambda i, j: (i, off + j))
    return _pcall(body, name="gate_bwd",
                  out_shape=(_sds((T, D), BF16), _sds((T, D), BF16), _sds((T, D), BF16), _sds((T, D), BF16)),
                  grid=(T // TB, 2), in_specs=[blk(UB_GC), blk(UB_GA), blk(0), blk(0), blk(0)],
                  out_specs=(blk(0), blk(0), blk(0), blk(0)),
                  semantics=("parallel", "parallel"))(u, u, ycv, yat, dm)


def _relu2_fwd(f):
    def body(f_ref, r_ref):
        a = jnp.maximum(f_ref[...], 0.0)
        r_ref[...] = (a * a).astype(BF16)

    blk = pl.BlockSpec((TB, 1024), lambda i, j: (i, j))
    return _pcall(body, name="relu2_fwd", out_shape=_sds((T, DFF), BF16), grid=(T // TB, DFF // 1024),
                  in_specs=[blk], out_specs=blk, semantics=("parallel", "parallel"))(f)


def _relu2_bwd(f, dr):
    def body(f_ref, dr_ref, r_ref, df_ref):
        a = jnp.maximum(f_ref[...], 0.0)
        r_ref[...] = (a * a).astype(BF16)
        df_ref[...] = (dr_ref[...] * (2.0 * a)).astype(BF16)

    blk = pl.BlockSpec((TB, 1024), lambda i, j: (i, j))
    return _pcall(body, name="relu2_bwd", out_shape=(_sds((T, DFF), BF16), _sds((T, DFF), BF16)),
                  grid=(T // TB, DFF // 1024), in_specs=[blk, blk], out_specs=(blk, blk),
                  semantics=("parallel", "parallel"))(f, dr)


def _loss_fwd_bwd(y, target):
    def body(y_ref, t_ref, loss_ref, dy_ref):
        e = y_ref[...] - t_ref[...]
        dy_ref[...] = e * (1.0 / D)

        @pl.when(pl.program_id(0) == 0)
        def _():
            loss_ref[...] = jnp.zeros_like(loss_ref)

        loss_ref[...] += 0.5 * jnp.sum(jnp.mean(e * e, axis=-1, keepdims=True))

    return _pcall(body, name="loss", out_shape=(_sds((8, 128)), _sds((T, D))), grid=(T // TB,),
                  in_specs=[_row_spec(D), _row_spec(D)],
                  out_specs=(pl.BlockSpec((8, 128), lambda i: (0, 0)), _row_spec(D)),
                  semantics=("arbitrary",))(y, target)


PAD = 32
CCH = 256


def _conv_fwd(u, dw_w, dw_b):
    def body(a_ref, gt_ref, w_ref, b_ref, z1_ref, zp_ref):
        zp_ref[0:PAD, :] = jnp.zeros((PAD, 128), F32)
        zp_ref[PAD:PAD + T, :] = a_ref[...].astype(F32) * _sigmoid(gt_ref[...].astype(F32))
        for c in range(T // CCH):
            acc = jnp.broadcast_to(b_ref[...], (CCH, 128))
            for j in range(KW):
                acc = acc + w_ref[j:j + 1, :] * zp_ref[pl.ds(c * CCH + j + PAD - (KW - 1), CCH), :]
            z1_ref[c * CCH:(c + 1) * CCH, :] = acc

    col = lambda off: pl.BlockSpec((T, 128), lambda j: (0, off * 4 + j))
    return _pcall(body, name="conv_fwd", out_shape=_sds((T, CONV)), grid=(CONV // 128,),
                  in_specs=[col(UB_A), col(UB_GT), pl.BlockSpec((KW, 128), lambda j: (0, j)),
                            pl.BlockSpec((1, 128), lambda j: (0, j))],
                  out_specs=col(0), scratch=[pltpu.VMEM((T + PAD, 128), F32)],
                  semantics=("parallel",))(u, u, dw_w, dw_b)


def _ln_silu_fwd(z1, g, b):
    def body(z_ref, g_ref, b_ref, o_ref):
        z = z_ref[...]
        mu = jnp.mean(z, axis=-1, keepdims=True)
        zc = z - mu
        zh = zc * lax.rsqrt(jnp.mean(zc * zc, axis=-1, keepdims=True) + EPS)
        z2 = zh * g_ref[...] + b_ref[...]
        o_ref[...] = (z2 * _sigmoid(z2)).astype(BF16)

    return _pcall(body, name="ln_silu_fwd", out_shape=_sds((T, CONV), BF16), grid=(T // TB,),
                  in_specs=[_row_spec(CONV), _vec_spec(CONV), _vec_spec(CONV)], out_specs=_row_spec(CONV),
                  semantics=("parallel",))(z1, g, b)


def _ln_silu_bwd(z1, g, b, dz3):
    def body(z_ref, g_ref, b_ref, d_ref, z3_ref, dz1_ref, dg_ref, db_ref):
        z = z_ref[...]
        mu = jnp.mean(z, axis=-1, keepdims=True)
        zc = z - mu
        rs = lax.rsqrt(jnp.mean(zc * zc, axis=-1, keepdims=True) + EPS)
        zh = zc * rs
        z2 = zh * g_ref[...] + b_ref[...]
        s = _sigmoid(z2)
        z3_ref[...] = (z2 * s).astype(BF16)
        dz2 = d_ref[...] * (s * (1.0 + z2 * (1.0 - s)))
        dzh = dz2 * g_ref[...]
        dz1_ref[...] = rs * (dzh - jnp.mean(dzh, axis=-1, keepdims=True)
                             - zh * jnp.mean(dzh * zh, axis=-1, keepdims=True))

        @pl.when(pl.program_id(0) == 0)
        def _():
            dg_ref[...] = jnp.zeros_like(dg_ref)
            db_ref[...] = jnp.zeros_like(db_ref)

        dg_ref[...] += jnp.sum(dz2 * zh, axis=0, keepdims=True)
        db_ref[...] += jnp.sum(dz2, axis=0, keepdims=True)

    return _pcall(body, name="ln_silu_bwd",
                  out_shape=(_sds((T, CONV), BF16), _sds((T, CONV)), _sds((1, CONV)), _sds((1, CONV))),
                  grid=(T // TB,),
                  in_specs=[_row_spec(CONV), _vec_spec(CONV), _vec_spec(CONV), _row_spec(CONV)],
                  out_specs=(_row_spec(CONV), _row_spec(CONV), _vec_spec(CONV), _vec_spec(CONV)),
                  semantics=("arbitrary",))(z1, g, b, dz3)


def _conv_bwd(u, dw_w, dz1):
    def body(a_ref, gt_ref, w_ref, dz1_ref, da_ref, dgt_ref, dw_ref, db_ref, zp_ref, dp_ref):
        a = a_ref[...].astype(F32)
        s = _sigmoid(gt_ref[...].astype(F32))
        zp_ref[0:PAD, :] = jnp.zeros((PAD, 128), F32)
        zp_ref[PAD:PAD + T, :] = a * s
        dp_ref[0:T, :] = dz1_ref[...]
        dp_ref[T:T + PAD, :] = jnp.zeros((PAD, 128), F32)
        db_ref[...] = jnp.sum(dz1_ref[...], axis=0, keepdims=True)
        for j in range(KW):
            tot = jnp.zeros((1, 128), F32)
            for c in range(T // CCH):
                tot = tot + jnp.sum(dz1_ref[c * CCH:(c + 1) * CCH, :]
                                    * zp_ref[pl.ds(c * CCH + j + PAD - (KW - 1), CCH), :], axis=0, keepdims=True)
            dw_ref[j:j + 1, :] = tot
        for c in range(T // CCH):
            acc = jnp.zeros((CCH, 128), F32)
            for j in range(KW):
                acc = acc + w_ref[j:j + 1, :] * dp_ref[pl.ds(c * CCH + (KW - 1) - j, CCH), :]
            rows = slice(c * CCH, (c + 1) * CCH)
            sc = _sigmoid(gt_ref[rows, :].astype(F32))
            da_ref[rows, :] = (acc * sc).astype(BF16)
            dgt_ref[rows, :] = (acc * a_ref[rows, :].astype(F32) * sc * (1.0 - sc)).astype(BF16)

    col = lambda off: pl.BlockSpec((T, 128), lambda j: (0, off * 4 + j))
    wspec = pl.BlockSpec((KW, 128), lambda j: (0, j))
    return _pcall(body, name="conv_bwd",
                  out_shape=(_sds((T, CONV), BF16), _sds((T, CONV), BF16), _sds((KW, CONV)), _sds((1, CONV))),
                  grid=(CONV // 128,), in_specs=[col(UB_A), col(UB_GT), wspec, col(0)],
                  out_specs=(col(0), col(0), wspec, pl.BlockSpec((1, 128), lambda j: (0, j))),
                  scratch=[pltpu.VMEM((T + PAD, 128), F32), pltpu.VMEM((T + PAD, 128), F32)],
                  semantics=("parallel",))(u, u, dw_w, dz1)


def _bucket_tables():
    qi = np.arange(BLK)[:, None]
    kj = np.arange(2 * BLK)[None, :]
    off = np.clip(qi + BLK - kj, 0, BLK)
    out = []
    for d in DIL:
        dist = (off * d).astype(np.int32)
        nf = np.maximum(dist, 1).astype(np.float32)
        large = 16 + (np.log(nf / np.float32(16)) / np.float32(math.log(2048 / 16)) * np.float32(16)).astype(np.int32)
        large = np.minimum(large, NBUCKET - 1)
        out.append(np.where(dist < 16, dist, large))
    return np.stack(out).astype(np.int32)


def _band():
    off = lax.broadcasted_iota(jnp.int32, (BLK, 2 * BLK), 0) + BLK - lax.broadcasted_iota(jnp.int32, (BLK, 2 * BLK), 1)
    return (off >= 0) & (off <= BLK)


def _bias_table(rel_bias_t, buckets):
    def body(rb_ref, bk_ref, o_ref):
        h = pl.program_id(0)
        bk = bk_ref[...]
        acc = jnp.zeros((BLK, 2 * BLK), F32)
        for b in range(NBUCKET):
            acc = jnp.where(bk == b, rb_ref[h, b], acc)
        o_ref[...] = jnp.where(_band(), acc, NEG)

    return _pcall(body, name="bias_table", out_shape=_sds((3 * 8, BLK, 2 * BLK)), grid=(24,),
                  in_specs=[pl.BlockSpec(memory_space=pltpu.SMEM),
                            pl.BlockSpec((None, BLK, 2 * BLK), lambda h: (h // 8, 0, 0))],
                  out_specs=pl.BlockSpec((None, BLK, 2 * BLK), lambda h: (h, 0, 0)),
                  semantics=("parallel",))(rel_bias_t, buckets)


def _bias_grad(ds_acc, buckets):
    def body(a_ref, bk_ref, o_ref):
        acc = a_ref[0]
        for l in range(1, DEPTH):
            acc = acc + a_ref[l]
        bk = bk_ref[...]
        lane = lax.broadcasted_iota(jnp.int32, (1, 128), 1)
        row = jnp.zeros((1, 128), F32)
        for b in range(NBUCKET):
            row = jnp.where(lane == b, jnp.sum(jnp.where(bk == b, acc, 0.0)), row)
        o_ref[...] = row

    return _pcall(body, name="bias_grad", out_shape=_sds((24, 1, 128)), grid=(24,),
                  in_specs=[pl.BlockSpec((DEPTH, None, BLK, 2 * BLK), lambda h: (0, h, 0, 0)),
                            pl.BlockSpec((None, BLK, 2 * BLK), lambda h: (h // 8, 0, 0))],
                  out_specs=pl.BlockSpec((None, 1, 128), lambda h: (h, 0, 0)),
                  semantics=("parallel",))(ds_acc, buckets)


def _head_mask():
    return lax.broadcasted_iota(jnp.int32, (1, 128), 1) < HD


def _seg_ones(width):
    r = lax.broadcasted_iota(jnp.int32, (width, width), 0) >> 6
    c = lax.broadcasted_iota(jnp.int32, (width, width), 1) >> 6
    return (r == c).astype(BF16)


def _seg_sum(x, ones):
    hi = x.astype(BF16)
    lo = (x - hi.astype(F32)).astype(BF16)
    return (jnp.dot(hi, ones, preferred_element_type=F32) + jnp.dot(lo, ones, preferred_element_type=F32))


def _dot(a, b, dims):
    return lax.dot_general(a, b, dims, preferred_element_type=F32)


def _tile_rows(d, r, n):
    stride = None if d == 1 else d
    q_rows = pl.ds(r + d * n * BLK, BLK, stride=stride)
    if n == 0:
        return q_rows, q_rows, BLK
    return q_rows, pl.ds(r + d * (n - 1) * BLK, 2 * BLK, stride=stride), 2 * BLK


def _stack_heads(x, m_a):
    return jnp.concatenate([jnp.where(m_a, x, 0.0), jnp.where(m_a, 0.0, x)], axis=0)


def _stack_cols(x, m_a):
    return jnp.concatenate([jnp.max(jnp.where(m_a, x, -3e38), axis=1, keepdims=True),
                            jnp.max(jnp.where(m_a, -3e38, x), axis=1, keepdims=True)], axis=0)


NCH = 256


def _qk_norm_prep(q_ref, k_ref, v_ref, gq_ref, gk_ref, qn_ref, kn_ref, vn_ref, ones):
    def prep(i, carry):
        rows = pl.ds(pl.multiple_of(i * NCH, NCH), NCH)
        q = q_ref[rows, :].astype(F32)
        qn_ref[rows, :] = q * lax.rsqrt(_seg_sum(q * q, ones) * (1.0 / HD) + EPS) * gq_ref[...] * (HD ** -0.5)
        k = k_ref[rows, :].astype(F32)
        kn_ref[rows, :] = k * lax.rsqrt(_seg_sum(k * k, ones) * (1.0 / HD) + EPS) * gk_ref[...]
        vn_ref[rows, :] = v_ref[rows, :].astype(F32)
        return carry

    lax.fori_loop(0, T // NCH, prep, 0)


def _attn_specs(g):
    ucol = lambda base: pl.BlockSpec((T, 128), lambda hp: (0, (base + g) * 4 + hp))
    col = pl.BlockSpec((T, 128), lambda hp: (0, hp))
    vec = pl.BlockSpec((1, 128), lambda hp: (0, 0))
    bm = pl.BlockSpec((2, BLK, 2 * BLK), lambda hp: (g * 4 + hp, 0, 0))
    return ucol, col, vec, bm


def _attn_fwd(g, u, gq, gk, bm, deps=()):
    d = DIL[g]

    def body(*refs):
        q_ref, k_ref, v_ref, gq_ref, gk_ref, bm_ref = refs[:6]
        o_ref, lse_ref, qn_ref, kn_ref, vn_ref = refs[-5:]
        ones = _seg_ones(128)
        _qk_norm_prep(q_ref, k_ref, v_ref, gq_ref, gk_ref, qn_ref, kn_ref, vn_ref, ones)
        m_a = _head_mask()
        for r in range(d):
            for n in range(T // d // BLK):
                q_rows, k_rows, nk = _tile_rows(d, r, n)
                qt = qn_ref[q_rows, :]
                kt = kn_ref[k_rows, :].astype(BF16)
                vt = vn_ref[k_rows, :].astype(BF16)
                q2 = _stack_heads(qt, m_a).astype(BF16)
                s = _dot(q2, kt, NT) + bm_ref[...].reshape(2 * BLK, 2 * BLK)[:, 2 * BLK - nk:]
                mx = jnp.max(s, axis=1, keepdims=True)
                p = jnp.exp(s - mx)
                l = jnp.sum(p, axis=1, keepdims=True)
                o2 = _dot(p.astype(BF16), vt, NN) / l
                lse2 = jnp.broadcast_to(mx + jnp.log(l), (2 * BLK, 128))
                o_ref[q_rows, :] = jnp.where(m_a, o2[:BLK], o2[BLK:])
                lse_ref[q_rows, :] = jnp.where(m_a, lse2[:BLK], lse2[BLK:])

    ucol, col, vec, bmspec = _attn_specs(g)
    return _pcall(body, name=f"attn_fwd_g{g}", out_shape=(_sds((T, AOUT)), _sds((T, AOUT))), grid=(4,),
                  in_specs=[ucol(UB_Q), ucol(UB_K), ucol(UB_V), vec, vec, bmspec] + [ANY_SPEC] * len(deps),
                  out_specs=(col, col), scratch=[pltpu.VMEM((T, 128), F32)] * 3,
                  semantics=("parallel",))(u, u, u, gq, gk, bm, *deps)


def _attn_bwd(g, u, gq, gk, bm, dog, cb, lse):
    d = DIL[g]

    def body(q_ref, k_ref, v_ref, gq_ref, gk_ref, bm_ref, do_ref, cb_ref, lse_ref,
             dqo_ref, dko_ref, dvo_ref, dgq_ref, dgk_ref, dsa_ref, qn_ref, kn_ref, vn_ref, dq_ref, dk_ref, dv_ref):
        ones = _seg_ones(128)
        _qk_norm_prep(q_ref, k_ref, v_ref, gq_ref, gk_ref, qn_ref, kn_ref, vn_ref, ones)
        m_a = _head_mask()
        dk_ref[...] = jnp.zeros_like(dk_ref)
        dv_ref[...] = jnp.zeros_like(dv_ref)
        dsa_ref[...] = jnp.zeros_like(dsa_ref)
        for r in range(d):
            for n in range(T // d // BLK):
                q_rows, k_rows, nk = _tile_rows(d, r, n)
                ktb = kn_ref[k_rows, :].astype(BF16)
                vtb = vn_ref[k_rows, :].astype(BF16)
                q2 = _stack_heads(qn_ref[q_rows, :], m_a).astype(BF16)
                do2 = _stack_heads(do_ref[q_rows, :], m_a).astype(BF16)
                lse_c = _stack_cols(lse_ref[q_rows, :], m_a)
                c_c = _stack_cols(cb_ref[q_rows, :], m_a)
                s = _dot(q2, ktb, NT) + bm_ref[...].reshape(2 * BLK, 2 * BLK)[:, 2 * BLK - nk:]
                p = jnp.exp(s - lse_c)
                ds = p * (_dot(do2, vtb, NT) + c_c)
                dsb = ds.astype(BF16)
                dq2 = _dot(dsb, ktb, NN)
                dq_ref[q_rows, :] = jnp.where(m_a, dq2[:BLK], dq2[BLK:])
                dk_ref[k_rows, :] += _dot(dsb, q2, TN)
                dv_ref[k_rows, :] += _dot(p.astype(BF16), do2, TN)
                dsa_ref[:, :, 2 * BLK - nk:] += ds.reshape(2, BLK, nk)

        @pl.when(pl.program_id(0) == 0)
        def _():
            dgq_ref[...] = jnp.zeros_like(dgq_ref)
            dgk_ref[...] = jnp.zeros_like(dgk_ref)

        def norm_bwd(i, carry):
            rows = pl.ds(pl.multiple_of(i * NCH, NCH), NCH)
            for x_ref, g_ref, dx_ref, dxo_ref, dg_ref, scale in (
                    (q_ref, gq_ref, dq_ref, dqo_ref, dgq_ref, HD ** -0.5), (k_ref, gk_ref, dk_ref, dko_ref, dgk_ref, 1.0)):
                x = x_ref[rows, :].astype(F32)
                rs = lax.rsqrt(_seg_sum(x * x, ones) * (1.0 / HD) + EPS)
                xh = x * rs
                dn = dx_ref[rows, :] * scale
                dxh = dn * g_ref[...]
                dxo_ref[rows, :] = (rs * (dxh - xh * (_seg_sum(dxh * xh, ones) * (1.0 / HD)))).astype(BF16)
                dg_ref[...] += jnp.sum(dn * xh, axis=0, keepdims=True)
            dvo_ref[rows, :] = dv_ref[rows, :].astype(BF16)
            return carry

        lax.fori_loop(0, T // NCH, norm_bwd, 0)

    ucol, col, vec, bmspec = _attn_specs(g)
    return _pcall(body, name=f"attn_bwd_g{g}",
                  out_shape=(_sds((T, AOUT), BF16), _sds((T, AOUT), BF16), _sds((T, AOUT), BF16), _sds((1, 128)),
                             _sds((1, 128)), _sds((8, BLK, 2 * BLK))),
                  grid=(4,),
                  in_specs=[ucol(UB_Q), ucol(UB_K), ucol(UB_V), vec, vec, bmspec, col, col, col],
                  out_specs=(col, col, col, vec, vec, pl.BlockSpec((2, BLK, 2 * BLK), lambda hp: (hp, 0, 0))),
                  scratch=[pltpu.VMEM((T, 128), F32)] * 6,
                  semantics=("arbitrary",))(u, u, u, gq, gk, bm, dog, cb, lse)


def _combine_fwd(ogs, lses):
    def body(o0, o1, o2, l0, l1, l2, o_ref):
        ls = [l0[...], l1[...], l2[...]]
        mx = jnp.maximum(jnp.maximum(ls[0], ls[1]), ls[2])
        es = [jnp.exp(l - mx) for l in ls]
        inv = 1.0 / (es[0] + es[1] + es[2])
        o_ref[...] = ((es[0] * o0[...] + es[1] * o1[...] + es[2] * o2[...]) * inv).astype(BF16)

    return _pcall(body, name="combine_fwd", out_shape=_sds((T, AOUT), BF16), grid=(T // TB,),
                  in_specs=[_row_spec(AOUT)] * 6, out_specs=_row_spec(AOUT), semantics=("parallel",))(*ogs, *lses)


def _combine_bwd(ogs, lses, do):
    def body(o0, o1, o2, l0, l1, l2, do_ref, d0, d1, d2, c0, c1, c2):
        ls = [l0[...], l1[...], l2[...]]
        mx = jnp.maximum(jnp.maximum(ls[0], ls[1]), ls[2])
        es = [jnp.exp(l - mx) for l in ls]
        inv = 1.0 / (es[0] + es[1] + es[2])
        ws = [e * inv for e in es]
        do = do_ref[...]
        o = ws[0] * o0[...] + ws[1] * o1[...] + ws[2] * o2[...]
        s = _seg_sum(do * o, _seg_ones(AOUT))
        for w, d_ref, c_ref in zip(ws, (d0, d1, d2), (c0, c1, c2)):
            d_ref[...] = w * do
            c_ref[...] = -(w * s)

    return _pcall(body, name="combine_bwd", out_shape=tuple(_sds((T, AOUT)) for _ in range(6)), grid=(T // TB,),
                  in_specs=[_row_spec(AOUT)] * 7, out_specs=tuple(_row_spec(AOUT) for _ in range(6)),
                  semantics=("parallel",))(*ogs, *lses, do)


def _layer_fwd(x, p, bm, deps=(), mid=None):
    h1 = _rms_fwd(x, p["n1g"])
    u = _mm_x_wcols("mm_u", h1, p["win4"], tm=T, tn=640, out_dtype=BF16, deps=deps)
    ogs, lses = [], []
    for g in range(NG):
        gdeps = (p["hook"](ogs[-1]),) if g == NG - 1 and "hook" in p else ()
        og, lse = _attn_fwd(g, u, p["gq"], p["gk"], bm, deps=gdeps)
        ogs.append(og)
        lses.append(lse)
    o = _combine_fwd(ogs, lses)
    z1 = _conv_fwd(u, p["dww"], p["dwb"])
    z3 = _ln_silu_fwd(z1, p["lng"], p["lnb"])
    if "rest" in p:
        p = {**p, **p["rest"](z3)}
    ycv = _mm_x_wcols("mm_ycv", z3, p["wco4"], tm=T, tn=256)
    yat = _mm_x_wcols("mm_yat", o, p["wao4"], tm=T, tn=256)
    m = _gate_fwd(u, ycv, yat)
    xm = _mm_x_wrows("mm_xmid", m, p["wout4"], x, tm=1024, tk=256, tn=1024)
    h2 = _rms_fwd(xm, p["n2g"])
    fa, r = _mm_ff1(h2, p["wff14"], tm=T, tn=512)
    tok = mid(r) if mid else None
    xo = _mm_x_wrows("mm_xout", r, p["wff24"], xm, tm=1024, tk=1024, tn=1024, deps=(tok,))
    saved = dict(x=x, h1=h1, u=u, z1=z1, ogs=ogs, lses=lses, o=o, ycv=ycv, yat=yat, m=m, xm=xm, h2=h2, fa=fa, r=r)
    return xo, p, saved


EARLY = ("w_ff2", "w_ff1", "w_out")
LATE = ("w_conv_out", "w_attn_out", "w_in")


def _layer_bwd(dx, s, p, bm, pipe=None, own_early=None, own_late=None):
    u = s["u"]
    tok = pipe.step0() if pipe else None
    df = _mm_dff2(dx, p["wff24"], s["fa"], tm=1024, tn=512, deps=(tok,))
    g_ff2 = _mm_dw_rows("mm_dwff2", s["r"], dx, ks=1024, tm=1024, tn=512)
    g_ff1 = _mm_dw_cols("mm_dwff1", s["h2"], df, ns=1024, tm=1024, tn=512)
    tok = pipe.step1(g_ff1) if pipe else None
    dh2 = _mm_g_wcols_t("mm_dh2", df, p["wff14"], tm=1024, tk=1024, tn=1024, deps=(tok,))
    dxm, d_n2g = _rms_bwd(s["xm"], p["n2g"], dh2, dx)

    dm = _mm_g_wrows_t("mm_dm", dxm, p["wout4"], tm=1024, tn=256)
    g_out = _mm_dw_rows("mm_dwout", s["m"], dxm, ks=256, tm=256, tn=512)
    early = own_early(dict(w_ff2=g_ff2, w_ff1=g_ff1, w_out=g_out)) if own_early else None
    tok_e = early.step0() if early else None
    dyc, dya, dgc, dga = _gate_bwd(u, s["ycv"], s["yat"], dm)

    dz3 = _mm_g_wcols_t("mm_dz3", dyc, p["wco4"], tm=T, tk=256, tn=512, deps=(tok_e,))
    z3, dz1, d_lng, d_lnb = _ln_silu_bwd(s["z1"], p["lng"], p["lnb"], dz3)
    g_co = _mm_dw_cols("mm_dwco", z3, dyc, ns=256, tm=512, tn=256)
    da, dgt, d_dww, d_dwb = _conv_bwd(u, p["dww"], dz1)

    tok_e = early.step1(da) if early else None
    do = _mm_g_wcols_t("mm_do", dya, p["wao4"], tm=T, tk=256, tn=512, deps=(tok_e,))
    g_ao = _mm_dw_cols("mm_dwao", s["o"], dya, ns=256, tm=512, tn=256)
    parts = _combine_bwd(s["ogs"], s["lses"], do)
    dqs, dks, dvs, d_gq, d_gk, dsas = [], [], [], [], [], []
    for g in range(NG):
        dq, dk, dv, dgq, dgk, dsa = _attn_bwd(g, u, p["gq"], p["gk"], bm, parts[g], parts[NG + g], s["lses"][g])
        dqs.append(dq)
        dks.append(dk)
        dvs.append(dv)
        d_gq.append(dgq)
        d_gk.append(dgk)
        dsas.append(dsa)
    du = jnp.concatenate([da, dgt] + dqs + dks + dvs + [dgc, dga], axis=1)
    tok = pipe.step2(du) if pipe else None
    tok_e = early.step2(du) if early else None
    g_in = _mm_dw_cols("mm_dwin", s["h1"], du, ns=1920, tm=1024, tn=640, deps=(tok, tok_e))
    late = own_late(dict(w_in=g_in, w_conv_out=g_co, w_attn_out=g_ao)) if own_late else None
    tok_l = late.step0() if late else None
    dh1 = _mm_g_wcols_t("mm_dh1", du, p["win4"], tm=1024, tk=1920, tn=1024, deps=(tok_l,))
    tok_l = late.step1(dh1) if late else None
    dxi, d_n1g = _rms_bwd(s["x"], p["n1g"], dh1, dxm, deps=(tok_l,))
    if pipe:
        pipe.step3(dxi)

    fold = lambda parts_: sum(v[0, :HD] + v[0, HD:] for v in parts_)
    big = dict(w_in=g_in, w_conv_out=g_co, w_attn_out=g_ao, w_out=g_out, w_ff1=g_ff1, w_ff2=g_ff2)
    small = dict(norm1_g=d_n1g[0], q_norm_g=fold(d_gq), k_norm_g=fold(d_gk), conv_dw_w=d_dww, conv_dw_b=d_dwb[0],
                 conv_ln_g=d_lng[0], conv_ln_b=d_lnb[0], norm2_g=d_n2g[0])
    return dxi, big, small, jnp.concatenate(dsas, axis=0), early, late


def _local_step(x, target, get_layer, rel_bias, make_pipe):
    buckets = jnp.asarray(_bucket_tables())
    bm = _bias_table(rel_bias.T, buckets)
    saved, layers = [], []
    for l in range(DEPTH):
        p, deps, mid = get_layer(l, x)
        x, p, s = _layer_fwd(x, p, bm, deps=deps, mid=mid)
        layers.append(p)
        saved.append(s)
    loss_blk, dx = _loss_fwd_bwd(x, target)
    smalls, dsas = [None] * DEPTH, [None] * DEPTH
    pipe, pipes = None, []
    for l in reversed(range(DEPTH)):
        if l > 0:
            dx, big, smalls[l], dsas[l], _, _ = _layer_bwd(dx, saved[l], layers[l], bm, pipe)
            pipe = make_pipe(l, BIG, "", big)
            pipes.append(pipe)
        else:
            dx, big, smalls[l], dsas[l], early, late = _layer_bwd(
                dx, saved[l], layers[l], bm, pipe, lambda big_: make_pipe(0, EARLY, "e", big_),
                lambda big_: make_pipe(0, LATE, "", big_))
    d_rel = _bias_grad(jnp.stack(dsas), buckets)[:, 0, :NBUCKET].T
    return loss_blk[0, 0], dx, smalls, d_rel, pipes, early, late


MESH = pl.DeviceIdType.MESH


def _me():
    return lax.axis_index("x"), lax.axis_index("y"), lax.axis_index("c")


def _other_chips(mx, my):
    return [(1 - mx, my), (mx, 1 - my), (1 - mx, 1 - my)]


def _rcopy(src, dst, send_sems, recv_sems, k, dev):
    return pltpu.make_async_remote_copy(src_ref=src, dst_ref=dst, send_sem=send_sems.at[k], recv_sem=recv_sems.at[k],
                                        device_id=dev, device_id_type=MESH)


def _comm_call(body, name, out_shape, n_in, n_sems):
    return pl.pallas_call(
        body, name=name, out_shape=out_shape, in_specs=[HBM_SPEC] * n_in,
        out_specs=jax.tree.map(lambda _: HBM_SPEC, out_shape),
        scratch_shapes=[pltpu.SemaphoreType.DMA((n_sems,)), pltpu.SemaphoreType.DMA((n_sems,)),
                        pltpu.SemaphoreType.DMA(())],
        compiler_params=pltpu.CompilerParams(has_side_effects=True))


def _all_gather_chips(x, name):
    def body(x_ref, o_ref, send_sems, recv_sems, local_sem):
        mx, my, mc = _me()
        local = pltpu.make_async_copy(x_ref, o_ref.at[2 * mx + my], local_sem)
        local.start()
        sends = [_rcopy(x_ref, o_ref.at[2 * mx + my], send_sems, recv_sems, k, (px, py, mc))
                 for k, (px, py) in enumerate(_other_chips(mx, my))]
        for cp in sends:
            cp.start()
        for k, (px, py) in enumerate(_other_chips(mx, my)):
            _rcopy(x_ref, o_ref.at[2 * px + py], send_sems, recv_sems, k, (px, py, mc)).wait_recv()
        for cp in sends:
            cp.wait_send()
        local.wait()

    return _comm_call(body, name, _sds((NCHIP,) + x.shape, x.dtype), 1, 3)(x)


EFFECT = pltpu.SideEffectType.DATAFLOW_SIDE_EFFECTING


def _hbm(a):
    return pltpu.with_memory_space_constraint(a, pltpu.HBM)


def _split_start(name, bufs, plan, n, after=None):
    nb = len(bufs)
    extra = [] if after is None else [after]
    ne = len(extra)

    def body(*refs):
        send_sems, recv_sems, token = refs[nb + ne], refs[nb + ne + 1], refs[-1]
        mx, my, mc = _me()
        for k, (src, dst, dev, _) in enumerate(plan(refs[:nb], mx, my, mc)):
            _rcopy(src, dst, send_sems, recv_sems, k, dev).start()
        token[...] = jnp.zeros_like(token)

    out = pl.pallas_call(
        body, name=name,
        out_shape=(pltpu.SemaphoreType.DMA((n,)), pltpu.SemaphoreType.DMA((n,)),
                   *[pltpu.HBM(b.shape, b.dtype) for b in bufs], _sds((8, 128))),
        in_specs=[HBM_SPEC] * nb + [ANY_SPEC] * ne,
        out_specs=(SEM_SPEC, SEM_SPEC, *[HBM_SPEC] * nb, pl.BlockSpec(memory_space=pltpu.VMEM)),
        input_output_aliases={i: 2 + i for i in range(nb)},
        compiler_params=pltpu.CompilerParams(has_side_effects=EFFECT))(*[_hbm(b) for b in bufs], *extra)
    return (out[0], out[1]), list(out[2:2 + nb]), out[-1]


def _split_wait(name, sems, bufs, plan, after):
    nb = len(bufs)

    def body(*refs):
        send_sems, recv_sems = refs[nb], refs[nb + 1]
        mx, my, mc = _me()
        for k, (src, dst, dev, land) in enumerate(plan(refs[:nb], mx, my, mc)):
            _rcopy(src, dst, send_sems, recv_sems, k, dev).wait_send()
            _rcopy(src, land, send_sems, recv_sems, k, dev).wait_recv()

    out = pl.pallas_call(
        body, name=name, out_shape=tuple(pltpu.HBM(b.shape, b.dtype) for b in bufs),
        in_specs=[HBM_SPEC] * nb + [SEM_SPEC, SEM_SPEC, ANY_SPEC], out_specs=(HBM_SPEC,) * nb,
        input_output_aliases={i: i for i in range(nb)},
        compiler_params=pltpu.CompilerParams(has_side_effects=EFFECT))(*bufs, sems[0], sems[1], after)
    return list(out)


def _plan_gather_chips(refs, mx, my, mc):
    me = 2 * mx + my
    return [(r.at[me, mc], r.at[me, mc], (px, py, mc), r.at[2 * px + py, mc])
            for r in refs for px, py in _other_chips(mx, my)]


def _plan_gather_pair(refs, mx, my, mc):
    return [(r.at[2 * px + py, mc], r.at[2 * px + py, mc], (mx, my, 1 - mc), r.at[2 * px + py, 1 - mc])
            for r in refs for px, py in _other_chips(mx, my)]


def _plan_gather_devices(refs, mx, my, mc):
    flip = lambda m, b: 1 - m if b else m
    peers = [(flip(mx, k >> 2 & 1), flip(my, k >> 1 & 1), flip(mc, k & 1)) for k in range(1, 8)]
    slot = lambda dev: 4 * dev[0] + 2 * dev[1] + dev[2]
    me = slot((mx, my, mc))
    return [(r.at[me], r.at[me], dev, r.at[slot(dev)]) for r in refs for dev in peers]


def _plan_pair_half(refs, mx, my, mc):
    n = len(refs) // 2
    return [(g.at[:, 1 - mc], r, (mx, my, 1 - mc), r) for g, r in zip(refs[:n], refs[n:])]


def _plan_scatter(refs, mx, my, mc):
    n = len(refs) // 2
    return [(q.at[2 * px + py], r.at[k], (px, py, mc), r.at[k])
            for q, r in zip(refs[:n], refs[n:]) for k, (px, py) in enumerate(_other_chips(mx, my))]


def _plan_pair_fill(refs, mx, my, mc):
    return [(r.at[mc], r.at[mc], (mx, my, 1 - mc), r.at[1 - mc]) for r in refs]


def _all_gather_devices(v, name):
    def body(v_ref, o_ref, send_sems, recv_sems, local_sem):
        mx, my, mc = _me()
        flip = lambda m, b: 1 - m if b else m
        peers = [(flip(mx, k >> 2 & 1), flip(my, k >> 1 & 1), flip(mc, k & 1)) for k in range(1, 8)]
        slot = lambda d: 4 * d[0] + 2 * d[1] + d[2]
        local = pltpu.make_async_copy(v_ref, o_ref.at[slot((mx, my, mc))], local_sem)
        local.start()
        sends = [_rcopy(v_ref, o_ref.at[slot((mx, my, mc))], send_sems, recv_sems, k, dev)
                 for k, dev in enumerate(peers)]
        for cp in sends:
            cp.start()
        for k, dev in enumerate(peers):
            _rcopy(v_ref, o_ref.at[slot(dev)], send_sems, recv_sems, k, dev).wait_recv()
        for cp in sends:
            cp.wait_send()
        local.wait()

    return _comm_call(body, name, _sds((8,) + v.shape, v.dtype), 1, 7)(v)


def _row_tile(rows, cols):
    t = 8
    while t * 2 * cols * 4 <= (1 << 20) and rows % (t * 2) == 0:
        t *= 2
    return t


def _prefetch_call(body, name, out_shape, grid, in_specs, out_specs):
    return pl.pallas_call(
        body, name=name, out_shape=out_shape,
        grid_spec=pltpu.PrefetchScalarGridSpec(num_scalar_prefetch=1, grid=grid, in_specs=in_specs,
                                               out_specs=out_specs),
        compiler_params=pltpu.CompilerParams(vmem_limit_bytes=VMEM_LIMIT,
                                             dimension_semantics=("parallel",) * len(grid)))


def _sum_half(g, r1, place, name):
    _, _, rr, ns = g.shape
    tr = _row_tile(rr, ns)

    def body(c_ref, g_ref, r_ref, o_ref, ob_ref):
        q = g_ref[...] + r_ref[...]
        ob_ref[...] = q.astype(BF16)

        @pl.when(pl.program_id(1) == c_ref[0])
        def _():
            o_ref[...] = q

    blk = pl.BlockSpec((None, tr, ns), lambda i, s, c: (s, i, 0))
    return pl.pallas_call(
        body, name=name, out_shape=(_sds((rr, ns)), _sds((NCHIP, rr, ns), BF16)),
        grid_spec=pltpu.PrefetchScalarGridSpec(
            num_scalar_prefetch=1, grid=(rr // tr, NCHIP),
            in_specs=[pl.BlockSpec((None, None, tr, ns), lambda i, s, c: (s, c[1], i, 0)), blk],
            out_specs=(pl.BlockSpec((tr, ns), lambda i, s, c: (i, 0)), blk)),
        compiler_params=pltpu.CompilerParams(vmem_limit_bytes=VMEM_LIMIT,
                                             dimension_semantics=("parallel", "arbitrary")))(place, g, r1)


def _sum_recv(q, r2, place, name):
    rr, ns = q.shape
    tr = _row_tile(rr, ns)

    def body(c_ref, q_ref, r_ref, o_ref):
        o_ref[...] = ((q_ref[...] + r_ref[0].astype(F32)) + r_ref[1].astype(F32)) + r_ref[2].astype(F32)

    return _prefetch_call(body, name, _sds((2, rr, ns)), (rr // tr,),
                          [pl.BlockSpec((tr, ns), lambda i, c: (i, 0)),
                           pl.BlockSpec((NCHIP - 1, tr, ns), lambda i, c: (0, i, 0))],
                          pl.BlockSpec((None, tr, ns), lambda i, c: (c[1], i, 0)))(place, q, r2)


def _sum_devices(v8):
    def body(v_ref, o_ref):
        acc = v_ref[0]
        for dev in range(1, 8):
            acc = acc + v_ref[dev]
        o_ref[...] = acc

    return _pcall(body, name="sum_devices", out_shape=_sds(v8.shape[1:]))(v8)


def _adamw(w, g, m, v, name):
    rows, cols = w.shape
    tr = _row_tile(rows, cols)

    def body(w_ref, g_ref, m_ref, v_ref, d_ref, m2_ref, v2_ref):
        g = g_ref[...]
        m2 = ADAM_B1 * m_ref[...] + (1.0 - ADAM_B1) * g
        v2 = ADAM_B2 * v_ref[...] + (1.0 - ADAM_B2) * (g * g)
        m_hat = m2 / (1.0 - ADAM_B1 ** ADAM_STEP)
        v_hat = v2 / (1.0 - ADAM_B2 ** ADAM_STEP)
        d_ref[...] = -ADAM_LR * (m_hat / (jnp.sqrt(v_hat) + ADAM_EPS) + ADAM_WD * w_ref[...])
        m2_ref[...] = m2
        v2_ref[...] = v2

    blk = pl.BlockSpec((tr, cols), lambda i: (i, 0))
    return _pcall(body, name=name, out_shape=(_sds((rows, cols)),) * 3, grid=(rows // tr,), in_specs=[blk] * 4,
                  out_specs=(blk,) * 3, semantics=("parallel",))(w, g, m, v)


BIG = ("w_in", "w_conv_out", "w_attn_out", "w_out", "w_ff1", "w_ff2")
SMALL = ("rel_bias", "norm1_g", "q_norm_g", "k_norm_g", "conv_dw_w", "conv_dw_b", "conv_ln_g", "conv_ln_b", "norm2_g")
WEIGHTS = ("rel_bias", "norm1_g", "w_in", "q_norm_g", "k_norm_g", "conv_dw_w", "conv_dw_b", "conv_ln_g", "conv_ln_b",
           "w_conv_out", "w_attn_out", "w_out", "norm2_g", "w_ff1", "w_ff2")


def _pack(arrays):
    flat = jnp.concatenate([a.reshape(-1) for a in arrays])
    n = flat.shape[0]
    rows = -(-n // 1024) * 8
    return jnp.pad(flat, (0, rows * 128 - n)).reshape(rows, 128)


def _unpack(packed, shapes):
    flat = packed.reshape(-1)
    out, off = [], 0
    for shp in shapes:
        n = int(np.prod(shp))
        out.append(flat[off:off + n].reshape(shp))
        off += n
    return out


def _adamw_layer(l, w, g, m, v, prev, name, deps=()):
    _, k, n = w.shape
    tr = _row_tile(k, n)
    deps = tuple(d for d in deps if d is not None)
    if prev is None:
        prev = tuple(lax.empty(w.shape, F32) for _ in range(4))

    def body(*refs):
        w_ref, g_ref, m_ref, v_ref = refs[:4]
        go_ref, d_ref, m2_ref, v2_ref = refs[-4:]
        g = g_ref[...]
        m2 = ADAM_B1 * m_ref[...] + (1.0 - ADAM_B1) * g
        v2 = ADAM_B2 * v_ref[...] + (1.0 - ADAM_B2) * (g * g)
        m_hat = m2 / (1.0 - ADAM_B1 ** ADAM_STEP)
        v_hat = v2 / (1.0 - ADAM_B2 ** ADAM_STEP)
        go_ref[...] = g
        d_ref[...] = -ADAM_LR * (m_hat / (jnp.sqrt(v_hat) + ADAM_EPS) + ADAM_WD * w_ref[...])
        m2_ref[...] = m2
        v2_ref[...] = v2

    lay = pl.BlockSpec((None, tr, n), lambda i: (l, i, 0))
    return _pcall(body, name=name, out_shape=(_sds(w.shape),) * 4, grid=(k // tr,),
                  in_specs=[lay, pl.BlockSpec((tr, n), lambda i: (i, 0)), lay, lay] + [ANY_SPEC] * (4 + len(deps)),
                  out_specs=(lay,) * 4, aliases={4: 0, 5: 1, 6: 2, 7: 3},
                  semantics=("parallel",))(w, g, m, v, *prev, *deps)


class _GradPipe:
    def __init__(self, l, kinds, tag, big, place, w, m, v, results):
        self.l, self.kinds, self.place, self.w, self.m, self.v, self.results = l, kinds, place, w, m, v, results
        self.id = f"l{l}{tag}"
        self.g = [big[n].reshape(NCHIP, 2, big[n].shape[1] // 2, big[n].shape[2]) for n in kinds]

    def step0(self):
        lands = [lax.empty((NCHIP,) + g.shape[2:], F32) for g in self.g]
        self.s1, self.b1, tok = _split_start(f"rs1_start_{self.id}", self.g + lands, _plan_pair_half, len(self.kinds))
        return tok

    def step1(self, after):
        nk = len(self.kinds)
        bufs = _split_wait(f"rs1_wait_{self.id}", self.s1, self.b1, _plan_pair_half, after)
        sums = [_sum_half(bufs[i], bufs[nk + i], self.place, f"rs1_sum_{n}") for i, n in enumerate(self.kinds)]
        self.q = [q for q, _ in sums]
        qb = [b for _, b in sums]
        lands = [lax.empty((NCHIP - 1,) + b.shape[1:], BF16) for b in qb]
        self.s2, self.b2, tok = _split_start(f"rs2_start_{self.id}", qb + lands, _plan_scatter, 3 * nk)
        return tok

    def step2(self, after):
        nk = len(self.kinds)
        bufs = _split_wait(f"rs2_wait_{self.id}", self.s2, self.b2, _plan_scatter, after)
        fin = [_sum_recv(self.q[i], bufs[nk + i], self.place, f"rs2_sum_{n}") for i, n in enumerate(self.kinds)]
        self.s3, self.b3, tok = _split_start(f"rs3_start_{self.id}", fin, _plan_pair_fill, nk)
        return tok

    def step3(self, after):
        self.fin = _split_wait(f"rs3_wait_{self.id}", self.s3, self.b3, _plan_pair_fill, after)

    def adam(self, deps=()):
        for i, n in enumerate(self.kinds):
            g2 = self.fin[i].reshape(self.fin[i].shape[1] * 2, self.fin[i].shape[2])
            self.results[n] = _adamw_layer(self.l, self.w[n], g2, self.m[n], self.v[n], self.results.get(n),
                                           f"adamw_{n}_l{self.l}", deps=deps if i == 0 else ())
        return self.results[self.kinds[-1]][1]


def kernel(x, rel_bias, norm1_g, w_in, q_norm_g, k_norm_g, conv_dw_w, conv_dw_b, conv_ln_g, conv_ln_b, w_conv_out, w_attn_out, w_out, norm2_g, w_ff1, w_ff2, loss_target, m_rel_bias, m_norm1_g, m_w_in, m_q_norm_g, m_k_norm_g, m_conv_dw_w, m_conv_dw_b, m_conv_ln_g, m_conv_ln_b, m_w_conv_out, m_w_attn_out, m_w_out, m_norm2_g, m_w_ff1, m_w_ff2, v_rel_bias, v_norm1_g, v_w_in, v_q_norm_g, v_k_norm_g, v_conv_dw_w, v_conv_dw_b, v_conv_ln_g, v_conv_ln_b, v_w_conv_out, v_w_attn_out, v_w_out, v_norm2_g, v_w_ff1, v_w_ff2):
    w = dict(rel_bias=rel_bias, norm1_g=norm1_g, w_in=w_in, q_norm_g=q_norm_g, k_norm_g=k_norm_g, conv_dw_w=conv_dw_w,
             conv_dw_b=conv_dw_b, conv_ln_g=conv_ln_g, conv_ln_b=conv_ln_b, w_conv_out=w_conv_out,
             w_attn_out=w_attn_out, w_out=w_out, norm2_g=norm2_g, w_ff1=w_ff1, w_ff2=w_ff2)
    m = dict(rel_bias=m_rel_bias, norm1_g=m_norm1_g, w_in=m_w_in, q_norm_g=m_q_norm_g, k_norm_g=m_k_norm_g,
             conv_dw_w=m_conv_dw_w, conv_dw_b=m_conv_dw_b, conv_ln_g=m_conv_ln_g, conv_ln_b=m_conv_ln_b,
             w_conv_out=m_w_conv_out, w_attn_out=m_w_attn_out, w_out=m_w_out, norm2_g=m_norm2_g, w_ff1=m_w_ff1,
             w_ff2=m_w_ff2)
    v = dict(rel_bias=v_rel_bias, norm1_g=v_norm1_g, w_in=v_w_in, q_norm_g=v_q_norm_g, k_norm_g=v_k_norm_g,
             conv_dw_w=v_conv_dw_w, conv_dw_b=v_conv_dw_b, conv_ln_g=v_conv_ln_g, conv_ln_b=v_conv_ln_b,
             w_conv_out=v_w_conv_out, w_attn_out=v_w_attn_out, w_out=v_w_out, norm2_g=v_norm2_g, w_ff1=v_w_ff1,
             w_ff2=v_w_ff2)
    chip_id = 2 * lax.axis_index("x") + lax.axis_index("y")
    place = jnp.stack([chip_id, lax.axis_index("c")]).astype(jnp.int32)

    dww4 = _all_gather_chips(conv_dw_w, "ag_conv_dw_w")
    dww = dww4.transpose(1, 2, 0, 3).reshape(DEPTH, KW, CONV)

    names = dict(w_in="win4", w_conv_out="wco4", w_attn_out="wao4", w_out="wout4", w_ff1="wff14", w_ff2="wff24")
    chips, pair = {}, {}

    def start_chips(key, l, kinds, after):
        lands = []
        for n in kinds:
            k, ns = w[n].shape[1:]
            land = lax.dynamic_update_slice(lax.empty((NCHIP, k, ns), BF16), w[n][l].astype(BF16)[None], (chip_id, 0, 0))
            lands.append(land.reshape(NCHIP, 2, k // 2, ns))
        chips[key] = _split_start(f"ag_chips_start_{key}", lands, _plan_gather_chips, 3 * len(kinds), after=after)
        return chips[key][2]

    def start_pair(key, after):
        sems, bufs, _ = chips[key]
        bufs = _split_wait(f"ag_chips_wait_{key}", sems, bufs, _plan_gather_chips, after)
        pair[key] = _split_start(f"ag_pair_start_{key}", bufs, _plan_gather_pair, len(bufs) * 3)
        return pair[key][2]

    def landed(key, kinds, after):
        sems, bufs, _ = pair[key]
        bufs = _split_wait(f"ag_pair_wait_{key}", sems, bufs, _plan_gather_pair, after)
        return {names[n]: b.reshape(NCHIP, 2 * b.shape[2], b.shape[3]) for n, b in zip(kinds, bufs)}

    first, rest = ("w_in",), tuple(n for n in BIG if n != "w_in")
    start_chips("l0b", 0, rest, start_pair("l0a", start_chips("l0a", 0, first, dww4)))

    def get_layer(l, after):
        p = dict(dww=dww[l], dwb=conv_dw_b[l][None], lng=conv_ln_g[l][None], lnb=conv_ln_b[l][None],
                 n1g=norm1_g[l][None], n2g=norm2_g[l][None], gq=jnp.tile(q_norm_g[l], 2)[None],
                 gk=jnp.tile(k_norm_g[l], 2)[None])
        p.update(landed(f"l{l}a", first, chips["l0b"][2] if l == 0 else after))
        more = l + 1 < DEPTH

        def hook(after_):
            tok = start_pair(f"l{l}b", after_)
            return start_chips(f"l{l + 1}a", l + 1, first, tok) if more else tok

        p["hook"] = hook
        p["rest"] = lambda after_: landed(f"l{l}b", rest, after_)
        mid = (lambda after_: start_chips(f"l{l + 1}b", l + 1, rest, start_pair(f"l{l + 1}a", after_))) if more else None
        return p, (), mid

    results = {}
    make_pipe = lambda l, kinds, tag, big: _GradPipe(l, kinds, tag, big, place, w, m, v, results)
    loss_share, dx, smalls, d_rel, pipes, early, late = _local_step(x[0], loss_target[0], get_layer, rel_bias,
                                                                    make_pipe)
    loss = lax.psum(loss_share, ("x", "y", "c"))

    local_small = dict(rel_bias=d_rel)
    for n in SMALL[1:]:
        local_small[n] = jnp.stack([smalls[l][n] for l in range(DEPTH)])
    small_shapes = [local_small[n].shape for n in SMALL]
    mine = _pack([local_small[n] for n in SMALL])
    slot = 4 * lax.axis_index("x") + 2 * lax.axis_index("y") + lax.axis_index("c")
    land = lax.dynamic_update_slice(lax.empty((8,) + mine.shape, F32), mine[None], (slot, 0, 0))
    small_sems, small_bufs, tok = _split_start("ag_small_start", [land], _plan_gather_devices, 7)
    for pipe in pipes:
        done = pipe.adam(deps=(tok,))
        tok = None
    early.step3(late.step2(done))
    done = early.adam()
    late.step3(done)
    done = late.adam()
    gathered = _split_wait("ag_small_wait", small_sems, small_bufs, _plan_gather_devices, done)[0]
    summed = _sum_devices(gathered)
    grads = dict(zip(SMALL, _unpack(summed, small_shapes)))
    grads["conv_dw_w"] = lax.dynamic_slice_in_dim(grads["conv_dw_w"], chip_id * 128, 128, axis=2)

    delta, new_m, new_v = {}, {}, {}
    small_w_shapes = [w[n].shape for n in SMALL]
    outs = _adamw(_pack([w[n] for n in SMALL]), _pack([grads[n] for n in SMALL]), _pack([m[n] for n in SMALL]),
                  _pack([v[n] for n in SMALL]), "adamw_small")
    for dst, packed in zip((delta, new_m, new_v), outs):
        dst.update(zip(SMALL, _unpack(packed, small_w_shapes)))

    for n in BIG:
        grads[n], delta[n], new_m[n], new_v[n] = results[n]

    return (loss, dx[None], *[grads[n] for n in WEIGHTS], *[delta[n] for n in WEIGHTS],
            *[new_m[n] for n in WEIGHTS], *[new_v[n] for n in WEIGHTS])
```

```python
import functools
import math

import numpy as np
import jax
import jax.numpy as jnp
from jax import lax
from jax.experimental import pallas as pl
from jax.experimental.pallas import tpu as pltpu

F32 = jnp.float32
BF16 = jnp.bfloat16

T = 2048
D = 1024
DEPTH = 4
CONV = 512
KW = 31
NG = 3
HD = 64
AOUT = 512
DFF = 4096
INC = 7680
DIL = (1, 4, 16)
BLK = 128
NBUCKET = 32
EPS = 1e-6
NEG = -1e30
NCHIP = 4
UB_A, UB_GT, UB_Q, UB_K, UB_V, UB_GC, UB_GA = 0, 1, 2, 5, 8, 11, 13

ADAM_LR, ADAM_B1, ADAM_B2, ADAM_EPS, ADAM_WD, ADAM_STEP = 0.001, 0.9, 0.999, 1e-08, 0.01, 10

VMEM_LIMIT = 48 * 1024 * 1024
TB = 256
HBM_SPEC = pl.BlockSpec(memory_space=pltpu.HBM)
ANY_SPEC = pl.BlockSpec(memory_space=pl.ANY)
SEM_SPEC = pl.BlockSpec(memory_space=pltpu.SEMAPHORE)


def _pcall(body, *, name, out_shape, grid=(), in_specs=None, out_specs=None, scratch=(), aliases=None,
           semantics=None, hbm_inputs=True):
    kw = {}
    if in_specs is not None:
        kw["in_specs"] = in_specs
    if out_specs is not None:
        kw["out_specs"] = out_specs
    call = pl.pallas_call(
        body, name=name, out_shape=out_shape, grid=grid, scratch_shapes=scratch,
        input_output_aliases=aliases or {},
        compiler_params=pltpu.CompilerParams(vmem_limit_bytes=VMEM_LIMIT, dimension_semantics=semantics),
        **kw)
    if not hbm_inputs:
        return call
    return lambda *ins: call(*[pltpu.with_memory_space_constraint(a, pltpu.HBM) for a in ins])


def _sds(shape, dtype=F32):
    return jax.ShapeDtypeStruct(shape, dtype)


NN = (((1,), (0,)), ((), ()))
NT = (((1,), (1,)), ((), ()))
TN = (((0,), (0,)), ((), ()))


def _mm(name, a, b, *, out_shape, out_dtype, grid, a_spec, b_spec, o_spec, acc_shape, dims, add=None,
        add_spec=None, deps=()):
    nk = grid[2]
    deps = tuple(d for d in deps if d is not None)
    n_scratch = 1 if nk > 1 else 0

    def body(*refs):
        n_out = 1 + n_scratch
        refs = refs[:len(refs) - n_out - len(deps)] + refs[len(refs) - n_out:]
        a_ref, b_ref = refs[0], refs[1]
        r_ref = refs[2] if add is not None else None
        o_ref = refs[-n_out]
        prod = lax.dot_general(a_ref[...].astype(BF16), b_ref[...].astype(BF16), dims, preferred_element_type=F32)
        if nk == 1:
            o_ref[...] = (prod if r_ref is None else prod + r_ref[...]).astype(out_dtype)
            return
        acc_ref = refs[-1]
        k = pl.program_id(2)

        @pl.when(k == 0)
        def _():
            acc_ref[...] = prod

        @pl.when(k > 0)
        def _():
            acc_ref[...] += prod

        @pl.when(k == nk - 1)
        def _():
            res = acc_ref[...]
            if r_ref is not None:
                res = res + r_ref[...]
            o_ref[...] = res.astype(out_dtype)

    ins = ([a, b] if add is None else [a, b, add]) + list(deps)
    specs = ([a_spec, b_spec] if add is None else [a_spec, b_spec, add_spec]) + [ANY_SPEC] * len(deps)
    return _pcall(body, name=name, out_shape=_sds(out_shape, out_dtype), grid=grid, in_specs=specs,
                  out_specs=o_spec, scratch=[pltpu.VMEM(acc_shape, F32)] * n_scratch,
                  semantics=("parallel", "parallel", "arbitrary"))(*ins)


def _mm_x_wcols(name, a, w4, *, tm, tn, out_dtype=F32, deps=()):
    _, k, ns = w4.shape
    nj = ns // tn
    return _mm(name, a, w4, out_shape=(T, NCHIP * ns), out_dtype=out_dtype, grid=(T // tm, NCHIP * nj, 1), deps=deps,
               a_spec=pl.BlockSpec((tm, k), lambda i, j, kk: (i, 0)),
               b_spec=pl.BlockSpec((None, k, tn), lambda i, j, kk: (j // nj, 0, j % nj)),
               o_spec=pl.BlockSpec((tm, tn), lambda i, j, kk: (i, j)), acc_shape=(tm, tn), dims=NN)


def _mm_ff1(a, w4, *, tm, tn):
    _, k, ns = w4.shape
    nj = ns // tn

    def body(a_ref, b_ref, f_ref, r_ref):
        p = jnp.maximum(jnp.dot(a_ref[...], b_ref[...], preferred_element_type=F32), 0.0)
        f_ref[...] = p.astype(BF16)
        r_ref[...] = (p * p).astype(BF16)

    out = pl.BlockSpec((tm, tn), lambda i, j: (i, j))
    return _pcall(body, name="mm_f", out_shape=(_sds((T, DFF), BF16), _sds((T, DFF), BF16)), grid=(T // tm, NCHIP * nj),
                  in_specs=[pl.BlockSpec((tm, k), lambda i, j: (i, 0)),
                            pl.BlockSpec((None, k, tn), lambda i, j: (j // nj, 0, j % nj))],
                  out_specs=(out, out), semantics=("parallel", "parallel"))(a, w4)


def _mm_x_wrows(name, a, w4, add, *, tm, tk, tn, deps=()):
    _, ks, n = w4.shape
    nkk = ks // tk
    return _mm(name, a, w4, out_shape=(T, n), out_dtype=F32, grid=(T // tm, n // tn, NCHIP * nkk), deps=deps,
               a_spec=pl.BlockSpec((tm, tk), lambda i, j, kk: (i, kk)),
               b_spec=pl.BlockSpec((None, tk, tn), lambda i, j, kk: (kk // nkk, kk % nkk, j)),
               o_spec=pl.BlockSpec((tm, tn), lambda i, j, kk: (i, j)), acc_shape=(tm, tn), dims=NN,
               add=add, add_spec=pl.BlockSpec((tm, tn), lambda i, j, kk: (i, j)))


def _mm_g_wcols_t(name, g, w4, *, tm, tk, tn, out_dtype=F32, deps=()):
    _, k, ns = w4.shape
    nkk = ns // tk
    return _mm(name, g, w4, out_shape=(T, k), out_dtype=out_dtype, grid=(T // tm, k // tn, NCHIP * nkk), deps=deps,
               a_spec=pl.BlockSpec((tm, tk), lambda i, j, kk: (i, kk)),
               b_spec=pl.BlockSpec((None, tn, tk), lambda i, j, kk: (kk // nkk, j, kk % nkk)),
               o_spec=pl.BlockSpec((tm, tn), lambda i, j, kk: (i, j)), acc_shape=(tm, tn), dims=NT)


def _mm_g_wrows_t(name, g, w4, *, tm, tn, out_dtype=F32, deps=()):
    _, ks, n = w4.shape
    nj = ks // tn
    return _mm(name, g, w4, out_shape=(T, NCHIP * ks), out_dtype=out_dtype, grid=(T // tm, NCHIP * nj, 1), deps=deps,
               a_spec=pl.BlockSpec((tm, n), lambda i, j, kk: (i, 0)),
               b_spec=pl.BlockSpec((None, tn, n), lambda i, j, kk: (j // nj, j % nj, 0)),
               o_spec=pl.BlockSpec((tm, tn), lambda i, j, kk: (i, j)), acc_shape=(tm, tn), dims=NT)


def _mm_dff2(dx, w4, fa, *, tm, tn, deps=()):
    _, ks, n = w4.shape
    nj = ks // tn
    deps = tuple(d for d in deps if d is not None)

    def body(*refs):
        dx_ref, b_ref, f_ref = refs[:3]
        df_ref = refs[-1]
        dr = lax.dot_general(dx_ref[...].astype(BF16), b_ref[...], NT, preferred_element_type=F32)
        df_ref[...] = (dr * (2.0 * f_ref[...].astype(F32))).astype(BF16)

    out = pl.BlockSpec((tm, tn), lambda i, j: (i, j))
    return _pcall(body, name="mm_dr", out_shape=_sds((T, DFF), BF16), grid=(T // tm, NCHIP * nj),
                  in_specs=[pl.BlockSpec((tm, n), lambda i, j: (i, 0)),
                            pl.BlockSpec((None, tn, n), lambda i, j: (j // nj, j % nj, 0)), out]
                  + [ANY_SPEC] * len(deps),
                  out_specs=out, semantics=("parallel", "parallel"))(dx, w4, fa, *deps)


TCH = 512


def _mm_dw(name, a, g, *, out_shape, out_map, tm, tn, deps=()):
    deps = tuple(d for d in deps if d is not None)

    def body(*refs):
        a_ref, g_ref = refs[:2]
        o_ref, at_ref = refs[-2:]

        @pl.when(pl.program_id(1) == 0)
        def _():
            for c in range(T // TCH):
                at_ref[:, c * TCH:(c + 1) * TCH] = a_ref[c * TCH:(c + 1) * TCH, :].T

        o_ref[...] = jnp.dot(at_ref[...], g_ref[...].astype(BF16), preferred_element_type=F32)

    return _pcall(body, name=name, out_shape=_sds(out_shape), grid=(a.shape[1] // tm, g.shape[1] // tn),
                  in_specs=[pl.BlockSpec((T, tm), lambda i, j: (0, i)), pl.BlockSpec((T, tn), lambda i, j: (0, j))]
                  + [ANY_SPEC] * len(deps),
                  out_specs=pl.BlockSpec((None, tm, tn), out_map), scratch=[pltpu.VMEM((tm, T), BF16)],
                  semantics=("parallel", "arbitrary"))(a, g, *deps)


def _mm_dw_cols(name, a, g, *, ns, tm, tn, deps=()):
    nj = ns // tn
    return _mm_dw(name, a, g, out_shape=(NCHIP, a.shape[1], ns), out_map=lambda i, j: (j // nj, i, j % nj),
                  tm=tm, tn=tn, deps=deps)


def _mm_dw_rows(name, a, g, *, ks, tm, tn):
    ni = ks // tm
    return _mm_dw(name, a, g, out_shape=(NCHIP, ks, g.shape[1]), out_map=lambda i, j: (i // ni, i % ni, j),
                  tm=tm, tn=tn)


def _row_spec(width, col=0):
    return pl.BlockSpec((TB, width), lambda i: (i, col))


def _vec_spec(width):
    return pl.BlockSpec((1, width), lambda i: (0, 0))


def _rms_fwd(x, g):
    def body(x_ref, g_ref, h_ref):
        x = x_ref[...]
        r = lax.rsqrt(jnp.mean(x * x, axis=-1, keepdims=True) + EPS)
        h_ref[...] = (x * r * g_ref[...]).astype(BF16)

    return _pcall(body, name="rms_fwd", out_shape=_sds((T, D), BF16), grid=(T // TB,),
                  in_specs=[_row_spec(D), _vec_spec(D)], out_specs=_row_spec(D), semantics=("parallel",))(x, g)


def _rms_bwd(x, g, dh, dres, deps=()):
    deps = tuple(d for d in deps if d is not None)

    def body(*refs):
        x_ref, g_ref, dh_ref, dres_ref = refs[:4]
        dx_ref, dg_ref = refs[-2:]
        x = x_ref[...]
        r = lax.rsqrt(jnp.mean(x * x, axis=-1, keepdims=True) + EPS)
        y = x * r
        dh = dh_ref[...]
        dy = dh * g_ref[...]
        dx_ref[...] = dres_ref[...] + r * (dy - y * jnp.mean(dy * y, axis=-1, keepdims=True))

        @pl.when(pl.program_id(0) == 0)
        def _():
            dg_ref[...] = jnp.zeros_like(dg_ref)

        dg_ref[...] += jnp.sum(dh * y, axis=0, keepdims=True)

    return _pcall(body, name="rms_bwd", out_shape=(_sds((T, D)), _sds((1, D))), grid=(T // TB,),
                  in_specs=[_row_spec(D), _vec_spec(D), _row_spec(D), _row_spec(D)] + [ANY_SPEC] * len(deps),
                  out_specs=(_row_spec(D), _vec_spec(D)), semantics=("arbitrary",))(x, g, dh, dres, *deps)


def _sigmoid(x):
    return 1.0 / (1.0 + jnp.exp(-x))


def _gate_fwd(u, ycv, yat):
    def body(gc_ref, ga_ref, yc_ref, ya_ref, m_ref):
        m_ref[...] = (_sigmoid(gc_ref[...].astype(F32)) * yc_ref[...]
                      + _sigmoid(ga_ref[...].astype(F32)) * ya_ref[...]).astype(BF16)

    blk = lambda off: pl.BlockSpec((TB, 512), lambda i, j: (i, off + j))
    return _pcall(body, name="gate_fwd", out_shape=_sds((T, D), BF16), grid=(T // TB, 2),
                  in_specs=[blk(UB_GC), blk(UB_GA), blk(0), blk(0)], out_specs=blk(0),
                  semantics=("parallel", "parallel"))(u, u, ycv, yat)


def _gate_bwd(u, ycv, yat, dm):
    def body(gc_ref, ga_ref, yc_ref, ya_ref, dm_ref, dyc_ref, dya_ref, dgc_ref, dga_ref):
        dm = dm_ref[...]
        sc = _sigmoid(gc_ref[...].astype(F32))
        sa = _sigmoid(ga_ref[...].astype(F32))
        dyc_ref[...] = (dm * sc).astype(BF16)
        dya_ref[...] = (dm * sa).astype(BF16)
        dgc_ref[...] = (dm * yc_ref[...] * sc * (1.0 - sc)).astype(BF16)
        dga_ref[...] = (dm * ya_ref[...] * sa * (1.0 - sa)).astype(BF16)

    blk = lambda off: pl.BlockSpec((TB, 512), lambda i, j: (i, off + j))
    return _pcall(body, name="gate_bwd",
                  out_shape=(_sds((T, D), BF16), _sds((T, D), BF16), _sds((T, D), BF16), _sds((T, D), BF16)),
                  grid=(T // TB, 2), in_specs=[blk(UB_GC), blk(UB_GA), blk(0), blk(0), blk(0)],
                  out_specs=(blk(0), blk(0), blk(0), blk(0)),
                  semantics=("parallel", "parallel"))(u, u, ycv, yat, dm)


def _relu2_fwd(f):
    def body(f_ref, r_ref):
        a = jnp.maximum(f_ref[...], 0.0)
        r_ref[...] = (a * a).astype(BF16)

    blk = pl.BlockSpec((TB, 1024), lambda i, j: (i, j))
    return _pcall(body, name="relu2_fwd", out_shape=_sds((T, DFF), BF16), grid=(T // TB, DFF // 1024),
                  in_specs=[blk], out_specs=blk, semantics=("parallel", "parallel"))(f)


def _relu2_bwd(f, dr):
    def body(f_ref, dr_ref, r_ref, df_ref):
        a = jnp.maximum(f_ref[...], 0.0)
        r_ref[...] = (a * a).astype(BF16)
        df_ref[...] = (dr_ref[...] * (2.0 * a)).astype(BF16)

    blk = pl.BlockSpec((TB, 1024), lambda i, j: (i, j))
    return _pcall(body, name="relu2_bwd", out_shape=(_sds((T, DFF), BF16), _sds((T, DFF), BF16)),
                  grid=(T // TB, DFF // 1024), in_specs=[blk, blk], out_specs=(blk, blk),
                  semantics=("parallel", "parallel"))(f, dr)


def _loss_fwd_bwd(y, target):
    def body(y_ref, t_ref, loss_ref, dy_ref):
        e = y_ref[...] - t_ref[...]
        dy_ref[...] = e * (1.0 / D)

        @pl.when(pl.program_id(0) == 0)
        def _():
            loss_ref[...] = jnp.zeros_like(loss_ref)

        loss_ref[...] += 0.5 * jnp.sum(jnp.mean(e * e, axis=-1, keepdims=True))

    return _pcall(body, name="loss", out_shape=(_sds((8, 128)), _sds((T, D))), grid=(T // TB,),
                  in_specs=[_row_spec(D), _row_spec(D)],
                  out_specs=(pl.BlockSpec((8, 128), lambda i: (0, 0)), _row_spec(D)),
                  semantics=("arbitrary",))(y, target)


PAD = 32
CCH = 256


def _conv_fwd(u, dw_w, dw_b):
    def body(a_ref, gt_ref, w_ref, b_ref, z1_ref, zp_ref):
        zp_ref[0:PAD, :] = jnp.zeros((PAD, 128), F32)
        zp_ref[PAD:PAD + T, :] = a_ref[...].astype(F32) * _sigmoid(gt_ref[...].astype(F32))
        for c in range(T // CCH):
            acc = jnp.broadcast_to(b_ref[...], (CCH, 128))
            for j in range(KW):
                acc = acc + w_ref[j:j + 1, :] * zp_ref[pl.ds(c * CCH + j + PAD - (KW - 1), CCH), :]
            z1_ref[c * CCH:(c + 1) * CCH, :] = acc

    col = lambda off: pl.BlockSpec((T, 128), lambda j: (0, off * 4 + j))
    return _pcall(body, name="conv_fwd", out_shape=_sds((T, CONV)), grid=(CONV // 128,),
                  in_specs=[col(UB_A), col(UB_GT), pl.BlockSpec((KW, 128), lambda j: (0, j)),
                            pl.BlockSpec((1, 128), lambda j: (0, j))],
                  out_specs=col(0), scratch=[pltpu.VMEM((T + PAD, 128), F32)],
                  semantics=("parallel",))(u, u, dw_w, dw_b)


def _ln_silu_fwd(z1, g, b):
    def body(z_ref, g_ref, b_ref, o_ref):
        z = z_ref[...]
        mu = jnp.mean(z, axis=-1, keepdims=True)
        zc = z - mu
        zh = zc * lax.rsqrt(jnp.mean(zc * zc, axis=-1, keepdims=True) + EPS)
        z2 = zh * g_ref[...] + b_ref[...]
        o_ref[...] = (z2 * _sigmoid(z2)).astype(BF16)

    return _pcall(body, name="ln_silu_fwd", out_shape=_sds((T, CONV), BF16), grid=(T // TB,),
                  in_specs=[_row_spec(CONV), _vec_spec(CONV), _vec_spec(CONV)], out_specs=_row_spec(CONV),
                  semantics=("parallel",))(z1, g, b)


def _ln_silu_bwd(z1, g, b, dz3):
    def body(z_ref, g_ref, b_ref, d_ref, z3_ref, dz1_ref, dg_ref, db_ref):
        z = z_ref[...]
        mu = jnp.mean(z, axis=-1, keepdims=True)
        zc = z - mu
        rs = lax.rsqrt(jnp.mean(zc * zc, axis=-1, keepdims=True) + EPS)
        zh = zc * rs
        z2 = zh * g_ref[...] + b_ref[...]
        s = _sigmoid(z2)
        z3_ref[...] = (z2 * s).astype(BF16)
        dz2 = d_ref[...] * (s * (1.0 + z2 * (1.0 - s)))
        dzh = dz2 * g_ref[...]
        dz1_ref[...] = rs * (dzh - jnp.mean(dzh, axis=-1, keepdims=True)
                             - zh * jnp.mean(dzh * zh, axis=-1, keepdims=True))

        @pl.when(pl.program_id(0) == 0)
        def _():
            dg_ref[...] = jnp.zeros_like(dg_ref)
            db_ref[...] = jnp.zeros_like(db_ref)

        dg_ref[...] += jnp.sum(dz2 * zh, axis=0, keepdims=True)
        db_ref[...] += jnp.sum(dz2, axis=0, keepdims=True)

    return _pcall(body, name="ln_silu_bwd",
                  out_shape=(_sds((T, CONV), BF16), _sds((T, CONV)), _sds((1, CONV)), _sds((1, CONV))),
                  grid=(T // TB,),
                  in_specs=[_row_spec(CONV), _vec_spec(CONV), _vec_spec(CONV), _row_spec(CONV)],
                  out_specs=(_row_spec(CONV), _row_spec(CONV), _vec_spec(CONV), _vec_spec(CONV)),
                  semantics=("arbitrary",))(z1, g, b, dz3)


def _conv_bwd(u, dw_w, dz1):
    def body(a_ref, gt_ref, w_ref, dz1_ref, da_ref, dgt_ref, dw_ref, db_ref, zp_ref, dp_ref):
        a = a_ref[...].astype(F32)
        s = _sigmoid(gt_ref[...].astype(F32))
        zp_ref[0:PAD, :] = jnp.zeros((PAD, 128), F32)
        zp_ref[PAD:PAD + T, :] = a * s
        dp_ref[0:T, :] = dz1_ref[...]
        dp_ref[T:T + PAD, :] = jnp.zeros((PAD, 128), F32)
        db_ref[...] = jnp.sum(dz1_ref[...], axis=0, keepdims=True)
        for j in range(KW):
            tot = jnp.zeros((1, 128), F32)
            for c in range(T // CCH):
                tot = tot + jnp.sum(dz1_ref[c * CCH:(c + 1) * CCH, :]
                                    * zp_ref[pl.ds(c * CCH + j + PAD - (KW - 1), CCH), :], axis=0, keepdims=True)
            dw_ref[j:j + 1, :] = tot
        for c in range(T // CCH):
            acc = jnp.zeros((CCH, 128), F32)
            for j in range(KW):
                acc = acc + w_ref[j:j + 1, :] * dp_ref[pl.ds(c * CCH + (KW - 1) - j, CCH), :]
            rows = slice(c * CCH, (c + 1) * CCH)
            sc = _sigmoid(gt_ref[rows, :].astype(F32))
            da_ref[rows, :] = (acc * sc).astype(BF16)
            dgt_ref[rows, :] = (acc * a_ref[rows, :].astype(F32) * sc * (1.0 - sc)).astype(BF16)

    col = lambda off: pl.BlockSpec((T, 128), lambda j: (0, off * 4 + j))
    wspec = pl.BlockSpec((KW, 128), lambda j: (0, j))
    return _pcall(body, name="conv_bwd",
                  out_shape=(_sds((T, CONV), BF16), _sds((T, CONV), BF16), _sds((KW, CONV)), _sds((1, CONV))),
                  grid=(CONV // 128,), in_specs=[col(UB_A), col(UB_GT), wspec, col(0)],
                  out_specs=(col(0), col(0), wspec, pl.BlockSpec((1, 128), lambda j: (0, j))),
                  scratch=[pltpu.VMEM((T + PAD, 128), F32), pltpu.VMEM((T + PAD, 128), F32)],
                  semantics=("parallel",))(u, u, dw_w, dz1)


def _bucket_tables():
    qi = np.arange(BLK)[:, None]
    kj = np.arange(2 * BLK)[None, :]
    off = np.clip(qi + BLK - kj, 0, BLK)
    out = []
    for d in DIL:
        dist = (off * d).astype(np.int32)
        nf = np.maximum(dist, 1).astype(np.float32)
        large = 16 + (np.log(nf / np.float32(16)) / np.float32(math.log(2048 / 16)) * np.float32(16)).astype(np.int32)
        large = np.minimum(large, NBUCKET - 1)
        out.append(np.where(dist < 16, dist, large))
    return np.stack(out).astype(np.int32)


def _band():
    off = lax.broadcasted_iota(jnp.int32, (BLK, 2 * BLK), 0) + BLK - lax.broadcasted_iota(jnp.int32, (BLK, 2 * BLK), 1)
    return (off >= 0) & (off <= BLK)


def _bias_table(rel_bias_t, buckets):
    def body(rb_ref, bk_ref, o_ref):
        h = pl.program_id(0)
        bk = bk_ref[...]
        acc = jnp.zeros((BLK, 2 * BLK), F32)
        for b in range(NBUCKET):
            acc = jnp.where(bk == b, rb_ref[h, b], acc)
        o_ref[...] = jnp.where(_band(), acc, NEG)

    return _pcall(body, name="bias_table", out_shape=_sds((3 * 8, BLK, 2 * BLK)), grid=(24,),
                  in_specs=[pl.BlockSpec(memory_space=pltpu.SMEM),
                            pl.BlockSpec((None, BLK, 2 * BLK), lambda h: (h // 8, 0, 0))],
                  out_specs=pl.BlockSpec((None, BLK, 2 * BLK), lambda h: (h, 0, 0)),
                  semantics=("parallel",), hbm_inputs=False)(rel_bias_t, buckets)


def _bias_grad(ds_acc, buckets):
    def body(a_ref, bk_ref, o_ref):
        acc = a_ref[0]
        for l in range(1, DEPTH):
            acc = acc + a_ref[l]
        bk = bk_ref[...]
        lane = lax.broadcasted_iota(jnp.int32, (1, 128), 1)
        row = jnp.zeros((1, 128), F32)
        for b in range(NBUCKET):
            row = jnp.where(lane == b, jnp.sum(jnp.where(bk == b, acc, 0.0)), row)
        o_ref[...] = row

    return _pcall(body, name="bias_grad", out_shape=_sds((24, 1, 128)), grid=(24,),
                  in_specs=[pl.BlockSpec((DEPTH, None, BLK, 2 * BLK), lambda h: (0, h, 0, 0)),
                            pl.BlockSpec((None, BLK, 2 * BLK), lambda h: (h // 8, 0, 0))],
                  out_specs=pl.BlockSpec((None, 1, 128), lambda h: (h, 0, 0)),
                  semantics=("parallel",))(ds_acc, buckets)


def _head_mask():
    return lax.broadcasted_iota(jnp.int32, (1, 128), 1) < HD


def _seg_ones(width):
    r = lax.broadcasted_iota(jnp.int32, (width, width), 0) >> 6
    c = lax.broadcasted_iota(jnp.int32, (width, width), 1) >> 6
    return (r == c).astype(BF16)


def _seg_sum(x, ones):
    hi = x.astype(BF16)
    lo = (x - hi.astype(F32)).astype(BF16)
    return (jnp.dot(hi, ones, preferred_element_type=F32) + jnp.dot(lo, ones, preferred_element_type=F32))


def _dot(a, b, dims):
    return lax.dot_general(a, b, dims, preferred_element_type=F32)


def _tile_rows(d, r, n):
    stride = None if d == 1 else d
    q_rows = pl.ds(r + d * n * BLK, BLK, stride=stride)
    if n == 0:
        return q_rows, q_rows, BLK
    return q_rows, pl.ds(r + d * (n - 1) * BLK, 2 * BLK, stride=stride), 2 * BLK


def _stack_heads(x, m_a):
    return jnp.concatenate([jnp.where(m_a, x, 0.0), jnp.where(m_a, 0.0, x)], axis=0)


def _stack_rows(x, m_a, width):
    other = pltpu.roll(x, HD, axis=1)
    both = jnp.concatenate([jnp.where(m_a, x, other), jnp.where(m_a, other, x)], axis=0)
    return both if width == 128 else jnp.concatenate([both] * (width // 128), axis=1)


NCH = 256


def _qk_norm_prep(q_ref, k_ref, v_ref, gq_ref, gk_ref, qn_ref, kn_ref, vn_ref, ones):
    def prep(i, carry):
        rows = pl.ds(pl.multiple_of(i * NCH, NCH), NCH)
        q = q_ref[rows, :].astype(F32)
        qn_ref[rows, :] = q * lax.rsqrt(_seg_sum(q * q, ones) * (1.0 / HD) + EPS) * gq_ref[...] * (HD ** -0.5)
        k = k_ref[rows, :].astype(F32)
        kn_ref[rows, :] = k * lax.rsqrt(_seg_sum(k * k, ones) * (1.0 / HD) + EPS) * gk_ref[...]
        vn_ref[rows, :] = v_ref[rows, :].astype(F32)
        return carry

    lax.fori_loop(0, T // NCH, prep, 0)


def _attn_specs(g):
    ucol = lambda base: pl.BlockSpec((T, 128), lambda hp: (0, (base + g) * 4 + hp))
    col = pl.BlockSpec((T, 128), lambda hp: (0, hp))
    vec = pl.BlockSpec((1, 128), lambda hp: (0, 0))
    bm = pl.BlockSpec((2, BLK, 2 * BLK), lambda hp: (g * 4 + hp, 0, 0))
    return ucol, col, vec, bm


def _attn_fwd(g, u, gq, gk, bm, deps=()):
    d = DIL[g]

    def body(*refs):
        q_ref, k_ref, v_ref, gq_ref, gk_ref, bm_ref = refs[:6]
        o_ref, lse_ref, qn_ref, kn_ref, vn_ref = refs[-5:]
        ones = _seg_ones(128)
        _qk_norm_prep(q_ref, k_ref, v_ref, gq_ref, gk_ref, qn_ref, kn_ref, vn_ref, ones)
        m_a = _head_mask()
        for r in range(d):
            for n in range(T // d // BLK):
                q_rows, k_rows, nk = _tile_rows(d, r, n)
                qt = qn_ref[q_rows, :]
                kt = kn_ref[k_rows, :].astype(BF16)
                vt = vn_ref[k_rows, :].astype(BF16)
                q2 = _stack_heads(qt, m_a).astype(BF16)
                s = _dot(q2, kt, NT) + bm_ref[...].reshape(2 * BLK, 2 * BLK)[:, 2 * BLK - nk:]
                mx = jnp.max(s, axis=1, keepdims=True)
                p = jnp.exp(s - mx)
                l = jnp.sum(p, axis=1, keepdims=True)
                o2 = _dot(p.astype(BF16), vt, NN) / l
                lse2 = jnp.broadcast_to(mx + jnp.log(l), (2 * BLK, 128))
                o_ref[q_rows, :] = jnp.where(m_a, o2[:BLK], o2[BLK:])
                lse_ref[q_rows, :] = jnp.where(m_a, lse2[:BLK], lse2[BLK:])

    ucol, col, vec, bmspec = _attn_specs(g)
    return _pcall(body, name=f"attn_fwd_g{g}", out_shape=(_sds((T, AOUT)), _sds((T, AOUT))), grid=(4,),
                  in_specs=[ucol(UB_Q), ucol(UB_K), ucol(UB_V), vec, vec, bmspec] + [ANY_SPEC] * len(deps),
                  out_specs=(col, col), scratch=[pltpu.VMEM((T, 128), F32)] * 3,
                  semantics=("parallel",))(u, u, u, gq, gk, bm, *deps)


def _attn_bwd(g, u, gq, gk, bm, dog, cb, lse):
    d = DIL[g]

    def body(q_ref, k_ref, v_ref, gq_ref, gk_ref, bm_ref, do_ref, cb_ref, lse_ref,
             dqo_ref, dko_ref, dvo_ref, dgq_ref, dgk_ref, dsa_ref, qn_ref, kn_ref, vn_ref, dq_ref, dk_ref, dv_ref):
        ones = _seg_ones(128)
        _qk_norm_prep(q_ref, k_ref, v_ref, gq_ref, gk_ref, qn_ref, kn_ref, vn_ref, ones)
        m_a = _head_mask()
        dk_ref[...] = jnp.zeros_like(dk_ref)
        dv_ref[...] = jnp.zeros_like(dv_ref)
        dsa_ref[...] = jnp.zeros_like(dsa_ref)
        for r in range(d):
            for n in range(T // d // BLK):
                q_rows, k_rows, nk = _tile_rows(d, r, n)
                ktb = kn_ref[k_rows, :].astype(BF16)
                vtb = vn_ref[k_rows, :].astype(BF16)
                q2 = _stack_heads(qn_ref[q_rows, :], m_a).astype(BF16)
                do2 = _stack_heads(do_ref[q_rows, :], m_a).astype(BF16)
                lse_c = _stack_rows(lse_ref[q_rows, :], m_a, nk)
                c_c = _stack_rows(cb_ref[q_rows, :], m_a, nk)
                s = _dot(q2, ktb, NT) + bm_ref[...].reshape(2 * BLK, 2 * BLK)[:, 2 * BLK - nk:]
                p = jnp.exp(s - lse_c)
                ds = p * (_dot(do2, vtb, NT) + c_c)
                dsb = ds.astype(BF16)
                dq2 = _dot(dsb, ktb, NN)
                dq_ref[q_rows, :] = jnp.where(m_a, dq2[:BLK], dq2[BLK:])
                dk_ref[k_rows, :] += _dot(dsb, q2, TN)
                dv_ref[k_rows, :] += _dot(p.astype(BF16), do2, TN)
                dsa_ref[:, :, 2 * BLK - nk:] += ds.reshape(2, BLK, nk)

        @pl.when(pl.program_id(0) == 0)
        def _():
            dgq_ref[...] = jnp.zeros_like(dgq_ref)
            dgk_ref[...] = jnp.zeros_like(dgk_ref)

        def norm_bwd(i, carry):
            rows = pl.ds(pl.multiple_of(i * NCH, NCH), NCH)
            for x_ref, g_ref, dx_ref, dxo_ref, dg_ref, scale in (
                    (q_ref, gq_ref, dq_ref, dqo_ref, dgq_ref, HD ** -0.5), (k_ref, gk_ref, dk_ref, dko_ref, dgk_ref, 1.0)):
                x = x_ref[rows, :].astype(F32)
                rs = lax.rsqrt(_seg_sum(x * x, ones) * (1.0 / HD) + EPS)
                xh = x * rs
                dn = dx_ref[rows, :] * scale
                dxh = dn * g_ref[...]
                dxo_ref[rows, :] = (rs * (dxh - xh * (_seg_sum(dxh * xh, ones) * (1.0 / HD)))).astype(BF16)
                dg_ref[...] += jnp.sum(dn * xh, axis=0, keepdims=True)
            dvo_ref[rows, :] = dv_ref[rows, :].astype(BF16)
            return carry

        lax.fori_loop(0, T // NCH, norm_bwd, 0)

    ucol, col, vec, bmspec = _attn_specs(g)
    return _pcall(body, name=f"attn_bwd_g{g}",
                  out_shape=(_sds((T, AOUT), BF16), _sds((T, AOUT), BF16), _sds((T, AOUT), BF16), _sds((1, 128)),
                             _sds((1, 128)), _sds((8, BLK, 2 * BLK))),
                  grid=(4,),
                  in_specs=[ucol(UB_Q), ucol(UB_K), ucol(UB_V), vec, vec, bmspec, col, col, col],
                  out_specs=(col, col, col, vec, vec, pl.BlockSpec((2, BLK, 2 * BLK), lambda hp: (hp, 0, 0))),
                  scratch=[pltpu.VMEM((T, 128), F32)] * 6,
                  semantics=("arbitrary",))(u, u, u, gq, gk, bm, dog, cb, lse)


def _combine_fwd(ogs, lses):
    def body(o0, o1, o2, l0, l1, l2, o_ref):
        ls = [l0[...], l1[...], l2[...]]
        mx = jnp.maximum(jnp.maximum(ls[0], ls[1]), ls[2])
        es = [jnp.exp(l - mx) for l in ls]
        inv = 1.0 / (es[0] + es[1] + es[2])
        o_ref[...] = ((es[0] * o0[...] + es[1] * o1[...] + es[2] * o2[...]) * inv).astype(BF16)

    return _pcall(body, name="combine_fwd", out_shape=_sds((T, AOUT), BF16), grid=(T // TB,),
                  in_specs=[_row_spec(AOUT)] * 6, out_specs=_row_spec(AOUT), semantics=("parallel",))(*ogs, *lses)


def _combine_bwd(ogs, lses, do):
    def body(o0, o1, o2, l0, l1, l2, do_ref, d0, d1, d2, c0, c1, c2):
        ls = [l0[...], l1[...], l2[...]]
        mx = jnp.maximum(jnp.maximum(ls[0], ls[1]), ls[2])
        es = [jnp.exp(l - mx) for l in ls]
        inv = 1.0 / (es[0] + es[1] + es[2])
        ws = [e * inv for e in es]
        do = do_ref[...]
        o = ws[0] * o0[...] + ws[1] * o1[...] + ws[2] * o2[...]
        s = _seg_sum(do * o, _seg_ones(AOUT))
        for w, d_ref, c_ref in zip(ws, (d0, d1, d2), (c0, c1, c2)):
            d_ref[...] = w * do
            c_ref[...] = -(w * s)

    return _pcall(body, name="combine_bwd", out_shape=tuple(_sds((T, AOUT)) for _ in range(6)), grid=(T // TB,),
                  in_specs=[_row_spec(AOUT)] * 7, out_specs=tuple(_row_spec(AOUT) for _ in range(6)),
                  semantics=("parallel",))(*ogs, *lses, do)


def _layer_fwd(x, p, bm, deps=(), mid=None):
    h1 = _rms_fwd(x, p["n1g"])
    u = _mm_x_wcols("mm_u", h1, p["win4"], tm=T, tn=640, out_dtype=BF16, deps=deps)
    ogs, lses = [], []
    for g in range(NG):
        gdeps = (p["hook"](ogs[-1]),) if g == NG - 1 and "hook" in p else ()
        og, lse = _attn_fwd(g, u, p["gq"], p["gk"], bm, deps=gdeps)
        ogs.append(og)
        lses.append(lse)
    o = _combine_fwd(ogs, lses)
    z1 = _conv_fwd(u, p["dww"], p["dwb"])
    z3 = _ln_silu_fwd(z1, p["lng"], p["lnb"])
    if "rest" in p:
        p = {**p, **p["rest"](z3)}
    ycv = _mm_x_wcols("mm_ycv", z3, p["wco4"], tm=T, tn=256)
    yat = _mm_x_wcols("mm_yat", o, p["wao4"], tm=T, tn=256)
    m = _gate_fwd(u, ycv, yat)
    xm = _mm_x_wrows("mm_xmid", m, p["wout4"], x, tm=1024, tk=256, tn=1024)
    h2 = _rms_fwd(xm, p["n2g"])
    fa, r = _mm_ff1(h2, p["wff14"], tm=T, tn=512)
    tok = mid(r) if mid else None
    xo = _mm_x_wrows("mm_xout", r, p["wff24"], xm, tm=1024, tk=1024, tn=1024, deps=(tok,))
    saved = dict(x=x, h1=h1, u=u, z1=z1, ogs=ogs, lses=lses, o=o, ycv=ycv, yat=yat, m=m, xm=xm, h2=h2, fa=fa, r=r)
    return xo, p, saved


EARLY = ("w_ff2", "w_ff1", "w_out")
LATE = ("w_conv_out", "w_attn_out", "w_in")


def _layer_bwd(dx, s, p, bm, pipe=None, own_early=None, own_late=None):
    u = s["u"]
    tok = pipe.step0() if pipe else None
    df = _mm_dff2(dx, p["wff24"], s["fa"], tm=1024, tn=512, deps=(tok,))
    g_ff2 = _mm_dw_rows("mm_dwff2", s["r"], dx, ks=1024, tm=1024, tn=512)
    g_ff1 = _mm_dw_cols("mm_dwff1", s["h2"], df, ns=1024, tm=1024, tn=512)
    tok = pipe.step1(g_ff1) if pipe else None
    dh2 = _mm_g_wcols_t("mm_dh2", df, p["wff14"], tm=1024, tk=1024, tn=1024, deps=(tok,))
    dxm, d_n2g = _rms_bwd(s["xm"], p["n2g"], dh2, dx)

    dm = _mm_g_wrows_t("mm_dm", dxm, p["wout4"], tm=1024, tn=256)
    g_out = _mm_dw_rows("mm_dwout", s["m"], dxm, ks=256, tm=256, tn=512)
    early = own_early(dict(w_ff2=g_ff2, w_ff1=g_ff1, w_out=g_out)) if own_early else None
    tok_e = early.step0() if early else None
    dyc, dya, dgc, dga = _gate_bwd(u, s["ycv"], s["yat"], dm)

    dz3 = _mm_g_wcols_t("mm_dz3", dyc, p["wco4"], tm=T, tk=256, tn=512, deps=(tok_e,))
    z3, dz1, d_lng, d_lnb = _ln_silu_bwd(s["z1"], p["lng"], p["lnb"], dz3)
    g_co = _mm_dw_cols("mm_dwco", z3, dyc, ns=256, tm=512, tn=256)
    da, dgt, d_dww, d_dwb = _conv_bwd(u, p["dww"], dz1)

    tok_e = early.step1(da) if early else None
    do = _mm_g_wcols_t("mm_do", dya, p["wao4"], tm=T, tk=256, tn=512, deps=(tok_e,))
    g_ao = _mm_dw_cols("mm_dwao", s["o"], dya, ns=256, tm=512, tn=256)
    parts = _combine_bwd(s["ogs"], s["lses"], do)
    dqs, dks, dvs, d_gq, d_gk, dsas = [], [], [], [], [], []
    for g in range(NG):
        dq, dk, dv, dgq, dgk, dsa = _attn_bwd(g, u, p["gq"], p["gk"], bm, parts[g], parts[NG + g], s["lses"][g])
        dqs.append(dq)
        dks.append(dk)
        dvs.append(dv)
        d_gq.append(dgq)
        d_gk.append(dgk)
        dsas.append(dsa)
    du = jnp.concatenate([da, dgt] + dqs + dks + dvs + [dgc, dga], axis=1)
    tok = pipe.step2(du) if pipe else None
    tok_e = early.step2(du) if early else None
    g_in = _mm_dw_cols("mm_dwin", s["h1"], du, ns=1920, tm=1024, tn=640, deps=(tok, tok_e))
    late = own_late(dict(w_in=g_in, w_conv_out=g_co, w_attn_out=g_ao)) if own_late else None
    tok_l = late.step0() if late else None
    dh1 = _mm_g_wcols_t("mm_dh1", du, p["win4"], tm=1024, tk=1920, tn=1024, deps=(tok_l,))
    tok_l = late.step1(dh1) if late else None
    dxi, d_n1g = _rms_bwd(s["x"], p["n1g"], dh1, dxm, deps=(tok_l,))
    if pipe:
        pipe.step3(dxi)

    fold = lambda parts_: sum(v[0, :HD] + v[0, HD:] for v in parts_)
    big = dict(w_in=g_in, w_conv_out=g_co, w_attn_out=g_ao, w_out=g_out, w_ff1=g_ff1, w_ff2=g_ff2)
    small = dict(norm1_g=d_n1g[0], q_norm_g=fold(d_gq), k_norm_g=fold(d_gk), conv_dw_w=d_dww, conv_dw_b=d_dwb[0],
                 conv_ln_g=d_lng[0], conv_ln_b=d_lnb[0], norm2_g=d_n2g[0])
    return dxi, big, small, jnp.concatenate(dsas, axis=0), early, late


def _local_step(x, target, get_layer, rel_bias, make_pipe):
    buckets = jnp.asarray(_bucket_tables())
    bm = _bias_table(rel_bias.T, buckets)
    saved, layers = [], []
    for l in range(DEPTH):
        p, deps, mid = get_layer(l, x)
        x, p, s = _layer_fwd(x, p, bm, deps=deps, mid=mid)
        layers.append(p)
        saved.append(s)
    loss_blk, dx = _loss_fwd_bwd(x, target)
    smalls, dsas = [None] * DEPTH, [None] * DEPTH
    pipe, pipes = None, []
    for l in reversed(range(DEPTH)):
        if l > 0:
            dx, big, smalls[l], dsas[l], _, _ = _layer_bwd(dx, saved[l], layers[l], bm, pipe)
            pipe = make_pipe(l, BIG, "", big)
            pipes.append(pipe)
        else:
            dx, big, smalls[l], dsas[l], early, late = _layer_bwd(
                dx, saved[l], layers[l], bm, pipe, lambda big_: make_pipe(0, EARLY, "e", big_),
                lambda big_: make_pipe(0, LATE, "", big_))
    d_rel = _bias_grad(jnp.stack(dsas), buckets)[:, 0, :NBUCKET].T
    return loss_blk[0, 0], dx, smalls, d_rel, pipes, early, late


MESH = pl.DeviceIdType.MESH


def _me():
    return lax.axis_index("x"), lax.axis_index("y"), lax.axis_index("c")


def _other_chips(mx, my):
    return [(1 - mx, my), (mx, 1 - my), (1 - mx, 1 - my)]


def _rcopy(src, dst, send_sems, recv_sems, k, dev):
    return pltpu.make_async_remote_copy(src_ref=src, dst_ref=dst, send_sem=send_sems.at[k], recv_sem=recv_sems.at[k],
                                        device_id=dev, device_id_type=MESH)


def _comm_call(body, name, out_shape, n_in, n_sems):
    return pl.pallas_call(
        body, name=name, out_shape=out_shape, in_specs=[HBM_SPEC] * n_in,
        out_specs=jax.tree.map(lambda _: HBM_SPEC, out_shape),
        scratch_shapes=[pltpu.SemaphoreType.DMA((n_sems,)), pltpu.SemaphoreType.DMA((n_sems,)),
                        pltpu.SemaphoreType.DMA(())],
        compiler_params=pltpu.CompilerParams(has_side_effects=True))


def _all_gather_chips(x, name):
    def body(x_ref, o_ref, send_sems, recv_sems, local_sem):
        mx, my, mc = _me()
        local = pltpu.make_async_copy(x_ref, o_ref.at[2 * mx + my], local_sem)
        local.start()
        sends = [_rcopy(x_ref, o_ref.at[2 * mx + my], send_sems, recv_sems, k, (px, py, mc))
                 for k, (px, py) in enumerate(_other_chips(mx, my))]
        for cp in sends:
            cp.start()
        for k, (px, py) in enumerate(_other_chips(mx, my)):
            _rcopy(x_ref, o_ref.at[2 * px + py], send_sems, recv_sems, k, (px, py, mc)).wait_recv()
        for cp in sends:
            cp.wait_send()
        local.wait()

    return _comm_call(body, name, _sds((NCHIP,) + x.shape, x.dtype), 1, 3)(x)


EFFECT = pltpu.SideEffectType.DATAFLOW_SIDE_EFFECTING


def _hbm(a):
    return pltpu.with_memory_space_constraint(a, pltpu.HBM)


def _split_start(name, bufs, plan, n, after=None):
    nb = len(bufs)
    extra = [] if after is None else [after]
    ne = len(extra)

    def body(*refs):
        send_sems, recv_sems, token = refs[nb + ne], refs[nb + ne + 1], refs[-1]
        mx, my, mc = _me()
        for k, (src, dst, dev, _) in enumerate(plan(refs[:nb], mx, my, mc)):
            _rcopy(src, dst, send_sems, recv_sems, k, dev).start()
        token[...] = jnp.zeros_like(token)

    out = pl.pallas_call(
        body, name=name,
        out_shape=(pltpu.SemaphoreType.DMA((n,)), pltpu.SemaphoreType.DMA((n,)),
                   *[pltpu.HBM(b.shape, b.dtype) for b in bufs], _sds((8, 128))),
        in_specs=[HBM_SPEC] * nb + [ANY_SPEC] * ne,
        out_specs=(SEM_SPEC, SEM_SPEC, *[HBM_SPEC] * nb, pl.BlockSpec(memory_space=pltpu.VMEM)),
        input_output_aliases={i: 2 + i for i in range(nb)},
        compiler_params=pltpu.CompilerParams(has_side_effects=EFFECT))(*[_hbm(b) for b in bufs], *extra)
    return (out[0], out[1]), list(out[2:2 + nb]), out[-1]


def _split_wait(name, sems, bufs, plan, after):
    nb = len(bufs)

    def body(*refs):
        send_sems, recv_sems = refs[nb], refs[nb + 1]
        mx, my, mc = _me()
        for k, (src, dst, dev, land) in enumerate(plan(refs[:nb], mx, my, mc)):
            _rcopy(src, dst, send_sems, recv_sems, k, dev).wait_send()
            _rcopy(src, land, send_sems, recv_sems, k, dev).wait_recv()

    out = pl.pallas_call(
        body, name=name, out_shape=tuple(pltpu.HBM(b.shape, b.dtype) for b in bufs),
        in_specs=[HBM_SPEC] * nb + [SEM_SPEC, SEM_SPEC, ANY_SPEC], out_specs=(HBM_SPEC,) * nb,
        input_output_aliases={i: i for i in range(nb)},
        compiler_params=pltpu.CompilerParams(has_side_effects=EFFECT))(*bufs, sems[0], sems[1], after)
    return list(out)


def _plan_gather_chips(refs, mx, my, mc):
    me = 2 * mx + my
    return [(r.at[me, mc], r.at[me, mc], (px, py, mc), r.at[2 * px + py, mc])
            for r in refs for px, py in _other_chips(mx, my)]


def _plan_gather_pair(refs, mx, my, mc):
    return [(r.at[2 * px + py, mc], r.at[2 * px + py, mc], (mx, my, 1 - mc), r.at[2 * px + py, 1 - mc])
            for r in refs for px, py in _other_chips(mx, my)]


def _plan_gather_devices(refs, mx, my, mc):
    flip = lambda m, b: 1 - m if b else m
    peers = [(flip(mx, k >> 2 & 1), flip(my, k >> 1 & 1), flip(mc, k & 1)) for k in range(1, 8)]
    slot = lambda dev: 4 * dev[0] + 2 * dev[1] + dev[2]
    me = slot((mx, my, mc))
    return [(r.at[me], r.at[me], dev, r.at[slot(dev)]) for r in refs for dev in peers]


def _plan_pair_half(refs, mx, my, mc):
    n = len(refs) // 2
    return [(g.at[:, 1 - mc], r, (mx, my, 1 - mc), r) for g, r in zip(refs[:n], refs[n:])]


def _plan_scatter(refs, mx, my, mc):
    n = len(refs) // 2
    return [(q.at[2 * px + py], r.at[k], (px, py, mc), r.at[k])
            for q, r in zip(refs[:n], refs[n:]) for k, (px, py) in enumerate(_other_chips(mx, my))]


def _plan_pair_fill(refs, mx, my, mc):
    return [(r.at[mc], r.at[mc], (mx, my, 1 - mc), r.at[1 - mc]) for r in refs]


def _all_gather_devices(v, name):
    def body(v_ref, o_ref, send_sems, recv_sems, local_sem):
        mx, my, mc = _me()
        flip = lambda m, b: 1 - m if b else m
        peers = [(flip(mx, k >> 2 & 1), flip(my, k >> 1 & 1), flip(mc, k & 1)) for k in range(1, 8)]
        slot = lambda d: 4 * d[0] + 2 * d[1] + d[2]
        local = pltpu.make_async_copy(v_ref, o_ref.at[slot((mx, my, mc))], local_sem)
        local.start()
        sends = [_rcopy(v_ref, o_ref.at[slot((mx, my, mc))], send_sems, recv_sems, k, dev)
                 for k, dev in enumerate(peers)]
        for cp in sends:
            cp.start()
        for k, dev in enumerate(peers):
            _rcopy(v_ref, o_ref.at[slot(dev)], send_sems, recv_sems, k, dev).wait_recv()
        for cp in sends:
            cp.wait_send()
        local.wait()

    return _comm_call(body, name, _sds((8,) + v.shape, v.dtype), 1, 7)(v)


def _row_tile(rows, cols):
    t = 8
    while t * 2 * cols * 4 <= (1 << 20) and rows % (t * 2) == 0:
        t *= 2
    return t


def _prefetch_call(body, name, out_shape, grid, in_specs, out_specs):
    return pl.pallas_call(
        body, name=name, out_shape=out_shape,
        grid_spec=pltpu.PrefetchScalarGridSpec(num_scalar_prefetch=1, grid=grid, in_specs=in_specs,
                                               out_specs=out_specs),
        compiler_params=pltpu.CompilerParams(vmem_limit_bytes=VMEM_LIMIT,
                                             dimension_semantics=("parallel",) * len(grid)))


def _sum_half(g, r1, place, name):
    _, _, rr, ns = g.shape
    tr = _row_tile(rr, ns)

    def body(c_ref, g_ref, r_ref, o_ref, ob_ref):
        q = g_ref[...] + r_ref[...]
        ob_ref[...] = q.astype(BF16)

        @pl.when(pl.program_id(1) == c_ref[0])
        def _():
            o_ref[...] = q

    blk = pl.BlockSpec((None, tr, ns), lambda i, s, c: (s, i, 0))
    return pl.pallas_call(
        body, name=name, out_shape=(_sds((rr, ns)), _sds((NCHIP, rr, ns), BF16)),
        grid_spec=pltpu.PrefetchScalarGridSpec(
            num_scalar_prefetch=1, grid=(rr // tr, NCHIP),
            in_specs=[pl.BlockSpec((None, None, tr, ns), lambda i, s, c: (s, c[1], i, 0)), blk],
            out_specs=(pl.BlockSpec((tr, ns), lambda i, s, c: (i, 0)), blk)),
        compiler_params=pltpu.CompilerParams(vmem_limit_bytes=VMEM_LIMIT,
                                             dimension_semantics=("parallel", "arbitrary")))(place, g, r1)


def _sum_recv(q, r2, place, name):
    rr, ns = q.shape
    tr = _row_tile(rr, ns)

    def body(c_ref, q_ref, r_ref, o_ref):
        o_ref[...] = ((q_ref[...] + r_ref[0].astype(F32)) + r_ref[1].astype(F32)) + r_ref[2].astype(F32)

    return _prefetch_call(body, name, _sds((2, rr, ns)), (rr // tr,),
                          [pl.BlockSpec((tr, ns), lambda i, c: (i, 0)),
                           pl.BlockSpec((NCHIP - 1, tr, ns), lambda i, c: (0, i, 0))],
                          pl.BlockSpec((None, tr, ns), lambda i, c: (c[1], i, 0)))(place, q, r2)


def _sum_devices(v8):
    def body(v_ref, o_ref):
        acc = v_ref[0]
        for dev in range(1, 8):
            acc = acc + v_ref[dev]
        o_ref[...] = acc

    return _pcall(body, name="sum_devices", out_shape=_sds(v8.shape[1:]), hbm_inputs=False)(v8)


def _adamw(w, g, m, v, name):
    rows, cols = w.shape
    tr = _row_tile(rows, cols)

    def body(w_ref, g_ref, m_ref, v_ref, d_ref, m2_ref, v2_ref):
        g = g_ref[...]
        m2 = ADAM_B1 * m_ref[...] + (1.0 - ADAM_B1) * g
        v2 = ADAM_B2 * v_ref[...] + (1.0 - ADAM_B2) * (g * g)
        m_hat = m2 / (1.0 - ADAM_B1 ** ADAM_STEP)
        v_hat = v2 / (1.0 - ADAM_B2 ** ADAM_STEP)
        d_ref[...] = -ADAM_LR * (m_hat / (jnp.sqrt(v_hat) + ADAM_EPS) + ADAM_WD * w_ref[...])
        m2_ref[...] = m2
        v2_ref[...] = v2

    blk = pl.BlockSpec((tr, cols), lambda i: (i, 0))
    return _pcall(body, name=name, out_shape=(_sds((rows, cols)),) * 3, grid=(rows // tr,), in_specs=[blk] * 4,
                  out_specs=(blk,) * 3, semantics=("parallel",))(w, g, m, v)


BIG = ("w_in", "w_conv_out", "w_attn_out", "w_out", "w_ff1", "w_ff2")
SMALL = ("rel_bias", "norm1_g", "q_norm_g", "k_norm_g", "conv_dw_w", "conv_dw_b", "conv_ln_g", "conv_ln_b", "norm2_g")
WEIGHTS = ("rel_bias", "norm1_g", "w_in", "q_norm_g", "k_norm_g", "conv_dw_w", "conv_dw_b", "conv_ln_g", "conv_ln_b",
           "w_conv_out", "w_attn_out", "w_out", "norm2_g", "w_ff1", "w_ff2")


def _pack(arrays):
    flat = jnp.concatenate([a.reshape(-1) for a in arrays])
    n = flat.shape[0]
    rows = -(-n // 1024) * 8
    return jnp.pad(flat, (0, rows * 128 - n)).reshape(rows, 128)


def _unpack(packed, shapes):
    flat = packed.reshape(-1)
    out, off = [], 0
    for shp in shapes:
        n = int(np.prod(shp))
        out.append(flat[off:off + n].reshape(shp))
        off += n
    return out


def _adamw_layer(l, w, g, m, v, prev, name, deps=()):
    _, k, n = w.shape
    tr = _row_tile(k, n)
    deps = tuple(d for d in deps if d is not None)
    if prev is None:
        prev = tuple(lax.empty(w.shape, F32) for _ in range(4))

    def body(*refs):
        w_ref, g_ref, m_ref, v_ref = refs[:4]
        go_ref, d_ref, m2_ref, v2_ref = refs[-4:]
        g = g_ref[...]
        m2 = ADAM_B1 * m_ref[...] + (1.0 - ADAM_B1) * g
        v2 = ADAM_B2 * v_ref[...] + (1.0 - ADAM_B2) * (g * g)
        m_hat = m2 / (1.0 - ADAM_B1 ** ADAM_STEP)
        v_hat = v2 / (1.0 - ADAM_B2 ** ADAM_STEP)
        go_ref[...] = g
        d_ref[...] = -ADAM_LR * (m_hat / (jnp.sqrt(v_hat) + ADAM_EPS) + ADAM_WD * w_ref[...])
        m2_ref[...] = m2
        v2_ref[...] = v2

    lay = pl.BlockSpec((None, tr, n), lambda i: (l, i, 0))
    return _pcall(body, name=name, out_shape=(_sds(w.shape),) * 4, grid=(k // tr,),
                  in_specs=[lay, pl.BlockSpec((tr, n), lambda i: (i, 0)), lay, lay] + [ANY_SPEC] * (4 + len(deps)),
                  out_specs=(lay,) * 4, aliases={4: 0, 5: 1, 6: 2, 7: 3},
                  semantics=("parallel",))(w, g, m, v, *prev, *deps)


class _GradPipe:
    def __init__(self, l, kinds, tag, big, place, w, m, v, results):
        self.l, self.kinds, self.place, self.w, self.m, self.v, self.results = l, kinds, place, w, m, v, results
        self.id = f"l{l}{tag}"
        self.g = [big[n].reshape(NCHIP, 2, big[n].shape[1] // 2, big[n].shape[2]) for n in kinds]

    def step0(self):
        lands = [lax.empty((NCHIP,) + g.shape[2:], F32) for g in self.g]
        self.s1, self.b1, tok = _split_start(f"rs1_start_{self.id}", self.g + lands, _plan_pair_half, len(self.kinds))
        return tok

    def step1(self, after):
        nk = len(self.kinds)
        bufs = _split_wait(f"rs1_wait_{self.id}", self.s1, self.b1, _plan_pair_half, after)
        sums = [_sum_half(bufs[i], bufs[nk + i], self.place, f"rs1_sum_{n}") for i, n in enumerate(self.kinds)]
        self.q = [q for q, _ in sums]
        qb = [b for _, b in sums]
        lands = [lax.empty((NCHIP - 1,) + b.shape[1:], BF16) for b in qb]
        self.s2, self.b2, tok = _split_start(f"rs2_start_{self.id}", qb + lands, _plan_scatter, 3 * nk)
        return tok

    def step2(self, after):
        nk = len(self.kinds)
        bufs = _split_wait(f"rs2_wait_{self.id}", self.s2, self.b2, _plan_scatter, after)
        fin = [_sum_recv(self.q[i], bufs[nk + i], self.place, f"rs2_sum_{n}") for i, n in enumerate(self.kinds)]
        self.s3, self.b3, tok = _split_start(f"rs3_start_{self.id}", fin, _plan_pair_fill, nk)
        return tok

    def step3(self, after):
        self.fin = _split_wait(f"rs3_wait_{self.id}", self.s3, self.b3, _plan_pair_fill, after)

    def adam(self, deps=()):
        for i, n in enumerate(self.kinds):
            g2 = self.fin[i].reshape(self.fin[i].shape[1] * 2, self.fin[i].shape[2])
            self.results[n] = _adamw_layer(self.l, self.w[n], g2, self.m[n], self.v[n], self.results.get(n),
                                           f"adamw_{n}_l{self.l}", deps=deps if i == 0 else ())
        return self.results[self.kinds[-1]][1]


def kernel(x, rel_bias, norm1_g, w_in, q_norm_g, k_norm_g, conv_dw_w, conv_dw_b, conv_ln_g, conv_ln_b, w_conv_out, w_attn_out, w_out, norm2_g, w_ff1, w_ff2, loss_target, m_rel_bias, m_norm1_g, m_w_in, m_q_norm_g, m_k_norm_g, m_conv_dw_w, m_conv_dw_b, m_conv_ln_g, m_conv_ln_b, m_w_conv_out, m_w_attn_out, m_w_out, m_norm2_g, m_w_ff1, m_w_ff2, v_rel_bias, v_norm1_g, v_w_in, v_q_norm_g, v_k_norm_g, v_conv_dw_w, v_conv_dw_b, v_conv_ln_g, v_conv_ln_b, v_w_conv_out, v_w_attn_out, v_w_out, v_norm2_g, v_w_ff1, v_w_ff2):
    w = dict(rel_bias=rel_bias, norm1_g=norm1_g, w_in=w_in, q_norm_g=q_norm_g, k_norm_g=k_norm_g, conv_dw_w=conv_dw_w,
             conv_dw_b=conv_dw_b, conv_ln_g=conv_ln_g, conv_ln_b=conv_ln_b, w_conv_out=w_conv_out,
             w_attn_out=w_attn_out, w_out=w_out, norm2_g=norm2_g, w_ff1=w_ff1, w_ff2=w_ff2)
    m = dict(rel_bias=m_rel_bias, norm1_g=m_norm1_g, w_in=m_w_in, q_norm_g=m_q_norm_g, k_norm_g=m_k_norm_g,
             conv_dw_w=m_conv_dw_w, conv_dw_b=m_conv_dw_b, conv_ln_g=m_conv_ln_g, conv_ln_b=m_conv_ln_b,
             w_conv_out=m_w_conv_out, w_attn_out=m_w_attn_out, w_out=m_w_out, norm2_g=m_norm2_g, w_ff1=m_w_ff1,
             w_ff2=m_w_ff2)
    v = dict(rel_bias=v_rel_bias, norm1_g=v_norm1_g, w_in=v_w_in, q_norm_g=v_q_norm_g, k_norm_g=v_k_norm_g,
             conv_dw_w=v_conv_dw_w, conv_dw_b=v_conv_dw_b, conv_ln_g=v_conv_ln_g, conv_ln_b=v_conv_ln_b,
             w_conv_out=v_w_conv_out, w_attn_out=v_w_attn_out, w_out=v_w_out, norm2_g=v_norm2_g, w_ff1=v_w_ff1,
             w_ff2=v_w_ff2)
    chip_id = 2 * lax.axis_index("x") + lax.axis_index("y")
    place = jnp.stack([chip_id, lax.axis_index("c")]).astype(jnp.int32)

    dww4 = _all_gather_chips(conv_dw_w, "ag_conv_dw_w")
    dww = dww4.transpose(1, 2, 0, 3).reshape(DEPTH, KW, CONV)

    names = dict(w_in="win4", w_conv_out="wco4", w_attn_out="wao4", w_out="wout4", w_ff1="wff14", w_ff2="wff24")
    chips, pair = {}, {}

    def start_chips(key, l, kinds, after):
        lands = []
        for n in kinds:
            k, ns = w[n].shape[1:]
            land = lax.dynamic_update_slice(lax.empty((NCHIP, k, ns), BF16), w[n][l].astype(BF16)[None], (chip_id, 0, 0))
            lands.append(land.reshape(NCHIP, 2, k // 2, ns))
        chips[key] = _split_start(f"ag_chips_start_{key}", lands, _plan_gather_chips, 3 * len(kinds), after=after)
        return chips[key][2]

    def start_pair(key, after):
        sems, bufs, _ = chips[key]
        bufs = _split_wait(f"ag_chips_wait_{key}", sems, bufs, _plan_gather_chips, after)
        pair[key] = _split_start(f"ag_pair_start_{key}", bufs, _plan_gather_pair, len(bufs) * 3)
        return pair[key][2]

    def landed(key, kinds, after):
        sems, bufs, _ = pair[key]
        bufs = _split_wait(f"ag_pair_wait_{key}", sems, bufs, _plan_gather_pair, after)
        return {names[n]: b.reshape(NCHIP, 2 * b.shape[2], b.shape[3]) for n, b in zip(kinds, bufs)}

    first, rest = ("w_in",), tuple(n for n in BIG if n != "w_in")
    start_chips("l0b", 0, rest, start_pair("l0a", start_chips("l0a", 0, first, dww4)))

    def get_layer(l, after):
        p = dict(dww=dww[l], dwb=conv_dw_b[l][None], lng=conv_ln_g[l][None], lnb=conv_ln_b[l][None],
                 n1g=norm1_g[l][None], n2g=norm2_g[l][None], gq=jnp.tile(q_norm_g[l], 2)[None],
                 gk=jnp.tile(k_norm_g[l], 2)[None])
        p.update(landed(f"l{l}a", first, chips["l0b"][2] if l == 0 else after))
        more = l + 1 < DEPTH

        def hook(after_):
            tok = start_pair(f"l{l}b", after_)
            return start_chips(f"l{l + 1}a", l + 1, first, tok) if more else tok

        p["hook"] = hook
        p["rest"] = lambda after_: landed(f"l{l}b", rest, after_)
        mid = (lambda after_: start_chips(f"l{l + 1}b", l + 1, rest, start_pair(f"l{l + 1}a", after_))) if more else None
        return p, (), mid

    results = {}
    make_pipe = lambda l, kinds, tag, big: _GradPipe(l, kinds, tag, big, place, w, m, v, results)
    loss_share, dx, smalls, d_rel, pipes, early, late = _local_step(x[0], loss_target[0], get_layer, rel_bias,
                                                                    make_pipe)
    loss = lax.psum(loss_share, ("x", "y", "c"))

    local_small = dict(rel_bias=d_rel)
    for n in SMALL[1:]:
        local_small[n] = jnp.stack([smalls[l][n] for l in range(DEPTH)])
    small_shapes = [local_small[n].shape for n in SMALL]
    mine = _pack([local_small[n] for n in SMALL])
    slot = 4 * lax.axis_index("x") + 2 * lax.axis_index("y") + lax.axis_index("c")
    land = lax.dynamic_update_slice(lax.empty((8,) + mine.shape, F32), mine[None], (slot, 0, 0))
    small_sems, small_bufs, tok = _split_start("ag_small_start", [land], _plan_gather_devices, 7)
    for pipe in pipes:
        done = pipe.adam(deps=(tok,))
        tok = None
    early.step3(late.step2(done))
    done = early.adam()
    late.step3(done)
    done = late.adam()
    gathered = _split_wait("ag_small_wait", small_sems, small_bufs, _plan_gather_devices, done)[0]
    summed = _sum_devices(gathered)
    grads = dict(zip(SMALL, _unpack(summed, small_shapes)))
    grads["conv_dw_w"] = lax.dynamic_slice_in_dim(grads["conv_dw_w"], chip_id * 128, 128, axis=2)

    delta, new_m, new_v = {}, {}, {}
    small_w_shapes = [w[n].shape for n in SMALL]
    outs = _adamw(_pack([w[n] for n in SMALL]), _pack([grads[n] for n in SMALL]), _pack([m[n] for n in SMALL]),
                  _pack([v[n] for n in SMALL]), "adamw_small")
    for dst, packed in zip((delta, new_m, new_v), outs):
        dst.update(zip(SMALL, _unpack(packed, small_w_shapes)))

    for n in BIG:
        grads[n], delta[n], new_m[n], new_v[n] = results[n]

    return (loss, dx[None], *[grads[n] for n in WEIGHTS], *[delta[n] for n in WEIGHTS],
            *[new_m[n] for n in WEIGHTS], *[new_v[n] for n in WEIGHTS])
```

```python
import functools
import math

import numpy as np
import jax
import jax.numpy as jnp
from jax import lax
from jax.experimental import pallas as pl
from jax.experimental.pallas import tpu as pltpu

F32 = jnp.float32
BF16 = jnp.bfloat16

T = 2048
D = 1024
DEPTH = 4
CONV = 512
KW = 31
NG = 3
HD = 64
AOUT = 512
DFF = 4096
INC = 7680
DIL = (1, 4, 16)
BLK = 128
NBUCKET = 32
EPS = 1e-6
NEG = -1e30
NCHIP = 4
UB_A, UB_GT, UB_Q, UB_K, UB_V, UB_GC, UB_GA = 0, 1, 2, 5, 8, 11, 13

ADAM_LR, ADAM_B1, ADAM_B2, ADAM_EPS, ADAM_WD, ADAM_STEP = 0.001, 0.9, 0.999, 1e-08, 0.01, 10

VMEM_LIMIT = 48 * 1024 * 1024
TB = 512
HBM_SPEC = pl.BlockSpec(memory_space=pltpu.HBM)
ANY_SPEC = pl.BlockSpec(memory_space=pl.ANY)
SEM_SPEC = pl.BlockSpec(memory_space=pltpu.SEMAPHORE)


def _pcall(body, *, name, out_shape, grid=(), in_specs=None, out_specs=None, scratch=(), aliases=None,
           semantics=None):
    kw = {}
    if in_specs is not None:
        kw["in_specs"] = in_specs
    if out_specs is not None:
        kw["out_specs"] = out_specs
    return pl.pallas_call(
        body, name=name, out_shape=out_shape, grid=grid, scratch_shapes=scratch,
        input_output_aliases=aliases or {},
        compiler_params=pltpu.CompilerParams(vmem_limit_bytes=VMEM_LIMIT, dimension_semantics=semantics),
        **kw)


def _sds(shape, dtype=F32):
    return jax.ShapeDtypeStruct(shape, dtype)


NN = (((1,), (0,)), ((), ()))
NT = (((1,), (1,)), ((), ()))
TN = (((0,), (0,)), ((), ()))


def _mm(name, a, b, *, out_shape, out_dtype, grid, a_spec, b_spec, o_spec, acc_shape, dims, add=None,
        add_spec=None, deps=()):
    nk = grid[2]
    deps = tuple(d for d in deps if d is not None)
    n_scratch = 1 if nk > 1 else 0

    def body(*refs):
        n_out = 1 + n_scratch
        refs = refs[:len(refs) - n_out - len(deps)] + refs[len(refs) - n_out:]
        a_ref, b_ref = refs[0], refs[1]
        r_ref = refs[2] if add is not None else None
        o_ref = refs[-n_out]
        prod = lax.dot_general(a_ref[...].astype(BF16), b_ref[...].astype(BF16), dims, preferred_element_type=F32)
        if nk == 1:
            o_ref[...] = (prod if r_ref is None else prod + r_ref[...]).astype(out_dtype)
            return
        acc_ref = refs[-1]
        k = pl.program_id(2)

        @pl.when(k == 0)
        def _():
            acc_ref[...] = prod

        @pl.when(k > 0)
        def _():
            acc_ref[...] += prod

        @pl.when(k == nk - 1)
        def _():
            res = acc_ref[...]
            if r_ref is not None:
                res = res + r_ref[...]
            o_ref[...] = res.astype(out_dtype)

    ins = ([a, b] if add is None else [a, b, add]) + list(deps)
    specs = ([a_spec, b_spec] if add is None else [a_spec, b_spec, add_spec]) + [ANY_SPEC] * len(deps)
    return _pcall(body, name=name, out_shape=_sds(out_shape, out_dtype), grid=grid, in_specs=specs,
                  out_specs=o_spec, scratch=[pltpu.VMEM(acc_shape, F32)] * n_scratch,
                  semantics=("parallel", "parallel", "arbitrary"))(*ins)


def _mm_x_wcols(name, a, w4, *, tm, tn, out_dtype=F32, deps=()):
    _, k, ns = w4.shape
    nj = ns // tn
    return _mm(name, a, w4, out_shape=(T, NCHIP * ns), out_dtype=out_dtype, grid=(T // tm, NCHIP * nj, 1), deps=deps,
               a_spec=pl.BlockSpec((tm, k), lambda i, j, kk: (i, 0)),
               b_spec=pl.BlockSpec((None, k, tn), lambda i, j, kk: (j // nj, 0, j % nj)),
               o_spec=pl.BlockSpec((tm, tn), lambda i, j, kk: (i, j)), acc_shape=(tm, tn), dims=NN)


def _mm_ff1(a, w4, *, tm, tn):
    _, k, ns = w4.shape
    nj = ns // tn

    def body(a_ref, b_ref, f_ref, r_ref):
        p = jnp.maximum(jnp.dot(a_ref[...], b_ref[...], preferred_element_type=F32), 0.0)
        f_ref[...] = p.astype(BF16)
        r_ref[...] = (p * p).astype(BF16)

    out = pl.BlockSpec((tm, tn), lambda i, j: (i, j))
    return _pcall(body, name="mm_f", out_shape=(_sds((T, DFF), BF16), _sds((T, DFF), BF16)), grid=(T // tm, NCHIP * nj),
                  in_specs=[pl.BlockSpec((tm, k), lambda i, j: (i, 0)),
                            pl.BlockSpec((None, k, tn), lambda i, j: (j // nj, 0, j % nj))],
                  out_specs=(out, out), semantics=("parallel", "parallel"))(a, w4)


def _mm_x_wrows(name, a, w4, add, *, tm, tk, tn, deps=()):
    _, ks, n = w4.shape
    nkk = ks // tk
    return _mm(name, a, w4, out_shape=(T, n), out_dtype=F32, grid=(T // tm, n // tn, NCHIP * nkk), deps=deps,
               a_spec=pl.BlockSpec((tm, tk), lambda i, j, kk: (i, kk)),
               b_spec=pl.BlockSpec((None, tk, tn), lambda i, j, kk: (kk // nkk, kk % nkk, j)),
               o_spec=pl.BlockSpec((tm, tn), lambda i, j, kk: (i, j)), acc_shape=(tm, tn), dims=NN,
               add=add, add_spec=pl.BlockSpec((tm, tn), lambda i, j, kk: (i, j)))


def _mm_g_wcols_t(name, g, w4, *, tm, tk, tn, out_dtype=F32, deps=()):
    _, k, ns = w4.shape
    nkk = ns // tk
    return _mm(name, g, w4, out_shape=(T, k), out_dtype=out_dtype, grid=(T // tm, k // tn, NCHIP * nkk), deps=deps,
               a_spec=pl.BlockSpec((tm, tk), lambda i, j, kk: (i, kk)),
               b_spec=pl.BlockSpec((None, tn, tk), lambda i, j, kk: (kk // nkk, j, kk % nkk)),
               o_spec=pl.BlockSpec((tm, tn), lambda i, j, kk: (i, j)), acc_shape=(tm, tn), dims=NT)


def _mm_g_wrows_t(name, g, w4, *, tm, tn, out_dtype=F32, deps=()):
    _, ks, n = w4.shape
    nj = ks // tn
    return _mm(name, g, w4, out_shape=(T, NCHIP * ks), out_dtype=out_dtype, grid=(T // tm, NCHIP * nj, 1), deps=deps,
               a_spec=pl.BlockSpec((tm, n), lambda i, j, kk: (i, 0)),
               b_spec=pl.BlockSpec((None, tn, n), lambda i, j, kk: (j // nj, j % nj, 0)),
               o_spec=pl.BlockSpec((tm, tn), lambda i, j, kk: (i, j)), acc_shape=(tm, tn), dims=NT)


def _mm_dff2(dx, w4, fa, *, tm, tn, deps=()):
    _, ks, n = w4.shape
    nj = ks // tn
    deps = tuple(d for d in deps if d is not None)

    def body(*refs):
        dx_ref, b_ref, f_ref = refs[:3]
        df_ref = refs[-1]
        dr = lax.dot_general(dx_ref[...].astype(BF16), b_ref[...], NT, preferred_element_type=F32)
        df_ref[...] = (dr * (2.0 * f_ref[...].astype(F32))).astype(BF16)

    out = pl.BlockSpec((tm, tn), lambda i, j: (i, j))
    return _pcall(body, name="mm_dr", out_shape=_sds((T, DFF), BF16), grid=(T // tm, NCHIP * nj),
                  in_specs=[pl.BlockSpec((tm, n), lambda i, j: (i, 0)),
                            pl.BlockSpec((None, tn, n), lambda i, j: (j // nj, j % nj, 0)), out]
                  + [ANY_SPEC] * len(deps),
                  out_specs=out, semantics=("parallel", "parallel"))(dx, w4, fa, *deps)


TCH = 512


def _mm_dw(name, a, g, *, out_shape, out_map, tm, tn, deps=()):
    deps = tuple(d for d in deps if d is not None)

    def body(*refs):
        a_ref, g_ref = refs[:2]
        o_ref, at_ref = refs[-2:]

        @pl.when(pl.program_id(1) == 0)
        def _():
            for c in range(T // TCH):
                at_ref[:, c * TCH:(c + 1) * TCH] = a_ref[c * TCH:(c + 1) * TCH, :].T

        o_ref[...] = jnp.dot(at_ref[...], g_ref[...].astype(BF16), preferred_element_type=F32)

    return _pcall(body, name=name, out_shape=_sds(out_shape), grid=(a.shape[1] // tm, g.shape[1] // tn),
                  in_specs=[pl.BlockSpec((T, tm), lambda i, j: (0, i)), pl.BlockSpec((T, tn), lambda i, j: (0, j))]
                  + [ANY_SPEC] * len(deps),
                  out_specs=pl.BlockSpec((None, tm, tn), out_map), scratch=[pltpu.VMEM((tm, T), BF16)],
                  semantics=("parallel", "arbitrary"))(a, g, *deps)


def _mm_dw_cols(name, a, g, *, ns, tm, tn, deps=()):
    nj = ns // tn
    return _mm_dw(name, a, g, out_shape=(NCHIP, a.shape[1], ns), out_map=lambda i, j: (j // nj, i, j % nj),
                  tm=tm, tn=tn, deps=deps)


def _mm_dw_rows(name, a, g, *, ks, tm, tn):
    ni = ks // tm
    return _mm_dw(name, a, g, out_shape=(NCHIP, ks, g.shape[1]), out_map=lambda i, j: (i // ni, i % ni, j),
                  tm=tm, tn=tn)


def _row_spec(width, col=0):
    return pl.BlockSpec((TB, width), lambda i: (i, col))


def _vec_spec(width):
    return pl.BlockSpec((1, width), lambda i: (0, 0))


def _rms_fwd(x, g):
    def body(x_ref, g_ref, h_ref):
        x = x_ref[...]
        r = lax.rsqrt(jnp.mean(x * x, axis=-1, keepdims=True) + EPS)
        h_ref[...] = (x * r * g_ref[...]).astype(BF16)

    return _pcall(body, name="rms_fwd", out_shape=_sds((T, D), BF16), grid=(T // TB,),
                  in_specs=[_row_spec(D), _vec_spec(D)], out_specs=_row_spec(D), semantics=("parallel",))(x, g)


def _rms_bwd(x, g, dh, dres, deps=()):
    deps = tuple(d for d in deps if d is not None)

    def body(*refs):
        x_ref, g_ref, dh_ref, dres_ref = refs[:4]
        dx_ref, dg_ref = refs[-2:]
        x = x_ref[...]
        r = lax.rsqrt(jnp.mean(x * x, axis=-1, keepdims=True) + EPS)
        y = x * r
        dh = dh_ref[...]
        dy = dh * g_ref[...]
        dx_ref[...] = dres_ref[...] + r * (dy - y * jnp.mean(dy * y, axis=-1, keepdims=True))

        @pl.when(pl.program_id(0) == 0)
        def _():
            dg_ref[...] = jnp.zeros_like(dg_ref)

        dg_ref[...] += jnp.sum(dh * y, axis=0, keepdims=True)

    return _pcall(body, name="rms_bwd", out_shape=(_sds((T, D)), _sds((1, D))), grid=(T // TB,),
                  in_specs=[_row_spec(D), _vec_spec(D), _row_spec(D), _row_spec(D)] + [ANY_SPEC] * len(deps),
                  out_specs=(_row_spec(D), _vec_spec(D)), semantics=("arbitrary",))(x, g, dh, dres, *deps)


def _sigmoid(x):
    return 1.0 / (1.0 + jnp.exp(-x))


def _gate_fwd(u, ycv, yat):
    def body(gc_ref, ga_ref, yc_ref, ya_ref, m_ref):
        m_ref[...] = (_sigmoid(gc_ref[...].astype(F32)) * yc_ref[...]
                      + _sigmoid(ga_ref[...].astype(F32)) * ya_ref[...]).astype(BF16)

    blk = lambda off: pl.BlockSpec((TB, 512), lambda i, j: (i, off + j))
    return _pcall(body, name="gate_fwd", out_shape=_sds((T, D), BF16), grid=(T // TB, 2),
                  in_specs=[blk(UB_GC), blk(UB_GA), blk(0), blk(0)], out_specs=blk(0),
                  semantics=("parallel", "parallel"))(u, u, ycv, yat)


def _gate_bwd(u, ycv, yat, dm):
    def body(gc_ref, ga_ref, yc_ref, ya_ref, dm_ref, dyc_ref, dya_ref, dgc_ref, dga_ref):
        dm = dm_ref[...]
        sc = _sigmoid(gc_ref[...].astype(F32))
        sa = _sigmoid(ga_ref[...].astype(F32))
        dyc_ref[...] = (dm * sc).astype(BF16)
        dya_ref[...] = (dm * sa).astype(BF16)
        dgc_ref[...] = (dm * yc_ref[...] * sc * (1.0 - sc)).astype(BF16)
        dga_ref[...] = (dm * ya_ref[...] * sa * (1.0 - sa)).astype(BF16)

    blk = lambda off: pl.BlockSpec((TB, 512), lambda i, j: (i, off + j))
    return _pcall(body, name="gate_bwd",
                  out_shape=(_sds((T, D), BF16), _sds((T, D), BF16), _sds((T, D), BF16), _sds((T, D), BF16)),
                  grid=(T // TB, 2), in_specs=[blk(UB_GC), blk(UB_GA), blk(0), blk(0), blk(0)],
                  out_specs=(blk(0), blk(0), blk(0), blk(0)),
                  semantics=("parallel", "parallel"))(u, u, ycv, yat, dm)


def _loss_fwd_bwd(y, target):
    def body(y_ref, t_ref, loss_ref, dy_ref):
        e = y_ref[...] - t_ref[...]
        dy_ref[...] = e * (1.0 / D)

        @pl.when(pl.program_id(0) == 0)
        def _():
            loss_ref[...] = jnp.zeros_like(loss_ref)

        loss_ref[...] += 0.5 * jnp.sum(jnp.mean(e * e, axis=-1, keepdims=True))

    return _pcall(body, name="loss", out_shape=(_sds((8, 128)), _sds((T, D))), grid=(T // TB,),
                  in_specs=[_row_spec(D), _row_spec(D)],
                  out_specs=(pl.BlockSpec((8, 128), lambda i: (0, 0)), _row_spec(D)),
                  semantics=("arbitrary",))(y, target)


PAD = 32
CCH = 256


def _conv_fwd(u, dw_w, dw_b):
    def body(a_ref, gt_ref, w_ref, b_ref, z1_ref, zp_ref):
        zp_ref[0:PAD, :] = jnp.zeros((PAD, 128), F32)
        zp_ref[PAD:PAD + T, :] = a_ref[...].astype(F32) * _sigmoid(gt_ref[...].astype(F32))
        for c in range(T // CCH):
            acc = jnp.broadcast_to(b_ref[...], (CCH, 128))
            for j in range(KW):
                acc = acc + w_ref[j:j + 1, :] * zp_ref[pl.ds(c * CCH + j + PAD - (KW - 1), CCH), :]
            z1_ref[c * CCH:(c + 1) * CCH, :] = acc

    col = lambda off: pl.BlockSpec((T, 128), lambda j: (0, off * 4 + j))
    return _pcall(body, name="conv_fwd", out_shape=_sds((T, CONV)), grid=(CONV // 128,),
                  in_specs=[col(UB_A), col(UB_GT), pl.BlockSpec((KW, 128), lambda j: (0, j)),
                            pl.BlockSpec((1, 128), lambda j: (0, j))],
                  out_specs=col(0), scratch=[pltpu.VMEM((T + PAD, 128), F32)],
                  semantics=("parallel",))(u, u, dw_w, dw_b)


def _ln_silu_fwd(z1, g, b):
    def body(z_ref, g_ref, b_ref, o_ref):
        z = z_ref[...]
        mu = jnp.mean(z, axis=-1, keepdims=True)
        zc = z - mu
        zh = zc * lax.rsqrt(jnp.mean(zc * zc, axis=-1, keepdims=True) + EPS)
        z2 = zh * g_ref[...] + b_ref[...]
        o_ref[...] = (z2 * _sigmoid(z2)).astype(BF16)

    return _pcall(body, name="ln_silu_fwd", out_shape=_sds((T, CONV), BF16), grid=(T // TB,),
                  in_specs=[_row_spec(CONV), _vec_spec(CONV), _vec_spec(CONV)], out_specs=_row_spec(CONV),
                  semantics=("parallel",))(z1, g, b)


def _ln_silu_bwd(z1, g, b, dz3):
    def body(z_ref, g_ref, b_ref, d_ref, z3_ref, dz1_ref, dg_ref, db_ref):
        z = z_ref[...]
        mu = jnp.mean(z, axis=-1, keepdims=True)
        zc = z - mu
        rs = lax.rsqrt(jnp.mean(zc * zc, axis=-1, keepdims=True) + EPS)
        zh = zc * rs
        z2 = zh * g_ref[...] + b_ref[...]
        s = _sigmoid(z2)
        z3_ref[...] = (z2 * s).astype(BF16)
        dz2 = d_ref[...] * (s * (1.0 + z2 * (1.0 - s)))
        dzh = dz2 * g_ref[...]
        dz1_ref[...] = rs * (dzh - jnp.mean(dzh, axis=-1, keepdims=True)
                             - zh * jnp.mean(dzh * zh, axis=-1, keepdims=True))

        @pl.when(pl.program_id(0) == 0)
        def _():
            dg_ref[...] = jnp.zeros_like(dg_ref)
            db_ref[...] = jnp.zeros_like(db_ref)

        dg_ref[...] += jnp.sum(dz2 * zh, axis=0, keepdims=True)
        db_ref[...] += jnp.sum(dz2, axis=0, keepdims=True)

    return _pcall(body, name="ln_silu_bwd",
                  out_shape=(_sds((T, CONV), BF16), _sds((T, CONV)), _sds((1, CONV)), _sds((1, CONV))),
                  grid=(T // TB,),
                  in_specs=[_row_spec(CONV), _vec_spec(CONV), _vec_spec(CONV), _row_spec(CONV)],
                  out_specs=(_row_spec(CONV), _row_spec(CONV), _vec_spec(CONV), _vec_spec(CONV)),
                  semantics=("arbitrary",))(z1, g, b, dz3)


def _conv_bwd(u, dw_w, dz1):
    def body(a_ref, gt_ref, w_ref, dz1_ref, da_ref, dgt_ref, dw_ref, db_ref, zp_ref, dp_ref):
        a = a_ref[...].astype(F32)
        s = _sigmoid(gt_ref[...].astype(F32))
        zp_ref[0:PAD, :] = jnp.zeros((PAD, 128), F32)
        zp_ref[PAD:PAD + T, :] = a * s
        dp_ref[0:T, :] = dz1_ref[...]
        dp_ref[T:T + PAD, :] = jnp.zeros((PAD, 128), F32)
        db_ref[...] = jnp.sum(dz1_ref[...], axis=0, keepdims=True)
        for j in range(KW):
            tot = jnp.zeros((1, 128), F32)
            for c in range(T // CCH):
                tot = tot + jnp.sum(dz1_ref[c * CCH:(c + 1) * CCH, :]
                                    * zp_ref[pl.ds(c * CCH + j + PAD - (KW - 1), CCH), :], axis=0, keepdims=True)
            dw_ref[j:j + 1, :] = tot
        for c in range(T // CCH):
            acc = jnp.zeros((CCH, 128), F32)
            for j in range(KW):
                acc = acc + w_ref[j:j + 1, :] * dp_ref[pl.ds(c * CCH + (KW - 1) - j, CCH), :]
            rows = slice(c * CCH, (c + 1) * CCH)
            sc = _sigmoid(gt_ref[rows, :].astype(F32))
            da_ref[rows, :] = (acc * sc).astype(BF16)
            dgt_ref[rows, :] = (acc * a_ref[rows, :].astype(F32) * sc * (1.0 - sc)).astype(BF16)

    col = lambda off: pl.BlockSpec((T, 128), lambda j: (0, off * 4 + j))
    wspec = pl.BlockSpec((KW, 128), lambda j: (0, j))
    return _pcall(body, name="conv_bwd",
                  out_shape=(_sds((T, CONV), BF16), _sds((T, CONV), BF16), _sds((KW, CONV)), _sds((1, CONV))),
                  grid=(CONV // 128,), in_specs=[col(UB_A), col(UB_GT), wspec, col(0)],
                  out_specs=(col(0), col(0), wspec, pl.BlockSpec((1, 128), lambda j: (0, j))),
                  scratch=[pltpu.VMEM((T + PAD, 128), F32), pltpu.VMEM((T + PAD, 128), F32)],
                  semantics=("parallel",))(u, u, dw_w, dz1)


def _bucket_tables():
    qi = np.arange(BLK)[:, None]
    kj = np.arange(2 * BLK)[None, :]
    off = np.clip(qi + BLK - kj, 0, BLK)
    out = []
    for d in DIL:
        dist = (off * d).astype(np.int32)
        nf = np.maximum(dist, 1).astype(np.float32)
        large = 16 + (np.log(nf / np.float32(16)) / np.float32(math.log(2048 / 16)) * np.float32(16)).astype(np.int32)
        large = np.minimum(large, NBUCKET - 1)
        out.append(np.where(dist < 16, dist, large))
    return np.stack(out).astype(np.int32)


def _band():
    off = lax.broadcasted_iota(jnp.int32, (BLK, 2 * BLK), 0) + BLK - lax.broadcasted_iota(jnp.int32, (BLK, 2 * BLK), 1)
    return (off >= 0) & (off <= BLK)


def _bias_table(rel_bias_t, buckets):
    def body(rb_ref, bk_ref, o_ref):
        h = pl.program_id(0)
        bk = bk_ref[...]
        acc = jnp.zeros((BLK, 2 * BLK), F32)
        for b in range(NBUCKET):
            acc = jnp.where(bk == b, rb_ref[h, b], acc)
        o_ref[...] = jnp.where(_band(), acc, NEG)

    return _pcall(body, name="bias_table", out_shape=_sds((3 * 8, BLK, 2 * BLK)), grid=(24,),
                  in_specs=[pl.BlockSpec(memory_space=pltpu.SMEM),
                            pl.BlockSpec((None, BLK, 2 * BLK), lambda h: (h // 8, 0, 0))],
                  out_specs=pl.BlockSpec((None, BLK, 2 * BLK), lambda h: (h, 0, 0)),
                  semantics=("parallel",))(rel_bias_t, buckets)


def _bias_grad(ds_acc, buckets):
    def body(a_ref, bk_ref, o_ref):
        acc = a_ref[0]
        for l in range(1, DEPTH):
            acc = acc + a_ref[l]
        bk = bk_ref[...]
        lane = lax.broadcasted_iota(jnp.int32, (1, 128), 1)
        row = jnp.zeros((1, 128), F32)
        for b in range(NBUCKET):
            row = jnp.where(lane == b, jnp.sum(jnp.where(bk == b, acc, 0.0)), row)
        o_ref[...] = row

    return _pcall(body, name="bias_grad", out_shape=_sds((24, 1, 128)), grid=(24,),
                  in_specs=[pl.BlockSpec((DEPTH, None, BLK, 2 * BLK), lambda h: (0, h, 0, 0)),
                            pl.BlockSpec((None, BLK, 2 * BLK), lambda h: (h // 8, 0, 0))],
                  out_specs=pl.BlockSpec((None, 1, 128), lambda h: (h, 0, 0)),
                  semantics=("parallel",))(ds_acc, buckets)


def _head_mask():
    return lax.broadcasted_iota(jnp.int32, (1, 128), 1) < HD


def _seg_ones(width):
    r = lax.broadcasted_iota(jnp.int32, (width, width), 0) >> 6
    c = lax.broadcasted_iota(jnp.int32, (width, width), 1) >> 6
    return (r == c).astype(BF16)


def _seg_sum(x, ones):
    hi = x.astype(BF16)
    lo = (x - hi.astype(F32)).astype(BF16)
    return (jnp.dot(hi, ones, preferred_element_type=F32) + jnp.dot(lo, ones, preferred_element_type=F32))


def _dot(a, b, dims):
    return lax.dot_general(a, b, dims, preferred_element_type=F32)


def _tile_rows(d, r, n):
    stride = None if d == 1 else d
    q_rows = pl.ds(r + d * n * BLK, BLK, stride=stride)
    if n == 0:
        return q_rows, q_rows, BLK
    return q_rows, pl.ds(r + d * (n - 1) * BLK, 2 * BLK, stride=stride), 2 * BLK


def _stack_heads(x, m_a):
    return jnp.concatenate([jnp.where(m_a, x, 0.0), jnp.where(m_a, 0.0, x)], axis=0)


def _stack_rows(x, m_a, width):
    other = pltpu.roll(x, HD, axis=1)
    both = jnp.concatenate([jnp.where(m_a, x, other), jnp.where(m_a, other, x)], axis=0)
    return both if width == 128 else jnp.concatenate([both] * (width // 128), axis=1)


NCH = 256


def _qk_norm_prep(q_ref, k_ref, v_ref, gq_ref, gk_ref, qn_ref, kn_ref, vn_ref, ones):
    def prep(i, carry):
        rows = pl.ds(pl.multiple_of(i * NCH, NCH), NCH)
        q = q_ref[rows, :].astype(F32)
        qn_ref[rows, :] = q * lax.rsqrt(_seg_sum(q * q, ones) * (1.0 / HD) + EPS) * gq_ref[...] * (HD ** -0.5)
        k = k_ref[rows, :].astype(F32)
        kn_ref[rows, :] = k * lax.rsqrt(_seg_sum(k * k, ones) * (1.0 / HD) + EPS) * gk_ref[...]
        vn_ref[rows, :] = v_ref[rows, :].astype(F32)
        return carry

    lax.fori_loop(0, T // NCH, prep, 0)


def _attn_specs(g):
    ucol = lambda base: pl.BlockSpec((T, 128), lambda hp: (0, (base + g) * 4 + hp))
    col = pl.BlockSpec((T, 128), lambda hp: (0, hp))
    vec = pl.BlockSpec((1, 128), lambda hp: (0, 0))
    bm = pl.BlockSpec((2, BLK, 2 * BLK), lambda hp: (g * 4 + hp, 0, 0))
    return ucol, col, vec, bm


def _attn_fwd(g, u, gq, gk, bm, deps=()):
    d = DIL[g]

    def body(*refs):
        q_ref, k_ref, v_ref, gq_ref, gk_ref, bm_ref = refs[:6]
        o_ref, lse_ref, qn_ref, kn_ref, vn_ref = refs[-5:]
        ones = _seg_ones(128)
        _qk_norm_prep(q_ref, k_ref, v_ref, gq_ref, gk_ref, qn_ref, kn_ref, vn_ref, ones)
        m_a = _head_mask()
        for r in range(d):
            for n in range(T // d // BLK):
                q_rows, k_rows, nk = _tile_rows(d, r, n)
                qt = qn_ref[q_rows, :]
                kt = kn_ref[k_rows, :].astype(BF16)
                vt = vn_ref[k_rows, :].astype(BF16)
                q2 = _stack_heads(qt, m_a).astype(BF16)
                s = _dot(q2, kt, NT) + bm_ref[...].reshape(2 * BLK, 2 * BLK)[:, 2 * BLK - nk:]
                mx = jnp.max(s, axis=1, keepdims=True)
                p = jnp.exp(s - mx)
                l = jnp.sum(p, axis=1, keepdims=True)
                o2 = _dot(p.astype(BF16), vt, NN) / l
                lse2 = jnp.broadcast_to(mx + jnp.log(l), (2 * BLK, 128))
                o_ref[q_rows, :] = jnp.where(m_a, o2[:BLK], o2[BLK:])
                lse_ref[q_rows, :] = jnp.where(m_a, lse2[:BLK], lse2[BLK:])

    ucol, col, vec, bmspec = _attn_specs(g)
    return _pcall(body, name=f"attn_fwd_g{g}", out_shape=(_sds((T, AOUT)), _sds((T, AOUT))), grid=(4,),
                  in_specs=[ucol(UB_Q), ucol(UB_K), ucol(UB_V), vec, vec, bmspec] + [ANY_SPEC] * len(deps),
                  out_specs=(col, col), scratch=[pltpu.VMEM((T, 128), F32)] * 3,
                  semantics=("parallel",))(u, u, u, gq, gk, bm, *deps)


def _attn_bwd(g, u, gq, gk, bm, dog, cb, lse):
    d = DIL[g]

    def body(q_ref, k_ref, v_ref, gq_ref, gk_ref, bm_ref, do_ref, cb_ref, lse_ref,
             dqo_ref, dko_ref, dvo_ref, dgq_ref, dgk_ref, dsa_ref, qn_ref, kn_ref, vn_ref, dq_ref, dk_ref, dv_ref):
        ones = _seg_ones(128)
        _qk_norm_prep(q_ref, k_ref, v_ref, gq_ref, gk_ref, qn_ref, kn_ref, vn_ref, ones)
        m_a = _head_mask()
        dk_ref[...] = jnp.zeros_like(dk_ref)
        dv_ref[...] = jnp.zeros_like(dv_ref)
        dsa_ref[...] = jnp.zeros_like(dsa_ref)
        for r in range(d):
            for n in range(T // d // BLK):
                q_rows, k_rows, nk = _tile_rows(d, r, n)
                ktb = kn_ref[k_rows, :].astype(BF16)
                vtb = vn_ref[k_rows, :].astype(BF16)
                q2 = _stack_heads(qn_ref[q_rows, :], m_a).astype(BF16)
                do2 = _stack_heads(do_ref[q_rows, :], m_a).astype(BF16)
                lse_c = _stack_rows(lse_ref[q_rows, :], m_a, nk)
                c_c = _stack_rows(cb_ref[q_rows, :], m_a, nk)
                s = _dot(q2, ktb, NT) + bm_ref[...].reshape(2 * BLK, 2 * BLK)[:, 2 * BLK - nk:]
                p = jnp.exp(s - lse_c)
                ds = p * (_dot(do2, vtb, NT) + c_c)
                dsb = ds.astype(BF16)
                dq2 = _dot(dsb, ktb, NN)
                dq_ref[q_rows, :] = jnp.where(m_a, dq2[:BLK], dq2[BLK:])
                dk_ref[k_rows, :] += _dot(dsb, q2, TN)
                dv_ref[k_rows, :] += _dot(p.astype(BF16), do2, TN)
                dsa_ref[:, :, 2 * BLK - nk:] += ds.reshape(2, BLK, nk)

        @pl.when(pl.program_id(0) == 0)
        def _():
            dgq_ref[...] = jnp.zeros_like(dgq_ref)
            dgk_ref[...] = jnp.zeros_like(dgk_ref)

        def norm_bwd(i, carry):
            rows = pl.ds(pl.multiple_of(i * NCH, NCH), NCH)
            for x_ref, g_ref, dx_ref, dxo_ref, dg_ref, scale in (
                    (q_ref, gq_ref, dq_ref, dqo_ref, dgq_ref, HD ** -0.5), (k_ref, gk_ref, dk_ref, dko_ref, dgk_ref, 1.0)):
                x = x_ref[rows, :].astype(F32)
                rs = lax.rsqrt(_seg_sum(x * x, ones) * (1.0 / HD) + EPS)
                xh = x * rs
                dn = dx_ref[rows, :] * scale
                dxh = dn * g_ref[...]
                dxo_ref[rows, :] = (rs * (dxh - xh * (_seg_sum(dxh * xh, ones) * (1.0 / HD)))).astype(BF16)
                dg_ref[...] += jnp.sum(dn * xh, axis=0, keepdims=True)
            dvo_ref[rows, :] = dv_ref[rows, :].astype(BF16)
            return carry

        lax.fori_loop(0, T // NCH, norm_bwd, 0)

    ucol, col, vec, bmspec = _attn_specs(g)
    return _pcall(body, name=f"attn_bwd_g{g}",
                  out_shape=(_sds((T, AOUT), BF16), _sds((T, AOUT), BF16), _sds((T, AOUT), BF16), _sds((1, 128)),
                             _sds((1, 128)), _sds((8, BLK, 2 * BLK))),
                  grid=(4,),
                  in_specs=[ucol(UB_Q), ucol(UB_K), ucol(UB_V), vec, vec, bmspec, col, col, col],
                  out_specs=(col, col, col, vec, vec, pl.BlockSpec((2, BLK, 2 * BLK), lambda hp: (hp, 0, 0))),
                  scratch=[pltpu.VMEM((T, 128), F32)] * 6,
                  semantics=("arbitrary",))(u, u, u, gq, gk, bm, dog, cb, lse)


def _combine_fwd(ogs, lses):
    def body(o0, o1, o2, l0, l1, l2, o_ref):
        ls = [l0[...], l1[...], l2[...]]
        mx = jnp.maximum(jnp.maximum(ls[0], ls[1]), ls[2])
        es = [jnp.exp(l - mx) for l in ls]
        inv = 1.0 / (es[0] + es[1] + es[2])
        o_ref[...] = ((es[0] * o0[...] + es[1] * o1[...] + es[2] * o2[...]) * inv).astype(BF16)

    return _pcall(body, name="combine_fwd", out_shape=_sds((T, AOUT), BF16), grid=(T // TB,),
                  in_specs=[_row_spec(AOUT)] * 6, out_specs=_row_spec(AOUT), semantics=("parallel",))(*ogs, *lses)


def _combine_bwd(ogs, lses, do):
    def body(o0, o1, o2, l0, l1, l2, do_ref, d0, d1, d2, c0, c1, c2):
        ls = [l0[...], l1[...], l2[...]]
        mx = jnp.maximum(jnp.maximum(ls[0], ls[1]), ls[2])
        es = [jnp.exp(l - mx) for l in ls]
        inv = 1.0 / (es[0] + es[1] + es[2])
        ws = [e * inv for e in es]
        do = do_ref[...]
        o = ws[0] * o0[...] + ws[1] * o1[...] + ws[2] * o2[...]
        s = _seg_sum(do * o, _seg_ones(AOUT))
        for w, d_ref, c_ref in zip(ws, (d0, d1, d2), (c0, c1, c2)):
            d_ref[...] = w * do
            c_ref[...] = -(w * s)

    return _pcall(body, name="combine_bwd", out_shape=tuple(_sds((T, AOUT)) for _ in range(6)), grid=(T // TB,),
                  in_specs=[_row_spec(AOUT)] * 7, out_specs=tuple(_row_spec(AOUT) for _ in range(6)),
                  semantics=("parallel",))(*ogs, *lses, do)


def _layer_fwd(x, p, bm, deps=(), mid=None):
    h1 = _rms_fwd(x, p["n1g"])
    u = _mm_x_wcols("mm_u", h1, p["win4"], tm=T, tn=640, out_dtype=BF16, deps=deps)
    ogs, lses = [], []
    for g in range(NG):
        gdeps = (p["hook"](ogs[-1]),) if g == NG - 1 and "hook" in p else ()
        og, lse = _attn_fwd(g, u, p["gq"], p["gk"], bm, deps=gdeps)
        ogs.append(og)
        lses.append(lse)
    o = _combine_fwd(ogs, lses)
    z1 = _conv_fwd(u, p["dww"], p["dwb"])
    z3 = _ln_silu_fwd(z1, p["lng"], p["lnb"])
    if "rest" in p:
        p = {**p, **p["rest"](z3)}
    ycv = _mm_x_wcols("mm_ycv", z3, p["wco4"], tm=T, tn=256)
    yat = _mm_x_wcols("mm_yat", o, p["wao4"], tm=T, tn=256)
    m = _gate_fwd(u, ycv, yat)
    xm = _mm_x_wrows("mm_xmid", m, p["wout4"], x, tm=1024, tk=256, tn=1024)
    h2 = _rms_fwd(xm, p["n2g"])
    fa, r = _mm_ff1(h2, p["wff14"], tm=T, tn=512)
    tok = mid(r) if mid else None
    xo = _mm_x_wrows("mm_xout", r, p["wff24"], xm, tm=1024, tk=1024, tn=1024, deps=(tok,))
    saved = dict(x=x, h1=h1, u=u, z1=z1, ogs=ogs, lses=lses, o=o, ycv=ycv, yat=yat, m=m, xm=xm, h2=h2, fa=fa, r=r)
    return xo, p, saved


EARLY = ("w_ff2", "w_ff1", "w_out")
LATE = ("w_conv_out", "w_attn_out", "w_in")


def _layer_bwd(dx, s, p, bm, pipe=None, own_early=None, own_late=None):
    u = s["u"]
    tok = pipe.step0() if pipe else None
    df = _mm_dff2(dx, p["wff24"], s["fa"], tm=T, tn=512, deps=(tok,))
    g_ff2 = _mm_dw_rows("mm_dwff2", s["r"], dx, ks=1024, tm=1024, tn=1024)
    g_ff1 = _mm_dw_cols("mm_dwff1", s["h2"], df, ns=1024, tm=1024, tn=512)
    tok = pipe.step1(g_ff1) if pipe else None
    dh2 = _mm_g_wcols_t("mm_dh2", df, p["wff14"], tm=1024, tk=1024, tn=1024, deps=(tok,))
    dxm, d_n2g = _rms_bwd(s["xm"], p["n2g"], dh2, dx)

    dm = _mm_g_wrows_t("mm_dm", dxm, p["wout4"], tm=1024, tn=256)
    g_out = _mm_dw_rows("mm_dwout", s["m"], dxm, ks=256, tm=256, tn=1024)
    early = own_early(dict(w_ff2=g_ff2, w_ff1=g_ff1, w_out=g_out)) if own_early else None
    tok_e = early.step0() if early else None
    dyc, dya, dgc, dga = _gate_bwd(u, s["ycv"], s["yat"], dm)

    dz3 = _mm_g_wcols_t("mm_dz3", dyc, p["wco4"], tm=T, tk=256, tn=512, deps=(tok_e,))
    z3, dz1, d_lng, d_lnb = _ln_silu_bwd(s["z1"], p["lng"], p["lnb"], dz3)
    g_co = _mm_dw_cols("mm_dwco", z3, dyc, ns=256, tm=512, tn=256)
    da, dgt, d_dww, d_dwb = _conv_bwd(u, p["dww"], dz1)

    tok_e = early.step1(da) if early else None
    do = _mm_g_wcols_t("mm_do", dya, p["wao4"], tm=T, tk=256, tn=512, deps=(tok_e,))
    g_ao = _mm_dw_cols("mm_dwao", s["o"], dya, ns=256, tm=512, tn=256)
    parts = _combine_bwd(s["ogs"], s["lses"], do)
    dqs, dks, dvs, d_gq, d_gk, dsas = [], [], [], [], [], []
    for g in range(NG):
        dq, dk, dv, dgq, dgk, dsa = _attn_bwd(g, u, p["gq"], p["gk"], bm, parts[g], parts[NG + g], s["lses"][g])
        dqs.append(dq)
        dks.append(dk)
        dvs.append(dv)
        d_gq.append(dgq)
        d_gk.append(dgk)
        dsas.append(dsa)
    du = jnp.concatenate([da, dgt] + dqs + dks + dvs + [dgc, dga], axis=1)
    tok = pipe.step2(du) if pipe else None
    tok_e = early.step2(du) if early else None
    g_in = _mm_dw_cols("mm_dwin", s["h1"], du, ns=1920, tm=1024, tn=640, deps=(tok, tok_e))
    late = own_late(dict(w_in=g_in, w_conv_out=g_co, w_attn_out=g_ao)) if own_late else None
    tok_l = late.step0() if late else None
    dh1 = _mm_g_wcols_t("mm_dh1", du, p["win4"], tm=1024, tk=1920, tn=1024, deps=(tok_l,))
    tok_l = late.step1(dh1) if late else None
    dxi, d_n1g = _rms_bwd(s["x"], p["n1g"], dh1, dxm, deps=(tok_l,))
    if pipe:
        pipe.step3(dxi)

    fold = lambda parts_: sum(v[0, :HD] + v[0, HD:] for v in parts_)
    big = dict(w_in=g_in, w_conv_out=g_co, w_attn_out=g_ao, w_out=g_out, w_ff1=g_ff1, w_ff2=g_ff2)
    small = dict(norm1_g=d_n1g[0], q_norm_g=fold(d_gq), k_norm_g=fold(d_gk), conv_dw_w=d_dww, conv_dw_b=d_dwb[0],
                 conv_ln_g=d_lng[0], conv_ln_b=d_lnb[0], norm2_g=d_n2g[0])
    return dxi, big, small, jnp.concatenate(dsas, axis=0), early, late


def _local_step(x, target, get_layer, rel_bias, make_pipe):
    buckets = jnp.asarray(_bucket_tables())
    bm = _bias_table(rel_bias.T, buckets)
    saved, layers = [], []
    for l in range(DEPTH):
        p, deps, mid = get_layer(l, x)
        x, p, s = _layer_fwd(x, p, bm, deps=deps, mid=mid)
        layers.append(p)
        saved.append(s)
    loss_blk, dx = _loss_fwd_bwd(x, target)
    smalls, dsas = [None] * DEPTH, [None] * DEPTH
    pipe, pipes = None, []
    for l in reversed(range(DEPTH)):
        if l > 0:
            dx, big, smalls[l], dsas[l], _, _ = _layer_bwd(dx, saved[l], layers[l], bm, pipe)
            pipe = make_pipe(l, BIG, "", big)
            pipes.append(pipe)
        else:
            dx, big, smalls[l], dsas[l], early, late = _layer_bwd(
                dx, saved[l], layers[l], bm, pipe, lambda big_: make_pipe(0, EARLY, "e", big_),
                lambda big_: make_pipe(0, LATE, "", big_))
    d_rel = _bias_grad(jnp.stack(dsas), buckets)[:, 0, :NBUCKET].T
    return loss_blk[0, 0], dx, smalls, d_rel, pipes, early, late


MESH = pl.DeviceIdType.MESH


def _me():
    return lax.axis_index("x"), lax.axis_index("y"), lax.axis_index("c")


def _other_chips(mx, my):
    return [(1 - mx, my), (mx, 1 - my), (1 - mx, 1 - my)]


def _rcopy(src, dst, send_sems, recv_sems, k, dev):
    return pltpu.make_async_remote_copy(src_ref=src, dst_ref=dst, send_sem=send_sems.at[k], recv_sem=recv_sems.at[k],
                                        device_id=dev, device_id_type=MESH)


def _comm_call(body, name, out_shape, n_in, n_sems):
    return pl.pallas_call(
        body, name=name, out_shape=out_shape, in_specs=[HBM_SPEC] * n_in,
        out_specs=jax.tree.map(lambda _: HBM_SPEC, out_shape),
        scratch_shapes=[pltpu.SemaphoreType.DMA((n_sems,)), pltpu.SemaphoreType.DMA((n_sems,)),
                        pltpu.SemaphoreType.DMA(())],
        compiler_params=pltpu.CompilerParams(has_side_effects=True))


def _all_gather_chips(x, name):
    def body(x_ref, o_ref, send_sems, recv_sems, local_sem):
        mx, my, mc = _me()
        local = pltpu.make_async_copy(x_ref, o_ref.at[2 * mx + my], local_sem)
        local.start()
        sends = [_rcopy(x_ref, o_ref.at[2 * mx + my], send_sems, recv_sems, k, (px, py, mc))
                 for k, (px, py) in enumerate(_other_chips(mx, my))]
        for cp in sends:
            cp.start()
        for k, (px, py) in enumerate(_other_chips(mx, my)):
            _rcopy(x_ref, o_ref.at[2 * px + py], send_sems, recv_sems, k, (px, py, mc)).wait_recv()
        for cp in sends:
            cp.wait_send()
        local.wait()

    return _comm_call(body, name, _sds((NCHIP,) + x.shape, x.dtype), 1, 3)(x)


EFFECT = pltpu.SideEffectType.DATAFLOW_SIDE_EFFECTING


def _hbm(a):
    return pltpu.with_memory_space_constraint(a, pltpu.HBM)


def _split_start(name, bufs, plan, n, after=None):
    nb = len(bufs)
    extra = [] if after is None else [after]
    ne = len(extra)

    def body(*refs):
        send_sems, recv_sems, token = refs[nb + ne], refs[nb + ne + 1], refs[-1]
        mx, my, mc = _me()
        for k, (src, dst, dev, _) in enumerate(plan(refs[:nb], mx, my, mc)):
            _rcopy(src, dst, send_sems, recv_sems, k, dev).start()
        token[...] = jnp.zeros_like(token)

    out = pl.pallas_call(
        body, name=name,
        out_shape=(pltpu.SemaphoreType.DMA((n,)), pltpu.SemaphoreType.DMA((n,)),
                   *[pltpu.HBM(b.shape, b.dtype) for b in bufs], _sds((8, 128))),
        in_specs=[HBM_SPEC] * nb + [ANY_SPEC] * ne,
        out_specs=(SEM_SPEC, SEM_SPEC, *[HBM_SPEC] * nb, pl.BlockSpec(memory_space=pltpu.VMEM)),
        input_output_aliases={i: 2 + i for i in range(nb)},
        compiler_params=pltpu.CompilerParams(has_side_effects=EFFECT))(*[_hbm(b) for b in bufs], *extra)
    return (out[0], out[1]), list(out[2:2 + nb]), out[-1]


def _split_wait(name, sems, bufs, plan, after):
    nb = len(bufs)

    def body(*refs):
        send_sems, recv_sems = refs[nb], refs[nb + 1]
        mx, my, mc = _me()
        for k, (src, dst, dev, land) in enumerate(plan(refs[:nb], mx, my, mc)):
            _rcopy(src, dst, send_sems, recv_sems, k, dev).wait_send()
            _rcopy(src, land, send_sems, recv_sems, k, dev).wait_recv()

    out = pl.pallas_call(
        body, name=name, out_shape=tuple(pltpu.HBM(b.shape, b.dtype) for b in bufs),
        in_specs=[HBM_SPEC] * nb + [SEM_SPEC, SEM_SPEC, ANY_SPEC], out_specs=(HBM_SPEC,) * nb,
        input_output_aliases={i: i for i in range(nb)},
        compiler_params=pltpu.CompilerParams(has_side_effects=EFFECT))(*bufs, sems[0], sems[1], after)
    return list(out)


def _plan_gather_chips(refs, mx, my, mc):
    me = 2 * mx + my
    return [(r.at[me, mc], r.at[me, mc], (px, py, mc), r.at[2 * px + py, mc])
            for r in refs for px, py in _other_chips(mx, my)]


def _plan_gather_pair(refs, mx, my, mc):
    return [(r.at[2 * px + py, mc], r.at[2 * px + py, mc], (mx, my, 1 - mc), r.at[2 * px + py, 1 - mc])
            for r in refs for px, py in _other_chips(mx, my)]


def _plan_gather_devices(refs, mx, my, mc):
    flip = lambda m, b: 1 - m if b else m
    peers = [(flip(mx, k >> 2 & 1), flip(my, k >> 1 & 1), flip(mc, k & 1)) for k in range(1, 8)]
    slot = lambda dev: 4 * dev[0] + 2 * dev[1] + dev[2]
    me = slot((mx, my, mc))
    return [(r.at[me], r.at[me], dev, r.at[slot(dev)]) for r in refs for dev in peers]


def _plan_pair_half(refs, mx, my, mc):
    n = len(refs) // 2
    return [(g.at[:, 1 - mc], r, (mx, my, 1 - mc), r) for g, r in zip(refs[:n], refs[n:])]


def _plan_scatter(refs, mx, my, mc):
    n = len(refs) // 2
    return [(q.at[2 * px + py], r.at[k], (px, py, mc), r.at[k])
            for q, r in zip(refs[:n], refs[n:]) for k, (px, py) in enumerate(_other_chips(mx, my))]


def _plan_pair_fill(refs, mx, my, mc):
    return [(r.at[mc], r.at[mc], (mx, my, 1 - mc), r.at[1 - mc]) for r in refs]


def _row_tile(rows, cols):
    t = 8
    while t * 2 * cols * 4 <= (1 << 20) and rows % (t * 2) == 0:
        t *= 2
    return t


def _prefetch_call(body, name, out_shape, grid, in_specs, out_specs):
    return pl.pallas_call(
        body, name=name, out_shape=out_shape,
        grid_spec=pltpu.PrefetchScalarGridSpec(num_scalar_prefetch=1, grid=grid, in_specs=in_specs,
                                               out_specs=out_specs),
        compiler_params=pltpu.CompilerParams(vmem_limit_bytes=VMEM_LIMIT,
                                             dimension_semantics=("parallel",) * len(grid)))


def _sum_half(g, r1, place, name):
    _, _, rr, ns = g.shape
    tr = _row_tile(rr, ns)

    def body(c_ref, g_ref, r_ref, o_ref, ob_ref):
        q = g_ref[...] + r_ref[...]
        ob_ref[...] = q.astype(BF16)

        @pl.when(pl.program_id(1) == c_ref[0])
        def _():
            o_ref[...] = q

    blk = pl.BlockSpec((None, tr, ns), lambda i, s, c: (s, i, 0))
    return pl.pallas_call(
        body, name=name, out_shape=(_sds((rr, ns)), _sds((NCHIP, rr, ns), BF16)),
        grid_spec=pltpu.PrefetchScalarGridSpec(
            num_scalar_prefetch=1, grid=(rr // tr, NCHIP),
            in_specs=[pl.BlockSpec((None, None, tr, ns), lambda i, s, c: (s, c[1], i, 0)), blk],
            out_specs=(pl.BlockSpec((tr, ns), lambda i, s, c: (i, 0)), blk)),
        compiler_params=pltpu.CompilerParams(vmem_limit_bytes=VMEM_LIMIT,
                                             dimension_semantics=("parallel", "arbitrary")))(place, g, r1)


def _sum_recv(q, r2, place, name):
    rr, ns = q.shape
    tr = _row_tile(rr, ns)

    def body(c_ref, q_ref, r_ref, o_ref):
        o_ref[...] = ((q_ref[...] + r_ref[0].astype(F32)) + r_ref[1].astype(F32)) + r_ref[2].astype(F32)

    return _prefetch_call(body, name, _sds((2, rr, ns)), (rr // tr,),
                          [pl.BlockSpec((tr, ns), lambda i, c: (i, 0)),
                           pl.BlockSpec((NCHIP - 1, tr, ns), lambda i, c: (0, i, 0))],
                          pl.BlockSpec((None, tr, ns), lambda i, c: (c[1], i, 0)))(place, q, r2)


def _sum_devices(v8):
    def body(v_ref, o_ref):
        acc = v_ref[0]
        for dev in range(1, 8):
            acc = acc + v_ref[dev]
        o_ref[...] = acc

    return _pcall(body, name="sum_devices", out_shape=_sds(v8.shape[1:]))(v8)


def _adamw(w, g, m, v, name):
    rows, cols = w.shape
    tr = _row_tile(rows, cols)

    def body(w_ref, g_ref, m_ref, v_ref, d_ref, m2_ref, v2_ref):
        g = g_ref[...]
        m2 = ADAM_B1 * m_ref[...] + (1.0 - ADAM_B1) * g
        v2 = ADAM_B2 * v_ref[...] + (1.0 - ADAM_B2) * (g * g)
        m_hat = m2 / (1.0 - ADAM_B1 ** ADAM_STEP)
        v_hat = v2 / (1.0 - ADAM_B2 ** ADAM_STEP)
        d_ref[...] = -ADAM_LR * (m_hat / (jnp.sqrt(v_hat) + ADAM_EPS) + ADAM_WD * w_ref[...])
        m2_ref[...] = m2
        v2_ref[...] = v2

    blk = pl.BlockSpec((tr, cols), lambda i: (i, 0))
    return _pcall(body, name=name, out_shape=(_sds((rows, cols)),) * 3, grid=(rows // tr,), in_specs=[blk] * 4,
                  out_specs=(blk,) * 3, semantics=("parallel",))(w, g, m, v)


BIG = ("w_in", "w_conv_out", "w_attn_out", "w_out", "w_ff1", "w_ff2")
SMALL = ("rel_bias", "norm1_g", "q_norm_g", "k_norm_g", "conv_dw_w", "conv_dw_b", "conv_ln_g", "conv_ln_b", "norm2_g")
WEIGHTS = ("rel_bias", "norm1_g", "w_in", "q_norm_g", "k_norm_g", "conv_dw_w", "conv_dw_b", "conv_ln_g", "conv_ln_b",
           "w_conv_out", "w_attn_out", "w_out", "norm2_g", "w_ff1", "w_ff2")


def _pack(arrays):
    flat = jnp.concatenate([a.reshape(-1) for a in arrays])
    n = flat.shape[0]
    rows = -(-n // 1024) * 8
    return jnp.pad(flat, (0, rows * 128 - n)).reshape(rows, 128)


def _unpack(packed, shapes):
    flat = packed.reshape(-1)
    out, off = [], 0
    for shp in shapes:
        n = int(np.prod(shp))
        out.append(flat[off:off + n].reshape(shp))
        off += n
    return out


def _adamw_layer(l, w, g, m, v, prev, name, deps=()):
    _, k, n = w.shape
    tr = _row_tile(k, n)
    deps = tuple(d for d in deps if d is not None)
    if prev is None:
        prev = tuple(lax.empty(w.shape, F32) for _ in range(4))

    def body(*refs):
        w_ref, g_ref, m_ref, v_ref = refs[:4]
        go_ref, d_ref, m2_ref, v2_ref = refs[-4:]
        g = g_ref[...]
        m2 = ADAM_B1 * m_ref[...] + (1.0 - ADAM_B1) * g
        v2 = ADAM_B2 * v_ref[...] + (1.0 - ADAM_B2) * (g * g)
        m_hat = m2 / (1.0 - ADAM_B1 ** ADAM_STEP)
        v_hat = v2 / (1.0 - ADAM_B2 ** ADAM_STEP)
        go_ref[...] = g
        d_ref[...] = -ADAM_LR * (m_hat / (jnp.sqrt(v_hat) + ADAM_EPS) + ADAM_WD * w_ref[...])
        m2_ref[...] = m2
        v2_ref[...] = v2

    lay = pl.BlockSpec((None, tr, n), lambda i: (l, i, 0))
    return _pcall(body, name=name, out_shape=(_sds(w.shape),) * 4, grid=(k // tr,),
                  in_specs=[lay, pl.BlockSpec((tr, n), lambda i: (i, 0)), lay, lay] + [ANY_SPEC] * (4 + len(deps)),
                  out_specs=(lay,) * 4, aliases={4: 0, 5: 1, 6: 2, 7: 3},
                  semantics=("parallel",))(w, g, m, v, *prev, *deps)


class _GradPipe:
    def __init__(self, l, kinds, tag, big, place, w, m, v, results):
        self.l, self.kinds, self.place, self.w, self.m, self.v, self.results = l, kinds, place, w, m, v, results
        self.id = f"l{l}{tag}"
        self.g = [big[n].reshape(NCHIP, 2, big[n].shape[1] // 2, big[n].shape[2]) for n in kinds]

    def step0(self):
        lands = [lax.empty((NCHIP,) + g.shape[2:], F32) for g in self.g]
        self.s1, self.b1, tok = _split_start(f"rs1_start_{self.id}", self.g + lands, _plan_pair_half, len(self.kinds))
        return tok

    def step1(self, after):
        nk = len(self.kinds)
        bufs = _split_wait(f"rs1_wait_{self.id}", self.s1, self.b1, _plan_pair_half, after)
        sums = [_sum_half(bufs[i], bufs[nk + i], self.place, f"rs1_sum_{n}") for i, n in enumerate(self.kinds)]
        self.q = [q for q, _ in sums]
        qb = [b for _, b in sums]
        lands = [lax.empty((NCHIP - 1,) + b.shape[1:], BF16) for b in qb]
        self.s2, self.b2, tok = _split_start(f"rs2_start_{self.id}", qb + lands, _plan_scatter, 3 * nk)
        return tok

    def step2(self, after):
        nk = len(self.kinds)
        bufs = _split_wait(f"rs2_wait_{self.id}", self.s2, self.b2, _plan_scatter, after)
        fin = [_sum_recv(self.q[i], bufs[nk + i], self.place, f"rs2_sum_{n}") for i, n in enumerate(self.kinds)]
        self.s3, self.b3, tok = _split_start(f"rs3_start_{self.id}", fin, _plan_pair_fill, nk)
        return tok

    def step3(self, after):
        self.fin = _split_wait(f"rs3_wait_{self.id}", self.s3, self.b3, _plan_pair_fill, after)

    def adam(self, deps=()):
        for i, n in enumerate(self.kinds):
            g2 = self.fin[i].reshape(self.fin[i].shape[1] * 2, self.fin[i].shape[2])
            self.results[n] = _adamw_layer(self.l, self.w[n], g2, self.m[n], self.v[n], self.results.get(n),
                                           f"adamw_{n}_l{self.l}", deps=deps if i == 0 else ())
        return self.results[self.kinds[-1]][1]


def kernel(x, rel_bias, norm1_g, w_in, q_norm_g, k_norm_g, conv_dw_w, conv_dw_b, conv_ln_g, conv_ln_b, w_conv_out, w_attn_out, w_out, norm2_g, w_ff1, w_ff2, loss_target, m_rel_bias, m_norm1_g, m_w_in, m_q_norm_g, m_k_norm_g, m_conv_dw_w, m_conv_dw_b, m_conv_ln_g, m_conv_ln_b, m_w_conv_out, m_w_attn_out, m_w_out, m_norm2_g, m_w_ff1, m_w_ff2, v_rel_bias, v_norm1_g, v_w_in, v_q_norm_g, v_k_norm_g, v_conv_dw_w, v_conv_dw_b, v_conv_ln_g, v_conv_ln_b, v_w_conv_out, v_w_attn_out, v_w_out, v_norm2_g, v_w_ff1, v_w_ff2):
    w = dict(rel_bias=rel_bias, norm1_g=norm1_g, w_in=w_in, q_norm_g=q_norm_g, k_norm_g=k_norm_g, conv_dw_w=conv_dw_w,
             conv_dw_b=conv_dw_b, conv_ln_g=conv_ln_g, conv_ln_b=conv_ln_b, w_conv_out=w_conv_out,
             w_attn_out=w_attn_out, w_out=w_out, norm2_g=norm2_g, w_ff1=w_ff1, w_ff2=w_ff2)
    m = dict(rel_bias=m_rel_bias, norm1_g=m_norm1_g, w_in=m_w_in, q_norm_g=m_q_norm_g, k_norm_g=m_k_norm_g,
             conv_dw_w=m_conv_dw_w, conv_dw_b=m_conv_dw_b, conv_ln_g=m_conv_ln_g, conv_ln_b=m_conv_ln_b,
             w_conv_out=m_w_conv_out, w_attn_out=m_w_attn_out, w_out=m_w_out, norm2_g=m_norm2_g, w_ff1=m_w_ff1,
             w_ff2=m_w_ff2)
    v = dict(rel_bias=v_rel_bias, norm1_g=v_norm1_g, w_in=v_w_in, q_norm_g=v_q_norm_g, k_norm_g=v_k_norm_g,
             conv_dw_w=v_conv_dw_w, conv_dw_b=v_conv_dw_b, conv_ln_g=v_conv_ln_g, conv_ln_b=v_conv_ln_b,
             w_conv_out=v_w_conv_out, w_attn_out=v_w_attn_out, w_out=v_w_out, norm2_g=v_norm2_g, w_ff1=v_w_ff1,
             w_ff2=v_w_ff2)
    chip_id = 2 * lax.axis_index("x") + lax.axis_index("y")
    place = jnp.stack([chip_id, lax.axis_index("c")]).astype(jnp.int32)

    dww4 = _all_gather_chips(conv_dw_w, "ag_conv_dw_w")
    dww = dww4.transpose(1, 2, 0, 3).reshape(DEPTH, KW, CONV)

    names = dict(w_in="win4", w_conv_out="wco4", w_attn_out="wao4", w_out="wout4", w_ff1="wff14", w_ff2="wff24")
    chips, pair = {}, {}

    def start_chips(key, l, kinds, after):
        lands = []
        for n in kinds:
            k, ns = w[n].shape[1:]
            land = lax.dynamic_update_slice(lax.empty((NCHIP, k, ns), BF16), w[n][l].astype(BF16)[None], (chip_id, 0, 0))
            lands.append(land.reshape(NCHIP, 2, k // 2, ns))
        chips[key] = _split_start(f"ag_chips_start_{key}", lands, _plan_gather_chips, 3 * len(kinds), after=after)
        return chips[key][2]

    def start_pair(key, after):
        sems, bufs, _ = chips[key]
        bufs = _split_wait(f"ag_chips_wait_{key}", sems, bufs, _plan_gather_chips, after)
        pair[key] = _split_start(f"ag_pair_start_{key}", bufs, _plan_gather_pair, len(bufs) * 3)
        return pair[key][2]

    def landed(key, kinds, after):
        sems, bufs, _ = pair[key]
        bufs = _split_wait(f"ag_pair_wait_{key}", sems, bufs, _plan_gather_pair, after)
        return {names[n]: b.reshape(NCHIP, 2 * b.shape[2], b.shape[3]) for n, b in zip(kinds, bufs)}

    first, rest = ("w_in",), tuple(n for n in BIG if n != "w_in")
    start_chips("l0b", 0, rest, start_pair("l0a", start_chips("l0a", 0, first, dww4)))

    def get_layer(l, after):
        p = dict(dww=dww[l], dwb=conv_dw_b[l][None], lng=conv_ln_g[l][None], lnb=conv_ln_b[l][None],
                 n1g=norm1_g[l][None], n2g=norm2_g[l][None], gq=jnp.tile(q_norm_g[l], 2)[None],
                 gk=jnp.tile(k_norm_g[l], 2)[None])
        p.update(landed(f"l{l}a", first, chips["l0b"][2] if l == 0 else after))
        more = l + 1 < DEPTH

        def hook(after_):
            tok = start_pair(f"l{l}b", after_)
            return start_chips(f"l{l + 1}a", l + 1, first, tok) if more else tok

        p["hook"] = hook
        p["rest"] = lambda after_: landed(f"l{l}b", rest, after_)
        mid = (lambda after_: start_chips(f"l{l + 1}b", l + 1, rest, start_pair(f"l{l + 1}a", after_))) if more else None
        return p, (), mid

    results = {}
    make_pipe = lambda l, kinds, tag, big: _GradPipe(l, kinds, tag, big, place, w, m, v, results)
    loss_share, dx, smalls, d_rel, pipes, early, late = _local_step(x[0], loss_target[0], get_layer, rel_bias,
                                                                    make_pipe)
    loss = lax.psum(loss_share, ("x", "y", "c"))

    local_small = dict(rel_bias=d_rel)
    for n in SMALL[1:]:
        local_small[n] = jnp.stack([smalls[l][n] for l in range(DEPTH)])
    small_shapes = [local_small[n].shape for n in SMALL]
    mine = _pack([local_small[n] for n in SMALL])
    slot = 4 * lax.axis_index("x") + 2 * lax.axis_index("y") + lax.axis_index("c")
    land = lax.dynamic_update_slice(lax.empty((8,) + mine.shape, F32), mine[None], (slot, 0, 0))
    small_sems, small_bufs, tok = _split_start("ag_small_start", [land], _plan_gather_devices, 7)
    for pipe in pipes:
        done = pipe.adam(deps=(tok,))
        tok = None
    early.step3(late.step2(done))
    done = early.adam()
    late.step3(done)
    done = late.adam()
    gathered = _split_wait("ag_small_wait", small_sems, small_bufs, _plan_gather_devices, done)[0]
    summed = _sum_devices(gathered)
    grads = dict(zip(SMALL, _unpack(summed, small_shapes)))
    grads["conv_dw_w"] = lax.dynamic_slice_in_dim(grads["conv_dw_w"], chip_id * 128, 128, axis=2)

    delta, new_m, new_v = {}, {}, {}
    small_w_shapes = [w[n].shape for n in SMALL]
    outs = _adamw(_pack([w[n] for n in SMALL]), _pack([grads[n] for n in SMALL]), _pack([m[n] for n in SMALL]),
                  _pack([v[n] for n in SMALL]), "adamw_small")
    for dst, packed in zip((delta, new_m, new_v), outs):
        dst.update(zip(SMALL, _unpack(packed, small_w_shapes)))

    for n in BIG:
        grads[n], delta[n], new_m[n], new_v[n] = results[n]

    return (loss, dx[None], *[grads[n] for n in WEIGHTS], *[delta[n] for n in WEIGHTS],
            *[new_m[n] for n in WEIGHTS], *[new_v[n] for n in WEIGHTS])
```

```python
import functools
import math

import numpy as np
import jax
import jax.numpy as jnp
from jax import lax
from jax.experimental import pallas as pl
from jax.experimental.pallas import tpu as pltpu

F32 = jnp.float32
BF16 = jnp.bfloat16

T = 2048
D = 1024
DEPTH = 4
CONV = 512
KW = 31
NG = 3
HD = 64
AOUT = 512
DFF = 4096
INC = 7680
DIL = (1, 4, 16)
BLK = 128
NBUCKET = 32
EPS = 1e-6
NEG = -1e30
NCHIP = 4
UB_A, UB_GT, UB_Q, UB_K, UB_V, UB_GC, UB_GA = 0, 1, 2, 5, 8, 11, 13

ADAM_LR, ADAM_B1, ADAM_B2, ADAM_EPS, ADAM_WD, ADAM_STEP = 0.001, 0.9, 0.999, 1e-08, 0.01, 10

VMEM_LIMIT = 48 * 1024 * 1024
TB = 512
HBM_SPEC = pl.BlockSpec(memory_space=pltpu.HBM)
ANY_SPEC = pl.BlockSpec(memory_space=pl.ANY)
SEM_SPEC = pl.BlockSpec(memory_space=pltpu.SEMAPHORE)


def _pcall(body, *, name, out_shape, grid=(), in_specs=None, out_specs=None, scratch=(), aliases=None,
           semantics=None):
    kw = {}
    if in_specs is not None:
        kw["in_specs"] = in_specs
    if out_specs is not None:
        kw["out_specs"] = out_specs
    return pl.pallas_call(
        body, name=name, out_shape=out_shape, grid=grid, scratch_shapes=scratch,
        input_output_aliases=aliases or {},
        compiler_params=pltpu.CompilerParams(vmem_limit_bytes=VMEM_LIMIT, dimension_semantics=semantics),
        **kw)


def _sds(shape, dtype=F32):
    return jax.ShapeDtypeStruct(shape, dtype)


NN = (((1,), (0,)), ((), ()))
NT = (((1,), (1,)), ((), ()))
TN = (((0,), (0,)), ((), ()))


def _mm(name, a, b, *, out_shape, out_dtype, grid, a_spec, b_spec, o_spec, acc_shape, dims, add=None,
        add_spec=None, deps=()):
    nk = grid[2]
    deps = tuple(d for d in deps if d is not None)
    n_scratch = 1 if nk > 1 else 0

    def body(*refs):
        n_out = 1 + n_scratch
        refs = refs[:len(refs) - n_out - len(deps)] + refs[len(refs) - n_out:]
        a_ref, b_ref = refs[0], refs[1]
        r_ref = refs[2] if add is not None else None
        o_ref = refs[-n_out]
        prod = lax.dot_general(a_ref[...].astype(BF16), b_ref[...].astype(BF16), dims, preferred_element_type=F32)
        if nk == 1:
            o_ref[...] = (prod if r_ref is None else prod + r_ref[...]).astype(out_dtype)
            return
        acc_ref = refs[-1]
        k = pl.program_id(2)

        @pl.when(k == 0)
        def _():
            acc_ref[...] = prod

        @pl.when(k > 0)
        def _():
            acc_ref[...] += prod

        @pl.when(k == nk - 1)
        def _():
            res = acc_ref[...]
            if r_ref is not None:
                res = res + r_ref[...]
            o_ref[...] = res.astype(out_dtype)

    ins = ([a, b] if add is None else [a, b, add]) + list(deps)
    specs = ([a_spec, b_spec] if add is None else [a_spec, b_spec, add_spec]) + [ANY_SPEC] * len(deps)
    return _pcall(body, name=name, out_shape=_sds(out_shape, out_dtype), grid=grid, in_specs=specs,
                  out_specs=o_spec, scratch=[pltpu.VMEM(acc_shape, F32)] * n_scratch,
                  semantics=("parallel", "parallel", "arbitrary"))(*ins)


def _mm_x_wcols(name, a, w4, *, tm, tn, out_dtype=F32, deps=()):
    _, k, ns = w4.shape
    nj = ns // tn
    return _mm(name, a, w4, out_shape=(T, NCHIP * ns), out_dtype=out_dtype, grid=(T // tm, NCHIP * nj, 1), deps=deps,
               a_spec=pl.BlockSpec((tm, k), lambda i, j, kk: (i, 0)),
               b_spec=pl.BlockSpec((None, k, tn), lambda i, j, kk: (j // nj, 0, j % nj)),
               o_spec=pl.BlockSpec((tm, tn), lambda i, j, kk: (i, j)), acc_shape=(tm, tn), dims=NN)


def _mm_ff1(a, w4, *, tm, tn):
    _, k, ns = w4.shape
    nj = ns // tn

    def body(a_ref, b_ref, f_ref, r_ref):
        p = jnp.maximum(jnp.dot(a_ref[...], b_ref[...], preferred_element_type=F32), 0.0)
        f_ref[...] = p.astype(BF16)
        r_ref[...] = (p * p).astype(BF16)

    out = pl.BlockSpec((tm, tn), lambda i, j: (i, j))
    return _pcall(body, name="mm_f", out_shape=(_sds((T, DFF), BF16), _sds((T, DFF), BF16)), grid=(T // tm, NCHIP * nj),
                  in_specs=[pl.BlockSpec((tm, k), lambda i, j: (i, 0)),
                            pl.BlockSpec((None, k, tn), lambda i, j: (j // nj, 0, j % nj))],
                  out_specs=(out, out), semantics=("parallel", "parallel"))(a, w4)


def _mm_x_wrows(name, a, w4, add, *, tm, tk, tn, deps=()):
    _, ks, n = w4.shape
    nkk = ks // tk
    return _mm(name, a, w4, out_shape=(T, n), out_dtype=F32, grid=(T // tm, n // tn, NCHIP * nkk), deps=deps,
               a_spec=pl.BlockSpec((tm, tk), lambda i, j, kk: (i, kk)),
               b_spec=pl.BlockSpec((None, tk, tn), lambda i, j, kk: (kk // nkk, kk % nkk, j)),
               o_spec=pl.BlockSpec((tm, tn), lambda i, j, kk: (i, j)), acc_shape=(tm, tn), dims=NN,
               add=add, add_spec=pl.BlockSpec((tm, tn), lambda i, j, kk: (i, j)))


def _mm_g_wcols_t(name, g, w4, *, tm, tk, tn, out_dtype=F32, deps=()):
    _, k, ns = w4.shape
    nkk = ns // tk
    return _mm(name, g, w4, out_shape=(T, k), out_dtype=out_dtype, grid=(T // tm, k // tn, NCHIP * nkk), deps=deps,
               a_spec=pl.BlockSpec((tm, tk), lambda i, j, kk: (i, kk)),
               b_spec=pl.BlockSpec((None, tn, tk), lambda i, j, kk: (kk // nkk, j, kk % nkk)),
               o_spec=pl.BlockSpec((tm, tn), lambda i, j, kk: (i, j)), acc_shape=(tm, tn), dims=NT)


def _mm_g_wrows_t(name, g, w4, *, tm, tn, out_dtype=F32, deps=()):
    _, ks, n = w4.shape
    nj = ks // tn
    return _mm(name, g, w4, out_shape=(T, NCHIP * ks), out_dtype=out_dtype, grid=(T // tm, NCHIP * nj, 1), deps=deps,
               a_spec=pl.BlockSpec((tm, n), lambda i, j, kk: (i, 0)),
               b_spec=pl.BlockSpec((None, tn, n), lambda i, j, kk: (j // nj, j % nj, 0)),
               o_spec=pl.BlockSpec((tm, tn), lambda i, j, kk: (i, j)), acc_shape=(tm, tn), dims=NT)


def _mm_dff2(dx, w4, fa, *, tm, tn, deps=()):
    _, ks, n = w4.shape
    nj = ks // tn
    deps = tuple(d for d in deps if d is not None)

    def body(*refs):
        dx_ref, b_ref, f_ref = refs[:3]
        df_ref = refs[-1]
        dr = lax.dot_general(dx_ref[...].astype(BF16), b_ref[...], NT, preferred_element_type=F32)
        df_ref[...] = (dr * (2.0 * f_ref[...].astype(F32))).astype(BF16)

    out = pl.BlockSpec((tm, tn), lambda i, j: (i, j))
    return _pcall(body, name="mm_dr", out_shape=_sds((T, DFF), BF16), grid=(T // tm, NCHIP * nj),
                  in_specs=[pl.BlockSpec((tm, n), lambda i, j: (i, 0)),
                            pl.BlockSpec((None, tn, n), lambda i, j: (j // nj, j % nj, 0)), out]
                  + [ANY_SPEC] * len(deps),
                  out_specs=out, semantics=("parallel", "parallel"))(dx, w4, fa, *deps)


TCH = 512


def _mm_dw(name, a, g, *, out_shape, out_map, tm, tn, deps=()):
    deps = tuple(d for d in deps if d is not None)

    def body(*refs):
        a_ref, g_ref = refs[:2]
        o_ref, at_ref = refs[-2:]

        @pl.when(pl.program_id(1) == 0)
        def _():
            for c in range(T // TCH):
                at_ref[:, c * TCH:(c + 1) * TCH] = a_ref[c * TCH:(c + 1) * TCH, :].T

        o_ref[...] = jnp.dot(at_ref[...], g_ref[...].astype(BF16), preferred_element_type=F32)

    return _pcall(body, name=name, out_shape=_sds(out_shape), grid=(a.shape[1] // tm, g.shape[1] // tn),
                  in_specs=[pl.BlockSpec((T, tm), lambda i, j: (0, i)), pl.BlockSpec((T, tn), lambda i, j: (0, j))]
                  + [ANY_SPEC] * len(deps),
                  out_specs=pl.BlockSpec((None, tm, tn), out_map), scratch=[pltpu.VMEM((tm, T), BF16)],
                  semantics=("parallel", "arbitrary"))(a, g, *deps)


def _mm_dw_cols(name, a, g, *, ns, tm, tn, deps=()):
    nj = ns // tn
    return _mm_dw(name, a, g, out_shape=(NCHIP, a.shape[1], ns), out_map=lambda i, j: (j // nj, i, j % nj),
                  tm=tm, tn=tn, deps=deps)


def _mm_dw_rows(name, a, g, *, ks, tm, tn):
    ni = ks // tm
    return _mm_dw(name, a, g, out_shape=(NCHIP, ks, g.shape[1]), out_map=lambda i, j: (i // ni, i % ni, j),
                  tm=tm, tn=tn)


def _row_spec(width, col=0):
    return pl.BlockSpec((TB, width), lambda i: (i, col))


def _vec_spec(width):
    return pl.BlockSpec((1, width), lambda i: (0, 0))


def _rms_fwd(x, g):
    def body(x_ref, g_ref, h_ref):
        x = x_ref[...]
        r = lax.rsqrt(jnp.mean(x * x, axis=-1, keepdims=True) + EPS)
        h_ref[...] = (x * r * g_ref[...]).astype(BF16)

    return _pcall(body, name="rms_fwd", out_shape=_sds((T, D), BF16), grid=(T // TB,),
                  in_specs=[_row_spec(D), _vec_spec(D)], out_specs=_row_spec(D), semantics=("parallel",))(x, g)


def _rms_bwd(x, g, dh, dres, deps=()):
    deps = tuple(d for d in deps if d is not None)

    def body(*refs):
        x_ref, g_ref, dh_ref, dres_ref = refs[:4]
        dx_ref, dg_ref = refs[-2:]
        x = x_ref[...]
        r = lax.rsqrt(jnp.mean(x * x, axis=-1, keepdims=True) + EPS)
        y = x * r
        dh = dh_ref[...]
        dy = dh * g_ref[...]
        dx_ref[...] = dres_ref[...] + r * (dy - y * jnp.mean(dy * y, axis=-1, keepdims=True))

        @pl.when(pl.program_id(0) == 0)
        def _():
            dg_ref[...] = jnp.zeros_like(dg_ref)

        dg_ref[...] += jnp.sum(dh * y, axis=0, keepdims=True)

    return _pcall(body, name="rms_bwd", out_shape=(_sds((T, D)), _sds((1, D))), grid=(T // TB,),
                  in_specs=[_row_spec(D), _vec_spec(D), _row_spec(D), _row_spec(D)] + [ANY_SPEC] * len(deps),
                  out_specs=(_row_spec(D), _vec_spec(D)), semantics=("arbitrary",))(x, g, dh, dres, *deps)


def _sigmoid(x):
    return 1.0 / (1.0 + jnp.exp(-x))


def _gate_fwd(u, ycv, yat):
    def body(gc_ref, ga_ref, yc_ref, ya_ref, m_ref):
        m_ref[...] = (_sigmoid(gc_ref[...].astype(F32)) * yc_ref[...]
                      + _sigmoid(ga_ref[...].astype(F32)) * ya_ref[...]).astype(BF16)

    blk = lambda off: pl.BlockSpec((TB, 512), lambda i, j: (i, off + j))
    return _pcall(body, name="gate_fwd", out_shape=_sds((T, D), BF16), grid=(T // TB, 2),
                  in_specs=[blk(UB_GC), blk(UB_GA), blk(0), blk(0)], out_specs=blk(0),
                  semantics=("parallel", "parallel"))(u, u, ycv, yat)


def _gate_bwd(u, ycv, yat, dm):
    def body(gc_ref, ga_ref, yc_ref, ya_ref, dm_ref, dyc_ref, dya_ref, dgc_ref, dga_ref):
        dm = dm_ref[...]
        sc = _sigmoid(gc_ref[...].astype(F32))
        sa = _sigmoid(ga_ref[...].astype(F32))
        dyc_ref[...] = (dm * sc).astype(BF16)
        dya_ref[...] = (dm * sa).astype(BF16)
        dgc_ref[...] = (dm * yc_ref[...] * sc * (1.0 - sc)).astype(BF16)
        dga_ref[...] = (dm * ya_ref[...] * sa * (1.0 - sa)).astype(BF16)

    blk = lambda off: pl.BlockSpec((TB, 512), lambda i, j: (i, off + j))
    return _pcall(body, name="gate_bwd",
                  out_shape=(_sds((T, D), BF16), _sds((T, D), BF16), _sds((T, D), BF16), _sds((T, D), BF16)),
                  grid=(T // TB, 2), in_specs=[blk(UB_GC), blk(UB_GA), blk(0), blk(0), blk(0)],
                  out_specs=(blk(0), blk(0), blk(0), blk(0)),
                  semantics=("parallel", "parallel"))(u, u, ycv, yat, dm)


def _loss_fwd_bwd(y, target):
    def body(y_ref, t_ref, loss_ref, dy_ref):
        e = y_ref[...] - t_ref[...]
        dy_ref[...] = e * (1.0 / D)

        @pl.when(pl.program_id(0) == 0)
        def _():
            loss_ref[...] = jnp.zeros_like(loss_ref)

        loss_ref[...] += 0.5 * jnp.sum(jnp.mean(e * e, axis=-1, keepdims=True))

    return _pcall(body, name="loss", out_shape=(_sds((8, 128)), _sds((T, D))), grid=(T // TB,),
                  in_specs=[_row_spec(D), _row_spec(D)],
                  out_specs=(pl.BlockSpec((8, 128), lambda i: (0, 0)), _row_spec(D)),
                  semantics=("arbitrary",))(y, target)


PAD = 32
CCH = 256


def _conv_fwd(u, dw_w, dw_b):
    def body(a_ref, gt_ref, w_ref, b_ref, z1_ref, zp_ref):
        zp_ref[0:PAD, :] = jnp.zeros((PAD, 128), F32)
        zp_ref[PAD:PAD + T, :] = a_ref[...].astype(F32) * _sigmoid(gt_ref[...].astype(F32))
        for c in range(T // CCH):
            acc = jnp.broadcast_to(b_ref[...], (CCH, 128))
            for j in range(KW):
                acc = acc + w_ref[j:j + 1, :] * zp_ref[pl.ds(c * CCH + j + PAD - (KW - 1), CCH), :]
            z1_ref[c * CCH:(c + 1) * CCH, :] = acc

    col = lambda off: pl.BlockSpec((T, 128), lambda j: (0, off * 4 + j))
    return _pcall(body, name="conv_fwd", out_shape=_sds((T, CONV)), grid=(CONV // 128,),
                  in_specs=[col(UB_A), col(UB_GT), pl.BlockSpec((KW, 128), lambda j: (0, j)),
                            pl.BlockSpec((1, 128), lambda j: (0, j))],
                  out_specs=col(0), scratch=[pltpu.VMEM((T + PAD, 128), F32)],
                  semantics=("parallel",))(u, u, dw_w, dw_b)


def _ln_silu_fwd(z1, g, b):
    def body(z_ref, g_ref, b_ref, o_ref):
        z = z_ref[...]
        mu = jnp.mean(z, axis=-1, keepdims=True)
        zc = z - mu
        zh = zc * lax.rsqrt(jnp.mean(zc * zc, axis=-1, keepdims=True) + EPS)
        z2 = zh * g_ref[...] + b_ref[...]
        o_ref[...] = (z2 * _sigmoid(z2)).astype(BF16)

    return _pcall(body, name="ln_silu_fwd", out_shape=_sds((T, CONV), BF16), grid=(T // TB,),
                  in_specs=[_row_spec(CONV), _vec_spec(CONV), _vec_spec(CONV)], out_specs=_row_spec(CONV),
                  semantics=("parallel",))(z1, g, b)


def _ln_silu_bwd(z1, g, b, dz3):
    def body(z_ref, g_ref, b_ref, d_ref, z3_ref, dz1_ref, dg_ref, db_ref):
        z = z_ref[...]
        mu = jnp.mean(z, axis=-1, keepdims=True)
        zc = z - mu
        rs = lax.rsqrt(jnp.mean(zc * zc, axis=-1, keepdims=True) + EPS)
        zh = zc * rs
        z2 = zh * g_ref[...] + b_ref[...]
        s = _sigmoid(z2)
        z3_ref[...] = (z2 * s).astype(BF16)
        dz2 = d_ref[...] * (s * (1.0 + z2 * (1.0 - s)))
        dzh = dz2 * g_ref[...]
        dz1_ref[...] = rs * (dzh - jnp.mean(dzh, axis=-1, keepdims=True)
                             - zh * jnp.mean(dzh * zh, axis=-1, keepdims=True))

        @pl.when(pl.program_id(0) == 0)
        def _():
            dg_ref[...] = jnp.zeros_like(dg_ref)
            db_ref[...] = jnp.zeros_like(db_ref)

        dg_ref[...] += jnp.sum(dz2 * zh, axis=0, keepdims=True)
        db_ref[...] += jnp.sum(dz2, axis=0, keepdims=True)

    return _pcall(body, name="ln_silu_bwd",
                  out_shape=(_sds((T, CONV), BF16), _sds((T, CONV)), _sds((1, CONV)), _sds((1, CONV))),
                  grid=(T // TB,),
                  in_specs=[_row_spec(CONV), _vec_spec(CONV), _vec_spec(CONV), _row_spec(CONV)],
                  out_specs=(_row_spec(CONV), _row_spec(CONV), _vec_spec(CONV), _vec_spec(CONV)),
                  semantics=("arbitrary",))(z1, g, b, dz3)


def _conv_bwd(u, dw_w, dz1):
    def body(a_ref, gt_ref, w_ref, dz1_ref, da_ref, dgt_ref, dw_ref, db_ref, zp_ref, dp_ref):
        a = a_ref[...].astype(F32)
        s = _sigmoid(gt_ref[...].astype(F32))
        zp_ref[0:PAD, :] = jnp.zeros((PAD, 128), F32)
        zp_ref[PAD:PAD + T, :] = a * s
        dp_ref[0:T, :] = dz1_ref[...]
        dp_ref[T:T + PAD, :] = jnp.zeros((PAD, 128), F32)
        db_ref[...] = jnp.sum(dz1_ref[...], axis=0, keepdims=True)
        for j in range(KW):
            tot = jnp.zeros((1, 128), F32)
            for c in range(T // CCH):
                tot = tot + jnp.sum(dz1_ref[c * CCH:(c + 1) * CCH, :]
                                    * zp_ref[pl.ds(c * CCH + j + PAD - (KW - 1), CCH), :], axis=0, keepdims=True)
            dw_ref[j:j + 1, :] = tot
        for c in range(T // CCH):
            acc = jnp.zeros((CCH, 128), F32)
            for j in range(KW):
                acc = acc + w_ref[j:j + 1, :] * dp_ref[pl.ds(c * CCH + (KW - 1) - j, CCH), :]
            rows = slice(c * CCH, (c + 1) * CCH)
            sc = _sigmoid(gt_ref[rows, :].astype(F32))
            da_ref[rows, :] = (acc * sc).astype(BF16)
            dgt_ref[rows, :] = (acc * a_ref[rows, :].astype(F32) * sc * (1.0 - sc)).astype(BF16)

    col = lambda off: pl.BlockSpec((T, 128), lambda j: (0, off * 4 + j))
    wspec = pl.BlockSpec((KW, 128), lambda j: (0, j))
    return _pcall(body, name="conv_bwd",
                  out_shape=(_sds((T, CONV), BF16), _sds((T, CONV), BF16), _sds((KW, CONV)), _sds((1, CONV))),
                  grid=(CONV // 128,), in_specs=[col(UB_A), col(UB_GT), wspec, col(0)],
                  out_specs=(col(0), col(0), wspec, pl.BlockSpec((1, 128), lambda j: (0, j))),
                  scratch=[pltpu.VMEM((T + PAD, 128), F32), pltpu.VMEM((T + PAD, 128), F32)],
                  semantics=("parallel",))(u, u, dw_w, dz1)


def _bucket_tables():
    qi = np.arange(BLK)[:, None]
    kj = np.arange(2 * BLK)[None, :]
    off = np.clip(qi + BLK - kj, 0, BLK)
    out = []
    for d in DIL:
        dist = (off * d).astype(np.int32)
        nf = np.maximum(dist, 1).astype(np.float32)
        large = 16 + (np.log(nf / np.float32(16)) / np.float32(math.log(2048 / 16)) * np.float32(16)).astype(np.int32)
        large = np.minimum(large, NBUCKET - 1)
        out.append(np.where(dist < 16, dist, large))
    return np.stack(out).astype(np.int32)


def _band():
    off = lax.broadcasted_iota(jnp.int32, (BLK, 2 * BLK), 0) + BLK - lax.broadcasted_iota(jnp.int32, (BLK, 2 * BLK), 1)
    return (off >= 0) & (off <= BLK)


def _bias_table(rel_bias_t, buckets):
    def body(rb_ref, bk_ref, o_ref):
        h = pl.program_id(0)
        bk = bk_ref[...]
        acc = jnp.zeros((BLK, 2 * BLK), F32)
        for b in range(NBUCKET):
            acc = jnp.where(bk == b, rb_ref[h, b], acc)
        o_ref[...] = jnp.where(_band(), acc, NEG)

    return _pcall(body, name="bias_table", out_shape=_sds((3 * 8, BLK, 2 * BLK)), grid=(24,),
                  in_specs=[pl.BlockSpec(memory_space=pltpu.SMEM),
                            pl.BlockSpec((None, BLK, 2 * BLK), lambda h: (h // 8, 0, 0))],
                  out_specs=pl.BlockSpec((None, BLK, 2 * BLK), lambda h: (h, 0, 0)),
                  semantics=("parallel",))(rel_bias_t, buckets)


def _bias_grad(ds_acc, buckets):
    def body(a_ref, bk_ref, o_ref):
        acc = a_ref[0]
        for l in range(1, DEPTH):
            acc = acc + a_ref[l]
        bk = bk_ref[...]
        lane = lax.broadcasted_iota(jnp.int32, (1, 128), 1)
        row = jnp.zeros((1, 128), F32)
        for b in range(NBUCKET):
            row = jnp.where(lane == b, jnp.sum(jnp.where(bk == b, acc, 0.0)), row)
        o_ref[...] = row

    return _pcall(body, name="bias_grad", out_shape=_sds((24, 1, 128)), grid=(24,),
                  in_specs=[pl.BlockSpec((DEPTH, None, BLK, 2 * BLK), lambda h: (0, h, 0, 0)),
                            pl.BlockSpec((None, BLK, 2 * BLK), lambda h: (h // 8, 0, 0))],
                  out_specs=pl.BlockSpec((None, 1, 128), lambda h: (h, 0, 0)),
                  semantics=("parallel",))(ds_acc, buckets)


def _head_mask():
    return lax.broadcasted_iota(jnp.int32, (1, 128), 1) < HD


def _seg_ones(width):
    r = lax.broadcasted_iota(jnp.int32, (width, width), 0) >> 6
    c = lax.broadcasted_iota(jnp.int32, (width, width), 1) >> 6
    return (r == c).astype(BF16)


def _seg_sum(x, ones):
    hi = x.astype(BF16)
    lo = (x - hi.astype(F32)).astype(BF16)
    return (jnp.dot(hi, ones, preferred_element_type=F32) + jnp.dot(lo, ones, preferred_element_type=F32))


def _dot(a, b, dims):
    return lax.dot_general(a, b, dims, preferred_element_type=F32)


def _tile_rows(d, r, n):
    stride = None if d == 1 else d
    q_rows = pl.ds(r + d * n * BLK, BLK, stride=stride)
    if n == 0:
        return q_rows, q_rows, BLK
    return q_rows, pl.ds(r + d * (n - 1) * BLK, 2 * BLK, stride=stride), 2 * BLK


def _stack_heads(x, m_a):
    return jnp.concatenate([jnp.where(m_a, x, 0.0), jnp.where(m_a, 0.0, x)], axis=0)


def _stack_rows(x, m_a, width):
    other = pltpu.roll(x, HD, axis=1)
    both = jnp.concatenate([jnp.where(m_a, x, other), jnp.where(m_a, other, x)], axis=0)
    return both if width == 128 else jnp.concatenate([both] * (width // 128), axis=1)


NCH = 256


def _qk_norm_prep(q_ref, k_ref, v_ref, gq_ref, gk_ref, qn_ref, kn_ref, vn_ref, ones):
    def prep(i, carry):
        rows = pl.ds(pl.multiple_of(i * NCH, NCH), NCH)
        q = q_ref[rows, :].astype(F32)
        qn_ref[rows, :] = q * lax.rsqrt(_seg_sum(q * q, ones) * (1.0 / HD) + EPS) * gq_ref[...] * (HD ** -0.5)
        k = k_ref[rows, :].astype(F32)
        kn_ref[rows, :] = k * lax.rsqrt(_seg_sum(k * k, ones) * (1.0 / HD) + EPS) * gk_ref[...]
        vn_ref[rows, :] = v_ref[rows, :].astype(F32)
        return carry

    lax.fori_loop(0, T // NCH, prep, 0)


def _attn_specs(g):
    ucol = lambda base: pl.BlockSpec((T, 128), lambda hp: (0, (base + g) * 4 + hp))
    col = pl.BlockSpec((T, 128), lambda hp: (0, hp))
    vec = pl.BlockSpec((1, 128), lambda hp: (0, 0))
    bm = pl.BlockSpec((2, BLK, 2 * BLK), lambda hp: (g * 4 + hp, 0, 0))
    return ucol, col, vec, bm


def _attn_fwd(g, u, gq, gk, bm, deps=()):
    d = DIL[g]

    def body(*refs):
        q_ref, k_ref, v_ref, gq_ref, gk_ref, bm_ref = refs[:6]
        o_ref, lse_ref, qn_ref, kn_ref, vn_ref = refs[-5:]
        ones = _seg_ones(128)
        _qk_norm_prep(q_ref, k_ref, v_ref, gq_ref, gk_ref, qn_ref, kn_ref, vn_ref, ones)
        m_a = _head_mask()
        for r in range(d):
            for n in range(T // d // BLK):
                q_rows, k_rows, nk = _tile_rows(d, r, n)
                qt = qn_ref[q_rows, :]
                kt = kn_ref[k_rows, :].astype(BF16)
                vt = vn_ref[k_rows, :].astype(BF16)
                q2 = _stack_heads(qt, m_a).astype(BF16)
                s = _dot(q2, kt, NT) + bm_ref[...].reshape(2 * BLK, 2 * BLK)[:, 2 * BLK - nk:]
                mx = jnp.max(s, axis=1, keepdims=True)
                p = jnp.exp(s - mx)
                l = jnp.sum(p, axis=1, keepdims=True)
                o2 = _dot(p.astype(BF16), vt, NN) / l
                lse2 = jnp.broadcast_to(mx + jnp.log(l), (2 * BLK, 128))
                o_ref[q_rows, :] = jnp.where(m_a, o2[:BLK], o2[BLK:])
                lse_ref[q_rows, :] = jnp.where(m_a, lse2[:BLK], lse2[BLK:])

    ucol, col, vec, bmspec = _attn_specs(g)
    return _pcall(body, name=f"attn_fwd_g{g}", out_shape=(_sds((T, AOUT)), _sds((T, AOUT))), grid=(4,),
                  in_specs=[ucol(UB_Q), ucol(UB_K), ucol(UB_V), vec, vec, bmspec] + [ANY_SPEC] * len(deps),
                  out_specs=(col, col), scratch=[pltpu.VMEM((T, 128), F32)] * 3,
                  semantics=("parallel",))(u, u, u, gq, gk, bm, *deps)


def _attn_bwd(g, u, gq, gk, bm, dog, cb, lse):
    d = DIL[g]

    def body(q_ref, k_ref, v_ref, gq_ref, gk_ref, bm_ref, do_ref, cb_ref, lse_ref,
             dqo_ref, dko_ref, dvo_ref, dgq_ref, dgk_ref, dsa_ref, qn_ref, kn_ref, vn_ref, dq_ref, dk_ref, dv_ref):
        ones = _seg_ones(128)
        _qk_norm_prep(q_ref, k_ref, v_ref, gq_ref, gk_ref, qn_ref, kn_ref, vn_ref, ones)
        m_a = _head_mask()
        dk_ref[...] = jnp.zeros_like(dk_ref)
        dv_ref[...] = jnp.zeros_like(dv_ref)
        dsa_ref[...] = jnp.zeros_like(dsa_ref)
        for r in range(d):
            for n in range(T // d // BLK):
                q_rows, k_rows, nk = _tile_rows(d, r, n)
                ktb = kn_ref[k_rows, :].astype(BF16)
                vtb = vn_ref[k_rows, :].astype(BF16)
                q2 = _stack_heads(qn_ref[q_rows, :], m_a).astype(BF16)
                do2 = _stack_heads(do_ref[q_rows, :], m_a).astype(BF16)
                lse_c = _stack_rows(lse_ref[q_rows, :], m_a, nk)
                c_c = _stack_rows(cb_ref[q_rows, :], m_a, nk)
                s = _dot(q2, ktb, NT) + bm_ref[...].reshape(2 * BLK, 2 * BLK)[:, 2 * BLK - nk:]
                p = jnp.exp(s - lse_c)
                ds = p * (_dot(do2, vtb, NT) + c_c)
                dsb = ds.astype(BF16)
                dq2 = _dot(dsb, ktb, NN)
                dq_ref[q_rows, :] = jnp.where(m_a, dq2[:BLK], dq2[BLK:])
                dk_ref[k_rows, :] += _dot(dsb, q2, TN)
                dv_ref[k_rows, :] += _dot(p.astype(BF16), do2, TN)
                dsa_ref[:, :, 2 * BLK - nk:] += ds.reshape(2, BLK, nk)

        @pl.when(pl.program_id(0) == 0)
        def _():
            dgq_ref[...] = jnp.zeros_like(dgq_ref)
            dgk_ref[...] = jnp.zeros_like(dgk_ref)

        def norm_bwd(i, carry):
            rows = pl.ds(pl.multiple_of(i * NCH, NCH), NCH)
            for x_ref, g_ref, dx_ref, dxo_ref, dg_ref, scale in (
                    (q_ref, gq_ref, dq_ref, dqo_ref, dgq_ref, HD ** -0.5), (k_ref, gk_ref, dk_ref, dko_ref, dgk_ref, 1.0)):
                x = x_ref[rows, :].astype(F32)
                rs = lax.rsqrt(_seg_sum(x * x, ones) * (1.0 / HD) + EPS)
                xh = x * rs
                dn = dx_ref[rows, :] * scale
                dxh = dn * g_ref[...]
                dxo_ref[rows, :] = (rs * (dxh - xh * (_seg_sum(dxh * xh, ones) * (1.0 / HD)))).astype(BF16)
                dg_ref[...] += jnp.sum(dn * xh, axis=0, keepdims=True)
            dvo_ref[rows, :] = dv_ref[rows, :].astype(BF16)
            return carry

        lax.fori_loop(0, T // NCH, norm_bwd, 0)

    ucol, col, vec, bmspec = _attn_specs(g)
    return _pcall(body, name=f"attn_bwd_g{g}",
                  out_shape=(_sds((T, AOUT), BF16), _sds((T, AOUT), BF16), _sds((T, AOUT), BF16), _sds((1, 128)),
                             _sds((1, 128)), _sds((8, BLK, 2 * BLK))),
                  grid=(4,),
                  in_specs=[ucol(UB_Q), ucol(UB_K), ucol(UB_V), vec, vec, bmspec, col, col, col],
                  out_specs=(col, col, col, vec, vec, pl.BlockSpec((2, BLK, 2 * BLK), lambda hp: (hp, 0, 0))),
                  scratch=[pltpu.VMEM((T, 128), F32)] * 6,
                  semantics=("arbitrary",))(u, u, u, gq, gk, bm, dog, cb, lse)


def _combine_fwd(ogs, lses):
    def body(o0, o1, o2, l0, l1, l2, o_ref):
        ls = [l0[...], l1[...], l2[...]]
        mx = jnp.maximum(jnp.maximum(ls[0], ls[1]), ls[2])
        es = [jnp.exp(l - mx) for l in ls]
        inv = 1.0 / (es[0] + es[1] + es[2])
        o_ref[...] = ((es[0] * o0[...] + es[1] * o1[...] + es[2] * o2[...]) * inv).astype(BF16)

    return _pcall(body, name="combine_fwd", out_shape=_sds((T, AOUT), BF16), grid=(T // TB,),
                  in_specs=[_row_spec(AOUT)] * 6, out_specs=_row_spec(AOUT), semantics=("parallel",))(*ogs, *lses)


def _combine_bwd(ogs, lses, do):
    def body(o0, o1, o2, l0, l1, l2, do_ref, d0, d1, d2, c0, c1, c2):
        ls = [l0[...], l1[...], l2[...]]
        mx = jnp.maximum(jnp.maximum(ls[0], ls[1]), ls[2])
        es = [jnp.exp(l - mx) for l in ls]
        inv = 1.0 / (es[0] + es[1] + es[2])
        ws = [e * inv for e in es]
        do = do_ref[...]
        o = ws[0] * o0[...] + ws[1] * o1[...] + ws[2] * o2[...]
        s = _seg_sum(do * o, _seg_ones(AOUT))
        for w, d_ref, c_ref in zip(ws, (d0, d1, d2), (c0, c1, c2)):
            d_ref[...] = w * do
            c_ref[...] = -(w * s)

    return _pcall(body, name="combine_bwd", out_shape=tuple(_sds((T, AOUT)) for _ in range(6)), grid=(T // TB,),
                  in_specs=[_row_spec(AOUT)] * 7, out_specs=tuple(_row_spec(AOUT) for _ in range(6)),
                  semantics=("parallel",))(*ogs, *lses, do)


def _layer_fwd(x, p, bm, deps=(), mid=None):
    h1 = _rms_fwd(x, p["n1g"])
    u = _mm_x_wcols("mm_u", h1, p["win4"], tm=T, tn=1920, out_dtype=BF16, deps=deps)
    ogs, lses = [], []
    for g in range(NG):
        gdeps = (p["hook"](ogs[-1]),) if g == NG - 1 and "hook" in p else ()
        og, lse = _attn_fwd(g, u, p["gq"], p["gk"], bm, deps=gdeps)
        ogs.append(og)
        lses.append(lse)
    o = _combine_fwd(ogs, lses)
    z1 = _conv_fwd(u, p["dww"], p["dwb"])
    z3 = _ln_silu_fwd(z1, p["lng"], p["lnb"])
    if "rest" in p:
        p = {**p, **p["rest"](z3)}
    ycv = _mm_x_wcols("mm_ycv", z3, p["wco4"], tm=T, tn=256)
    yat = _mm_x_wcols("mm_yat", o, p["wao4"], tm=T, tn=256)
    m = _gate_fwd(u, ycv, yat)
    xm = _mm_x_wrows("mm_xmid", m, p["wout4"], x, tm=1024, tk=256, tn=1024)
    h2 = _rms_fwd(xm, p["n2g"])
    fa, r = _mm_ff1(h2, p["wff14"], tm=T, tn=1024)
    tok = mid(r) if mid else None
    xo = _mm_x_wrows("mm_xout", r, p["wff24"], xm, tm=1024, tk=1024, tn=1024, deps=(tok,))
    saved = dict(x=x, h1=h1, u=u, z1=z1, ogs=ogs, lses=lses, o=o, ycv=ycv, yat=yat, m=m, xm=xm, h2=h2, fa=fa, r=r)
    return xo, p, saved


EARLY = ("w_ff2", "w_ff1", "w_out")
LATE = ("w_conv_out", "w_attn_out", "w_in")


def _layer_bwd(dx, s, p, bm, pipe=None, own_early=None, own_late=None):
    u = s["u"]
    tok = pipe.step0() if pipe else None
    df = _mm_dff2(dx, p["wff24"], s["fa"], tm=T, tn=512, deps=(tok,))
    g_ff2 = _mm_dw_rows("mm_dwff2", s["r"], dx, ks=1024, tm=1024, tn=1024)
    g_ff1 = _mm_dw_cols("mm_dwff1", s["h2"], df, ns=1024, tm=1024, tn=1024)
    tok = pipe.step1(g_ff1) if pipe else None
    dh2 = _mm_g_wcols_t("mm_dh2", df, p["wff14"], tm=1024, tk=1024, tn=1024, deps=(tok,))
    dxm, d_n2g = _rms_bwd(s["xm"], p["n2g"], dh2, dx)

    dm = _mm_g_wrows_t("mm_dm", dxm, p["wout4"], tm=1024, tn=256)
    g_out = _mm_dw_rows("mm_dwout", s["m"], dxm, ks=256, tm=256, tn=1024)
    early = own_early(dict(w_ff2=g_ff2, w_ff1=g_ff1, w_out=g_out)) if own_early else None
    tok_e = early.step0() if early else None
    dyc, dya, dgc, dga = _gate_bwd(u, s["ycv"], s["yat"], dm)

    dz3 = _mm_g_wcols_t("mm_dz3", dyc, p["wco4"], tm=T, tk=256, tn=512, deps=(tok_e,))
    z3, dz1, d_lng, d_lnb = _ln_silu_bwd(s["z1"], p["lng"], p["lnb"], dz3)
    g_co = _mm_dw_cols("mm_dwco", z3, dyc, ns=256, tm=512, tn=256)
    da, dgt, d_dww, d_dwb = _conv_bwd(u, p["dww"], dz1)

    tok_e = early.step1(da) if early else None
    do = _mm_g_wcols_t("mm_do", dya, p["wao4"], tm=T, tk=256, tn=512, deps=(tok_e,))
    g_ao = _mm_dw_cols("mm_dwao", s["o"], dya, ns=256, tm=512, tn=256)
    parts = _combine_bwd(s["ogs"], s["lses"], do)
    dqs, dks, dvs, d_gq, d_gk, dsas = [], [], [], [], [], []
    for g in range(NG):
        dq, dk, dv, dgq, dgk, dsa = _attn_bwd(g, u, p["gq"], p["gk"], bm, parts[g], parts[NG + g], s["lses"][g])
        dqs.append(dq)
        dks.append(dk)
        dvs.append(dv)
        d_gq.append(dgq)
        d_gk.append(dgk)
        dsas.append(dsa)
    du = jnp.concatenate([da, dgt] + dqs + dks + dvs + [dgc, dga], axis=1)
    tok = pipe.step2(du) if pipe else None
    tok_e = early.step2(du) if early else None
    g_in = _mm_dw_cols("mm_dwin", s["h1"], du, ns=1920, tm=1024, tn=640, deps=(tok, tok_e))
    late = own_late(dict(w_in=g_in, w_conv_out=g_co, w_attn_out=g_ao)) if own_late else None
    tok_l = late.step0() if late else None
    dh1 = _mm_g_wcols_t("mm_dh1", du, p["win4"], tm=1024, tk=1920, tn=1024, deps=(tok_l,))
    tok_l = late.step1(dh1) if late else None
    dxi, d_n1g = _rms_bwd(s["x"], p["n1g"], dh1, dxm, deps=(tok_l,))
    if pipe:
        pipe.step3(dxi)

    fold = lambda parts_: sum(v[0, :HD] + v[0, HD:] for v in parts_)
    big = dict(w_in=g_in, w_conv_out=g_co, w_attn_out=g_ao, w_out=g_out, w_ff1=g_ff1, w_ff2=g_ff2)
    small = dict(norm1_g=d_n1g[0], q_norm_g=fold(d_gq), k_norm_g=fold(d_gk), conv_dw_w=d_dww, conv_dw_b=d_dwb[0],
                 conv_ln_g=d_lng[0], conv_ln_b=d_lnb[0], norm2_g=d_n2g[0])
    return dxi, big, small, jnp.concatenate(dsas, axis=0), early, late


def _local_step(x, target, get_layer, rel_bias, make_pipe):
    buckets = jnp.asarray(_bucket_tables())
    bm = _bias_table(rel_bias.T, buckets)
    saved, layers = [], []
    for l in range(DEPTH):
        p, deps, mid = get_layer(l, x)
        x, p, s = _layer_fwd(x, p, bm, deps=deps, mid=mid)
        layers.append(p)
        saved.append(s)
    loss_blk, dx = _loss_fwd_bwd(x, target)
    smalls, dsas = [None] * DEPTH, [None] * DEPTH
    pipe, pipes = None, []
    for l in reversed(range(DEPTH)):
        if l > 0:
            dx, big, smalls[l], dsas[l], _, _ = _layer_bwd(dx, saved[l], layers[l], bm, pipe)
            pipe = make_pipe(l, BIG, "", big)
            pipes.append(pipe)
        else:
            dx, big, smalls[l], dsas[l], early, late = _layer_bwd(
                dx, saved[l], layers[l], bm, pipe, lambda big_: make_pipe(0, EARLY, "e", big_),
                lambda big_: make_pipe(0, LATE, "", big_))
    d_rel = _bias_grad(jnp.stack(dsas), buckets)[:, 0, :NBUCKET].T
    return loss_blk[0, 0], dx, smalls, d_rel, pipes, early, late


MESH = pl.DeviceIdType.MESH


def _me():
    return lax.axis_index("x"), lax.axis_index("y"), lax.axis_index("c")


def _other_chips(mx, my):
    return [(1 - mx, my), (mx, 1 - my), (1 - mx, 1 - my)]


def _rcopy(src, dst, send_sems, recv_sems, k, dev):
    return pltpu.make_async_remote_copy(src_ref=src, dst_ref=dst, send_sem=send_sems.at[k], recv_sem=recv_sems.at[k],
                                        device_id=dev, device_id_type=MESH)


def _comm_call(body, name, out_shape, n_in, n_sems):
    return pl.pallas_call(
        body, name=name, out_shape=out_shape, in_specs=[HBM_SPEC] * n_in,
        out_specs=jax.tree.map(lambda _: HBM_SPEC, out_shape),
        scratch_shapes=[pltpu.SemaphoreType.DMA((n_sems,)), pltpu.SemaphoreType.DMA((n_sems,)),
                        pltpu.SemaphoreType.DMA(())],
        compiler_params=pltpu.CompilerParams(has_side_effects=True))


def _all_gather_chips(x, name):
    def body(x_ref, o_ref, send_sems, recv_sems, local_sem):
        mx, my, mc = _me()
        local = pltpu.make_async_copy(x_ref, o_ref.at[2 * mx + my], local_sem)
        local.start()
        sends = [_rcopy(x_ref, o_ref.at[2 * mx + my], send_sems, recv_sems, k, (px, py, mc))
                 for k, (px, py) in enumerate(_other_chips(mx, my))]
        for cp in sends:
            cp.start()
        for k, (px, py) in enumerate(_other_chips(mx, my)):
            _rcopy(x_ref, o_ref.at[2 * px + py], send_sems, recv_sems, k, (px, py, mc)).wait_recv()
        for cp in sends:
            cp.wait_send()
        local.wait()

    return _comm_call(body, name, _sds((NCHIP,) + x.shape, x.dtype), 1, 3)(x)


EFFECT = pltpu.SideEffectType.DATAFLOW_SIDE_EFFECTING


def _hbm(a):
    return pltpu.with_memory_space_constraint(a, pltpu.HBM)


def _split_start(name, bufs, plan, n, after=None):
    nb = len(bufs)
    extra = [] if after is None else [after]
    ne = len(extra)

    def body(*refs):
        send_sems, recv_sems, token = refs[nb + ne], refs[nb + ne + 1], refs[-1]
        mx, my, mc = _me()
        for k, (src, dst, dev, _) in enumerate(plan(refs[:nb], mx, my, mc)):
            _rcopy(src, dst, send_sems, recv_sems, k, dev).start()
        token[...] = jnp.zeros_like(token)

    out = pl.pallas_call(
        body, name=name,
        out_shape=(pltpu.SemaphoreType.DMA((n,)), pltpu.SemaphoreType.DMA((n,)),
                   *[pltpu.HBM(b.shape, b.dtype) for b in bufs], _sds((8, 128))),
        in_specs=[HBM_SPEC] * nb + [ANY_SPEC] * ne,
        out_specs=(SEM_SPEC, SEM_SPEC, *[HBM_SPEC] * nb, pl.BlockSpec(memory_space=pltpu.VMEM)),
        input_output_aliases={i: 2 + i for i in range(nb)},
        compiler_params=pltpu.CompilerParams(has_side_effects=EFFECT))(*[_hbm(b) for b in bufs], *extra)
    return (out[0], out[1]), list(out[2:2 + nb]), out[-1]


def _split_wait(name, sems, bufs, plan, after):
    nb = len(bufs)

    def body(*refs):
        send_sems, recv_sems = refs[nb], refs[nb + 1]
        mx, my, mc = _me()
        for k, (src, dst, dev, land) in enumerate(plan(refs[:nb], mx, my, mc)):
            _rcopy(src, dst, send_sems, recv_sems, k, dev).wait_send()
            _rcopy(src, land, send_sems, recv_sems, k, dev).wait_recv()

    out = pl.pallas_call(
        body, name=name, out_shape=tuple(pltpu.HBM(b.shape, b.dtype) for b in bufs),
        in_specs=[HBM_SPEC] * nb + [SEM_SPEC, SEM_SPEC, ANY_SPEC], out_specs=(HBM_SPEC,) * nb,
        input_output_aliases={i: i for i in range(nb)},
        compiler_params=pltpu.CompilerParams(has_side_effects=EFFECT))(*bufs, sems[0], sems[1], after)
    return list(out)


def _plan_gather_chips(refs, mx, my, mc):
    me = 2 * mx + my
    return [(r.at[me, mc], r.at[me, mc], (px, py, mc), r.at[2 * px + py, mc])
            for r in refs for px, py in _other_chips(mx, my)]


def _plan_gather_pair(refs, mx, my, mc):
    return [(r.at[2 * px + py, mc], r.at[2 * px + py, mc], (mx, my, 1 - mc), r.at[2 * px + py, 1 - mc])
            for r in refs for px, py in _other_chips(mx, my)]


def _plan_gather_devices(refs, mx, my, mc):
    flip = lambda m, b: 1 - m if b else m
    peers = [(flip(mx, k >> 2 & 1), flip(my, k >> 1 & 1), flip(mc, k & 1)) for k in range(1, 8)]
    slot = lambda dev: 4 * dev[0] + 2 * dev[1] + dev[2]
    me = slot((mx, my, mc))
    return [(r.at[me], r.at[me], dev, r.at[slot(dev)]) for r in refs for dev in peers]


def _plan_pair_half(refs, mx, my, mc):
    n = len(refs) // 2
    return [(g.at[:, 1 - mc], r, (mx, my, 1 - mc), r) for g, r in zip(refs[:n], refs[n:])]


def _plan_scatter(refs, mx, my, mc):
    n = len(refs) // 2
    return [(q.at[2 * px + py], r.at[k], (px, py, mc), r.at[k])
            for q, r in zip(refs[:n], refs[n:]) for k, (px, py) in enumerate(_other_chips(mx, my))]


def _plan_pair_fill(refs, mx, my, mc):
    return [(r.at[mc], r.at[mc], (mx, my, 1 - mc), r.at[1 - mc]) for r in refs]


def _row_tile(rows, cols):
    t = 8
    while t * 2 * cols * 4 <= (1 << 21) and rows % (t * 2) == 0:
        t *= 2
    return t


def _prefetch_call(body, name, out_shape, grid, in_specs, out_specs):
    return pl.pallas_call(
        body, name=name, out_shape=out_shape,
        grid_spec=pltpu.PrefetchScalarGridSpec(num_scalar_prefetch=1, grid=grid, in_specs=in_specs,
                                               out_specs=out_specs),
        compiler_params=pltpu.CompilerParams(vmem_limit_bytes=VMEM_LIMIT,
                                             dimension_semantics=("parallel",) * len(grid)))


def _sum_half(g, r1, place, name):
    _, _, rr, ns = g.shape
    tr = _row_tile(rr, ns)

    def body(c_ref, g_ref, r_ref, o_ref, ob_ref):
        q = g_ref[...] + r_ref[...]
        ob_ref[...] = q.astype(BF16)

        @pl.when(pl.program_id(1) == c_ref[0])
        def _():
            o_ref[...] = q

    blk = pl.BlockSpec((None, tr, ns), lambda i, s, c: (s, i, 0))
    return pl.pallas_call(
        body, name=name, out_shape=(_sds((rr, ns)), _sds((NCHIP, rr, ns), BF16)),
        grid_spec=pltpu.PrefetchScalarGridSpec(
            num_scalar_prefetch=1, grid=(rr // tr, NCHIP),
            in_specs=[pl.BlockSpec((None, None, tr, ns), lambda i, s, c: (s, c[1], i, 0)), blk],
            out_specs=(pl.BlockSpec((tr, ns), lambda i, s, c: (i, 0)), blk)),
        compiler_params=pltpu.CompilerParams(vmem_limit_bytes=VMEM_LIMIT,
                                             dimension_semantics=("parallel", "arbitrary")))(place, g, r1)


def _sum_recv(q, r2, place, name):
    rr, ns = q.shape
    tr = _row_tile(rr, ns)

    def body(c_ref, q_ref, r_ref, o_ref):
        o_ref[...] = ((q_ref[...] + r_ref[0].astype(F32)) + r_ref[1].astype(F32)) + r_ref[2].astype(F32)

    return _prefetch_call(body, name, _sds((2, rr, ns)), (rr // tr,),
                          [pl.BlockSpec((tr, ns), lambda i, c: (i, 0)),
                           pl.BlockSpec((NCHIP - 1, tr, ns), lambda i, c: (0, i, 0))],
                          pl.BlockSpec((None, tr, ns), lambda i, c: (c[1], i, 0)))(place, q, r2)


def _sum_devices(v8):
    def body(v_ref, o_ref):
        acc = v_ref[0]
        for dev in range(1, 8):
            acc = acc + v_ref[dev]
        o_ref[...] = acc

    return _pcall(body, name="sum_devices", out_shape=_sds(v8.shape[1:]))(v8)


def _adamw(w, g, m, v, name):
    rows, cols = w.shape
    tr = _row_tile(rows, cols)

    def body(w_ref, g_ref, m_ref, v_ref, d_ref, m2_ref, v2_ref):
        g = g_ref[...]
        m2 = ADAM_B1 * m_ref[...] + (1.0 - ADAM_B1) * g
        v2 = ADAM_B2 * v_ref[...] + (1.0 - ADAM_B2) * (g * g)
        m_hat = m2 / (1.0 - ADAM_B1 ** ADAM_STEP)
        v_hat = v2 / (1.0 - ADAM_B2 ** ADAM_STEP)
        d_ref[...] = -ADAM_LR * (m_hat / (jnp.sqrt(v_hat) + ADAM_EPS) + ADAM_WD * w_ref[...])
        m2_ref[...] = m2
        v2_ref[...] = v2

    blk = pl.BlockSpec((tr, cols), lambda i: (i, 0))
    return _pcall(body, name=name, out_shape=(_sds((rows, cols)),) * 3, grid=(rows // tr,), in_specs=[blk] * 4,
                  out_specs=(blk,) * 3, semantics=("parallel",))(w, g, m, v)


BIG = ("w_in", "w_conv_out", "w_attn_out", "w_out", "w_ff1", "w_ff2")
SMALL = ("rel_bias", "norm1_g", "q_norm_g", "k_norm_g", "conv_dw_w", "conv_dw_b", "conv_ln_g", "conv_ln_b", "norm2_g")
WEIGHTS = ("rel_bias", "norm1_g", "w_in", "q_norm_g", "k_norm_g", "conv_dw_w", "conv_dw_b", "conv_ln_g", "conv_ln_b",
           "w_conv_out", "w_attn_out", "w_out", "norm2_g", "w_ff1", "w_ff2")


def _pack(arrays):
    flat = jnp.concatenate([a.reshape(-1) for a in arrays])
    n = flat.shape[0]
    rows = -(-n // 1024) * 8
    return jnp.pad(flat, (0, rows * 128 - n)).reshape(rows, 128)


def _unpack(packed, shapes):
    flat = packed.reshape(-1)
    out, off = [], 0
    for shp in shapes:
        n = int(np.prod(shp))
        out.append(flat[off:off + n].reshape(shp))
        off += n
    return out


def _adamw_layer(l, w, g, m, v, prev, name, deps=()):
    _, k, n = w.shape
    tr = _row_tile(k, n)
    deps = tuple(d for d in deps if d is not None)
    if prev is None:
        prev = tuple(lax.empty(w.shape, F32) for _ in range(4))

    def body(*refs):
        w_ref, g_ref, m_ref, v_ref = refs[:4]
        go_ref, d_ref, m2_ref, v2_ref = refs[-4:]
        g = g_ref[...]
        m2 = ADAM_B1 * m_ref[...] + (1.0 - ADAM_B1) * g
        v2 = ADAM_B2 * v_ref[...] + (1.0 - ADAM_B2) * (g * g)
        m_hat = m2 / (1.0 - ADAM_B1 ** ADAM_STEP)
        v_hat = v2 / (1.0 - ADAM_B2 ** ADAM_STEP)
        go_ref[...] = g
        d_ref[...] = -ADAM_LR * (m_hat / (jnp.sqrt(v_hat) + ADAM_EPS) + ADAM_WD * w_ref[...])
        m2_ref[...] = m2
        v2_ref[...] = v2

    lay = pl.BlockSpec((None, tr, n), lambda i: (l, i, 0))
    return _pcall(body, name=name, out_shape=(_sds(w.shape),) * 4, grid=(k // tr,),
                  in_specs=[lay, pl.BlockSpec((tr, n), lambda i: (i, 0)), lay, lay] + [ANY_SPEC] * (4 + len(deps)),
                  out_specs=(lay,) * 4, aliases={4: 0, 5: 1, 6: 2, 7: 3},
                  semantics=("parallel",))(w, g, m, v, *prev, *deps)


class _GradPipe:
    def __init__(self, l, kinds, tag, big, place, w, m, v, results):
        self.l, self.kinds, self.place, self.w, self.m, self.v, self.results = l, kinds, place, w, m, v, results
        self.id = f"l{l}{tag}"
        self.g = [big[n].reshape(NCHIP, 2, big[n].shape[1] // 2, big[n].shape[2]) for n in kinds]

    def step0(self):
        lands = [lax.empty((NCHIP,) + g.shape[2:], F32) for g in self.g]
        self.s1, self.b1, tok = _split_start(f"rs1_start_{self.id}", self.g + lands, _plan_pair_half, len(self.kinds))
        return tok

    def step1(self, after):
        nk = len(self.kinds)
        bufs = _split_wait(f"rs1_wait_{self.id}", self.s1, self.b1, _plan_pair_half, after)
        sums = [_sum_half(bufs[i], bufs[nk + i], self.place, f"rs1_sum_{n}") for i, n in enumerate(self.kinds)]
        self.q = [q for q, _ in sums]
        qb = [b for _, b in sums]
        lands = [lax.empty((NCHIP - 1,) + b.shape[1:], BF16) for b in qb]
        self.s2, self.b2, tok = _split_start(f"rs2_start_{self.id}", qb + lands, _plan_scatter, 3 * nk)
        return tok

    def step2(self, after):
        nk = len(self.kinds)
        bufs = _split_wait(f"rs2_wait_{self.id}", self.s2, self.b2, _plan_scatter, after)
        fin = [_sum_recv(self.q[i], bufs[nk + i], self.place, f"rs2_sum_{n}") for i, n in enumerate(self.kinds)]
        self.s3, self.b3, tok = _split_start(f"rs3_start_{self.id}", fin, _plan_pair_fill, nk)
        return tok

    def step3(self, after):
        self.fin = _split_wait(f"rs3_wait_{self.id}", self.s3, self.b3, _plan_pair_fill, after)

    def adam(self, deps=()):
        for i, n in enumerate(self.kinds):
            g2 = self.fin[i].reshape(self.fin[i].shape[1] * 2, self.fin[i].shape[2])
            self.results[n] = _adamw_layer(self.l, self.w[n], g2, self.m[n], self.v[n], self.results.get(n),
                                           f"adamw_{n}_l{self.l}", deps=deps if i == 0 else ())
        return self.results[self.kinds[-1]][1]


def kernel(x, rel_bias, norm1_g, w_in, q_norm_g, k_norm_g, conv_dw_w, conv_dw_b, conv_ln_g, conv_ln_b, w_conv_out, w_attn_out, w_out, norm2_g, w_ff1, w_ff2, loss_target, m_rel_bias, m_norm1_g, m_w_in, m_q_norm_g, m_k_norm_g, m_conv_dw_w, m_conv_dw_b, m_conv_ln_g, m_conv_ln_b, m_w_conv_out, m_w_attn_out, m_w_out, m_norm2_g, m_w_ff1, m_w_ff2, v_rel_bias, v_norm1_g, v_w_in, v_q_norm_g, v_k_norm_g, v_conv_dw_w, v_conv_dw_b, v_conv_ln_g, v_conv_ln_b, v_w_conv_out, v_w_attn_out, v_w_out, v_norm2_g, v_w_ff1, v_w_ff2):
    w = dict(rel_bias=rel_bias, norm1_g=norm1_g, w_in=w_in, q_norm_g=q_norm_g, k_norm_g=k_norm_g, conv_dw_w=conv_dw_w,
             conv_dw_b=conv_dw_b, conv_ln_g=conv_ln_g, conv_ln_b=conv_ln_b, w_conv_out=w_conv_out,
             w_attn_out=w_attn_out, w_out=w_out, norm2_g=norm2_g, w_ff1=w_ff1, w_ff2=w_ff2)
    m = dict(rel_bias=m_rel_bias, norm1_g=m_norm1_g, w_in=m_w_in, q_norm_g=m_q_norm_g, k_norm_g=m_k_norm_g,
             conv_dw_w=m_conv_dw_w, conv_dw_b=m_conv_dw_b, conv_ln_g=m_conv_ln_g, conv_ln_b=m_conv_ln_b,
             w_conv_out=m_w_conv_out, w_attn_out=m_w_attn_out, w_out=m_w_out, norm2_g=m_norm2_g, w_ff1=m_w_ff1,
             w_ff2=m_w_ff2)
    v = dict(rel_bias=v_rel_bias, norm1_g=v_norm1_g, w_in=v_w_in, q_norm_g=v_q_norm_g, k_norm_g=v_k_norm_g,
             conv_dw_w=v_conv_dw_w, conv_dw_b=v_conv_dw_b, conv_ln_g=v_conv_ln_g, conv_ln_b=v_conv_ln_b,
             w_conv_out=v_w_conv_out, w_attn_out=v_w_attn_out, w_out=v_w_out, norm2_g=v_norm2_g, w_ff1=v_w_ff1,
             w_ff2=v_w_ff2)
    chip_id = 2 * lax.axis_index("x") + lax.axis_index("y")
    place = jnp.stack([chip_id, lax.axis_index("c")]).astype(jnp.int32)

    dww4 = _all_gather_chips(conv_dw_w, "ag_conv_dw_w")
    dww = dww4.transpose(1, 2, 0, 3).reshape(DEPTH, KW, CONV)

    names = dict(w_in="win4", w_conv_out="wco4", w_attn_out="wao4", w_out="wout4", w_ff1="wff14", w_ff2="wff24")
    chips, pair = {}, {}

    def start_chips(key, l, kinds, after):
        lands = []
        for n in kinds:
            k, ns = w[n].shape[1:]
            land = lax.dynamic_update_slice(lax.empty((NCHIP, k, ns), BF16), w[n][l].astype(BF16)[None], (chip_id, 0, 0))
            lands.append(land.reshape(NCHIP, 2, k // 2, ns))
        chips[key] = _split_start(f"ag_chips_start_{key}", lands, _plan_gather_chips, 3 * len(kinds), after=after)
        return chips[key][2]

    def start_pair(key, after):
        sems, bufs, _ = chips[key]
        bufs = _split_wait(f"ag_chips_wait_{key}", sems, bufs, _plan_gather_chips, after)
        pair[key] = _split_start(f"ag_pair_start_{key}", bufs, _plan_gather_pair, len(bufs) * 3)
        return pair[key][2]

    def landed(key, kinds, after):
        sems, bufs, _ = pair[key]
        bufs = _split_wait(f"ag_pair_wait_{key}", sems, bufs, _plan_gather_pair, after)
        return {names[n]: b.reshape(NCHIP, 2 * b.shape[2], b.shape[3]) for n, b in zip(kinds, bufs)}

    first, rest = ("w_in",), tuple(n for n in BIG if n != "w_in")
    start_chips("l0b", 0, rest, start_pair("l0a", start_chips("l0a", 0, first, dww4)))

    def get_layer(l, after):
        p = dict(dww=dww[l], dwb=conv_dw_b[l][None], lng=conv_ln_g[l][None], lnb=conv_ln_b[l][None],
                 n1g=norm1_g[l][None], n2g=norm2_g[l][None], gq=jnp.tile(q_norm_g[l], 2)[None],
                 gk=jnp.tile(k_norm_g[l], 2)[None])
        p.update(landed(f"l{l}a", first, chips["l0b"][2] if l == 0 else after))
        more = l + 1 < DEPTH

        def hook(after_):
            tok = start_pair(f"l{l}b", after_)
            return start_chips(f"l{l + 1}a", l + 1, first, tok) if more else tok

        p["hook"] = hook
        p["rest"] = lambda after_: landed(f"l{l}b", rest, after_)
        mid = (lambda after_: start_chips(f"l{l + 1}b", l + 1, rest, start_pair(f"l{l + 1}a", after_))) if more else None
        return p, (), mid

    results = {}
    make_pipe = lambda l, kinds, tag, big: _GradPipe(l, kinds, tag, big, place, w, m, v, results)
    loss_share, dx, smalls, d_rel, pipes, early, late = _local_step(x[0], loss_target[0], get_layer, rel_bias,
                                                                    make_pipe)
    loss = lax.psum(loss_share, ("x", "y", "c"))

    local_small = dict(rel_bias=d_rel)
    for n in SMALL[1:]:
        local_small[n] = jnp.stack([smalls[l][n] for l in range(DEPTH)])
    small_shapes = [local_small[n].shape for n in SMALL]
    mine = _pack([local_small[n] for n in SMALL])
    slot = 4 * lax.axis_index("x") + 2 * lax.axis_index("y") + lax.axis_index("c")
    land = lax.dynamic_update_slice(lax.empty((8,) + mine.shape, F32), mine[None], (slot, 0, 0))
    small_sems, small_bufs, tok = _split_start("ag_small_start", [land], _plan_gather_devices, 7)
    for pipe in pipes:
        done = pipe.adam(deps=(tok,))
        tok = None
    early.step3(late.step2(done))
    done = early.adam()
    late.step3(done)
    done = late.adam()
    gathered = _split_wait("ag_small_wait", small_sems, small_bufs, _plan_gather_devices, done)[0]
    summed = _sum_devices(gathered)
    grads = dict(zip(SMALL, _unpack(summed, small_shapes)))
    grads["conv_dw_w"] = lax.dynamic_slice_in_dim(grads["conv_dw_w"], chip_id * 128, 128, axis=2)

    delta, new_m, new_v = {}, {}, {}
    small_w_shapes = [w[n].shape for n in SMALL]
    outs = _adamw(_pack([w[n] for n in SMALL]), _pack([grads[n] for n in SMALL]), _pack([m[n] for n in SMALL]),
                  _pack([v[n] for n in SMALL]), "adamw_small")
    for dst, packed in zip((delta, new_m, new_v), outs):
        dst.update(zip(SMALL, _unpack(packed, small_w_shapes)))

    for n in BIG:
        grads[n], delta[n], new_m[n], new_v[n] = results[n]

    return (loss, dx[None], *[grads[n] for n in WEIGHTS], *[delta[n] for n in WEIGHTS],
            *[new_m[n] for n in WEIGHTS], *[new_v[n] for n in WEIGHTS])
```

```python
import functools
import math

import numpy as np
import jax
import jax.numpy as jnp
from jax import lax
from jax.experimental import pallas as pl
from jax.experimental.pallas import tpu as pltpu

F32 = jnp.float32
BF16 = jnp.bfloat16

T = 2048
D = 1024
DEPTH = 4
CONV = 512
KW = 31
NG = 3
HD = 64
AOUT = 512
DFF = 4096
INC = 7680
DIL = (1, 4, 16)
BLK = 128
NBUCKET = 32
EPS = 1e-6
NEG = -1e30
NCHIP = 4
UB_A, UB_GT, UB_Q, UB_K, UB_V, UB_GC, UB_GA = 0, 1, 2, 5, 8, 11, 13

ADAM_LR, ADAM_B1, ADAM_B2, ADAM_EPS, ADAM_WD, ADAM_STEP = 0.001, 0.9, 0.999, 1e-08, 0.01, 10

VMEM_LIMIT = 48 * 1024 * 1024
TB = 512
HBM_SPEC = pl.BlockSpec(memory_space=pltpu.HBM)
ANY_SPEC = pl.BlockSpec(memory_space=pl.ANY)
SEM_SPEC = pl.BlockSpec(memory_space=pltpu.SEMAPHORE)


def _pcall(body, *, name, out_shape, grid=(), in_specs=None, out_specs=None, scratch=(), aliases=None,
           semantics=None):
    kw = {}
    if in_specs is not None:
        kw["in_specs"] = in_specs
    if out_specs is not None:
        kw["out_specs"] = out_specs
    return pl.pallas_call(
        body, name=name, out_shape=out_shape, grid=grid, scratch_shapes=scratch,
        input_output_aliases=aliases or {},
        compiler_params=pltpu.CompilerParams(vmem_limit_bytes=VMEM_LIMIT, dimension_semantics=semantics),
        **kw)


def _sds(shape, dtype=F32):
    return jax.ShapeDtypeStruct(shape, dtype)


NN = (((1,), (0,)), ((), ()))
NT = (((1,), (1,)), ((), ()))
TN = (((0,), (0,)), ((), ()))


def _mm(name, a, b, *, out_shape, out_dtype, grid, a_spec, b_spec, o_spec, acc_shape, dims, add=None,
        add_spec=None, deps=()):
    nk = grid[2]
    deps = tuple(d for d in deps if d is not None)
    n_scratch = 1 if nk > 1 else 0

    def body(*refs):
        n_out = 1 + n_scratch
        refs = refs[:len(refs) - n_out - len(deps)] + refs[len(refs) - n_out:]
        a_ref, b_ref = refs[0], refs[1]
        r_ref = refs[2] if add is not None else None
        o_ref = refs[-n_out]
        prod = lax.dot_general(a_ref[...].astype(BF16), b_ref[...].astype(BF16), dims, preferred_element_type=F32)
        if nk == 1:
            o_ref[...] = (prod if r_ref is None else prod + r_ref[...]).astype(out_dtype)
            return
        acc_ref = refs[-1]
        k = pl.program_id(2)

        @pl.when(k == 0)
        def _():
            acc_ref[...] = prod

        @pl.when(k > 0)
        def _():
            acc_ref[...] += prod

        @pl.when(k == nk - 1)
        def _():
            res = acc_ref[...]
            if r_ref is not None:
                res = res + r_ref[...]
            o_ref[...] = res.astype(out_dtype)

    ins = ([a, b] if add is None else [a, b, add]) + list(deps)
    specs = ([a_spec, b_spec] if add is None else [a_spec, b_spec, add_spec]) + [ANY_SPEC] * len(deps)
    return _pcall(body, name=name, out_shape=_sds(out_shape, out_dtype), grid=grid, in_specs=specs,
                  out_specs=o_spec, scratch=[pltpu.VMEM(acc_shape, F32)] * n_scratch,
                  semantics=("parallel", "parallel", "arbitrary"))(*ins)


def _mm_x_wcols(name, a, w4, *, tm, tn, out_dtype=F32, deps=()):
    _, k, ns = w4.shape
    nj = ns // tn
    return _mm(name, a, w4, out_shape=(T, NCHIP * ns), out_dtype=out_dtype, grid=(T // tm, NCHIP * nj, 1), deps=deps,
               a_spec=pl.BlockSpec((tm, k), lambda i, j, kk: (i, 0)),
               b_spec=pl.BlockSpec((None, k, tn), lambda i, j, kk: (j // nj, 0, j % nj)),
               o_spec=pl.BlockSpec((tm, tn), lambda i, j, kk: (i, j)), acc_shape=(tm, tn), dims=NN)


def _mm_ff1(a, w4, *, tm, tn):
    _, k, ns = w4.shape
    nj = ns // tn

    def body(a_ref, b_ref, f_ref, r_ref):
        p = jnp.maximum(jnp.dot(a_ref[...], b_ref[...], preferred_element_type=F32), 0.0)
        f_ref[...] = p.astype(BF16)
        r_ref[...] = (p * p).astype(BF16)

    out = pl.BlockSpec((tm, tn), lambda i, j: (i, j))
    return _pcall(body, name="mm_f", out_shape=(_sds((T, DFF), BF16), _sds((T, DFF), BF16)), grid=(T // tm, NCHIP * nj),
                  in_specs=[pl.BlockSpec((tm, k), lambda i, j: (i, 0)),
                            pl.BlockSpec((None, k, tn), lambda i, j: (j // nj, 0, j % nj))],
                  out_specs=(out, out), semantics=("parallel", "parallel"))(a, w4)


def _mm_x_wrows(name, a, w4, add, *, tm, tk, tn, deps=()):
    _, ks, n = w4.shape
    nkk = ks // tk
    return _mm(name, a, w4, out_shape=(T, n), out_dtype=F32, grid=(T // tm, n // tn, NCHIP * nkk), deps=deps,
               a_spec=pl.BlockSpec((tm, tk), lambda i, j, kk: (i, kk)),
               b_spec=pl.BlockSpec((None, tk, tn), lambda i, j, kk: (kk // nkk, kk % nkk, j)),
               o_spec=pl.BlockSpec((tm, tn), lambda i, j, kk: (i, j)), acc_shape=(tm, tn), dims=NN,
               add=add, add_spec=pl.BlockSpec((tm, tn), lambda i, j, kk: (i, j)))


def _mm_g_wcols_t(name, g, w4, *, tm, tk, tn, out_dtype=F32, deps=()):
    _, k, ns = w4.shape
    nkk = ns // tk
    return _mm(name, g, w4, out_shape=(T, k), out_dtype=out_dtype, grid=(T // tm, k // tn, NCHIP * nkk), deps=deps,
               a_spec=pl.BlockSpec((tm, tk), lambda i, j, kk: (i, kk)),
               b_spec=pl.BlockSpec((None, tn, tk), lambda i, j, kk: (kk // nkk, j, kk % nkk)),
               o_spec=pl.BlockSpec((tm, tn), lambda i, j, kk: (i, j)), acc_shape=(tm, tn), dims=NT)


def _mm_g_wrows_t(name, g, w4, *, tm, tn, out_dtype=F32, deps=()):
    _, ks, n = w4.shape
    nj = ks // tn
    return _mm(name, g, w4, out_shape=(T, NCHIP * ks), out_dtype=out_dtype, grid=(T // tm, NCHIP * nj, 1), deps=deps,
               a_spec=pl.BlockSpec((tm, n), lambda i, j, kk: (i, 0)),
               b_spec=pl.BlockSpec((None, tn, n), lambda i, j, kk: (j // nj, j % nj, 0)),
               o_spec=pl.BlockSpec((tm, tn), lambda i, j, kk: (i, j)), acc_shape=(tm, tn), dims=NT)


def _mm_dff2(dx, w4, fa, *, tm, tn, deps=()):
    _, ks, n = w4.shape
    nj = ks // tn
    deps = tuple(d for d in deps if d is not None)

    def body(*refs):
        dx_ref, b_ref, f_ref = refs[:3]
        df_ref = refs[-1]
        dr = lax.dot_general(dx_ref[...].astype(BF16), b_ref[...], NT, preferred_element_type=F32)
        df_ref[...] = (dr * (2.0 * f_ref[...].astype(F32))).astype(BF16)

    out = pl.BlockSpec((tm, tn), lambda i, j: (i, j))
    return _pcall(body, name="mm_dr", out_shape=_sds((T, DFF), BF16), grid=(T // tm, NCHIP * nj),
                  in_specs=[pl.BlockSpec((tm, n), lambda i, j: (i, 0)),
                            pl.BlockSpec((None, tn, n), lambda i, j: (j // nj, j % nj, 0)), out]
                  + [ANY_SPEC] * len(deps),
                  out_specs=out, semantics=("parallel", "parallel"))(dx, w4, fa, *deps)


TCH = 512


def _mm_dw(name, a, g, *, out_shape, out_map, tm, tn, deps=()):
    deps = tuple(d for d in deps if d is not None)

    def body(*refs):
        a_ref, g_ref = refs[:2]
        o_ref, at_ref = refs[-2:]

        @pl.when(pl.program_id(1) == 0)
        def _():
            for c in range(T // TCH):
                at_ref[:, c * TCH:(c + 1) * TCH] = a_ref[c * TCH:(c + 1) * TCH, :].T

        o_ref[...] = jnp.dot(at_ref[...], g_ref[...].astype(BF16), preferred_element_type=F32)

    return _pcall(body, name=name, out_shape=_sds(out_shape), grid=(a.shape[1] // tm, g.shape[1] // tn),
                  in_specs=[pl.BlockSpec((T, tm), lambda i, j: (0, i)), pl.BlockSpec((T, tn), lambda i, j: (0, j))]
                  + [ANY_SPEC] * len(deps),
                  out_specs=pl.BlockSpec((None, tm, tn), out_map), scratch=[pltpu.VMEM((tm, T), BF16)],
                  semantics=("parallel", "arbitrary"))(a, g, *deps)


def _mm_dw_cols(name, a, g, *, ns, tm, tn, deps=()):
    nj = ns // tn
    return _mm_dw(name, a, g, out_shape=(NCHIP, a.shape[1], ns), out_map=lambda i, j: (j // nj, i, j % nj),
                  tm=tm, tn=tn, deps=deps)


def _mm_dw_rows(name, a, g, *, ks, tm, tn):
    ni = ks // tm
    return _mm_dw(name, a, g, out_shape=(NCHIP, ks, g.shape[1]), out_map=lambda i, j: (i // ni, i % ni, j),
                  tm=tm, tn=tn)


def _row_spec(width, col=0):
    return pl.BlockSpec((TB, width), lambda i: (i, col))


def _vec_spec(width):
    return pl.BlockSpec((1, width), lambda i: (0, 0))


def _rms_fwd(x, g):
    def body(x_ref, g_ref, h_ref):
        x = x_ref[...]
        r = lax.rsqrt(jnp.mean(x * x, axis=-1, keepdims=True) + EPS)
        h_ref[...] = (x * r * g_ref[...]).astype(BF16)

    return _pcall(body, name="rms_fwd", out_shape=_sds((T, D), BF16), grid=(T // TB,),
                  in_specs=[_row_spec(D), _vec_spec(D)], out_specs=_row_spec(D), semantics=("parallel",))(x, g)


def _rms_bwd(x, g, dh, dres, deps=()):
    deps = tuple(d for d in deps if d is not None)

    def body(*refs):
        x_ref, g_ref, dh_ref, dres_ref = refs[:4]
        dx_ref, dg_ref = refs[-2:]
        x = x_ref[...]
        r = lax.rsqrt(jnp.mean(x * x, axis=-1, keepdims=True) + EPS)
        y = x * r
        dh = dh_ref[...]
        dy = dh * g_ref[...]
        dx_ref[...] = dres_ref[...] + r * (dy - y * jnp.mean(dy * y, axis=-1, keepdims=True))

        @pl.when(pl.program_id(0) == 0)
        def _():
            dg_ref[...] = jnp.zeros_like(dg_ref)

        dg_ref[...] += jnp.sum(dh * y, axis=0, keepdims=True)

    return _pcall(body, name="rms_bwd", out_shape=(_sds((T, D)), _sds((1, D))), grid=(T // TB,),
                  in_specs=[_row_spec(D), _vec_spec(D), _row_spec(D), _row_spec(D)] + [ANY_SPEC] * len(deps),
                  out_specs=(_row_spec(D), _vec_spec(D)), semantics=("arbitrary",))(x, g, dh, dres, *deps)


def _sigmoid(x):
    return 1.0 / (1.0 + jnp.exp(-x))


def _gate_fwd(u, ycv, yat):
    def body(gc_ref, ga_ref, yc_ref, ya_ref, m_ref):
        m_ref[...] = (_sigmoid(gc_ref[...].astype(F32)) * yc_ref[...]
                      + _sigmoid(ga_ref[...].astype(F32)) * ya_ref[...]).astype(BF16)

    blk = lambda off: pl.BlockSpec((TB, 512), lambda i, j: (i, off + j))
    return _pcall(body, name="gate_fwd", out_shape=_sds((T, D), BF16), grid=(T // TB, 2),
                  in_specs=[blk(UB_GC), blk(UB_GA), blk(0), blk(0)], out_specs=blk(0),
                  semantics=("parallel", "parallel"))(u, u, ycv, yat)


def _gate_bwd(u, ycv, yat, dm):
    def body(gc_ref, ga_ref, yc_ref, ya_ref, dm_ref, dyc_ref, dya_ref, dgc_ref, dga_ref):
        dm = dm_ref[...]
        sc = _sigmoid(gc_ref[...].astype(F32))
        sa = _sigmoid(ga_ref[...].astype(F32))
        dyc_ref[...] = (dm * sc).astype(BF16)
        dya_ref[...] = (dm * sa).astype(BF16)
        dgc_ref[...] = (dm * yc_ref[...] * sc * (1.0 - sc)).astype(BF16)
        dga_ref[...] = (dm * ya_ref[...] * sa * (1.0 - sa)).astype(BF16)

    blk = lambda off: pl.BlockSpec((TB, 512), lambda i, j: (i, off + j))
    return _pcall(body, name="gate_bwd",
                  out_shape=(_sds((T, D), BF16), _sds((T, D), BF16), _sds((T, D), BF16), _sds((T, D), BF16)),
                  grid=(T // TB, 2), in_specs=[blk(UB_GC), blk(UB_GA), blk(0), blk(0), blk(0)],
                  out_specs=(blk(0), blk(0), blk(0), blk(0)),
                  semantics=("parallel", "parallel"))(u, u, ycv, yat, dm)


def _loss_fwd_bwd(y, target):
    def body(y_ref, t_ref, loss_ref, dy_ref):
        e = y_ref[...] - t_ref[...]
        dy_ref[...] = e * (1.0 / D)

        @pl.when(pl.program_id(0) == 0)
        def _():
            loss_ref[...] = jnp.zeros_like(loss_ref)

        loss_ref[...] += 0.5 * jnp.sum(jnp.mean(e * e, axis=-1, keepdims=True))

    return _pcall(body, name="loss", out_shape=(_sds((8, 128)), _sds((T, D))), grid=(T // TB,),
                  in_specs=[_row_spec(D), _row_spec(D)],
                  out_specs=(pl.BlockSpec((8, 128), lambda i: (0, 0)), _row_spec(D)),
                  semantics=("arbitrary",))(y, target)


PAD = 32
CCH = 256


def _conv_fwd(u, dw_w, dw_b):
    def body(a_ref, gt_ref, w_ref, b_ref, z1_ref, zp_ref):
        zp_ref[0:PAD, :] = jnp.zeros((PAD, 128), F32)
        zp_ref[PAD:PAD + T, :] = a_ref[...].astype(F32) * _sigmoid(gt_ref[...].astype(F32))
        for c in range(T // CCH):
            acc = jnp.broadcast_to(b_ref[...], (CCH, 128))
            for j in range(KW):
                acc = acc + w_ref[j:j + 1, :] * zp_ref[pl.ds(c * CCH + j + PAD - (KW - 1), CCH), :]
            z1_ref[c * CCH:(c + 1) * CCH, :] = acc

    col = lambda off: pl.BlockSpec((T, 128), lambda j: (0, off * 4 + j))
    return _pcall(body, name="conv_fwd", out_shape=_sds((T, CONV)), grid=(CONV // 128,),
                  in_specs=[col(UB_A), col(UB_GT), pl.BlockSpec((KW, 128), lambda j: (0, j)),
                            pl.BlockSpec((1, 128), lambda j: (0, j))],
                  out_specs=col(0), scratch=[pltpu.VMEM((T + PAD, 128), F32)],
                  semantics=("parallel",))(u, u, dw_w, dw_b)


def _ln_silu_fwd(z1, g, b):
    def body(z_ref, g_ref, b_ref, o_ref):
        z = z_ref[...]
        mu = jnp.mean(z, axis=-1, keepdims=True)
        zc = z - mu
        zh = zc * lax.rsqrt(jnp.mean(zc * zc, axis=-1, keepdims=True) + EPS)
        z2 = zh * g_ref[...] + b_ref[...]
        o_ref[...] = (z2 * _sigmoid(z2)).astype(BF16)

    return _pcall(body, name="ln_silu_fwd", out_shape=_sds((T, CONV), BF16), grid=(T // TB,),
                  in_specs=[_row_spec(CONV), _vec_spec(CONV), _vec_spec(CONV)], out_specs=_row_spec(CONV),
                  semantics=("parallel",))(z1, g, b)


def _ln_silu_bwd(z1, g, b, dz3):
    def body(z_ref, g_ref, b_ref, d_ref, z3_ref, dz1_ref, dg_ref, db_ref):
        z = z_ref[...]
        mu = jnp.mean(z, axis=-1, keepdims=True)
        zc = z - mu
        rs = lax.rsqrt(jnp.mean(zc * zc, axis=-1, keepdims=True) + EPS)
        zh = zc * rs
        z2 = zh * g_ref[...] + b_ref[...]
        s = _sigmoid(z2)
        z3_ref[...] = (z2 * s).astype(BF16)
        dz2 = d_ref[...] * (s * (1.0 + z2 * (1.0 - s)))
        dzh = dz2 * g_ref[...]
        dz1_ref[...] = rs * (dzh - jnp.mean(dzh, axis=-1, keepdims=True)
                             - zh * jnp.mean(dzh * zh, axis=-1, keepdims=True))

        @pl.when(pl.program_id(0) == 0)
        def _():
            dg_ref[...] = jnp.zeros_like(dg_ref)
            db_ref[...] = jnp.zeros_like(db_ref)

        dg_ref[...] += jnp.sum(dz2 * zh, axis=0, keepdims=True)
        db_ref[...] += jnp.sum(dz2, axis=0, keepdims=True)

    return _pcall(body, name="ln_silu_bwd",
                  out_shape=(_sds((T, CONV), BF16), _sds((T, CONV)), _sds((1, CONV)), _sds((1, CONV))),
                  grid=(T // TB,),
                  in_specs=[_row_spec(CONV), _vec_spec(CONV), _vec_spec(CONV), _row_spec(CONV)],
                  out_specs=(_row_spec(CONV), _row_spec(CONV), _vec_spec(CONV), _vec_spec(CONV)),
                  semantics=("arbitrary",))(z1, g, b, dz3)


def _conv_bwd(u, dw_w, dz1):
    def body(a_ref, gt_ref, w_ref, dz1_ref, da_ref, dgt_ref, dw_ref, db_ref, zp_ref, dp_ref):
        a = a_ref[...].astype(F32)
        s = _sigmoid(gt_ref[...].astype(F32))
        zp_ref[0:PAD, :] = jnp.zeros((PAD, 128), F32)
        zp_ref[PAD:PAD + T, :] = a * s
        dp_ref[0:T, :] = dz1_ref[...]
        dp_ref[T:T + PAD, :] = jnp.zeros((PAD, 128), F32)
        db_ref[...] = jnp.sum(dz1_ref[...], axis=0, keepdims=True)
        for j in range(KW):
            tot = jnp.zeros((1, 128), F32)
            for c in range(T // CCH):
                tot = tot + jnp.sum(dz1_ref[c * CCH:(c + 1) * CCH, :]
                                    * zp_ref[pl.ds(c * CCH + j + PAD - (KW - 1), CCH), :], axis=0, keepdims=True)
            dw_ref[j:j + 1, :] = tot
        for c in range(T // CCH):
            acc = jnp.zeros((CCH, 128), F32)
            for j in range(KW):
                acc = acc + w_ref[j:j + 1, :] * dp_ref[pl.ds(c * CCH + (KW - 1) - j, CCH), :]
            rows = slice(c * CCH, (c + 1) * CCH)
            sc = _sigmoid(gt_ref[rows, :].astype(F32))
            da_ref[rows, :] = (acc * sc).astype(BF16)
            dgt_ref[rows, :] = (acc * a_ref[rows, :].astype(F32) * sc * (1.0 - sc)).astype(BF16)

    col = lambda off: pl.BlockSpec((T, 128), lambda j: (0, off * 4 + j))
    wspec = pl.BlockSpec((KW, 128), lambda j: (0, j))
    return _pcall(body, name="conv_bwd",
                  out_shape=(_sds((T, CONV), BF16), _sds((T, CONV), BF16), _sds((KW, CONV)), _sds((1, CONV))),
                  grid=(CONV // 128,), in_specs=[col(UB_A), col(UB_GT), wspec, col(0)],
                  out_specs=(col(0), col(0), wspec, pl.BlockSpec((1, 128), lambda j: (0, j))),
                  scratch=[pltpu.VMEM((T + PAD, 128), F32), pltpu.VMEM((T + PAD, 128), F32)],
                  semantics=("parallel",))(u, u, dw_w, dz1)


def _bucket_tables():
    qi = np.arange(BLK)[:, None]
    kj = np.arange(2 * BLK)[None, :]
    off = np.clip(qi + BLK - kj, 0, BLK)
    out = []
    for d in DIL:
        dist = (off * d).astype(np.int32)
        nf = np.maximum(dist, 1).astype(np.float32)
        large = 16 + (np.log(nf / np.float32(16)) / np.float32(math.log(2048 / 16)) * np.float32(16)).astype(np.int32)
        large = np.minimum(large, NBUCKET - 1)
        out.append(np.where(dist < 16, dist, large))
    return np.stack(out).astype(np.int32)


def _band():
    off = lax.broadcasted_iota(jnp.int32, (BLK, 2 * BLK), 0) + BLK - lax.broadcasted_iota(jnp.int32, (BLK, 2 * BLK), 1)
    return (off >= 0) & (off <= BLK)


def _bias_table(rel_bias_t, buckets):
    def body(rb_ref, bk_ref, o_ref):
        h = pl.program_id(0)
        bk = bk_ref[...]
        acc = jnp.zeros((BLK, 2 * BLK), F32)
        for b in range(NBUCKET):
            acc = jnp.where(bk == b, rb_ref[h, b], acc)
        o_ref[...] = jnp.where(_band(), acc, NEG)

    return _pcall(body, name="bias_table", out_shape=_sds((3 * 8, BLK, 2 * BLK)), grid=(24,),
                  in_specs=[pl.BlockSpec(memory_space=pltpu.SMEM),
                            pl.BlockSpec((None, BLK, 2 * BLK), lambda h: (h // 8, 0, 0))],
                  out_specs=pl.BlockSpec((None, BLK, 2 * BLK), lambda h: (h, 0, 0)),
                  semantics=("parallel",))(rel_bias_t, buckets)


def _bias_grad(ds_acc, buckets):
    def body(a_ref, bk_ref, o_ref):
        acc = a_ref[0]
        for l in range(1, DEPTH):
            acc = acc + a_ref[l]
        bk = bk_ref[...]
        lane = lax.broadcasted_iota(jnp.int32, (1, 128), 1)
        row = jnp.zeros((1, 128), F32)
        for b in range(NBUCKET):
            row = jnp.where(lane == b, jnp.sum(jnp.where(bk == b, acc, 0.0)), row)
        o_ref[...] = row

    return _pcall(body, name="bias_grad", out_shape=_sds((24, 1, 128)), grid=(24,),
                  in_specs=[pl.BlockSpec((DEPTH, None, BLK, 2 * BLK), lambda h: (0, h, 0, 0)),
                            pl.BlockSpec((None, BLK, 2 * BLK), lambda h: (h // 8, 0, 0))],
                  out_specs=pl.BlockSpec((None, 1, 128), lambda h: (h, 0, 0)),
                  semantics=("parallel",))(ds_acc, buckets)


def _head_mask():
    return lax.broadcasted_iota(jnp.int32, (1, 128), 1) < HD


def _seg_ones(width):
    r = lax.broadcasted_iota(jnp.int32, (width, width), 0) >> 6
    c = lax.broadcasted_iota(jnp.int32, (width, width), 1) >> 6
    return (r == c).astype(BF16)


def _seg_sum(x, ones):
    hi = x.astype(BF16)
    lo = (x - hi.astype(F32)).astype(BF16)
    return (jnp.dot(hi, ones, preferred_element_type=F32) + jnp.dot(lo, ones, preferred_element_type=F32))


def _dot(a, b, dims):
    return lax.dot_general(a, b, dims, preferred_element_type=F32)


def _tile_rows(d, r, n):
    stride = None if d == 1 else d
    q_rows = pl.ds(r + d * n * BLK, BLK, stride=stride)
    if n == 0:
        return q_rows, q_rows, BLK
    return q_rows, pl.ds(r + d * (n - 1) * BLK, 2 * BLK, stride=stride), 2 * BLK


def _stack_heads(x, m_a):
    return jnp.concatenate([jnp.where(m_a, x, 0.0), jnp.where(m_a, 0.0, x)], axis=0)


def _stack_rows(x, m_a, width):
    other = pltpu.roll(x, HD, axis=1)
    both = jnp.concatenate([jnp.where(m_a, x, other), jnp.where(m_a, other, x)], axis=0)
    return both if width == 128 else jnp.concatenate([both] * (width // 128), axis=1)


NCH = 256


def _qk_norm_prep(q_ref, k_ref, v_ref, gq_ref, gk_ref, qn_ref, kn_ref, vn_ref, ones):
    def prep(i, carry):
        rows = pl.ds(pl.multiple_of(i * NCH, NCH), NCH)
        q = q_ref[rows, :].astype(F32)
        qn_ref[rows, :] = q * lax.rsqrt(_seg_sum(q * q, ones) * (1.0 / HD) + EPS) * gq_ref[...] * (HD ** -0.5)
        k = k_ref[rows, :].astype(F32)
        kn_ref[rows, :] = k * lax.rsqrt(_seg_sum(k * k, ones) * (1.0 / HD) + EPS) * gk_ref[...]
        vn_ref[rows, :] = v_ref[rows, :].astype(F32)
        return carry

    lax.fori_loop(0, T // NCH, prep, 0)


def _attn_specs(g):
    ucol = lambda base: pl.BlockSpec((T, 128), lambda hp: (0, (base + g) * 4 + hp))
    col = pl.BlockSpec((T, 128), lambda hp: (0, hp))
    vec = pl.BlockSpec((1, 128), lambda hp: (0, 0))
    bm = pl.BlockSpec((2, BLK, 2 * BLK), lambda hp: (g * 4 + hp, 0, 0))
    return ucol, col, vec, bm


def _attn_fwd(g, u, gq, gk, bm, deps=()):
    d = DIL[g]

    def body(*refs):
        q_ref, k_ref, v_ref, gq_ref, gk_ref, bm_ref = refs[:6]
        o_ref, lse_ref, qn_ref, kn_ref, vn_ref = refs[-5:]
        ones = _seg_ones(128)
        _qk_norm_prep(q_ref, k_ref, v_ref, gq_ref, gk_ref, qn_ref, kn_ref, vn_ref, ones)
        m_a = _head_mask()
        for r in range(d):
            for n in range(T // d // BLK):
                q_rows, k_rows, nk = _tile_rows(d, r, n)
                qt = qn_ref[q_rows, :]
                kt = kn_ref[k_rows, :].astype(BF16)
                vt = vn_ref[k_rows, :].astype(BF16)
                q2 = _stack_heads(qt, m_a).astype(BF16)
                s = _dot(q2, kt, NT) + bm_ref[...].reshape(2 * BLK, 2 * BLK)[:, 2 * BLK - nk:]
                mx = jnp.max(s, axis=1, keepdims=True)
                p = jnp.exp(s - mx)
                l = jnp.sum(p, axis=1, keepdims=True)
                o2 = _dot(p.astype(BF16), vt, NN) / l
                lse2 = jnp.broadcast_to(mx + jnp.log(l), (2 * BLK, 128))
                o_ref[q_rows, :] = jnp.where(m_a, o2[:BLK], o2[BLK:])
                lse_ref[q_rows, :] = jnp.where(m_a, lse2[:BLK], lse2[BLK:])

    ucol, col, vec, bmspec = _attn_specs(g)
    return _pcall(body, name=f"attn_fwd_g{g}", out_shape=(_sds((T, AOUT)), _sds((T, AOUT))), grid=(4,),
                  in_specs=[ucol(UB_Q), ucol(UB_K), ucol(UB_V), vec, vec, bmspec] + [ANY_SPEC] * len(deps),
                  out_specs=(col, col), scratch=[pltpu.VMEM((T, 128), F32)] * 3,
                  semantics=("parallel",))(u, u, u, gq, gk, bm, *deps)


def _attn_bwd(g, u, gq, gk, bm, dog, cb, lse):
    d = DIL[g]

    def body(q_ref, k_ref, v_ref, gq_ref, gk_ref, bm_ref, do_ref, cb_ref, lse_ref,
             dqo_ref, dko_ref, dvo_ref, dgq_ref, dgk_ref, dsa_ref, qn_ref, kn_ref, vn_ref, dq_ref, dk_ref, dv_ref):
        ones = _seg_ones(128)
        _qk_norm_prep(q_ref, k_ref, v_ref, gq_ref, gk_ref, qn_ref, kn_ref, vn_ref, ones)
        m_a = _head_mask()
        dk_ref[...] = jnp.zeros_like(dk_ref)
        dv_ref[...] = jnp.zeros_like(dv_ref)
        dsa_ref[...] = jnp.zeros_like(dsa_ref)
        for r in range(d):
            for n in range(T // d // BLK):
                q_rows, k_rows, nk = _tile_rows(d, r, n)
                ktb = kn_ref[k_rows, :].astype(BF16)
                vtb = vn_ref[k_rows, :].astype(BF16)
                q2 = _stack_heads(qn_ref[q_rows, :], m_a).astype(BF16)
                do2 = _stack_heads(do_ref[q_rows, :], m_a).astype(BF16)
                lse_c = _stack_rows(lse_ref[q_rows, :], m_a, nk)
                c_c = _stack_rows(cb_ref[q_rows, :], m_a, nk)
                s = _dot(q2, ktb, NT) + bm_ref[...].reshape(2 * BLK, 2 * BLK)[:, 2 * BLK - nk:]
                p = jnp.exp(s - lse_c)
                ds = p * (_dot(do2, vtb, NT) + c_c)
                dsb = ds.astype(BF16)
                dq2 = _dot(dsb, ktb, NN)
                dq_ref[q_rows, :] = jnp.where(m_a, dq2[:BLK], dq2[BLK:])
                dk_ref[k_rows, :] += _dot(dsb, q2, TN)
                dv_ref[k_rows, :] += _dot(p.astype(BF16), do2, TN)
                dsa_ref[:, :, 2 * BLK - nk:] += ds.reshape(2, BLK, nk)

        @pl.when(pl.program_id(0) == 0)
        def _():
            dgq_ref[...] = jnp.zeros_like(dgq_ref)
            dgk_ref[...] = jnp.zeros_like(dgk_ref)

        def norm_bwd(i, carry):
            rows = pl.ds(pl.multiple_of(i * NCH, NCH), NCH)
            for x_ref, g_ref, dx_ref, dxo_ref, dg_ref, scale in (
                    (q_ref, gq_ref, dq_ref, dqo_ref, dgq_ref, HD ** -0.5), (k_ref, gk_ref, dk_ref, dko_ref, dgk_ref, 1.0)):
                x = x_ref[rows, :].astype(F32)
                rs = lax.rsqrt(_seg_sum(x * x, ones) * (1.0 / HD) + EPS)
                xh = x * rs
                dn = dx_ref[rows, :] * scale
                dxh = dn * g_ref[...]
                dxo_ref[rows, :] = (rs * (dxh - xh * (_seg_sum(dxh * xh, ones) * (1.0 / HD)))).astype(BF16)
                dg_ref[...] += jnp.sum(dn * xh, axis=0, keepdims=True)
            dvo_ref[rows, :] = dv_ref[rows, :].astype(BF16)
            return carry

        lax.fori_loop(0, T // NCH, norm_bwd, 0)

    ucol, col, vec, bmspec = _attn_specs(g)
    return _pcall(body, name=f"attn_bwd_g{g}",
                  out_shape=(_sds((T, AOUT), BF16), _sds((T, AOUT), BF16), _sds((T, AOUT), BF16), _sds((1, 128)),
                             _sds((1, 128)), _sds((8, BLK, 2 * BLK))),
                  grid=(4,),
                  in_specs=[ucol(UB_Q), ucol(UB_K), ucol(UB_V), vec, vec, bmspec, col, col, col],
                  out_specs=(col, col, col, vec, vec, pl.BlockSpec((2, BLK, 2 * BLK), lambda hp: (hp, 0, 0))),
                  scratch=[pltpu.VMEM((T, 128), F32)] * 6,
                  semantics=("arbitrary",))(u, u, u, gq, gk, bm, dog, cb, lse)


def _combine_fwd(ogs, lses):
    def body(o0, o1, o2, l0, l1, l2, o_ref):
        ls = [l0[...], l1[...], l2[...]]
        mx = jnp.maximum(jnp.maximum(ls[0], ls[1]), ls[2])
        es = [jnp.exp(l - mx) for l in ls]
        inv = 1.0 / (es[0] + es[1] + es[2])
        o_ref[...] = ((es[0] * o0[...] + es[1] * o1[...] + es[2] * o2[...]) * inv).astype(BF16)

    return _pcall(body, name="combine_fwd", out_shape=_sds((T, AOUT), BF16), grid=(T // TB,),
                  in_specs=[_row_spec(AOUT)] * 6, out_specs=_row_spec(AOUT), semantics=("parallel",))(*ogs, *lses)


def _combine_bwd(ogs, lses, do):
    def body(o0, o1, o2, l0, l1, l2, do_ref, d0, d1, d2, c0, c1, c2):
        ls = [l0[...], l1[...], l2[...]]
        mx = jnp.maximum(jnp.maximum(ls[0], ls[1]), ls[2])
        es = [jnp.exp(l - mx) for l in ls]
        inv = 1.0 / (es[0] + es[1] + es[2])
        ws = [e * inv for e in es]
        do = do_ref[...]
        o = ws[0] * o0[...] + ws[1] * o1[...] + ws[2] * o2[...]
        s = _seg_sum(do * o, _seg_ones(AOUT))
        for w, d_ref, c_ref in zip(ws, (d0, d1, d2), (c0, c1, c2)):
            d_ref[...] = w * do
            c_ref[...] = -(w * s)

    return _pcall(body, name="combine_bwd", out_shape=tuple(_sds((T, AOUT)) for _ in range(6)), grid=(T // TB,),
                  in_specs=[_row_spec(AOUT)] * 7, out_specs=tuple(_row_spec(AOUT) for _ in range(6)),
                  semantics=("parallel",))(*ogs, *lses, do)


def _layer_fwd(x, p, bm, deps=(), mid=None):
    h1 = _rms_fwd(x, p["n1g"])
    u = _mm_x_wcols("mm_u", h1, p["win4"], tm=T, tn=1920, out_dtype=BF16, deps=deps)
    ogs, lses = [], []
    for g in range(NG):
        gdeps = (p["hook"](ogs[-1]),) if g == NG - 1 and "hook" in p else ()
        og, lse = _attn_fwd(g, u, p["gq"], p["gk"], bm, deps=gdeps)
        ogs.append(og)
        lses.append(lse)
    o = _combine_fwd(ogs, lses)
    z1 = _conv_fwd(u, p["dww"], p["dwb"])
    z3 = _ln_silu_fwd(z1, p["lng"], p["lnb"])
    if "rest" in p:
        p = {**p, **p["rest"](z3)}
    ycv = _mm_x_wcols("mm_ycv", z3, p["wco4"], tm=T, tn=256)
    yat = _mm_x_wcols("mm_yat", o, p["wao4"], tm=T, tn=256)
    m = _gate_fwd(u, ycv, yat)
    xm = _mm_x_wrows("mm_xmid", m, p["wout4"], x, tm=1024, tk=256, tn=1024)
    h2 = _rms_fwd(xm, p["n2g"])
    fa, r = _mm_ff1(h2, p["wff14"], tm=T, tn=1024)
    tok = mid(r) if mid else None
    xo = _mm_x_wrows("mm_xout", r, p["wff24"], xm, tm=1024, tk=1024, tn=1024, deps=(tok,))
    saved = dict(x=x, h1=h1, u=u, z1=z1, ogs=ogs, lses=lses, o=o, ycv=ycv, yat=yat, m=m, xm=xm, h2=h2, fa=fa, r=r)
    return xo, p, saved


EARLY = ("w_ff2", "w_ff1", "w_out")
LATE = ("w_conv_out", "w_attn_out", "w_in")


def _layer_bwd(dx, s, p, bm, pipe=None, own_early=None, own_late=None):
    u = s["u"]
    tok = pipe.step0() if pipe else None
    df = _mm_dff2(dx, p["wff24"], s["fa"], tm=T, tn=1024, deps=(tok,))
    g_ff2 = _mm_dw_rows("mm_dwff2", s["r"], dx, ks=1024, tm=1024, tn=1024)
    g_ff1 = _mm_dw_cols("mm_dwff1", s["h2"], df, ns=1024, tm=1024, tn=1024)
    tok = pipe.step1(g_ff1) if pipe else None
    dh2 = _mm_g_wcols_t("mm_dh2", df, p["wff14"], tm=T, tk=1024, tn=1024, deps=(tok,))
    dxm, d_n2g = _rms_bwd(s["xm"], p["n2g"], dh2, dx)

    dm = _mm_g_wrows_t("mm_dm", dxm, p["wout4"], tm=1024, tn=256)
    g_out = _mm_dw_rows("mm_dwout", s["m"], dxm, ks=256, tm=256, tn=1024)
    early = own_early(dict(w_ff2=g_ff2, w_ff1=g_ff1, w_out=g_out)) if own_early else None
    tok_e = early.step0() if early else None
    dyc, dya, dgc, dga = _gate_bwd(u, s["ycv"], s["yat"], dm)

    dz3 = _mm_g_wcols_t("mm_dz3", dyc, p["wco4"], tm=T, tk=256, tn=512, deps=(tok_e,))
    z3, dz1, d_lng, d_lnb = _ln_silu_bwd(s["z1"], p["lng"], p["lnb"], dz3)
    g_co = _mm_dw_cols("mm_dwco", z3, dyc, ns=256, tm=512, tn=256)
    da, dgt, d_dww, d_dwb = _conv_bwd(u, p["dww"], dz1)

    tok_e = early.step1(da) if early else None
    do = _mm_g_wcols_t("mm_do", dya, p["wao4"], tm=T, tk=256, tn=512, deps=(tok_e,))
    g_ao = _mm_dw_cols("mm_dwao", s["o"], dya, ns=256, tm=512, tn=256)
    parts = _combine_bwd(s["ogs"], s["lses"], do)
    dqs, dks, dvs, d_gq, d_gk, dsas = [], [], [], [], [], []
    for g in range(NG):
        dq, dk, dv, dgq, dgk, dsa = _attn_bwd(g, u, p["gq"], p["gk"], bm, parts[g], parts[NG + g], s["lses"][g])
        dqs.append(dq)
        dks.append(dk)
        dvs.append(dv)
        d_gq.append(dgq)
        d_gk.append(dgk)
        dsas.append(dsa)
    du = jnp.concatenate([da, dgt] + dqs + dks + dvs + [dgc, dga], axis=1)
    tok = pipe.step2(du) if pipe else None
    tok_e = early.step2(du) if early else None
    g_in = _mm_dw_cols("mm_dwin", s["h1"], du, ns=1920, tm=1024, tn=1920, deps=(tok, tok_e))
    late = own_late(dict(w_in=g_in, w_conv_out=g_co, w_attn_out=g_ao)) if own_late else None
    tok_l = late.step0() if late else None
    dh1 = _mm_g_wcols_t("mm_dh1", du, p["win4"], tm=1024, tk=1920, tn=1024, deps=(tok_l,))
    tok_l = late.step1(dh1) if late else None
    dxi, d_n1g = _rms_bwd(s["x"], p["n1g"], dh1, dxm, deps=(tok_l,))
    if pipe:
        pipe.step3(dxi)

    fold = lambda parts_: sum(v[0, :HD] + v[0, HD:] for v in parts_)
    big = dict(w_in=g_in, w_conv_out=g_co, w_attn_out=g_ao, w_out=g_out, w_ff1=g_ff1, w_ff2=g_ff2)
    small = dict(norm1_g=d_n1g[0], q_norm_g=fold(d_gq), k_norm_g=fold(d_gk), conv_dw_w=d_dww, conv_dw_b=d_dwb[0],
                 conv_ln_g=d_lng[0], conv_ln_b=d_lnb[0], norm2_g=d_n2g[0])
    return dxi, big, small, jnp.concatenate(dsas, axis=0), early, late


def _local_step(x, target, get_layer, rel_bias, make_pipe):
    buckets = jnp.asarray(_bucket_tables())
    bm = _bias_table(rel_bias.T, buckets)
    saved, layers = [], []
    for l in range(DEPTH):
        p, deps, mid = get_layer(l, x)
        x, p, s = _layer_fwd(x, p, bm, deps=deps, mid=mid)
        layers.append(p)
        saved.append(s)
    loss_blk, dx = _loss_fwd_bwd(x, target)
    smalls, dsas = [None] * DEPTH, [None] * DEPTH
    pipe, pipes = None, []
    for l in reversed(range(DEPTH)):
        if l > 0:
            dx, big, smalls[l], dsas[l], _, _ = _layer_bwd(dx, saved[l], layers[l], bm, pipe)
            pipe = make_pipe(l, BIG, "", big)
            pipes.append(pipe)
        else:
            dx, big, smalls[l], dsas[l], early, late = _layer_bwd(
                dx, saved[l], layers[l], bm, pipe, lambda big_: make_pipe(0, EARLY, "e", big_),
                lambda big_: make_pipe(0, LATE, "", big_))
    d_rel = _bias_grad(jnp.stack(dsas), buckets)[:, 0, :NBUCKET].T
    return loss_blk[0, 0], dx, smalls, d_rel, pipes, early, late


MESH = pl.DeviceIdType.MESH


def _me():
    return lax.axis_index("x"), lax.axis_index("y"), lax.axis_index("c")


def _other_chips(mx, my):
    return [(1 - mx, my), (mx, 1 - my), (1 - mx, 1 - my)]


def _rcopy(src, dst, send_sems, recv_sems, k, dev):
    return pltpu.make_async_remote_copy(src_ref=src, dst_ref=dst, send_sem=send_sems.at[k], recv_sem=recv_sems.at[k],
                                        device_id=dev, device_id_type=MESH)


def _comm_call(body, name, out_shape, n_in, n_sems):
    return pl.pallas_call(
        body, name=name, out_shape=out_shape, in_specs=[HBM_SPEC] * n_in,
        out_specs=jax.tree.map(lambda _: HBM_SPEC, out_shape),
        scratch_shapes=[pltpu.SemaphoreType.DMA((n_sems,)), pltpu.SemaphoreType.DMA((n_sems,)),
                        pltpu.SemaphoreType.DMA(())],
        compiler_params=pltpu.CompilerParams(has_side_effects=True))


def _all_gather_chips(x, name):
    def body(x_ref, o_ref, send_sems, recv_sems, local_sem):
        mx, my, mc = _me()
        local = pltpu.make_async_copy(x_ref, o_ref.at[2 * mx + my], local_sem)
        local.start()
        sends = [_rcopy(x_ref, o_ref.at[2 * mx + my], send_sems, recv_sems, k, (px, py, mc))
                 for k, (px, py) in enumerate(_other_chips(mx, my))]
        for cp in sends:
            cp.start()
        for k, (px, py) in enumerate(_other_chips(mx, my)):
            _rcopy(x_ref, o_ref.at[2 * px + py], send_sems, recv_sems, k, (px, py, mc)).wait_recv()
        for cp in sends:
            cp.wait_send()
        local.wait()

    return _comm_call(body, name, _sds((NCHIP,) + x.shape, x.dtype), 1, 3)(x)


EFFECT = pltpu.SideEffectType.DATAFLOW_SIDE_EFFECTING


def _hbm(a):
    return pltpu.with_memory_space_constraint(a, pltpu.HBM)


def _split_start(name, bufs, plan, n, after=None):
    nb = len(bufs)
    extra = [] if after is None else [after]
    ne = len(extra)

    def body(*refs):
        send_sems, recv_sems, token = refs[nb + ne], refs[nb + ne + 1], refs[-1]
        mx, my, mc = _me()
        for k, (src, dst, dev, _) in enumerate(plan(refs[:nb], mx, my, mc)):
            _rcopy(src, dst, send_sems, recv_sems, k, dev).start()
        token[...] = jnp.zeros_like(token)

    out = pl.pallas_call(
        body, name=name,
        out_shape=(pltpu.SemaphoreType.DMA((n,)), pltpu.SemaphoreType.DMA((n,)),
                   *[pltpu.HBM(b.shape, b.dtype) for b in bufs], _sds((8, 128))),
        in_specs=[HBM_SPEC] * nb + [ANY_SPEC] * ne,
        out_specs=(SEM_SPEC, SEM_SPEC, *[HBM_SPEC] * nb, pl.BlockSpec(memory_space=pltpu.VMEM)),
        input_output_aliases={i: 2 + i for i in range(nb)},
        compiler_params=pltpu.CompilerParams(has_side_effects=EFFECT))(*[_hbm(b) for b in bufs], *extra)
    return (out[0], out[1]), list(out[2:2 + nb]), out[-1]


def _split_wait(name, sems, bufs, plan, after):
    nb = len(bufs)

    def body(*refs):
        send_sems, recv_sems = refs[nb], refs[nb + 1]
        mx, my, mc = _me()
        for k, (src, dst, dev, land) in enumerate(plan(refs[:nb], mx, my, mc)):
            _rcopy(src, dst, send_sems, recv_sems, k, dev).wait_send()
            _rcopy(src, land, send_sems, recv_sems, k, dev).wait_recv()

    out = pl.pallas_call(
        body, name=name, out_shape=tuple(pltpu.HBM(b.shape, b.dtype) for b in bufs),
        in_specs=[HBM_SPEC] * nb + [SEM_SPEC, SEM_SPEC, ANY_SPEC], out_specs=(HBM_SPEC,) * nb,
        input_output_aliases={i: i for i in range(nb)},
        compiler_params=pltpu.CompilerParams(has_side_effects=EFFECT))(*bufs, sems[0], sems[1], after)
    return list(out)


def _plan_gather_chips(refs, mx, my, mc):
    me = 2 * mx + my
    return [(r.at[me, mc], r.at[me, mc], (px, py, mc), r.at[2 * px + py, mc])
            for r in refs for px, py in _other_chips(mx, my)]


def _plan_gather_pair(refs, mx, my, mc):
    return [(r.at[2 * px + py, mc], r.at[2 * px + py, mc], (mx, my, 1 - mc), r.at[2 * px + py, 1 - mc])
            for r in refs for px, py in _other_chips(mx, my)]


def _plan_gather_devices(refs, mx, my, mc):
    flip = lambda m, b: 1 - m if b else m
    peers = [(flip(mx, k >> 2 & 1), flip(my, k >> 1 & 1), flip(mc, k & 1)) for k in range(1, 8)]
    slot = lambda dev: 4 * dev[0] + 2 * dev[1] + dev[2]
    me = slot((mx, my, mc))
    return [(r.at[me], r.at[me], dev, r.at[slot(dev)]) for r in refs for dev in peers]


def _plan_pair_half(refs, mx, my, mc):
    n = len(refs) // 2
    return [(g.at[:, 1 - mc], r, (mx, my, 1 - mc), r) for g, r in zip(refs[:n], refs[n:])]


def _plan_scatter(refs, mx, my, mc):
    n = len(refs) // 2
    return [(q.at[2 * px + py], r.at[k], (px, py, mc), r.at[k])
            for q, r in zip(refs[:n], refs[n:]) for k, (px, py) in enumerate(_other_chips(mx, my))]


def _plan_pair_fill(refs, mx, my, mc):
    return [(r.at[mc], r.at[mc], (mx, my, 1 - mc), r.at[1 - mc]) for r in refs]


def _row_tile(rows, cols):
    t = 8
    while t * 2 * cols * 4 <= (1 << 21) and rows % (t * 2) == 0:
        t *= 2
    return t


def _prefetch_call(body, name, out_shape, grid, in_specs, out_specs):
    return pl.pallas_call(
        body, name=name, out_shape=out_shape,
        grid_spec=pltpu.PrefetchScalarGridSpec(num_scalar_prefetch=1, grid=grid, in_specs=in_specs,
                                               out_specs=out_specs),
        compiler_params=pltpu.CompilerParams(vmem_limit_bytes=VMEM_LIMIT,
                                             dimension_semantics=("parallel",) * len(grid)))


def _sum_half(g, r1, place, name):
    _, _, rr, ns = g.shape
    tr = _row_tile(rr, ns)

    def body(c_ref, g_ref, r_ref, o_ref, ob_ref):
        q = g_ref[...] + r_ref[...]
        ob_ref[...] = q.astype(BF16)

        @pl.when(pl.program_id(1) == c_ref[0])
        def _():
            o_ref[...] = q

    blk = pl.BlockSpec((None, tr, ns), lambda i, s, c: (s, i, 0))
    return pl.pallas_call(
        body, name=name, out_shape=(_sds((rr, ns)), _sds((NCHIP, rr, ns), BF16)),
        grid_spec=pltpu.PrefetchScalarGridSpec(
            num_scalar_prefetch=1, grid=(rr // tr, NCHIP),
            in_specs=[pl.BlockSpec((None, None, tr, ns), lambda i, s, c: (s, c[1], i, 0)), blk],
            out_specs=(pl.BlockSpec((tr, ns), lambda i, s, c: (i, 0)), blk)),
        compiler_params=pltpu.CompilerParams(vmem_limit_bytes=VMEM_LIMIT,
                                             dimension_semantics=("parallel", "arbitrary")))(place, g, r1)


def _sum_recv(q, r2, place, name):
    rr, ns = q.shape
    tr = _row_tile(rr, ns)

    def body(c_ref, q_ref, r_ref, o_ref):
        o_ref[...] = ((q_ref[...] + r_ref[0].astype(F32)) + r_ref[1].astype(F32)) + r_ref[2].astype(F32)

    return _prefetch_call(body, name, _sds((2, rr, ns)), (rr // tr,),
                          [pl.BlockSpec((tr, ns), lambda i, c: (i, 0)),
                           pl.BlockSpec((NCHIP - 1, tr, ns), lambda i, c: (0, i, 0))],
                          pl.BlockSpec((None, tr, ns), lambda i, c: (c[1], i, 0)))(place, q, r2)


def _sum_devices(v8):
    def body(v_ref, o_ref):
        acc = v_ref[0]
        for dev in range(1, 8):
            acc = acc + v_ref[dev]
        o_ref[...] = acc

    return _pcall(body, name="sum_devices", out_shape=_sds(v8.shape[1:]))(v8)


def _adamw(w, g, m, v, name):
    rows, cols = w.shape
    tr = _row_tile(rows, cols)

    def body(w_ref, g_ref, m_ref, v_ref, d_ref, m2_ref, v2_ref):
        g = g_ref[...]
        m2 = ADAM_B1 * m_ref[...] + (1.0 - ADAM_B1) * g
        v2 = ADAM_B2 * v_ref[...] + (1.0 - ADAM_B2) * (g * g)
        m_hat = m2 / (1.0 - ADAM_B1 ** ADAM_STEP)
        v_hat = v2 / (1.0 - ADAM_B2 ** ADAM_STEP)
        d_ref[...] = -ADAM_LR * (m_hat / (jnp.sqrt(v_hat) + ADAM_EPS) + ADAM_WD * w_ref[...])
        m2_ref[...] = m2
        v2_ref[...] = v2

    blk = pl.BlockSpec((tr, cols), lambda i: (i, 0))
    return _pcall(body, name=name, out_shape=(_sds((rows, cols)),) * 3, grid=(rows // tr,), in_specs=[blk] * 4,
                  out_specs=(blk,) * 3, semantics=("parallel",))(w, g, m, v)


BIG = ("w_in", "w_conv_out", "w_attn_out", "w_out", "w_ff1", "w_ff2")
SMALL = ("rel_bias", "norm1_g", "q_norm_g", "k_norm_g", "conv_dw_w", "conv_dw_b", "conv_ln_g", "conv_ln_b", "norm2_g")
WEIGHTS = ("rel_bias", "norm1_g", "w_in", "q_norm_g", "k_norm_g", "conv_dw_w", "conv_dw_b", "conv_ln_g", "conv_ln_b",
           "w_conv_out", "w_attn_out", "w_out", "norm2_g", "w_ff1", "w_ff2")


def _pack(arrays):
    flat = jnp.concatenate([a.reshape(-1) for a in arrays])
    n = flat.shape[0]
    rows = -(-n // 1024) * 8
    return jnp.pad(flat, (0, rows * 128 - n)).reshape(rows, 128)


def _unpack(packed, shapes):
    flat = packed.reshape(-1)
    out, off = [], 0
    for shp in shapes:
        n = int(np.prod(shp))
        out.append(flat[off:off + n].reshape(shp))
        off += n
    return out


def _adamw_layer(l, w, g, m, v, prev, name, deps=()):
    _, k, n = w.shape
    tr = _row_tile(k, n)
    deps = tuple(d for d in deps if d is not None)
    if prev is None:
        prev = tuple(lax.empty(w.shape, F32) for _ in range(4))

    def body(*refs):
        w_ref, g_ref, m_ref, v_ref = refs[:4]
        go_ref, d_ref, m2_ref, v2_ref = refs[-4:]
        g = g_ref[...]
        m2 = ADAM_B1 * m_ref[...] + (1.0 - ADAM_B1) * g
        v2 = ADAM_B2 * v_ref[...] + (1.0 - ADAM_B2) * (g * g)
        m_hat = m2 / (1.0 - ADAM_B1 ** ADAM_STEP)
        v_hat = v2 / (1.0 - ADAM_B2 ** ADAM_STEP)
        go_ref[...] = g
        d_ref[...] = -ADAM_LR * (m_hat / (jnp.sqrt(v_hat) + ADAM_EPS) + ADAM_WD * w_ref[...])
        m2_ref[...] = m2
        v2_ref[...] = v2

    lay = pl.BlockSpec((None, tr, n), lambda i: (l, i, 0))
    return _pcall(body, name=name, out_shape=(_sds(w.shape),) * 4, grid=(k // tr,),
                  in_specs=[lay, pl.BlockSpec((tr, n), lambda i: (i, 0)), lay, lay] + [ANY_SPEC] * (4 + len(deps)),
                  out_specs=(lay,) * 4, aliases={4: 0, 5: 1, 6: 2, 7: 3},
                  semantics=("parallel",))(w, g, m, v, *prev, *deps)


class _GradPipe:
    def __init__(self, l, kinds, tag, big, place, w, m, v, results):
        self.l, self.kinds, self.place, self.w, self.m, self.v, self.results = l, kinds, place, w, m, v, results
        self.id = f"l{l}{tag}"
        self.g = [big[n].reshape(NCHIP, 2, big[n].shape[1] // 2, big[n].shape[2]) for n in kinds]

    def step0(self):
        lands = [lax.empty((NCHIP,) + g.shape[2:], F32) for g in self.g]
        self.s1, self.b1, tok = _split_start(f"rs1_start_{self.id}", self.g + lands, _plan_pair_half, len(self.kinds))
        return tok

    def step1(self, after):
        nk = len(self.kinds)
        bufs = _split_wait(f"rs1_wait_{self.id}", self.s1, self.b1, _plan_pair_half, after)
        sums = [_sum_half(bufs[i], bufs[nk + i], self.place, f"rs1_sum_{n}") for i, n in enumerate(self.kinds)]
        self.q = [q for q, _ in sums]
        qb = [b for _, b in sums]
        lands = [lax.empty((NCHIP - 1,) + b.shape[1:], BF16) for b in qb]
        self.s2, self.b2, tok = _split_start(f"rs2_start_{self.id}", qb + lands, _plan_scatter, 3 * nk)
        return tok

    def step2(self, after):
        nk = len(self.kinds)
        bufs = _split_wait(f"rs2_wait_{self.id}", self.s2, self.b2, _plan_scatter, after)
        fin = [_sum_recv(self.q[i], bufs[nk + i], self.place, f"rs2_sum_{n}") for i, n in enumerate(self.kinds)]
        self.s3, self.b3, tok = _split_start(f"rs3_start_{self.id}", fin, _plan_pair_fill, nk)
        return tok

    def step3(self, after):
        self.fin = _split_wait(f"rs3_wait_{self.id}", self.s3, self.b3, _plan_pair_fill, after)

    def adam(self, deps=()):
        for i, n in enumerate(self.kinds):
            g2 = self.fin[i].reshape(self.fin[i].shape[1] * 2, self.fin[i].shape[2])
            self.results[n] = _adamw_layer(self.l, self.w[n], g2, self.m[n], self.v[n], self.results.get(n),
                                           f"adamw_{n}_l{self.l}", deps=deps if i == 0 else ())
        return self.results[self.kinds[-1]][1]


def kernel(x, rel_bias, norm1_g, w_in, q_norm_g, k_norm_g, conv_dw_w, conv_dw_b, conv_ln_g, conv_ln_b, w_conv_out, w_attn_out, w_out, norm2_g, w_ff1, w_ff2, loss_target, m_rel_bias, m_norm1_g, m_w_in, m_q_norm_g, m_k_norm_g, m_conv_dw_w, m_conv_dw_b, m_conv_ln_g, m_conv_ln_b, m_w_conv_out, m_w_attn_out, m_w_out, m_norm2_g, m_w_ff1, m_w_ff2, v_rel_bias, v_norm1_g, v_w_in, v_q_norm_g, v_k_norm_g, v_conv_dw_w, v_conv_dw_b, v_conv_ln_g, v_conv_ln_b, v_w_conv_out, v_w_attn_out, v_w_out, v_norm2_g, v_w_ff1, v_w_ff2):
    w = dict(rel_bias=rel_bias, norm1_g=norm1_g, w_in=w_in, q_norm_g=q_norm_g, k_norm_g=k_norm_g, conv_dw_w=conv_dw_w,
             conv_dw_b=conv_dw_b, conv_ln_g=conv_ln_g, conv_ln_b=conv_ln_b, w_conv_out=w_conv_out,
             w_attn_out=w_attn_out, w_out=w_out, norm2_g=norm2_g, w_ff1=w_ff1, w_ff2=w_ff2)
    m = dict(rel_bias=m_rel_bias, norm1_g=m_norm1_g, w_in=m_w_in, q_norm_g=m_q_norm_g, k_norm_g=m_k_norm_g,
             conv_dw_w=m_conv_dw_w, conv_dw_b=m_conv_dw_b, conv_ln_g=m_conv_ln_g, conv_ln_b=m_conv_ln_b,
             w_conv_out=m_w_conv_out, w_attn_out=m_w_attn_out, w_out=m_w_out, norm2_g=m_norm2_g, w_ff1=m_w_ff1,
             w_ff2=m_w_ff2)
    v = dict(rel_bias=v_rel_bias, norm1_g=v_norm1_g, w_in=v_w_in, q_norm_g=v_q_norm_g, k_norm_g=v_k_norm_g,
             conv_dw_w=v_conv_dw_w, conv_dw_b=v_conv_dw_b, conv_ln_g=v_conv_ln_g, conv_ln_b=v_conv_ln_b,
             w_conv_out=v_w_conv_out, w_attn_out=v_w_attn_out, w_out=v_w_out, norm2_g=v_norm2_g, w_ff1=v_w_ff1,
             w_ff2=v_w_ff2)
    chip_id = 2 * lax.axis_index("x") + lax.axis_index("y")
    place = jnp.stack([chip_id, lax.axis_index("c")]).astype(jnp.int32)

    dww4 = _all_gather_chips(conv_dw_w, "ag_conv_dw_w")
    dww = dww4.transpose(1, 2, 0, 3).reshape(DEPTH, KW, CONV)

    names = dict(w_in="win4", w_conv_out="wco4", w_attn_out="wao4", w_out="wout4", w_ff1="wff14", w_ff2="wff24")
    chips, pair = {}, {}

    def start_chips(key, l, kinds, after):
        lands = []
        for n in kinds:
            k, ns = w[n].shape[1:]
            land = lax.dynamic_update_slice(lax.empty((NCHIP, k, ns), BF16), w[n][l].astype(BF16)[None], (chip_id, 0, 0))
            lands.append(land.reshape(NCHIP, 2, k // 2, ns))
        chips[key] = _split_start(f"ag_chips_start_{key}", lands, _plan_gather_chips, 3 * len(kinds), after=after)
        return chips[key][2]

    def start_pair(key, after):
        sems, bufs, _ = chips[key]
        bufs = _split_wait(f"ag_chips_wait_{key}", sems, bufs, _plan_gather_chips, after)
        pair[key] = _split_start(f"ag_pair_start_{key}", bufs, _plan_gather_pair, len(bufs) * 3)
        return pair[key][2]

    def landed(key, kinds, after):
        sems, bufs, _ = pair[key]
        bufs = _split_wait(f"ag_pair_wait_{key}", sems, bufs, _plan_gather_pair, after)
        return {names[n]: b.reshape(NCHIP, 2 * b.shape[2], b.shape[3]) for n, b in zip(kinds, bufs)}

    first, rest = ("w_in",), tuple(n for n in BIG if n != "w_in")
    start_chips("l0b", 0, rest, start_pair("l0a", start_chips("l0a", 0, first, dww4)))

    def get_layer(l, after):
        p = dict(dww=dww[l], dwb=conv_dw_b[l][None], lng=conv_ln_g[l][None], lnb=conv_ln_b[l][None],
                 n1g=norm1_g[l][None], n2g=norm2_g[l][None], gq=jnp.tile(q_norm_g[l], 2)[None],
                 gk=jnp.tile(k_norm_g[l], 2)[None])
        p.update(landed(f"l{l}a", first, chips["l0b"][2] if l == 0 else after))
        more = l + 1 < DEPTH

        def hook(after_):
            tok = start_pair(f"l{l}b", after_)
            return start_chips(f"l{l + 1}a", l + 1, first, tok) if more else tok

        p["hook"] = hook
        p["rest"] = lambda after_: landed(f"l{l}b", rest, after_)
        mid = (lambda after_: start_chips(f"l{l + 1}b", l + 1, rest, start_pair(f"l{l + 1}a", after_))) if more else None
        return p, (), mid

    results = {}
    make_pipe = lambda l, kinds, tag, big: _GradPipe(l, kinds, tag, big, place, w, m, v, results)
    loss_share, dx, smalls, d_rel, pipes, early, late = _local_step(x[0], loss_target[0], get_layer, rel_bias,
                                                                    make_pipe)
    loss = lax.psum(loss_share, ("x", "y", "c"))

    local_small = dict(rel_bias=d_rel)
    for n in SMALL[1:]:
        local_small[n] = jnp.stack([smalls[l][n] for l in range(DEPTH)])
    small_shapes = [local_small[n].shape for n in SMALL]
    mine = _pack([local_small[n] for n in SMALL])
    slot = 4 * lax.axis_index("x") + 2 * lax.axis_index("y") + lax.axis_index("c")
    land = lax.dynamic_update_slice(lax.empty((8,) + mine.shape, F32), mine[None], (slot, 0, 0))
    small_sems, small_bufs, tok = _split_start("ag_small_start", [land], _plan_gather_devices, 7)
    for pipe in pipes:
        done = pipe.adam(deps=(tok,))
        tok = None
    early.step3(late.step2(done))
    done = early.adam()
    late.step3(done)
    done = late.adam()
    gathered = _split_wait("ag_small_wait", small_sems, small_bufs, _plan_gather_devices, done)[0]
    summed = _sum_devices(gathered)
    grads = dict(zip(SMALL, _unpack(summed, small_shapes)))
    grads["conv_dw_w"] = lax.dynamic_slice_in_dim(grads["conv_dw_w"], chip_id * 128, 128, axis=2)

    delta, new_m, new_v = {}, {}, {}
    small_w_shapes = [w[n].shape for n in SMALL]
    outs = _adamw(_pack([w[n] for n in SMALL]), _pack([grads[n] for n in SMALL]), _pack([m[n] for n in SMALL]),
                  _pack([v[n] for n in SMALL]), "adamw_small")
    for dst, packed in zip((delta, new_m, new_v), outs):
        dst.update(zip(SMALL, _unpack(packed, small_w_shapes)))

    for n in BIG:
        grads[n], delta[n], new_m[n], new_v[n] = results[n]

    return (loss, dx[None], *[grads[n] for n in WEIGHTS], *[delta[n] for n in WEIGHTS],
            *[new_m[n] for n in WEIGHTS], *[new_v[n] for n in WEIGHTS])
```

```python
import functools
import math

import numpy as np
import jax
import jax.numpy as jnp
from jax import lax
from jax.experimental import pallas as pl
from jax.experimental.pallas import tpu as pltpu

F32 = jnp.float32
BF16 = jnp.bfloat16

T = 2048
D = 1024
DEPTH = 4
CONV = 512
KW = 31
NG = 3
HD = 64
AOUT = 512
DFF = 4096
INC = 7680
DIL = (1, 4, 16)
BLK = 128
NBUCKET = 32
EPS = 1e-6
NEG = -1e30
NCHIP = 4
UB_A, UB_GT, UB_Q, UB_K, UB_V, UB_GC, UB_GA = 0, 1, 2, 5, 8, 11, 13

ADAM_LR, ADAM_B1, ADAM_B2, ADAM_EPS, ADAM_WD, ADAM_STEP = 0.001, 0.9, 0.999, 1e-08, 0.01, 10

VMEM_LIMIT = 48 * 1024 * 1024
TB = 1024
HBM_SPEC = pl.BlockSpec(memory_space=pltpu.HBM)
ANY_SPEC = pl.BlockSpec(memory_space=pl.ANY)
SEM_SPEC = pl.BlockSpec(memory_space=pltpu.SEMAPHORE)


def _pcall(body, *, name, out_shape, grid=(), in_specs=None, out_specs=None, scratch=(), aliases=None,
           semantics=None):
    kw = {}
    if in_specs is not None:
        kw["in_specs"] = in_specs
    if out_specs is not None:
        kw["out_specs"] = out_specs
    return pl.pallas_call(
        body, name=name, out_shape=out_shape, grid=grid, scratch_shapes=scratch,
        input_output_aliases=aliases or {},
        compiler_params=pltpu.CompilerParams(vmem_limit_bytes=VMEM_LIMIT, dimension_semantics=semantics),
        **kw)


def _sds(shape, dtype=F32):
    return jax.ShapeDtypeStruct(shape, dtype)


NN = (((1,), (0,)), ((), ()))
NT = (((1,), (1,)), ((), ()))
TN = (((0,), (0,)), ((), ()))


def _mm(name, a, b, *, out_shape, out_dtype, grid, a_spec, b_spec, o_spec, acc_shape, dims, add=None,
        add_spec=None, deps=()):
    nk = grid[2]
    deps = tuple(d for d in deps if d is not None)
    n_scratch = 1 if nk > 1 else 0

    def body(*refs):
        n_out = 1 + n_scratch
        refs = refs[:len(refs) - n_out - len(deps)] + refs[len(refs) - n_out:]
        a_ref, b_ref = refs[0], refs[1]
        r_ref = refs[2] if add is not None else None
        o_ref = refs[-n_out]
        prod = lax.dot_general(a_ref[...].astype(BF16), b_ref[...].astype(BF16), dims, preferred_element_type=F32)
        if nk == 1:
            o_ref[...] = (prod if r_ref is None else prod + r_ref[...]).astype(out_dtype)
            return
        acc_ref = refs[-1]
        k = pl.program_id(2)

        @pl.when(k == 0)
        def _():
            acc_ref[...] = prod

        @pl.when(k > 0)
        def _():
            acc_ref[...] += prod

        @pl.when(k == nk - 1)
        def _():
            res = acc_ref[...]
            if r_ref is not None:
                res = res + r_ref[...]
            o_ref[...] = res.astype(out_dtype)

    ins = ([a, b] if add is None else [a, b, add]) + list(deps)
    specs = ([a_spec, b_spec] if add is None else [a_spec, b_spec, add_spec]) + [ANY_SPEC] * len(deps)
    return _pcall(body, name=name, out_shape=_sds(out_shape, out_dtype), grid=grid, in_specs=specs,
                  out_specs=o_spec, scratch=[pltpu.VMEM(acc_shape, F32)] * n_scratch,
                  semantics=("parallel", "parallel", "arbitrary"))(*ins)


def _mm_x_wcols(name, a, w4, *, tm, tn, out_dtype=F32, deps=()):
    _, k, ns = w4.shape
    nj = ns // tn
    return _mm(name, a, w4, out_shape=(T, NCHIP * ns), out_dtype=out_dtype, grid=(T // tm, NCHIP * nj, 1), deps=deps,
               a_spec=pl.BlockSpec((tm, k), lambda i, j, kk: (i, 0)),
               b_spec=pl.BlockSpec((None, k, tn), lambda i, j, kk: (j // nj, 0, j % nj)),
               o_spec=pl.BlockSpec((tm, tn), lambda i, j, kk: (i, j)), acc_shape=(tm, tn), dims=NN)


def _mm_ff1(a, w4, *, tm, tn):
    _, k, ns = w4.shape
    nj = ns // tn

    def body(a_ref, b_ref, f_ref, r_ref):
        p = jnp.maximum(jnp.dot(a_ref[...], b_ref[...], preferred_element_type=F32), 0.0)
        f_ref[...] = p.astype(BF16)
        r_ref[...] = (p * p).astype(BF16)

    out = pl.BlockSpec((tm, tn), lambda i, j: (i, j))
    return _pcall(body, name="mm_f", out_shape=(_sds((T, DFF), BF16), _sds((T, DFF), BF16)), grid=(T // tm, NCHIP * nj),
                  in_specs=[pl.BlockSpec((tm, k), lambda i, j: (i, 0)),
                            pl.BlockSpec((None, k, tn), lambda i, j: (j // nj, 0, j % nj))],
                  out_specs=(out, out), semantics=("parallel", "parallel"))(a, w4)


def _mm_x_wrows(name, a, w4, add, *, tm, tk, tn, deps=()):
    _, ks, n = w4.shape
    nkk = ks // tk
    return _mm(name, a, w4, out_shape=(T, n), out_dtype=F32, grid=(T // tm, n // tn, NCHIP * nkk), deps=deps,
               a_spec=pl.BlockSpec((tm, tk), lambda i, j, kk: (i, kk)),
               b_spec=pl.BlockSpec((None, tk, tn), lambda i, j, kk: (kk // nkk, kk % nkk, j)),
               o_spec=pl.BlockSpec((tm, tn), lambda i, j, kk: (i, j)), acc_shape=(tm, tn), dims=NN,
               add=add, add_spec=pl.BlockSpec((tm, tn), lambda i, j, kk: (i, j)))


def _mm_g_wcols_t(name, g, w4, *, tm, tk, tn, out_dtype=F32, deps=()):
    _, k, ns = w4.shape
    nkk = ns // tk
    return _mm(name, g, w4, out_shape=(T, k), out_dtype=out_dtype, grid=(T // tm, k // tn, NCHIP * nkk), deps=deps,
               a_spec=pl.BlockSpec((tm, tk), lambda i, j, kk: (i, kk)),
               b_spec=pl.BlockSpec((None, tn, tk), lambda i, j, kk: (kk // nkk, j, kk % nkk)),
               o_spec=pl.BlockSpec((tm, tn), lambda i, j, kk: (i, j)), acc_shape=(tm, tn), dims=NT)


def _mm_g_wrows_t(name, g, w4, *, tm, tn, out_dtype=F32, deps=()):
    _, ks, n = w4.shape
    nj = ks // tn
    return _mm(name, g, w4, out_shape=(T, NCHIP * ks), out_dtype=out_dtype, grid=(T // tm, NCHIP * nj, 1), deps=deps,
               a_spec=pl.BlockSpec((tm, n), lambda i, j, kk: (i, 0)),
               b_spec=pl.BlockSpec((None, tn, n), lambda i, j, kk: (j // nj, j % nj, 0)),
               o_spec=pl.BlockSpec((tm, tn), lambda i, j, kk: (i, j)), acc_shape=(tm, tn), dims=NT)


def _mm_dff2(dx, w4, fa, *, tm, tn, deps=()):
    _, ks, n = w4.shape
    nj = ks // tn
    deps = tuple(d for d in deps if d is not None)

    def body(*refs):
        dx_ref, b_ref, f_ref = refs[:3]
        df_ref = refs[-1]
        dr = lax.dot_general(dx_ref[...].astype(BF16), b_ref[...], NT, preferred_element_type=F32)
        df_ref[...] = (dr * (2.0 * f_ref[...].astype(F32))).astype(BF16)

    out = pl.BlockSpec((tm, tn), lambda i, j: (i, j))
    return _pcall(body, name="mm_dr", out_shape=_sds((T, DFF), BF16), grid=(T // tm, NCHIP * nj),
                  in_specs=[pl.BlockSpec((tm, n), lambda i, j: (i, 0)),
                            pl.BlockSpec((None, tn, n), lambda i, j: (j // nj, j % nj, 0)), out]
                  + [ANY_SPEC] * len(deps),
                  out_specs=out, semantics=("parallel", "parallel"))(dx, w4, fa, *deps)


TCH = 512


def _mm_dw(name, a, g, *, out_shape, out_map, tm, tn, deps=()):
    deps = tuple(d for d in deps if d is not None)

    def body(*refs):
        a_ref, g_ref = refs[:2]
        o_ref, at_ref = refs[-2:]

        @pl.when(pl.program_id(1) == 0)
        def _():
            for c in range(T // TCH):
                at_ref[:, c * TCH:(c + 1) * TCH] = a_ref[c * TCH:(c + 1) * TCH, :].T

        o_ref[...] = jnp.dot(at_ref[...], g_ref[...].astype(BF16), preferred_element_type=F32)

    return _pcall(body, name=name, out_shape=_sds(out_shape), grid=(a.shape[1] // tm, g.shape[1] // tn),
                  in_specs=[pl.BlockSpec((T, tm), lambda i, j: (0, i)), pl.BlockSpec((T, tn), lambda i, j: (0, j))]
                  + [ANY_SPEC] * len(deps),
                  out_specs=pl.BlockSpec((None, tm, tn), out_map), scratch=[pltpu.VMEM((tm, T), BF16)],
                  semantics=("parallel", "arbitrary"))(a, g, *deps)


def _mm_dw_cols(name, a, g, *, ns, tm, tn, deps=()):
    nj = ns // tn
    return _mm_dw(name, a, g, out_shape=(NCHIP, a.shape[1], ns), out_map=lambda i, j: (j // nj, i, j % nj),
                  tm=tm, tn=tn, deps=deps)


def _mm_dw_rows(name, a, g, *, ks, tm, tn):
    ni = ks // tm
    return _mm_dw(name, a, g, out_shape=(NCHIP, ks, g.shape[1]), out_map=lambda i, j: (i // ni, i % ni, j),
                  tm=tm, tn=tn)


def _row_spec(width, col=0, tb=None):
    return pl.BlockSpec((tb or TB, width), lambda i: (i, col))


def _vec_spec(width):
    return pl.BlockSpec((1, width), lambda i: (0, 0))


def _rms_fwd(x, g):
    def body(x_ref, g_ref, h_ref):
        x = x_ref[...]
        r = lax.rsqrt(jnp.mean(x * x, axis=-1, keepdims=True) + EPS)
        h_ref[...] = (x * r * g_ref[...]).astype(BF16)

    return _pcall(body, name="rms_fwd", out_shape=_sds((T, D), BF16), grid=(T // TB,),
                  in_specs=[_row_spec(D), _vec_spec(D)], out_specs=_row_spec(D), semantics=("parallel",))(x, g)


def _rms_bwd(x, g, dh, dres, deps=()):
    deps = tuple(d for d in deps if d is not None)

    def body(*refs):
        x_ref, g_ref, dh_ref, dres_ref = refs[:4]
        dx_ref, dg_ref = refs[-2:]
        x = x_ref[...]
        r = lax.rsqrt(jnp.mean(x * x, axis=-1, keepdims=True) + EPS)
        y = x * r
        dh = dh_ref[...]
        dy = dh * g_ref[...]
        dx_ref[...] = dres_ref[...] + r * (dy - y * jnp.mean(dy * y, axis=-1, keepdims=True))

        @pl.when(pl.program_id(0) == 0)
        def _():
            dg_ref[...] = jnp.zeros_like(dg_ref)

        dg_ref[...] += jnp.sum(dh * y, axis=0, keepdims=True)

    return _pcall(body, name="rms_bwd", out_shape=(_sds((T, D)), _sds((1, D))), grid=(T // TB,),
                  in_specs=[_row_spec(D), _vec_spec(D), _row_spec(D), _row_spec(D)] + [ANY_SPEC] * len(deps),
                  out_specs=(_row_spec(D), _vec_spec(D)), semantics=("arbitrary",))(x, g, dh, dres, *deps)


def _sigmoid(x):
    return 1.0 / (1.0 + jnp.exp(-x))


def _gate_fwd(u, ycv, yat):
    def body(gc_ref, ga_ref, yc_ref, ya_ref, m_ref):
        m_ref[...] = (_sigmoid(gc_ref[...].astype(F32)) * yc_ref[...]
                      + _sigmoid(ga_ref[...].astype(F32)) * ya_ref[...]).astype(BF16)

    blk = lambda off: pl.BlockSpec((TB, 512), lambda i, j: (i, off + j))
    return _pcall(body, name="gate_fwd", out_shape=_sds((T, D), BF16), grid=(T // TB, 2),
                  in_specs=[blk(UB_GC), blk(UB_GA), blk(0), blk(0)], out_specs=blk(0),
                  semantics=("parallel", "parallel"))(u, u, ycv, yat)


def _gate_bwd(u, ycv, yat, dm):
    def body(gc_ref, ga_ref, yc_ref, ya_ref, dm_ref, dyc_ref, dya_ref, dgc_ref, dga_ref):
        dm = dm_ref[...]
        sc = _sigmoid(gc_ref[...].astype(F32))
        sa = _sigmoid(ga_ref[...].astype(F32))
        dyc_ref[...] = (dm * sc).astype(BF16)
        dya_ref[...] = (dm * sa).astype(BF16)
        dgc_ref[...] = (dm * yc_ref[...] * sc * (1.0 - sc)).astype(BF16)
        dga_ref[...] = (dm * ya_ref[...] * sa * (1.0 - sa)).astype(BF16)

    blk = lambda off: pl.BlockSpec((TB, 512), lambda i, j: (i, off + j))
    return _pcall(body, name="gate_bwd",
                  out_shape=(_sds((T, D), BF16), _sds((T, D), BF16), _sds((T, D), BF16), _sds((T, D), BF16)),
                  grid=(T // TB, 2), in_specs=[blk(UB_GC), blk(UB_GA), blk(0), blk(0), blk(0)],
                  out_specs=(blk(0), blk(0), blk(0), blk(0)),
                  semantics=("parallel", "parallel"))(u, u, ycv, yat, dm)


def _loss_fwd_bwd(y, target):
    def body(y_ref, t_ref, loss_ref, dy_ref):
        e = y_ref[...] - t_ref[...]
        dy_ref[...] = e * (1.0 / D)

        @pl.when(pl.program_id(0) == 0)
        def _():
            loss_ref[...] = jnp.zeros_like(loss_ref)

        loss_ref[...] += 0.5 * jnp.sum(jnp.mean(e * e, axis=-1, keepdims=True))

    return _pcall(body, name="loss", out_shape=(_sds((8, 128)), _sds((T, D))), grid=(T // TB,),
                  in_specs=[_row_spec(D), _row_spec(D)],
                  out_specs=(pl.BlockSpec((8, 128), lambda i: (0, 0)), _row_spec(D)),
                  semantics=("arbitrary",))(y, target)


PAD = 32
CCH = 256


def _conv_fwd(u, dw_w, dw_b):
    def body(a_ref, gt_ref, w_ref, b_ref, z1_ref, zp_ref):
        zp_ref[0:PAD, :] = jnp.zeros((PAD, 128), F32)
        zp_ref[PAD:PAD + T, :] = a_ref[...].astype(F32) * _sigmoid(gt_ref[...].astype(F32))
        for c in range(T // CCH):
            acc = jnp.broadcast_to(b_ref[...], (CCH, 128))
            for j in range(KW):
                acc = acc + w_ref[j:j + 1, :] * zp_ref[pl.ds(c * CCH + j + PAD - (KW - 1), CCH), :]
            z1_ref[c * CCH:(c + 1) * CCH, :] = acc

    col = lambda off: pl.BlockSpec((T, 128), lambda j: (0, off * 4 + j))
    return _pcall(body, name="conv_fwd", out_shape=_sds((T, CONV)), grid=(CONV // 128,),
                  in_specs=[col(UB_A), col(UB_GT), pl.BlockSpec((KW, 128), lambda j: (0, j)),
                            pl.BlockSpec((1, 128), lambda j: (0, j))],
                  out_specs=col(0), scratch=[pltpu.VMEM((T + PAD, 128), F32)],
                  semantics=("parallel",))(u, u, dw_w, dw_b)


def _ln_silu_fwd(z1, g, b):
    def body(z_ref, g_ref, b_ref, o_ref):
        z = z_ref[...]
        mu = jnp.mean(z, axis=-1, keepdims=True)
        zc = z - mu
        zh = zc * lax.rsqrt(jnp.mean(zc * zc, axis=-1, keepdims=True) + EPS)
        z2 = zh * g_ref[...] + b_ref[...]
        o_ref[...] = (z2 * _sigmoid(z2)).astype(BF16)

    return _pcall(body, name="ln_silu_fwd", out_shape=_sds((T, CONV), BF16), grid=(T // TB,),
                  in_specs=[_row_spec(CONV), _vec_spec(CONV), _vec_spec(CONV)], out_specs=_row_spec(CONV),
                  semantics=("parallel",))(z1, g, b)


def _ln_silu_bwd(z1, g, b, dz3):
    def body(z_ref, g_ref, b_ref, d_ref, z3_ref, dz1_ref, dg_ref, db_ref):
        z = z_ref[...]
        mu = jnp.mean(z, axis=-1, keepdims=True)
        zc = z - mu
        rs = lax.rsqrt(jnp.mean(zc * zc, axis=-1, keepdims=True) + EPS)
        zh = zc * rs
        z2 = zh * g_ref[...] + b_ref[...]
        s = _sigmoid(z2)
        z3_ref[...] = (z2 * s).astype(BF16)
        dz2 = d_ref[...] * (s * (1.0 + z2 * (1.0 - s)))
        dzh = dz2 * g_ref[...]
        dz1_ref[...] = rs * (dzh - jnp.mean(dzh, axis=-1, keepdims=True)
                             - zh * jnp.mean(dzh * zh, axis=-1, keepdims=True))

        @pl.when(pl.program_id(0) == 0)
        def _():
            dg_ref[...] = jnp.zeros_like(dg_ref)
            db_ref[...] = jnp.zeros_like(db_ref)

        dg_ref[...] += jnp.sum(dz2 * zh, axis=0, keepdims=True)
        db_ref[...] += jnp.sum(dz2, axis=0, keepdims=True)

    return _pcall(body, name="ln_silu_bwd",
                  out_shape=(_sds((T, CONV), BF16), _sds((T, CONV)), _sds((1, CONV)), _sds((1, CONV))),
                  grid=(T // TB,),
                  in_specs=[_row_spec(CONV), _vec_spec(CONV), _vec_spec(CONV), _row_spec(CONV)],
                  out_specs=(_row_spec(CONV), _row_spec(CONV), _vec_spec(CONV), _vec_spec(CONV)),
                  semantics=("arbitrary",))(z1, g, b, dz3)


def _conv_bwd(u, dw_w, dz1):
    def body(a_ref, gt_ref, w_ref, dz1_ref, da_ref, dgt_ref, dw_ref, db_ref, zp_ref, dp_ref):
        a = a_ref[...].astype(F32)
        s = _sigmoid(gt_ref[...].astype(F32))
        zp_ref[0:PAD, :] = jnp.zeros((PAD, 128), F32)
        zp_ref[PAD:PAD + T, :] = a * s
        dp_ref[0:T, :] = dz1_ref[...]
        dp_ref[T:T + PAD, :] = jnp.zeros((PAD, 128), F32)
        db_ref[...] = jnp.sum(dz1_ref[...], axis=0, keepdims=True)
        for j in range(KW):
            tot = jnp.zeros((1, 128), F32)
            for c in range(T // CCH):
                tot = tot + jnp.sum(dz1_ref[c * CCH:(c + 1) * CCH, :]
                                    * zp_ref[pl.ds(c * CCH + j + PAD - (KW - 1), CCH), :], axis=0, keepdims=True)
            dw_ref[j:j + 1, :] = tot
        for c in range(T // CCH):
            acc = jnp.zeros((CCH, 128), F32)
            for j in range(KW):
                acc = acc + w_ref[j:j + 1, :] * dp_ref[pl.ds(c * CCH + (KW - 1) - j, CCH), :]
            rows = slice(c * CCH, (c + 1) * CCH)
            sc = _sigmoid(gt_ref[rows, :].astype(F32))
            da_ref[rows, :] = (acc * sc).astype(BF16)
            dgt_ref[rows, :] = (acc * a_ref[rows, :].astype(F32) * sc * (1.0 - sc)).astype(BF16)

    col = lambda off: pl.BlockSpec((T, 128), lambda j: (0, off * 4 + j))
    wspec = pl.BlockSpec((KW, 128), lambda j: (0, j))
    return _pcall(body, name="conv_bwd",
                  out_shape=(_sds((T, CONV), BF16), _sds((T, CONV), BF16), _sds((KW, CONV)), _sds((1, CONV))),
                  grid=(CONV // 128,), in_specs=[col(UB_A), col(UB_GT), wspec, col(0)],
                  out_specs=(col(0), col(0), wspec, pl.BlockSpec((1, 128), lambda j: (0, j))),
                  scratch=[pltpu.VMEM((T + PAD, 128), F32), pltpu.VMEM((T + PAD, 128), F32)],
                  semantics=("parallel",))(u, u, dw_w, dz1)


def _bucket_tables():
    qi = np.arange(BLK)[:, None]
    kj = np.arange(2 * BLK)[None, :]
    off = np.clip(qi + BLK - kj, 0, BLK)
    out = []
    for d in DIL:
        dist = (off * d).astype(np.int32)
        nf = np.maximum(dist, 1).astype(np.float32)
        large = 16 + (np.log(nf / np.float32(16)) / np.float32(math.log(2048 / 16)) * np.float32(16)).astype(np.int32)
        large = np.minimum(large, NBUCKET - 1)
        out.append(np.where(dist < 16, dist, large))
    return np.stack(out).astype(np.int32)


def _band():
    off = lax.broadcasted_iota(jnp.int32, (BLK, 2 * BLK), 0) + BLK - lax.broadcasted_iota(jnp.int32, (BLK, 2 * BLK), 1)
    return (off >= 0) & (off <= BLK)


def _bias_table(rel_bias_t, buckets):
    def body(rb_ref, bk_ref, o_ref):
        h = pl.program_id(0)
        bk = bk_ref[...]
        acc = jnp.zeros((BLK, 2 * BLK), F32)
        for b in range(NBUCKET):
            acc = jnp.where(bk == b, rb_ref[h, b], acc)
        o_ref[...] = jnp.where(_band(), acc, NEG)

    return _pcall(body, name="bias_table", out_shape=_sds((3 * 8, BLK, 2 * BLK)), grid=(24,),
                  in_specs=[pl.BlockSpec(memory_space=pltpu.SMEM),
                            pl.BlockSpec((None, BLK, 2 * BLK), lambda h: (h // 8, 0, 0))],
                  out_specs=pl.BlockSpec((None, BLK, 2 * BLK), lambda h: (h, 0, 0)),
                  semantics=("parallel",))(rel_bias_t, buckets)


def _bias_grad(ds_acc, buckets):
    def body(a_ref, bk_ref, o_ref):
        acc = a_ref[0]
        for l in range(1, DEPTH):
            acc = acc + a_ref[l]
        bk = bk_ref[...]
        lane = lax.broadcasted_iota(jnp.int32, (1, 128), 1)
        row = jnp.zeros((1, 128), F32)
        for b in range(NBUCKET):
            row = jnp.where(lane == b, jnp.sum(jnp.where(bk == b, acc, 0.0)), row)
        o_ref[...] = row

    return _pcall(body, name="bias_grad", out_shape=_sds((24, 1, 128)), grid=(24,),
                  in_specs=[pl.BlockSpec((DEPTH, None, BLK, 2 * BLK), lambda h: (0, h, 0, 0)),
                            pl.BlockSpec((None, BLK, 2 * BLK), lambda h: (h // 8, 0, 0))],
                  out_specs=pl.BlockSpec((None, 1, 128), lambda h: (h, 0, 0)),
                  semantics=("parallel",))(ds_acc, buckets)


def _head_mask():
    return lax.broadcasted_iota(jnp.int32, (1, 128), 1) < HD


def _seg_ones(width):
    r = lax.broadcasted_iota(jnp.int32, (width, width), 0) >> 6
    c = lax.broadcasted_iota(jnp.int32, (width, width), 1) >> 6
    return (r == c).astype(BF16)


def _seg_sum(x, ones):
    hi = x.astype(BF16)
    lo = (x - hi.astype(F32)).astype(BF16)
    return (jnp.dot(hi, ones, preferred_element_type=F32) + jnp.dot(lo, ones, preferred_element_type=F32))


def _dot(a, b, dims):
    return lax.dot_general(a, b, dims, preferred_element_type=F32)


def _tile_rows(d, r, n):
    stride = None if d == 1 else d
    q_rows = pl.ds(r + d * n * BLK, BLK, stride=stride)
    if n == 0:
        return q_rows, q_rows, BLK
    return q_rows, pl.ds(r + d * (n - 1) * BLK, 2 * BLK, stride=stride), 2 * BLK


def _stack_heads(x, m_a):
    return jnp.concatenate([jnp.where(m_a, x, 0.0), jnp.where(m_a, 0.0, x)], axis=0)


def _stack_rows(x, m_a, width):
    other = pltpu.roll(x, HD, axis=1)
    both = jnp.concatenate([jnp.where(m_a, x, other), jnp.where(m_a, other, x)], axis=0)
    return both if width == 128 else jnp.concatenate([both] * (width // 128), axis=1)


NCH = 256


def _qk_norm_prep(q_ref, k_ref, v_ref, gq_ref, gk_ref, qn_ref, kn_ref, vn_ref, ones):
    def prep(i, carry):
        rows = pl.ds(pl.multiple_of(i * NCH, NCH), NCH)
        q = q_ref[rows, :].astype(F32)
        qn_ref[rows, :] = q * lax.rsqrt(_seg_sum(q * q, ones) * (1.0 / HD) + EPS) * gq_ref[...] * (HD ** -0.5)
        k = k_ref[rows, :].astype(F32)
        kn_ref[rows, :] = k * lax.rsqrt(_seg_sum(k * k, ones) * (1.0 / HD) + EPS) * gk_ref[...]
        vn_ref[rows, :] = v_ref[rows, :].astype(F32)
        return carry

    lax.fori_loop(0, T // NCH, prep, 0)


def _attn_specs(g):
    ucol = lambda base: pl.BlockSpec((T, 128), lambda hp: (0, (base + g) * 4 + hp))
    col = pl.BlockSpec((T, 128), lambda hp: (0, hp))
    vec = pl.BlockSpec((1, 128), lambda hp: (0, 0))
    bm = pl.BlockSpec((2, BLK, 2 * BLK), lambda hp: (g * 4 + hp, 0, 0))
    return ucol, col, vec, bm


def _attn_fwd(g, u, gq, gk, bm, deps=()):
    d = DIL[g]

    def body(*refs):
        q_ref, k_ref, v_ref, gq_ref, gk_ref, bm_ref = refs[:6]
        o_ref, lse_ref, qn_ref, kn_ref, vn_ref = refs[-5:]
        ones = _seg_ones(128)
        _qk_norm_prep(q_ref, k_ref, v_ref, gq_ref, gk_ref, qn_ref, kn_ref, vn_ref, ones)
        m_a = _head_mask()
        for r in range(d):
            for n in range(T // d // BLK):
                q_rows, k_rows, nk = _tile_rows(d, r, n)
                qt = qn_ref[q_rows, :]
                kt = kn_ref[k_rows, :].astype(BF16)
                vt = vn_ref[k_rows, :].astype(BF16)
                q2 = _stack_heads(qt, m_a).astype(BF16)
                s = _dot(q2, kt, NT) + bm_ref[...].reshape(2 * BLK, 2 * BLK)[:, 2 * BLK - nk:]
                mx = jnp.max(s, axis=1, keepdims=True)
                p = jnp.exp(s - mx)
                l = jnp.sum(p, axis=1, keepdims=True)
                o2 = _dot(p.astype(BF16), vt, NN) / l
                lse2 = jnp.broadcast_to(mx + jnp.log(l), (2 * BLK, 128))
                o_ref[q_rows, :] = jnp.where(m_a, o2[:BLK], o2[BLK:])
                lse_ref[q_rows, :] = jnp.where(m_a, lse2[:BLK], lse2[BLK:])

    ucol, col, vec, bmspec = _attn_specs(g)
    return _pcall(body, name=f"attn_fwd_g{g}", out_shape=(_sds((T, AOUT)), _sds((T, AOUT))), grid=(4,),
                  in_specs=[ucol(UB_Q), ucol(UB_K), ucol(UB_V), vec, vec, bmspec] + [ANY_SPEC] * len(deps),
                  out_specs=(col, col), scratch=[pltpu.VMEM((T, 128), F32)] * 3,
                  semantics=("parallel",))(u, u, u, gq, gk, bm, *deps)


def _attn_bwd(g, u, gq, gk, bm, dog, cb, lse):
    d = DIL[g]

    def body(q_ref, k_ref, v_ref, gq_ref, gk_ref, bm_ref, do_ref, cb_ref, lse_ref,
             dqo_ref, dko_ref, dvo_ref, dgq_ref, dgk_ref, dsa_ref, qn_ref, kn_ref, vn_ref, dq_ref, dk_ref, dv_ref):
        ones = _seg_ones(128)
        _qk_norm_prep(q_ref, k_ref, v_ref, gq_ref, gk_ref, qn_ref, kn_ref, vn_ref, ones)
        m_a = _head_mask()
        dk_ref[...] = jnp.zeros_like(dk_ref)
        dv_ref[...] = jnp.zeros_like(dv_ref)
        dsa_ref[...] = jnp.zeros_like(dsa_ref)
        for r in range(d):
            for n in range(T // d // BLK):
                q_rows, k_rows, nk = _tile_rows(d, r, n)
                ktb = kn_ref[k_rows, :].astype(BF16)
                vtb = vn_ref[k_rows, :].astype(BF16)
                q2 = _stack_heads(qn_ref[q_rows, :], m_a).astype(BF16)
                do2 = _stack_heads(do_ref[q_rows, :], m_a).astype(BF16)
                lse_c = _stack_rows(lse_ref[q_rows, :], m_a, nk)
                c_c = _stack_rows(cb_ref[q_rows, :], m_a, nk)
                s = _dot(q2, ktb, NT) + bm_ref[...].reshape(2 * BLK, 2 * BLK)[:, 2 * BLK - nk:]
                p = jnp.exp(s - lse_c)
                ds = p * (_dot(do2, vtb, NT) + c_c)
                dsb = ds.astype(BF16)
                dq2 = _dot(dsb, ktb, NN)
                dq_ref[q_rows, :] = jnp.where(m_a, dq2[:BLK], dq2[BLK:])
                dk_ref[k_rows, :] += _dot(dsb, q2, TN)
                dv_ref[k_rows, :] += _dot(p.astype(BF16), do2, TN)
                dsa_ref[:, :, 2 * BLK - nk:] += ds.reshape(2, BLK, nk)

        @pl.when(pl.program_id(0) == 0)
        def _():
            dgq_ref[...] = jnp.zeros_like(dgq_ref)
            dgk_ref[...] = jnp.zeros_like(dgk_ref)

        def norm_bwd(i, carry):
            rows = pl.ds(pl.multiple_of(i * NCH, NCH), NCH)
            for x_ref, g_ref, dx_ref, dxo_ref, dg_ref, scale in (
                    (q_ref, gq_ref, dq_ref, dqo_ref, dgq_ref, HD ** -0.5), (k_ref, gk_ref, dk_ref, dko_ref, dgk_ref, 1.0)):
                x = x_ref[rows, :].astype(F32)
                rs = lax.rsqrt(_seg_sum(x * x, ones) * (1.0 / HD) + EPS)
                xh = x * rs
                dn = dx_ref[rows, :] * scale
                dxh = dn * g_ref[...]
                dxo_ref[rows, :] = (rs * (dxh - xh * (_seg_sum(dxh * xh, ones) * (1.0 / HD)))).astype(BF16)
                dg_ref[...] += jnp.sum(dn * xh, axis=0, keepdims=True)
            dvo_ref[rows, :] = dv_ref[rows, :].astype(BF16)
            return carry

        lax.fori_loop(0, T // NCH, norm_bwd, 0)

    ucol, col, vec, bmspec = _attn_specs(g)
    return _pcall(body, name=f"attn_bwd_g{g}",
                  out_shape=(_sds((T, AOUT), BF16), _sds((T, AOUT), BF16), _sds((T, AOUT), BF16), _sds((1, 128)),
                             _sds((1, 128)), _sds((8, BLK, 2 * BLK))),
                  grid=(4,),
                  in_specs=[ucol(UB_Q), ucol(UB_K), ucol(UB_V), vec, vec, bmspec, col, col, col],
                  out_specs=(col, col, col, vec, vec, pl.BlockSpec((2, BLK, 2 * BLK), lambda hp: (hp, 0, 0))),
                  scratch=[pltpu.VMEM((T, 128), F32)] * 6,
                  semantics=("arbitrary",))(u, u, u, gq, gk, bm, dog, cb, lse)


def _combine_fwd(ogs, lses):
    def body(o0, o1, o2, l0, l1, l2, o_ref):
        ls = [l0[...], l1[...], l2[...]]
        mx = jnp.maximum(jnp.maximum(ls[0], ls[1]), ls[2])
        es = [jnp.exp(l - mx) for l in ls]
        inv = 1.0 / (es[0] + es[1] + es[2])
        o_ref[...] = ((es[0] * o0[...] + es[1] * o1[...] + es[2] * o2[...]) * inv).astype(BF16)

    return _pcall(body, name="combine_fwd", out_shape=_sds((T, AOUT), BF16), grid=(T // TB,),
                  in_specs=[_row_spec(AOUT)] * 6, out_specs=_row_spec(AOUT), semantics=("parallel",))(*ogs, *lses)


def _combine_bwd(ogs, lses, do):
    def body(o0, o1, o2, l0, l1, l2, do_ref, d0, d1, d2, c0, c1, c2):
        ls = [l0[...], l1[...], l2[...]]
        mx = jnp.maximum(jnp.maximum(ls[0], ls[1]), ls[2])
        es = [jnp.exp(l - mx) for l in ls]
        inv = 1.0 / (es[0] + es[1] + es[2])
        ws = [e * inv for e in es]
        do = do_ref[...]
        o = ws[0] * o0[...] + ws[1] * o1[...] + ws[2] * o2[...]
        s = _seg_sum(do * o, _seg_ones(AOUT))
        for w, d_ref, c_ref in zip(ws, (d0, d1, d2), (c0, c1, c2)):
            d_ref[...] = w * do
            c_ref[...] = -(w * s)

    tb = 512
    return _pcall(body, name="combine_bwd", out_shape=tuple(_sds((T, AOUT)) for _ in range(6)), grid=(T // tb,),
                  in_specs=[_row_spec(AOUT, tb=tb)] * 7, out_specs=tuple(_row_spec(AOUT, tb=tb) for _ in range(6)),
                  semantics=("parallel",))(*ogs, *lses, do)


def _layer_fwd(x, p, bm, deps=(), mid=None):
    h1 = _rms_fwd(x, p["n1g"])
    u = _mm_x_wcols("mm_u", h1, p["win4"], tm=T, tn=1920, out_dtype=BF16, deps=deps)
    ogs, lses = [], []
    for g in range(NG):
        gdeps = (p["hook"](ogs[-1]),) if g == NG - 1 and "hook" in p else ()
        og, lse = _attn_fwd(g, u, p["gq"], p["gk"], bm, deps=gdeps)
        ogs.append(og)
        lses.append(lse)
    o = _combine_fwd(ogs, lses)
    z1 = _conv_fwd(u, p["dww"], p["dwb"])
    z3 = _ln_silu_fwd(z1, p["lng"], p["lnb"])
    if "rest" in p:
        p = {**p, **p["rest"](z3)}
    ycv = _mm_x_wcols("mm_ycv", z3, p["wco4"], tm=T, tn=256)
    yat = _mm_x_wcols("mm_yat", o, p["wao4"], tm=T, tn=256)
    m = _gate_fwd(u, ycv, yat)
    xm = _mm_x_wrows("mm_xmid", m, p["wout4"], x, tm=1024, tk=256, tn=1024)
    h2 = _rms_fwd(xm, p["n2g"])
    fa, r = _mm_ff1(h2, p["wff14"], tm=T, tn=1024)
    tok = mid(r) if mid else None
    xo = _mm_x_wrows("mm_xout", r, p["wff24"], xm, tm=1024, tk=1024, tn=1024, deps=(tok,))
    saved = dict(x=x, h1=h1, u=u, z1=z1, ogs=ogs, lses=lses, o=o, ycv=ycv, yat=yat, m=m, xm=xm, h2=h2, fa=fa, r=r)
    return xo, p, saved


EARLY = ("w_ff2", "w_ff1", "w_out")
LATE = ("w_conv_out", "w_attn_out", "w_in")


def _layer_bwd(dx, s, p, bm, pipe=None, own_early=None, own_late=None):
    u = s["u"]
    tok = pipe.step0() if pipe else None
    df = _mm_dff2(dx, p["wff24"], s["fa"], tm=T, tn=1024, deps=(tok,))
    g_ff2 = _mm_dw_rows("mm_dwff2", s["r"], dx, ks=1024, tm=1024, tn=1024)
    g_ff1 = _mm_dw_cols("mm_dwff1", s["h2"], df, ns=1024, tm=1024, tn=1024)
    tok = pipe.step1(g_ff1) if pipe else None
    dh2 = _mm_g_wcols_t("mm_dh2", df, p["wff14"], tm=T, tk=1024, tn=1024, deps=(tok,))
    dxm, d_n2g = _rms_bwd(s["xm"], p["n2g"], dh2, dx)

    dm = _mm_g_wrows_t("mm_dm", dxm, p["wout4"], tm=1024, tn=256)
    g_out = _mm_dw_rows("mm_dwout", s["m"], dxm, ks=256, tm=256, tn=1024)
    early = own_early(dict(w_ff2=g_ff2, w_ff1=g_ff1, w_out=g_out)) if own_early else None
    tok_e = early.step0() if early else None
    dyc, dya, dgc, dga = _gate_bwd(u, s["ycv"], s["yat"], dm)

    dz3 = _mm_g_wcols_t("mm_dz3", dyc, p["wco4"], tm=T, tk=256, tn=512, deps=(tok_e,))
    z3, dz1, d_lng, d_lnb = _ln_silu_bwd(s["z1"], p["lng"], p["lnb"], dz3)
    g_co = _mm_dw_cols("mm_dwco", z3, dyc, ns=256, tm=512, tn=256)
    da, dgt, d_dww, d_dwb = _conv_bwd(u, p["dww"], dz1)

    tok_e = early.step1(da) if early else None
    do = _mm_g_wcols_t("mm_do", dya, p["wao4"], tm=T, tk=256, tn=512, deps=(tok_e,))
    g_ao = _mm_dw_cols("mm_dwao", s["o"], dya, ns=256, tm=512, tn=256)
    parts = _combine_bwd(s["ogs"], s["lses"], do)
    dqs, dks, dvs, d_gq, d_gk, dsas = [], [], [], [], [], []
    for g in range(NG):
        dq, dk, dv, dgq, dgk, dsa = _attn_bwd(g, u, p["gq"], p["gk"], bm, parts[g], parts[NG + g], s["lses"][g])
        dqs.append(dq)
        dks.append(dk)
        dvs.append(dv)
        d_gq.append(dgq)
        d_gk.append(dgk)
        dsas.append(dsa)
    du = jnp.concatenate([da, dgt] + dqs + dks + dvs + [dgc, dga], axis=1)
    tok = pipe.step2(du) if pipe else None
    tok_e = early.step2(du) if early else None
    g_in = _mm_dw_cols("mm_dwin", s["h1"], du, ns=1920, tm=1024, tn=1920, deps=(tok, tok_e))
    late = own_late(dict(w_in=g_in, w_conv_out=g_co, w_attn_out=g_ao)) if own_late else None
    tok_l = late.step0() if late else None
    dh1 = _mm_g_wcols_t("mm_dh1", du, p["win4"], tm=1024, tk=1920, tn=1024, deps=(tok_l,))
    tok_l = late.step1(dh1) if late else None
    dxi, d_n1g = _rms_bwd(s["x"], p["n1g"], dh1, dxm, deps=(tok_l,))
    if pipe:
        pipe.step3(dxi)

    fold = lambda parts_: sum(v[0, :HD] + v[0, HD:] for v in parts_)
    big = dict(w_in=g_in, w_conv_out=g_co, w_attn_out=g_ao, w_out=g_out, w_ff1=g_ff1, w_ff2=g_ff2)
    small = dict(norm1_g=d_n1g[0], q_norm_g=fold(d_gq), k_norm_g=fold(d_gk), conv_dw_w=d_dww, conv_dw_b=d_dwb[0],
                 conv_ln_g=d_lng[0], conv_ln_b=d_lnb[0], norm2_g=d_n2g[0])
    return dxi, big, small, jnp.concatenate(dsas, axis=0), early, late


def _local_step(x, target, get_layer, rel_bias, make_pipe):
    buckets = jnp.asarray(_bucket_tables())
    bm = _bias_table(rel_bias.T, buckets)
    saved, layers = [], []
    for l in range(DEPTH):
        p, deps, mid = get_layer(l, x)
        x, p, s = _layer_fwd(x, p, bm, deps=deps, mid=mid)
        layers.append(p)
        saved.append(s)
    loss_blk, dx = _loss_fwd_bwd(x, target)
    smalls, dsas = [None] * DEPTH, [None] * DEPTH
    pipe, pipes = None, []
    for l in reversed(range(DEPTH)):
        if l > 0:
            dx, big, smalls[l], dsas[l], _, _ = _layer_bwd(dx, saved[l], layers[l], bm, pipe)
            pipe = make_pipe(l, BIG, "", big)
            pipes.append(pipe)
        else:
            dx, big, smalls[l], dsas[l], early, late = _layer_bwd(
                dx, saved[l], layers[l], bm, pipe, lambda big_: make_pipe(0, EARLY, "e", big_),
                lambda big_: make_pipe(0, LATE, "", big_))
    d_rel = _bias_grad(jnp.stack(dsas), buckets)[:, 0, :NBUCKET].T
    return loss_blk[0, 0], dx, smalls, d_rel, pipes, early, late


MESH = pl.DeviceIdType.MESH


def _me():
    return lax.axis_index("x"), lax.axis_index("y"), lax.axis_index("c")


def _other_chips(mx, my):
    return [(1 - mx, my), (mx, 1 - my), (1 - mx, 1 - my)]


def _rcopy(src, dst, send_sems, recv_sems, k, dev):
    return pltpu.make_async_remote_copy(src_ref=src, dst_ref=dst, send_sem=send_sems.at[k], recv_sem=recv_sems.at[k],
                                        device_id=dev, device_id_type=MESH)


def _comm_call(body, name, out_shape, n_in, n_sems):
    return pl.pallas_call(
        body, name=name, out_shape=out_shape, in_specs=[HBM_SPEC] * n_in,
        out_specs=jax.tree.map(lambda _: HBM_SPEC, out_shape),
        scratch_shapes=[pltpu.SemaphoreType.DMA((n_sems,)), pltpu.SemaphoreType.DMA((n_sems,)),
                        pltpu.SemaphoreType.DMA(())],
        compiler_params=pltpu.CompilerParams(has_side_effects=True))


def _all_gather_chips(x, name):
    def body(x_ref, o_ref, send_sems, recv_sems, local_sem):
        mx, my, mc = _me()
        local = pltpu.make_async_copy(x_ref, o_ref.at[2 * mx + my], local_sem)
        local.start()
        sends = [_rcopy(x_ref, o_ref.at[2 * mx + my], send_sems, recv_sems, k, (px, py, mc))
                 for k, (px, py) in enumerate(_other_chips(mx, my))]
        for cp in sends:
            cp.start()
        for k, (px, py) in enumerate(_other_chips(mx, my)):
            _rcopy(x_ref, o_ref.at[2 * px + py], send_sems, recv_sems, k, (px, py, mc)).wait_recv()
        for cp in sends:
            cp.wait_send()
        local.wait()

    return _comm_call(body, name, _sds((NCHIP,) + x.shape, x.dtype), 1, 3)(x)


EFFECT = pltpu.SideEffectType.DATAFLOW_SIDE_EFFECTING


def _hbm(a):
    return pltpu.with_memory_space_constraint(a, pltpu.HBM)


def _split_start(name, bufs, plan, n, after=None):
    nb = len(bufs)
    extra = [] if after is None else [after]
    ne = len(extra)

    def body(*refs):
        send_sems, recv_sems, token = refs[nb + ne], refs[nb + ne + 1], refs[-1]
        mx, my, mc = _me()
        for k, (src, dst, dev, _) in enumerate(plan(refs[:nb], mx, my, mc)):
            _rcopy(src, dst, send_sems, recv_sems, k, dev).start()
        token[...] = jnp.zeros_like(token)

    out = pl.pallas_call(
        body, name=name,
        out_shape=(pltpu.SemaphoreType.DMA((n,)), pltpu.SemaphoreType.DMA((n,)),
                   *[pltpu.HBM(b.shape, b.dtype) for b in bufs], _sds((8, 128))),
        in_specs=[HBM_SPEC] * nb + [ANY_SPEC] * ne,
        out_specs=(SEM_SPEC, SEM_SPEC, *[HBM_SPEC] * nb, pl.BlockSpec(memory_space=pltpu.VMEM)),
        input_output_aliases={i: 2 + i for i in range(nb)},
        compiler_params=pltpu.CompilerParams(has_side_effects=EFFECT))(*[_hbm(b) for b in bufs], *extra)
    return (out[0], out[1]), list(out[2:2 + nb]), out[-1]


def _split_wait(name, sems, bufs, plan, after):
    nb = len(bufs)

    def body(*refs):
        send_sems, recv_sems = refs[nb], refs[nb + 1]
        mx, my, mc = _me()
        for k, (src, dst, dev, land) in enumerate(plan(refs[:nb], mx, my, mc)):
            _rcopy(src, dst, send_sems, recv_sems, k, dev).wait_send()
            _rcopy(src, land, send_sems, recv_sems, k, dev).wait_recv()

    out = pl.pallas_call(
        body, name=name, out_shape=tuple(pltpu.HBM(b.shape, b.dtype) for b in bufs),
        in_specs=[HBM_SPEC] * nb + [SEM_SPEC, SEM_SPEC, ANY_SPEC], out_specs=(HBM_SPEC,) * nb,
        input_output_aliases={i: i for i in range(nb)},
        compiler_params=pltpu.CompilerParams(has_side_effects=EFFECT))(*bufs, sems[0], sems[1], after)
    return list(out)


def _plan_gather_chips(refs, mx, my, mc):
    me = 2 * mx + my
    return [(r.at[me, mc], r.at[me, mc], (px, py, mc), r.at[2 * px + py, mc])
            for r in refs for px, py in _other_chips(mx, my)]


def _plan_gather_pair(refs, mx, my, mc):
    return [(r.at[2 * px + py, mc], r.at[2 * px + py, mc], (mx, my, 1 - mc), r.at[2 * px + py, 1 - mc])
            for r in refs for px, py in _other_chips(mx, my)]


def _plan_gather_devices(refs, mx, my, mc):
    flip = lambda m, b: 1 - m if b else m
    peers = [(flip(mx, k >> 2 & 1), flip(my, k >> 1 & 1), flip(mc, k & 1)) for k in range(1, 8)]
    slot = lambda dev: 4 * dev[0] + 2 * dev[1] + dev[2]
    me = slot((mx, my, mc))
    return [(r.at[me], r.at[me], dev, r.at[slot(dev)]) for r in refs for dev in peers]


def _plan_pair_half(refs, mx, my, mc):
    n = len(refs) // 2
    return [(g.at[:, 1 - mc], r, (mx, my, 1 - mc), r) for g, r in zip(refs[:n], refs[n:])]


def _plan_scatter(refs, mx, my, mc):
    n = len(refs) // 2
    return [(q.at[2 * px + py], r.at[k], (px, py, mc), r.at[k])
            for q, r in zip(refs[:n], refs[n:]) for k, (px, py) in enumerate(_other_chips(mx, my))]


def _plan_pair_fill(refs, mx, my, mc):
    return [(r.at[mc], r.at[mc], (mx, my, 1 - mc), r.at[1 - mc]) for r in refs]


def _row_tile(rows, cols):
    t = 8
    while t * 2 * cols * 4 <= (1 << 21) and rows % (t * 2) == 0:
        t *= 2
    return t


def _prefetch_call(body, name, out_shape, grid, in_specs, out_specs):
    return pl.pallas_call(
        body, name=name, out_shape=out_shape,
        grid_spec=pltpu.PrefetchScalarGridSpec(num_scalar_prefetch=1, grid=grid, in_specs=in_specs,
                                               out_specs=out_specs),
        compiler_params=pltpu.CompilerParams(vmem_limit_bytes=VMEM_LIMIT,
                                             dimension_semantics=("parallel",) * len(grid)))


def _sum_half(g, r1, place, name):
    _, _, rr, ns = g.shape
    tr = _row_tile(rr, ns)

    def body(c_ref, g_ref, r_ref, o_ref, ob_ref):
        q = g_ref[...] + r_ref[...]
        ob_ref[...] = q.astype(BF16)

        @pl.when(pl.program_id(1) == c_ref[0])
        def _():
            o_ref[...] = q

    blk = pl.BlockSpec((None, tr, ns), lambda i, s, c: (s, i, 0))
    return pl.pallas_call(
        body, name=name, out_shape=(_sds((rr, ns)), _sds((NCHIP, rr, ns), BF16)),
        grid_spec=pltpu.PrefetchScalarGridSpec(
            num_scalar_prefetch=1, grid=(rr // tr, NCHIP),
            in_specs=[pl.BlockSpec((None, None, tr, ns), lambda i, s, c: (s, c[1], i, 0)), blk],
            out_specs=(pl.BlockSpec((tr, ns), lambda i, s, c: (i, 0)), blk)),
        compiler_params=pltpu.CompilerParams(vmem_limit_bytes=VMEM_LIMIT,
                                             dimension_semantics=("parallel", "arbitrary")))(place, g, r1)


def _sum_recv(q, r2, place, name):
    rr, ns = q.shape
    tr = _row_tile(rr, ns)

    def body(c_ref, q_ref, r_ref, o_ref):
        o_ref[...] = ((q_ref[...] + r_ref[0].astype(F32)) + r_ref[1].astype(F32)) + r_ref[2].astype(F32)

    return _prefetch_call(body, name, _sds((2, rr, ns)), (rr // tr,),
                          [pl.BlockSpec((tr, ns), lambda i, c: (i, 0)),
                           pl.BlockSpec((NCHIP - 1, tr, ns), lambda i, c: (0, i, 0))],
                          pl.BlockSpec((None, tr, ns), lambda i, c: (c[1], i, 0)))(place, q, r2)


def _sum_devices(v8):
    def body(v_ref, o_ref):
        acc = v_ref[0]
        for dev in range(1, 8):
            acc = acc + v_ref[dev]
        o_ref[...] = acc

    return _pcall(body, name="sum_devices", out_shape=_sds(v8.shape[1:]))(v8)


def _adamw(w, g, m, v, name):
    rows, cols = w.shape
    tr = _row_tile(rows, cols)

    def body(w_ref, g_ref, m_ref, v_ref, d_ref, m2_ref, v2_ref):
        g = g_ref[...]
        m2 = ADAM_B1 * m_ref[...] + (1.0 - ADAM_B1) * g
        v2 = ADAM_B2 * v_ref[...] + (1.0 - ADAM_B2) * (g * g)
        m_hat = m2 / (1.0 - ADAM_B1 ** ADAM_STEP)
        v_hat = v2 / (1.0 - ADAM_B2 ** ADAM_STEP)
        d_ref[...] = -ADAM_LR * (m_hat / (jnp.sqrt(v_hat) + ADAM_EPS) + ADAM_WD * w_ref[...])
        m2_ref[...] = m2
        v2_ref[...] = v2

    blk = pl.BlockSpec((tr, cols), lambda i: (i, 0))
    return _pcall(body, name=name, out_shape=(_sds((rows, cols)),) * 3, grid=(rows // tr,), in_specs=[blk] * 4,
                  out_specs=(blk,) * 3, semantics=("parallel",))(w, g, m, v)


BIG = ("w_in", "w_conv_out", "w_attn_out", "w_out", "w_ff1", "w_ff2")
SMALL = ("rel_bias", "norm1_g", "q_norm_g", "k_norm_g", "conv_dw_w", "conv_dw_b", "conv_ln_g", "conv_ln_b", "norm2_g")
WEIGHTS = ("rel_bias", "norm1_g", "w_in", "q_norm_g", "k_norm_g", "conv_dw_w", "conv_dw_b", "conv_ln_g", "conv_ln_b",
           "w_conv_out", "w_attn_out", "w_out", "norm2_g", "w_ff1", "w_ff2")


def _pack(arrays):
    flat = jnp.concatenate([a.reshape(-1) for a in arrays])
    n = flat.shape[0]
    rows = -(-n // 1024) * 8
    return jnp.pad(flat, (0, rows * 128 - n)).reshape(rows, 128)


def _unpack(packed, shapes):
    flat = packed.reshape(-1)
    out, off = [], 0
    for shp in shapes:
        n = int(np.prod(shp))
        out.append(flat[off:off + n].reshape(shp))
        off += n
    return out


def _adamw_layer(l, w, g, m, v, prev, name, deps=()):
    _, k, n = w.shape
    tr = _row_tile(k, n)
    deps = tuple(d for d in deps if d is not None)
    if prev is None:
        prev = tuple(lax.empty(w.shape, F32) for _ in range(4))

    def body(*refs):
        w_ref, g_ref, m_ref, v_ref = refs[:4]
        go_ref, d_ref, m2_ref, v2_ref = refs[-4:]
        g = g_ref[...]
        m2 = ADAM_B1 * m_ref[...] + (1.0 - ADAM_B1) * g
        v2 = ADAM_B2 * v_ref[...] + (1.0 - ADAM_B2) * (g * g)
        m_hat = m2 / (1.0 - ADAM_B1 ** ADAM_STEP)
        v_hat = v2 / (1.0 - ADAM_B2 ** ADAM_STEP)
        go_ref[...] = g
        d_ref[...] = -ADAM_LR * (m_hat / (jnp.sqrt(v_hat) + ADAM_EPS) + ADAM_WD * w_ref[...])
        m2_ref[...] = m2
        v2_ref[...] = v2

    lay = pl.BlockSpec((None, tr, n), lambda i: (l, i, 0))
    return _pcall(body, name=name, out_shape=(_sds(w.shape),) * 4, grid=(k // tr,),
                  in_specs=[lay, pl.BlockSpec((tr, n), lambda i: (i, 0)), lay, lay] + [ANY_SPEC] * (4 + len(deps)),
                  out_specs=(lay,) * 4, aliases={4: 0, 5: 1, 6: 2, 7: 3},
                  semantics=("parallel",))(w, g, m, v, *prev, *deps)


class _GradPipe:
    def __init__(self, l, kinds, tag, big, place, w, m, v, results):
        self.l, self.kinds, self.place, self.w, self.m, self.v, self.results = l, kinds, place, w, m, v, results
        self.id = f"l{l}{tag}"
        self.g = [big[n].reshape(NCHIP, 2, big[n].shape[1] // 2, big[n].shape[2]) for n in kinds]

    def step0(self):
        lands = [lax.empty((NCHIP,) + g.shape[2:], F32) for g in self.g]
        self.s1, self.b1, tok = _split_start(f"rs1_start_{self.id}", self.g + lands, _plan_pair_half, len(self.kinds))
        return tok

    def step1(self, after):
        nk = len(self.kinds)
        bufs = _split_wait(f"rs1_wait_{self.id}", self.s1, self.b1, _plan_pair_half, after)
        sums = [_sum_half(bufs[i], bufs[nk + i], self.place, f"rs1_sum_{n}") for i, n in enumerate(self.kinds)]
        self.q = [q for q, _ in sums]
        qb = [b for _, b in sums]
        lands = [lax.empty((NCHIP - 1,) + b.shape[1:], BF16) for b in qb]
        self.s2, self.b2, tok = _split_start(f"rs2_start_{self.id}", qb + lands, _plan_scatter, 3 * nk)
        return tok

    def step2(self, after):
        nk = len(self.kinds)
        bufs = _split_wait(f"rs2_wait_{self.id}", self.s2, self.b2, _plan_scatter, after)
        fin = [_sum_recv(self.q[i], bufs[nk + i], self.place, f"rs2_sum_{n}") for i, n in enumerate(self.kinds)]
        self.s3, self.b3, tok = _split_start(f"rs3_start_{self.id}", fin, _plan_pair_fill, nk)
        return tok

    def step3(self, after):
        self.fin = _split_wait(f"rs3_wait_{self.id}", self.s3, self.b3, _plan_pair_fill, after)

    def adam(self, deps=()):
        for i, n in enumerate(self.kinds):
            g2 = self.fin[i].reshape(self.fin[i].shape[1] * 2, self.fin[i].shape[2])
            self.results[n] = _adamw_layer(self.l, self.w[n], g2, self.m[n], self.v[n], self.results.get(n),
                                           f"adamw_{n}_l{self.l}", deps=deps if i == 0 else ())
        return self.results[self.kinds[-1]][1]


def kernel(x, rel_bias, norm1_g, w_in, q_norm_g, k_norm_g, conv_dw_w, conv_dw_b, conv_ln_g, conv_ln_b, w_conv_out, w_attn_out, w_out, norm2_g, w_ff1, w_ff2, loss_target, m_rel_bias, m_norm1_g, m_w_in, m_q_norm_g, m_k_norm_g, m_conv_dw_w, m_conv_dw_b, m_conv_ln_g, m_conv_ln_b, m_w_conv_out, m_w_attn_out, m_w_out, m_norm2_g, m_w_ff1, m_w_ff2, v_rel_bias, v_norm1_g, v_w_in, v_q_norm_g, v_k_norm_g, v_conv_dw_w, v_conv_dw_b, v_conv_ln_g, v_conv_ln_b, v_w_conv_out, v_w_attn_out, v_w_out, v_norm2_g, v_w_ff1, v_w_ff2):
    w = dict(rel_bias=rel_bias, norm1_g=norm1_g, w_in=w_in, q_norm_g=q_norm_g, k_norm_g=k_norm_g, conv_dw_w=conv_dw_w,
             conv_dw_b=conv_dw_b, conv_ln_g=conv_ln_g, conv_ln_b=conv_ln_b, w_conv_out=w_conv_out,
             w_attn_out=w_attn_out, w_out=w_out, norm2_g=norm2_g, w_ff1=w_ff1, w_ff2=w_ff2)
    m = dict(rel_bias=m_rel_bias, norm1_g=m_norm1_g, w_in=m_w_in, q_norm_g=m_q_norm_g, k_norm_g=m_k_norm_g,
             conv_dw_w=m_conv_dw_w, conv_dw_b=m_conv_dw_b, conv_ln_g=m_conv_ln_g, conv_ln_b=m_conv_ln_b,
             w_conv_out=m_w_conv_out, w_attn_out=m_w_attn_out, w_out=m_w_out, norm2_g=m_norm2_g, w_ff1=m_w_ff1,
             w_ff2=m_w_ff2)
    v = dict(rel_bias=v_rel_bias, norm1_g=v_norm1_g, w_in=v_w_in, q_norm_g=v_q_norm_g, k_norm_g=v_k_norm_g,
             conv_dw_w=v_conv_dw_w, conv_dw_b=v_conv_dw_b, conv_ln_g=v_conv_ln_g, conv_ln_b=v_conv_ln_b,
             w_conv_out=v_w_conv_out, w_attn_out=v_w_attn_out, w_out=v_w_out, norm2_g=v_norm2_g, w_ff1=v_w_ff1,
             w_ff2=v_w_ff2)
    chip_id = 2 * lax.axis_index("x") + lax.axis_index("y")
    place = jnp.stack([chip_id, lax.axis_index("c")]).astype(jnp.int32)

    dww4 = _all_gather_chips(conv_dw_w, "ag_conv_dw_w")
    dww = dww4.transpose(1, 2, 0, 3).reshape(DEPTH, KW, CONV)

    names = dict(w_in="win4", w_conv_out="wco4", w_attn_out="wao4", w_out="wout4", w_ff1="wff14", w_ff2="wff24")
    chips, pair = {}, {}

    def start_chips(key, l, kinds, after):
        lands = []
        for n in kinds:
            k, ns = w[n].shape[1:]
            land = lax.dynamic_update_slice(lax.empty((NCHIP, k, ns), BF16), w[n][l].astype(BF16)[None], (chip_id, 0, 0))
            lands.append(land.reshape(NCHIP, 2, k // 2, ns))
        chips[key] = _split_start(f"ag_chips_start_{key}", lands, _plan_gather_chips, 3 * len(kinds), after=after)
        return chips[key][2]

    def start_pair(key, after):
        sems, bufs, _ = chips[key]
        bufs = _split_wait(f"ag_chips_wait_{key}", sems, bufs, _plan_gather_chips, after)
        pair[key] = _split_start(f"ag_pair_start_{key}", bufs, _plan_gather_pair, len(bufs) * 3)
        return pair[key][2]

    def landed(key, kinds, after):
        sems, bufs, _ = pair[key]
        bufs = _split_wait(f"ag_pair_wait_{key}", sems, bufs, _plan_gather_pair, after)
        return {names[n]: b.reshape(NCHIP, 2 * b.shape[2], b.shape[3]) for n, b in zip(kinds, bufs)}

    first, rest = ("w_in",), tuple(n for n in BIG if n != "w_in")
    start_chips("l0b", 0, rest, start_pair("l0a", start_chips("l0a", 0, first, dww4)))

    def get_layer(l, after):
        p = dict(dww=dww[l], dwb=conv_dw_b[l][None], lng=conv_ln_g[l][None], lnb=conv_ln_b[l][None],
                 n1g=norm1_g[l][None], n2g=norm2_g[l][None], gq=jnp.tile(q_norm_g[l], 2)[None],
                 gk=jnp.tile(k_norm_g[l], 2)[None])
        p.update(landed(f"l{l}a", first, chips["l0b"][2] if l == 0 else after))
        more = l + 1 < DEPTH

        def hook(after_):
            tok = start_pair(f"l{l}b", after_)
            return start_chips(f"l{l + 1}a", l + 1, first, tok) if more else tok

        p["hook"] = hook
        p["rest"] = lambda after_: landed(f"l{l}b", rest, after_)
        mid = (lambda after_: start_chips(f"l{l + 1}b", l + 1, rest, start_pair(f"l{l + 1}a", after_))) if more else None
        return p, (), mid

    results = {}
    make_pipe = lambda l, kinds, tag, big: _GradPipe(l, kinds, tag, big, place, w, m, v, results)
    loss_share, dx, smalls, d_rel, pipes, early, late = _local_step(x[0], loss_target[0], get_layer, rel_bias,
                                                                    make_pipe)
    loss = lax.psum(loss_share, ("x", "y", "c"))

    local_small = dict(rel_bias=d_rel)
    for n in SMALL[1:]:
        local_small[n] = jnp.stack([smalls[l][n] for l in range(DEPTH)])
    small_shapes = [local_small[n].shape for n in SMALL]
    mine = _pack([local_small[n] for n in SMALL])
    slot = 4 * lax.axis_index("x") + 2 * lax.axis_index("y") + lax.axis_index("c")
    land = lax.dynamic_update_slice(lax.empty((8,) + mine.shape, F32), mine[None], (slot, 0, 0))
    small_sems, small_bufs, tok = _split_start("ag_small_start", [land], _plan_gather_devices, 7)
    for pipe in pipes:
        done = pipe.adam(deps=(tok,))
        tok = None
    early.step3(late.step2(done))
    done = early.adam()
    late.step3(done)
    done = late.adam()
    gathered = _split_wait("ag_small_wait", small_sems, small_bufs, _plan_gather_devices, done)[0]
    summed = _sum_devices(gathered)
    grads = dict(zip(SMALL, _unpack(summed, small_shapes)))
    grads["conv_dw_w"] = lax.dynamic_slice_in_dim(grads["conv_dw_w"], chip_id * 128, 128, axis=2)

    delta, new_m, new_v = {}, {}, {}
    small_w_shapes = [w[n].shape for n in SMALL]
    outs = _adamw(_pack([w[n] for n in SMALL]), _pack([grads[n] for n in SMALL]), _pack([m[n] for n in SMALL]),
                  _pack([v[n] for n in SMALL]), "adamw_small")
    for dst, packed in zip((delta, new_m, new_v), outs):
        dst.update(zip(SMALL, _unpack(packed, small_w_shapes)))

    for n in BIG:
        grads[n], delta[n], new_m[n], new_v[n] = results[n]

    return (loss, dx[None], *[grads[n] for n in WEIGHTS], *[delta[n] for n in WEIGHTS],
            *[new_m[n] for n in WEIGHTS], *[new_v[n] for n in WEIGHTS])
```
